```python
import math, functools
import jax, jax.numpy as jnp
from jax import lax
import numpy as np

D_MODEL = 1024
BATCH = 4
SEQ = 4096
DEPTH = 1
DEC_BATCH = 128
DEC_SEQ = 8
PAST_LEN = 2048
PAGE_SIZE = 128

N_HEADS_A = 16
HEAD_DIM_A = 64
ATTN_W = N_HEADS_A * HEAD_DIM_A
Q_BLOCK = 128
D_INNER = 2 * D_MODEL
HEAD_DIM_S = 64
N_HEADS_S = D_INNER // HEAD_DIM_S
N_GROUPS_S = 4
D_STATE = 128
CONV_K = 4
CONV_DIM = D_INNER + 2 * N_GROUPS_S * D_STATE
SSD_CHUNK = 128
N_EXPERTS = 32
TOP_K = 4
D_FF = D_MODEL
SWIGLU_LIMIT = 7.0
SWIGLU_ALPHA = 1.702
MOE_BLOCK = 128
IN_SIZES = (ATTN_W, ATTN_W, ATTN_W, N_HEADS_A, D_INNER, CONV_DIM, N_HEADS_S, D_MODEL, D_MODEL)
IN_W = sum(IN_SIZES)
EPS = 1e-6

kernel_name = 'fox_ssd_moe_adaln_hybrid_step'


def rms_norm(x, w):
    xf = x.astype(jnp.float32)
    y = xf * lax.rsqrt(jnp.mean(xf * xf, axis=-1, keepdims=True) + EPS)
    return (y * w.astype(jnp.float32)).astype(x.dtype)


def fox_core(q, k, v, f_q, f_k, q_pos, k_pos):
    s = jnp.einsum('bqhd,bkhd->bhqk', q, k).astype(jnp.float32) * (HEAD_DIM_A ** -0.5)
    s = s + (jnp.swapaxes(f_q, 1, 2)[:, :, :, None] - jnp.swapaxes(f_k, 1, 2)[:, :, None, :])
    s = jnp.where(k_pos[None, None, None, :] <= q_pos[None, None, :, None], s, -jnp.inf)
    p = jax.nn.softmax(s, axis=-1).astype(v.dtype)
    return jnp.einsum('bhqk,bkhd->bqhd', p, v)


def fox_prompt(q, k, v, logf):
    b, L, h, d = q.shape
    F = jnp.cumsum(logf.astype(jnp.float32), axis=1)
    nb = L // Q_BLOCK
    q_blocks = jnp.swapaxes(q.reshape(b, nb, Q_BLOCK, h, d), 0, 1)
    f_blocks = jnp.swapaxes(F.reshape(b, nb, Q_BLOCK, h), 0, 1)
    k_pos = jnp.arange(L)

    def one_block(args):
        q_blk, f_blk, blk = args
        q_pos = blk * Q_BLOCK + jnp.arange(Q_BLOCK)
        return fox_core(q_blk, k, v, f_blk, F, q_pos, k_pos)

    out = lax.map(one_block, (q_blocks, f_blocks, jnp.arange(nb)))
    return jnp.swapaxes(out, 0, 1).reshape(b, L, h, d)


def fox_sample(q, k, v, logf, cache_k, cache_v, cache_logf, page_table):
    b, L = q.shape[:2]
    past = page_table.shape[1] * PAGE_SIZE
    k_all = jnp.concatenate([cache_k[page_table].reshape(b, past, N_HEADS_A, HEAD_DIM_A), k], axis=1)
    v_all = jnp.concatenate([cache_v[page_table].reshape(b, past, N_HEADS_A, HEAD_DIM_A), v], axis=1)
    lf_all = jnp.concatenate([cache_logf[page_table].reshape(b, past, N_HEADS_A).astype(jnp.float32),
                              logf.astype(jnp.float32)], axis=1)
    F = jnp.cumsum(lf_all, axis=1)
    q_pos = past + jnp.arange(L)
    k_pos = jnp.arange(past + L)
    return fox_core(q, k_all, v_all, F[:, past:], F, q_pos, k_pos)


def causal_conv(xbc, prev, conv_w, conv_b):
    L = xbc.shape[1]
    xp = jnp.concatenate([prev.astype(xbc.dtype), xbc], axis=1)
    y = conv_b + sum(xp[:, i:i + L] * conv_w[i] for i in range(CONV_K))
    return jax.nn.silu(y), xp[:, -(CONV_K - 1):]


def ssd_scan(x, dt, A, Bm, Cm, state0, chunk):
    b, L, h, p = x.shape
    g, n = Bm.shape[2], Bm.shape[3]
    r = h // g
    nc = L // chunk

    def to_chunks(t):
        return jnp.swapaxes(t.reshape((b, nc, chunk) + t.shape[2:]), 0, 1)

    xs = to_chunks(x.astype(jnp.float32).reshape(b, L, g, r, p))
    dts = to_chunks(dt.reshape(b, L, g, r))
    Bs = to_chunks(Bm.astype(jnp.float32))
    Cs = to_chunks(Cm.astype(jnp.float32))
    a_g = A.reshape(g, r)
    tri = jnp.tril(jnp.ones((chunk, chunk), dtype=bool))[None, :, :, None, None]

    def step(S, inp):
        xc, dtc, Bc, Cc = inp
        a = jnp.cumsum(dtc * a_g, axis=1)
        seg = a[:, :, None] - a[:, None, :]
        decay = jnp.exp(jnp.where(tri, seg, -jnp.inf))
        cb = jnp.einsum('bqgn,bsgn->bqsg', Cc, Bc)
        xdt = xc * dtc[..., None]
        y_diag = jnp.einsum('bqsg,bqsgr,bsgrp->bqgrp', cb, decay, xdt)
        y_off = jnp.einsum('bqgn,bgrpn->bqgrp', Cc, S) * jnp.exp(a)[..., None]
        last = a[:, -1]
        w = jnp.exp(last[:, None] - a) * dtc
        S_new = S * jnp.exp(last)[..., None, None] + jnp.einsum('bsgr,bsgrp,bsgn->bgrpn', w, xc, Bc)
        return S_new, y_diag + y_off

    S0 = state0.astype(jnp.float32).reshape(b, g, r, p, n)
    S, ys = lax.scan(step, S0, (xs, dts, Bs, Cs))
    y = jnp.swapaxes(ys, 0, 1).reshape(b, L, h, p)
    return y, S.reshape(b, h, p, n)


def mixer(h, attn_fn, ssm0, conv0, lw):
    b, L, _ = h.shape
    split_points = np.cumsum(IN_SIZES)[:-1].tolist()
    q, k, v, f_pre, z, xbc, dt_pre, g_a, g_s = jnp.split(h @ lw['w_in'], split_points, axis=-1)
    q = q.reshape(b, L, N_HEADS_A, HEAD_DIM_A)
    k = k.reshape(b, L, N_HEADS_A, HEAD_DIM_A)
    v = v.reshape(b, L, N_HEADS_A, HEAD_DIM_A)
    logf = jax.nn.log_sigmoid(f_pre.astype(jnp.float32) + lw['b_fgate'].astype(jnp.float32))
    o_attn = attn_fn(q, k, v, logf).reshape(b, L, ATTN_W)
    y_attn = o_attn @ lw['w_attn_out']
    xbc_act, conv_new = causal_conv(xbc, conv0, lw['conv_w'], lw['conv_b'])
    xs, Bm, Cm = jnp.split(xbc_act, [D_INNER, D_INNER + N_GROUPS_S * D_STATE], axis=-1)
    dt = jax.nn.softplus(dt_pre.astype(jnp.float32) + lw['dt_bias'].astype(jnp.float32))
    A = -jnp.exp(lw['a_log'].astype(jnp.float32))
    xh = xs.reshape(b, L, N_HEADS_S, HEAD_DIM_S)
    chunk = SSD_CHUNK if L % SSD_CHUNK == 0 else L
    y, ssm_new = ssd_scan(xh, dt, A, Bm.reshape(b, L, N_GROUPS_S, D_STATE),
                          Cm.reshape(b, L, N_GROUPS_S, D_STATE), ssm0, chunk)
    y = y + xh.astype(jnp.float32) * lw['d_skip'].astype(jnp.float32)[:, None]
    y = y.reshape(b, L, D_INNER) * jax.nn.silu(z.astype(jnp.float32))
    gsz = D_INNER // N_GROUPS_S
    y = rms_norm(y.reshape(b, L, N_GROUPS_S, gsz), lw['ssm_norm_w'].reshape(N_GROUPS_S, gsz))
    y_ssm = y.reshape(b, L, D_INNER).astype(h.dtype) @ lw['w_ssm_out']
    merged = jax.nn.sigmoid(g_a) * y_attn + jax.nn.sigmoid(g_s) * y_ssm
    return merged @ lw['w_o'], (k, v, logf, ssm_new, conv_new)


def moe(h, lw):
    b, L, d = h.shape
    t = h.reshape(-1, d)
    T = t.shape[0]
    logits = (t @ lw['w_router'] + lw['b_router']).astype(jnp.float32)
    top_v, top_e = lax.top_k(logits, TOP_K)
    top_w = jax.nn.softmax(top_v, axis=-1)
    e_flat = top_e.reshape(-1)
    tok_flat = jnp.repeat(jnp.arange(T, dtype=jnp.int32), TOP_K)
    w_flat = top_w.reshape(-1)
    order = jnp.argsort(e_flat)
    e_sorted = e_flat[order]
    counts = jnp.bincount(e_flat, length=N_EXPERTS)
    padded = (counts + MOE_BLOCK - 1) // MOE_BLOCK * MOE_BLOCK
    start = jnp.cumsum(counts) - counts
    pend = jnp.cumsum(padded)
    pstart = pend - padded
    dest = pstart[e_sorted] + (jnp.arange(T * TOP_K) - start[e_sorted])
    n_blocks = -(-(T * TOP_K) // MOE_BLOCK) + N_EXPERTS
    P = n_blocks * MOE_BLOCK
    row_tok = jnp.zeros((P,), jnp.int32).at[dest].set(tok_flat[order])
    row_w = jnp.zeros((P,), jnp.float32).at[dest].set(w_flat[order])
    block_e = jnp.minimum(jnp.searchsorted(pend, jnp.arange(n_blocks) * MOE_BLOCK, side='right'), N_EXPERTS - 1)
    w_gu, b_gu, w_dn, b_dn = lw['w_gate_up'], lw['b_gate_up'], lw['w_down'], lw['b_down']

    def expert_block(args):
        tok, e = args
        gu = t[tok] @ w_gu[e] + b_gu[e]
        gate = jnp.minimum(gu[:, 0::2], SWIGLU_LIMIT)
        up = jnp.clip(gu[:, 1::2], -SWIGLU_LIMIT, SWIGLU_LIMIT)
        act = (up + 1) * (gate * jax.nn.sigmoid(SWIGLU_ALPHA * gate))
        return act @ w_dn[e] + b_dn[e]

    rows = lax.map(expert_block, (row_tok.reshape(n_blocks, MOE_BLOCK), block_e)).reshape(P, d)
    rows = rows * row_w[:, None].astype(rows.dtype)
    return jax.ops.segment_sum(rows, row_tok, num_segments=T).reshape(b, L, d)


def block(x, c, attn_fn, ssm0, conv0, lw):
    mod = (jax.nn.silu(c) @ lw['w_cond'] + lw['b_cond'])[:, None, :].astype(x.dtype)
    sh1, sc1, g1, sh2, sc2, g2 = jnp.split(mod, 6, axis=-1)
    hmix = rms_norm(x, lw['norm1_w']) * (1 + sc1) + sh1
    m, new_state = mixer(hmix, attn_fn, ssm0, conv0, lw)
    x = x + g1 * m
    hffn = rms_norm(x, lw['norm2_w']) * (1 + sc2) + sh2
    x = x + g2 * moe(hffn, lw)
    return x, new_state


def setup_inputs(seed: int = 0) -> dict:
    key = jax.random.key(seed)
    keys = list(jax.random.split(key, 40))

    def nrm(shape, scale):
        return jax.random.normal(keys.pop(), shape, jnp.float32) * scale

    n_pages = PAST_LEN // PAGE_SIZE
    n_phys = (5 * DEC_BATCH * n_pages) // 4
    perm = jax.random.permutation(keys.pop(), n_phys)
    page_table = perm[:DEC_BATCH * n_pages].reshape(DEC_BATCH, n_pages).astype(jnp.int32)
    dt0 = jnp.exp(jax.random.uniform(keys.pop(), (DEPTH, N_HEADS_S), jnp.float32,
                                     math.log(1e-3), math.log(1e-1)))
    dt_bias = dt0 + jnp.log(-jnp.expm1(-dt0))
    a_log = jnp.log(jax.random.uniform(keys.pop(), (DEPTH, N_HEADS_S), jnp.float32, 1.0, 16.0))
    return {
        'x_prompt': nrm((BATCH, SEQ, D_MODEL), 1.0),
        'x_sample': nrm((DEC_BATCH, DEC_SEQ, D_MODEL), 1.0),
        'cache_k': nrm((DEPTH, n_phys, PAGE_SIZE, N_HEADS_A, HEAD_DIM_A), 1.0),
        'cache_v': nrm((DEPTH, n_phys, PAGE_SIZE, N_HEADS_A, HEAD_DIM_A), 1.0),
        'cache_logf': jax.nn.log_sigmoid(4.0 + nrm((DEPTH, n_phys, PAGE_SIZE, N_HEADS_A), 1.0)),
        'state_ssm': nrm((DEPTH, DEC_BATCH, N_HEADS_S, HEAD_DIM_S, D_STATE), 0.1),
        'state_conv': nrm((DEPTH, DEC_BATCH, CONV_K - 1, CONV_DIM), 1.0),
        'page_table': page_table,
        'c_prompt': nrm((BATCH, D_MODEL), 1.0),
        'c_sample': nrm((DEC_BATCH, D_MODEL), 1.0),
        'w_cond': nrm((DEPTH, D_MODEL, 6 * D_MODEL), D_MODEL ** -0.5),
        'b_cond': nrm((DEPTH, 6 * D_MODEL), 0.02),
        'norm1_w': 1.0 + nrm((DEPTH, D_MODEL), 0.02),
        'w_in': nrm((DEPTH, D_MODEL, IN_W), D_MODEL ** -0.5),
        'b_fgate': 4.0 + nrm((DEPTH, N_HEADS_A), 0.5),
        'conv_w': nrm((DEPTH, CONV_K, CONV_DIM), CONV_K ** -0.5),
        'conv_b': nrm((DEPTH, CONV_DIM), 0.02),
        'dt_bias': dt_bias,
        'a_log': a_log,
        'd_skip': 1.0 + nrm((DEPTH, N_HEADS_S), 0.1),
        'ssm_norm_w': 1.0 + nrm((DEPTH, D_INNER), 0.02),
        'w_attn_out': nrm((DEPTH, ATTN_W, D_MODEL), ATTN_W ** -0.5),
        'w_ssm_out': nrm((DEPTH, D_INNER, D_MODEL), D_INNER ** -0.5),
        'w_o': nrm((DEPTH, D_MODEL, D_MODEL), D_MODEL ** -0.5),
        'norm2_w': 1.0 + nrm((DEPTH, D_MODEL), 0.02),
        'w_router': nrm((DEPTH, D_MODEL, N_EXPERTS), D_MODEL ** -0.5),
        'b_router': nrm((DEPTH, N_EXPERTS), 0.01),
        'w_gate_up': nrm((DEPTH, N_EXPERTS, D_MODEL, 2 * D_FF), D_MODEL ** -0.5),
        'b_gate_up': nrm((DEPTH, N_EXPERTS, 2 * D_FF), 0.02),
        'w_down': nrm((DEPTH, N_EXPERTS, D_FF, D_MODEL), D_FF ** -0.5),
        'b_down': nrm((DEPTH, N_EXPERTS, D_MODEL), 0.02),
        'norm_f_w': 1.0 + nrm((D_MODEL,), 0.02),
    }


def reference(x_prompt, x_sample, cache_k, cache_v, cache_logf, state_ssm, state_conv, page_table,
              c_prompt, c_sample, w_cond, b_cond, norm1_w, w_in, b_fgate, conv_w, conv_b, dt_bias,
              a_log, d_skip, ssm_norm_w, w_attn_out, w_ssm_out, w_o, norm2_w, w_router, b_router,
              w_gate_up, b_gate_up, w_down, b_down, norm_f_w):
    xp, xs = x_prompt, x_sample
    st_prompt, st_sample = [], []
    for l in range(DEPTH):
        lw = dict(w_cond=w_cond[l], b_cond=b_cond[l], norm1_w=norm1_w[l], w_in=w_in[l],
                  b_fgate=b_fgate[l], conv_w=conv_w[l], conv_b=conv_b[l], dt_bias=dt_bias[l],
                  a_log=a_log[l], d_skip=d_skip[l], ssm_norm_w=ssm_norm_w[l],
                  w_attn_out=w_attn_out[l], w_ssm_out=w_ssm_out[l], w_o=w_o[l], norm2_w=norm2_w[l],
                  w_router=w_router[l], b_router=b_router[l], w_gate_up=w_gate_up[l],
                  b_gate_up=b_gate_up[l], w_down=w_down[l], b_down=b_down[l])
        ssm0_p = jnp.zeros((xp.shape[0], N_HEADS_S, HEAD_DIM_S, D_STATE), jnp.float32)
        conv0_p = jnp.zeros((xp.shape[0], CONV_K - 1, CONV_DIM), xp.dtype)
        xp, sp = block(xp, c_prompt, fox_prompt, ssm0_p, conv0_p, lw)
        attn_s = functools.partial(fox_sample, cache_k=cache_k[l], cache_v=cache_v[l],
                                   cache_logf=cache_logf[l], page_table=page_table)
        xs, ss = block(xs, c_sample, attn_s, state_ssm[l], state_conv[l], lw)
        st_prompt.append(sp)
        st_sample.append(ss)
    y_prompt = rms_norm(xp, norm_f_w)
    y_sample = rms_norm(xs, norm_f_w)
    k_prompt = jnp.stack([s[0] for s in st_prompt])
    v_prompt = jnp.stack([s[1] for s in st_prompt])
    logf_prompt = jnp.stack([s[2] for s in st_prompt])
    ssm_prompt = jnp.stack([s[3] for s in st_prompt])
    conv_prompt = jnp.stack([s[4] for s in st_prompt])
    k_sample = jnp.stack([s[0] for s in st_sample])
    v_sample = jnp.stack([s[1] for s in st_sample])
    logf_sample = jnp.stack([s[2] for s in st_sample])
    ssm_sample = jnp.stack([s[3] for s in st_sample])
    conv_sample = jnp.stack([s[4] for s in st_sample])
    return (y_prompt, y_sample, k_prompt, v_prompt, logf_prompt, ssm_prompt, conv_prompt,
            k_sample, v_sample, logf_sample, ssm_sample, conv_sample)
```

```python
import functools

import numpy as np
import jax
import jax.numpy as jnp
from jax import lax
from jax.experimental import pallas as pl
from jax.experimental.pallas import tpu as pltpu

F32 = jnp.float32
BF16 = jnp.bfloat16

D_MODEL = 1024
N_HEADS_A = 16
HEAD_DIM_A = 64
PAGE_SIZE = 128
D_INNER = 2048
HEAD_DIM_S = 64
N_HEADS_S = 32
N_GROUPS_S = 4
HEADS_PER_GROUP = N_HEADS_S // N_GROUPS_S
GROUP_W = D_INNER // N_GROUPS_S
D_STATE = 128
CONV_K = 4
CONV_DIM = D_INNER + 2 * N_GROUPS_S * D_STATE
SSD_CHUNK = 128
N_EXPERTS = 32
TOP_K = 4
D_FF = D_MODEL
SWIGLU_LIMIT = 7.0
SWIGLU_ALPHA = 1.702
EPS = 1e-6

LANES = 128
DT_LANE0 = N_HEADS_A
NEG_INF = float("-inf")
VMEM_LIMIT = 56 * 1024 * 1024


def _cparams(sem):
    return pltpu.CompilerParams(dimension_semantics=sem, vmem_limit_bytes=VMEM_LIMIT)


def _split3(x):
    hi = x.astype(BF16)
    r = x - hi.astype(F32)
    mid = r.astype(BF16)
    lo = (r - mid.astype(F32)).astype(BF16)
    return hi, mid, lo


def _dot(a, b):
    return jnp.dot(a, b, preferred_element_type=F32)


def _dot_nt(a, b):
    return lax.dot_general(a, b, (((1,), (1,)), ((), ())), preferred_element_type=F32)


def _dot_sel_left(sel_bf16, x):
    hi, mid, lo = _split3(x)
    return _dot(sel_bf16, hi) + _dot(sel_bf16, mid) + _dot(sel_bf16, lo)


def _dot3(a, b):
    ah = a.astype(BF16)
    am = (a - ah.astype(F32)).astype(BF16)
    bh = b.astype(BF16)
    bm = (b - bh.astype(F32)).astype(BF16)
    return _dot(ah, bh) + _dot(ah, bm) + _dot(am, bh)


def _tri_lower(n):
    r = lax.broadcasted_iota(jnp.int32, (n, n), 0)
    c = lax.broadcasted_iota(jnp.int32, (n, n), 1)
    return r >= c


def _sigmoid(x):
    return 1.0 / (1.0 + jnp.exp(-x))


def _silu(x):
    return x * _sigmoid(x)


def _mod_body(c_ref, w_ref, b_ref, o_ref):
    c = c_ref[...]
    o_ref[...] = _dot3(_silu(c), w_ref[...]) + b_ref[...]


def _adaln_mod(c_all, w_cond, b_cond):
    rows, d = c_all.shape
    n = w_cond.shape[1]
    tn = 1024
    return pl.pallas_call(
        _mod_body,
        grid=(n // tn,),
        in_specs=[pl.BlockSpec((rows, d), lambda j: (0, 0)),
                  pl.BlockSpec((d, tn), lambda j: (0, j)),
                  pl.BlockSpec((1, tn), lambda j: (0, j))],
        out_specs=pl.BlockSpec((rows, tn), lambda j: (0, j)),
        out_shape=jax.ShapeDtypeStruct((rows, n), F32),
        compiler_params=_cparams(("arbitrary",)),
        name="adaln_mod",
    )(c_all, w_cond, b_cond.reshape(1, n))


def _norm_in_body(x_ref, sh_ref, sc_ref, nw_ref, ws_ref, bs_ref, h_ref, sm_ref):
    x = x_ref[...]
    bb, bl, d = x.shape
    ms = jnp.mean(x * x, axis=-1, keepdims=True)
    y = x * lax.rsqrt(ms + EPS) * nw_ref[...]
    h = (y * (1.0 + sc_ref[...]) + sh_ref[...]).reshape(bb * bl, d)
    h_ref[...] = h.astype(BF16)
    sm = _dot3(h, ws_ref[...]) + bs_ref[...]
    lane = lax.broadcasted_iota(jnp.int32, sm.shape, 1)
    t = jnp.log(1.0 + jnp.exp(-jnp.abs(sm)))
    sm_ref[...] = jnp.where(lane < N_HEADS_A, jnp.minimum(sm, 0.0) - t, jnp.maximum(sm, 0.0) + t)


def _row_blocks(b, l, rows):
    rows = min(rows, b * l)
    if l >= rows:
        assert l % rows == 0
        return 1, rows, rows
    assert rows % l == 0 and b % (rows // l) == 0
    return rows // l, l, rows


def _norm_in(x, mod, norm_w, w_small, b_small, rows=512):
    b, l, d = x.shape
    bb, bl, rows = _row_blocks(b, l, rows)
    nl = l // bl
    grid = (b // bb, nl)
    return pl.pallas_call(
        _norm_in_body,
        grid=grid,
        in_specs=[pl.BlockSpec((bb, bl, d), lambda i, j: (i, j, 0)),
                  pl.BlockSpec((bb, 1, d), lambda i, j: (i, 0, 0)),
                  pl.BlockSpec((bb, 1, d), lambda i, j: (i, 0, 1)),
                  pl.BlockSpec((1, d), lambda i, j: (0, 0)),
                  pl.BlockSpec((d, LANES), lambda i, j: (0, 0)),
                  pl.BlockSpec((1, LANES), lambda i, j: (0, 0))],
        out_specs=[pl.BlockSpec((rows, d), lambda i, j: (i * nl + j, 0)),
                   pl.BlockSpec((rows, LANES), lambda i, j: (i * nl + j, 0))],
        out_shape=[jax.ShapeDtypeStruct((b * l, d), BF16),
                   jax.ShapeDtypeStruct((b * l, LANES), F32)],
        compiler_params=_cparams(("arbitrary", "arbitrary")),
        name="norm_in",
    )(x, mod, mod, norm_w.reshape(1, d), w_small, b_small)


def _mm_body(a_ref, w_ref, o_ref):
    o_ref[...] = _dot_nt(a_ref[...], w_ref[...])


def _matmul_nt(a, w_t, tm=512, tn=1024):
    m, k = a.shape
    n = w_t.shape[0]
    tn = min(tn, n)
    return pl.pallas_call(
        _mm_body,
        grid=(n // tn, m // tm),
        in_specs=[pl.BlockSpec((tm, k), lambda j, i: (i, 0)),
                  pl.BlockSpec((tn, k), lambda j, i: (j, 0))],
        out_specs=pl.BlockSpec((tm, tn), lambda j, i: (i, j)),
        out_shape=jax.ShapeDtypeStruct((m, n), F32),
        compiler_params=_cparams(("arbitrary", "arbitrary")),
        name="proj_matmul",
    )(a, w_t)


def _cumsum_t_body(x_ref, o_ref, pad_scr, carry_scr, *, width, carry):
    j = pl.program_id(1)

    @pl.when(j == 0)
    def _():
        pad_scr[...] = jnp.zeros_like(pad_scr)
        carry_scr[...] = jnp.zeros_like(carry_scr)

    if width == LANES:
        blk = x_ref[0]
    else:
        pad_scr[:, 0:width] = x_ref[0]
        blk = pad_scr[...]
    tri = _tri_lower(PAGE_SIZE).astype(BF16)
    cs = _dot_sel_left(tri, blk) + carry_scr[...]
    if carry:
        carry_scr[...] = cs[PAGE_SIZE - 1:PAGE_SIZE, :]
    o_ref[0] = cs.T[0:N_HEADS_A, :]


def _cumsum_t(x, nper, carry):
    n, p, width = x.shape
    assert p == PAGE_SIZE and n % nper == 0
    return pl.pallas_call(
        functools.partial(_cumsum_t_body, width=width, carry=carry),
        grid=(n // nper, nper),
        in_specs=[pl.BlockSpec((1, p, width), lambda i, j: (i * nper + j, 0, 0))],
        out_specs=pl.BlockSpec((1, N_HEADS_A, p), lambda i, j: (i, 0, j)),
        out_shape=jax.ShapeDtypeStruct((n // nper, N_HEADS_A, nper * p), F32),
        scratch_shapes=[pltpu.VMEM((p, LANES), F32), pltpu.VMEM((1, LANES), F32)],
        compiler_params=_cparams(("arbitrary", "arbitrary")),
        name="logf_cumsum",
    )(x)


def _cumsum_lanes_body(x_ref, o_ref):
    n = x_ref.shape[1]
    r = lax.broadcasted_iota(jnp.int32, (n, n), 0)
    c = lax.broadcasted_iota(jnp.int32, (n, n), 1)
    tri = (r <= c).astype(BF16)
    hi, mid, lo = _split3(x_ref[...])
    o_ref[...] = _dot(hi, tri) + _dot(mid, tri) + _dot(lo, tri)


def _cumsum_lanes(x, rows=1024):
    r, n = x.shape
    assert r % rows == 0
    return pl.pallas_call(
        _cumsum_lanes_body,
        grid=(r // rows,),
        in_specs=[pl.BlockSpec((rows, n), lambda i: (i, 0))],
        out_specs=pl.BlockSpec((rows, n), lambda i: (i, 0)),
        out_shape=jax.ShapeDtypeStruct((r, n), F32),
        compiler_params=_cparams(("arbitrary",)),
        name="page_logf_cumsum",
    )(x)


def _fox_prompt_body(it_ref, jt_ref, q_ref, k_ref, v_ref, f_ref, o_ref, m_scr, l_scr, acc_scr, *, tile):
    t = pl.program_id(2)
    i = it_ref[t]
    j = jt_ref[t]

    @pl.when(j == 0)
    def _():
        m_scr[...] = jnp.full_like(m_scr, -1e30)
        l_scr[...] = jnp.zeros_like(l_scr)
        acc_scr[...] = jnp.zeros_like(acc_scr)

    lane = lax.broadcasted_iota(jnp.int32, (tile, LANES), 1)
    first = lane < HEAD_DIM_A

    def step(diagonal):
        q = q_ref[0] * (HEAD_DIM_A ** -0.5)
        kb = k_ref[0].astype(BF16)
        vb = v_ref[0].astype(BF16)
        alphas, pvs = [], []
        for h in range(2):
            qh = jnp.where(first if h == 0 else jnp.logical_not(first), q, 0.0).astype(BF16)
            s = _dot_nt(qh, kb) - f_ref[0, 0, h:h + 1, :]
            if diagonal:
                row = lax.broadcasted_iota(jnp.int32, (tile, tile), 0)
                col = lax.broadcasted_iota(jnp.int32, (tile, tile), 1)
                s = jnp.where(col <= row, s, NEG_INF)
            m_old = m_scr[h]
            m_new = jnp.maximum(m_old, jnp.max(s, axis=-1, keepdims=True))
            alpha = jnp.exp(m_old - m_new)
            p = jnp.exp(s - m_new)
            l_scr[h] = alpha * l_scr[h] + jnp.sum(p, axis=-1, keepdims=True)
            m_scr[h] = m_new
            alphas.append(alpha)
            pvs.append(_dot(p.astype(BF16), vb))
        acc_scr[...] = (acc_scr[...] * jnp.where(first, alphas[0], alphas[1])
                        + jnp.where(first, pvs[0], pvs[1]))

    @pl.when(j < i)
    def _():
        step(False)

    @pl.when(j == i)
    def _():
        step(True)
        o_ref[0] = acc_scr[...] / jnp.where(first, l_scr[0], l_scr[1])


def _fox_prompt(q, k, v, fcum, tile=512):
    b, l, w = q.shape
    npair = w // LANES
    nt = l // tile
    it = np.concatenate([np.full(i + 1, i, np.int32) for i in range(nt)])
    jt = np.concatenate([np.arange(i + 1, dtype=np.int32) for i in range(nt)])
    grid_spec = pltpu.PrefetchScalarGridSpec(
        num_scalar_prefetch=2,
        grid=(b, npair, len(it)),
        in_specs=[pl.BlockSpec((1, tile, LANES), lambda bi, p, t, it, jt: (bi, it[t], p)),
                  pl.BlockSpec((1, tile, LANES), lambda bi, p, t, it, jt: (bi, jt[t], p)),
                  pl.BlockSpec((1, tile, LANES), lambda bi, p, t, it, jt: (bi, jt[t], p)),
                  pl.BlockSpec((1, 1, 2, tile), lambda bi, p, t, it, jt: (bi, p, 0, jt[t]))],
        out_specs=pl.BlockSpec((1, tile, LANES), lambda bi, p, t, it, jt: (bi, it[t], p)),
        scratch_shapes=[pltpu.VMEM((2, tile, 1), F32), pltpu.VMEM((2, tile, 1), F32),
                        pltpu.VMEM((tile, LANES), F32)],
    )
    return pl.pallas_call(
        functools.partial(_fox_prompt_body, tile=tile),
        grid_spec=grid_spec,
        out_shape=jax.ShapeDtypeStruct((b, l, w), F32),
        compiler_params=_cparams(("arbitrary", "arbitrary", "arbitrary")),
        name="fox_prompt",
    )(jnp.asarray(it), jnp.asarray(jt), q, k, v, fcum)


def _fox_sample_body(pt_ref, q_ref, kn_ref, vn_ref, cn_ref, *rest, pages_per_step):
    pg = pages_per_step
    k_refs = rest[0:pg]
    v_refs = rest[pg:2 * pg]
    c_refs = rest[2 * pg:3 * pg]
    o_ref, qbd_scr, m_scr, l_scr, carry_scr, acc_scr = rest[3 * pg:]
    j = pl.program_id(1)
    nq = q_ref.shape[1]
    rows = N_HEADS_A * nq
    width = N_HEADS_A * HEAD_DIM_A

    @pl.when(j == 0)
    def _():
        q = q_ref[0] * (HEAD_DIM_A ** -0.5)
        qt = jnp.broadcast_to(q[None], (N_HEADS_A, nq, width)).reshape(rows, width)
        row = lax.broadcasted_iota(jnp.int32, (rows, width), 0)
        lane = lax.broadcasted_iota(jnp.int32, (rows, width), 1)
        own = (lane // HEAD_DIM_A) == (row // nq)
        qbd_scr[...] = jnp.where(own, qt, 0.0).astype(BF16)
        m_scr[...] = jnp.full_like(m_scr, -1e30)
        l_scr[...] = jnp.zeros_like(l_scr)
        carry_scr[...] = jnp.zeros_like(carry_scr)
        acc_scr[...] = jnp.zeros_like(acc_scr)

    def attend(kf, vf, c16, valid, transposed):
        if transposed:
            s = _dot(qbd_scr[...], kf.astype(BF16))
        else:
            s = _dot_nt(qbd_scr[...], kf.astype(BF16))
        nk = s.shape[1]
        cexp = jnp.broadcast_to(c16[:, None, :], (N_HEADS_A, nq, nk)).reshape(rows, nk)
        s = s - (cexp + carry_scr[...])
        if valid is not None:
            s = jnp.where(valid, s, NEG_INF)
        m_old = m_scr[...]
        m_new = jnp.maximum(m_old, jnp.max(s, axis=-1, keepdims=True))
        alpha = jnp.exp(m_old - m_new)
        p = jnp.exp(s - m_new)
        l_scr[...] = alpha * l_scr[...] + jnp.sum(p, axis=-1, keepdims=True)
        m_scr[...] = m_new
        if transposed:
            pv = _dot_nt(p.astype(BF16), vf.astype(BF16))
        else:
            pv = _dot(p.astype(BF16), vf.astype(BF16))
        acc_scr[...] = acc_scr[...] * alpha + pv
        carry_scr[...] = carry_scr[...] + cexp[:, nk - 1:nk]

    for g in range(pg):
        attend(k_refs[g][0], v_refs[g][0], c_refs[g][0], None, True)

    @pl.when(j == pl.num_programs(1) - 1)
    def _():
        zpad = jnp.zeros((PAGE_SIZE - nq, width), F32)
        kpad = jnp.concatenate([kn_ref[0], zpad], axis=0)
        vpad = jnp.concatenate([vn_ref[0], zpad], axis=0)
        row = lax.broadcasted_iota(jnp.int32, (rows, PAGE_SIZE), 0)
        key = lax.broadcasted_iota(jnp.int32, (rows, PAGE_SIZE), 1)
        attend(kpad, vpad, cn_ref[0], key <= (row % nq), False)
        o = acc_scr[...] / l_scr[...]
        for h in range(N_HEADS_A):
            o_ref[0, :, h * HEAD_DIM_A:(h + 1) * HEAD_DIM_A] = (
                o[h * nq:(h + 1) * nq, h * HEAD_DIM_A:(h + 1) * HEAD_DIM_A])


def _fox_sample(q, k_new, v_new, c_new, cache_k, cache_v, c_pages, page_table, pages_per_step=8):
    b, nq, w = q.shape
    n_pages = page_table.shape[1]
    pg = pages_per_step
    assert n_pages % pg == 0
    rows = N_HEADS_A * nq

    def page_map(g):
        return lambda bi, j, pt: (pt[bi * n_pages + j * pg + g], 0, 0)

    seq_map = lambda bi, j, pt: (bi, 0, 0)
    in_specs = [pl.BlockSpec((1, nq, w), seq_map), pl.BlockSpec((1, nq, w), seq_map),
                pl.BlockSpec((1, nq, w), seq_map), pl.BlockSpec((1, N_HEADS_A, PAGE_SIZE), seq_map)]
    in_specs += [pl.BlockSpec((1, w, PAGE_SIZE), page_map(g)) for g in range(pg)]
    in_specs += [pl.BlockSpec((1, w, PAGE_SIZE), page_map(g)) for g in range(pg)]
    in_specs += [pl.BlockSpec((1, N_HEADS_A, PAGE_SIZE), page_map(g)) for g in range(pg)]
    grid_spec = pltpu.PrefetchScalarGridSpec(
        num_scalar_prefetch=1,
        grid=(b, n_pages // pg),
        in_specs=in_specs,
        out_specs=pl.BlockSpec((1, nq, w), seq_map),
        scratch_shapes=[pltpu.VMEM((rows, w), BF16), pltpu.VMEM((rows, 1), F32), pltpu.VMEM((rows, 1), F32),
                        pltpu.VMEM((rows, 1), F32), pltpu.VMEM((rows, w), F32)],
    )
    args = [q, k_new, v_new, c_new] + [cache_k] * pg + [cache_v] * pg + [c_pages] * pg
    return pl.pallas_call(
        functools.partial(_fox_sample_body, pages_per_step=pg),
        grid_spec=grid_spec,
        out_shape=jax.ShapeDtypeStruct((b, nq, w), F32),
        compiler_params=_cparams(("arbitrary", "arbitrary")),
        name="fox_sample",
    )(page_table.reshape(-1), *args)


def _causal_conv_silu(ext_scr, cw_ref, cb_ref, n):
    acc = cb_ref[...] + cw_ref[0:1, :] * ext_scr[8 - (CONV_K - 1):8 - (CONV_K - 1) + n, :]
    for i in range(1, CONV_K):
        acc = acc + cw_ref[i:i + 1, :] * ext_scr[8 - (CONV_K - 1) + i:8 - (CONV_K - 1) + i + n, :]
    return _silu(acc)


def _pair_select(first, a, b):
    return jnp.where(first, a, b)


def _ssd_prompt_body(xbc_ref, z_ref, sm_ref, cw_ref, cb_ref, alog_ref, dskip_ref, nw_ref,
                     y_ref, st_ref, st_scr, ext_scr):
    c = pl.program_id(1)
    q = SSD_CHUNK
    pair_w = 2 * HEAD_DIM_S

    @pl.when(c == 0)
    def _():
        st_scr[...] = jnp.zeros_like(st_scr)
        ext_scr[0:8, :] = jnp.zeros((8, CONV_DIM), F32)

    cur = xbc_ref[0]
    ext_scr[8:8 + q, :] = cur
    act = _causal_conv_silu(ext_scr, cw_ref, cb_ref, q)
    ext_scr[0:8, :] = cur[q - 8:q, :]

    sm = sm_ref[0]
    a_lane = -jnp.exp(alog_ref[...])
    tri_mask = _tri_lower(q)
    a_all = _dot_sel_left(tri_mask.astype(BF16), sm * a_lane)
    a_t = a_all.T
    dt_t = sm.T
    w_t = jnp.exp(a_t[:, q - 1:q] - a_t) * dt_t
    ea_all = jnp.exp(a_all)
    lane = lax.broadcasted_iota(jnp.int32, (1, pair_w), 1)
    first = lane < HEAD_DIM_S

    for g in range(N_GROUPS_S):
        bc = act[:, D_INNER + g * D_STATE:D_INNER + (g + 1) * D_STATE]
        cc = act[:, D_INNER + (N_GROUPS_S + g) * D_STATE:D_INNER + (N_GROUPS_S + g + 1) * D_STATE]
        cb = _dot_nt(cc.astype(BF16), bc.astype(BF16))
        bc_t = bc.T
        gated = []
        ssq = jnp.zeros((q, 1), F32)
        for pp in range(HEADS_PER_GROUP // 2):
            h0 = g * HEADS_PER_GROUP + 2 * pp
            col = h0 * HEAD_DIM_S
            x_pair = act[:, col:col + pair_w]
            s_pair = st_scr[:, col:col + pair_w]
            rhs = jnp.concatenate([x_pair.astype(BF16), s_pair.astype(BF16)], axis=0)
            ys, us, el = [], [], []
            for h in (h0, h0 + 1):
                li = DT_LANE0 + h
                seg = a_all[:, li:li + 1] - a_t[li:li + 1, :]
                dec = jnp.exp(jnp.where(tri_mask, seg, NEG_INF))
                m_h = cb * dec * dt_t[li:li + 1, :]
                lhs = jnp.concatenate([m_h, cc * ea_all[:, li:li + 1]], axis=1).astype(BF16)
                ys.append(_dot(lhs, rhs))
                us.append(_dot((bc_t * w_t[li:li + 1, :]).astype(BF16), x_pair.astype(BF16)))
                el.append(ea_all[q - 1:q, li:li + 1])
            y_pair = _pair_select(first, ys[0], ys[1]) + x_pair * dskip_ref[:, col:col + pair_w]
            st_scr[:, col:col + pair_w] = (s_pair * _pair_select(first, el[0], el[1])
                                           + _pair_select(first, us[0], us[1]))
            gp = y_pair * _silu(z_ref[0, :, col:col + pair_w])
            ssq = ssq + jnp.sum(gp * gp, axis=-1, keepdims=True)
            gated.append((col, gp))
        rs = lax.rsqrt(ssq / GROUP_W + EPS)
        for col, gp in gated:
            y_ref[0, :, col:col + pair_w] = (gp * rs * nw_ref[:, col:col + pair_w]).astype(y_ref.dtype)

    @pl.when(c == pl.num_programs(1) - 1)
    def _():
        for blk in range(D_INNER // pair_w):
            st_ref[0, blk * pair_w:(blk + 1) * pair_w, :] = st_scr[:, blk * pair_w:(blk + 1) * pair_w].T


def _ssd_prompt(xbc, z, small, conv_w, conv_b, alog_lane, dskip_exp, norm_w):
    b, l, _ = xbc.shape
    q = SSD_CHUNK
    full = lambda shape: pl.BlockSpec(shape, lambda bi, c: (0,) * len(shape))
    return pl.pallas_call(
        _ssd_prompt_body,
        grid=(b, l // q),
        in_specs=[pl.BlockSpec((1, q, CONV_DIM), lambda bi, c: (bi, c, 0)),
                  pl.BlockSpec((1, q, D_INNER), lambda bi, c: (bi, c, 0)),
                  pl.BlockSpec((1, q, LANES), lambda bi, c: (bi, c, 0)),
                  full((CONV_K, CONV_DIM)), full((1, CONV_DIM)), full((1, LANES)),
                  full((1, D_INNER)), full((1, D_INNER))],
        out_specs=[pl.BlockSpec((1, q, D_INNER), lambda bi, c: (bi, c, 0)),
                   pl.BlockSpec((1, D_INNER, D_STATE), lambda bi, c: (bi, 0, 0))],
        out_shape=[jax.ShapeDtypeStruct((b, l, D_INNER), BF16),
                   jax.ShapeDtypeStruct((b, D_INNER, D_STATE), F32)],
        scratch_shapes=[pltpu.VMEM((D_STATE, D_INNER), F32), pltpu.VMEM((8 + q, CONV_DIM), F32)],
        compiler_params=_cparams(("arbitrary", "arbitrary")),
        name="ssd_prompt",
    )(xbc, z, small, conv_w, conv_b, alog_lane, dskip_exp, norm_w)


def _ssd_sample_body(xbc_ref, c0_ref, z_ref, sm_ref, st_ref, cw_ref, cb_ref, alog_ref, dskip_ref, nw_ref,
                     y_ref, so_ref, ext_scr):
    l = xbc_ref.shape[1]
    pair_w = 2 * HEAD_DIM_S
    ext_scr[0:8, :] = jnp.zeros((8, CONV_DIM), F32)
    ext_scr[8 - (CONV_K - 1):8, :] = c0_ref[0]
    ext_scr[8:8 + l, :] = xbc_ref[0]
    act = _causal_conv_silu(ext_scr, cw_ref, cb_ref, l)

    sm = sm_ref[0]
    dta = sm * (-jnp.exp(alog_ref[...]))
    trow = lax.broadcasted_iota(jnp.int32, (l, LANES), 0)
    a_c = jnp.zeros((l, LANES), F32)
    for s in range(l):
        a_c = a_c + jnp.where(trow >= s, dta[s:s + 1, :], 0.0)
    lane = lax.broadcasted_iota(jnp.int32, (1, pair_w), 1)
    first = lane < HEAD_DIM_S

    def expand(v):
        cols = []
        for pp in range(N_HEADS_S // 2):
            li = DT_LANE0 + 2 * pp
            cols.append(_pair_select(first, v[:, li:li + 1], v[:, li + 1:li + 2]))
        return jnp.concatenate(cols, axis=1)

    a_x = expand(a_c)
    dt_x = expand(sm)
    ea_x = jnp.exp(a_x)
    w_x = jnp.exp(a_x[l - 1:l, :] - a_x) * dt_x
    xs = act[:, 0:D_INNER]
    zpad = jnp.zeros((PAGE_SIZE - l, D_STATE), F32)
    trow_x = lax.broadcasted_iota(jnp.int32, (l, GROUP_W), 0)

    xw_pad = jnp.concatenate([xs * w_x, jnp.zeros((PAGE_SIZE - l, D_INNER), F32)], axis=0)
    ea_last = jnp.exp(a_c[l - 1:l, :])
    rowsel = lax.broadcasted_iota(jnp.int32, (pair_w, 1), 0) < HEAD_DIM_S

    for g in range(N_GROUPS_S):
        gc = g * GROUP_W
        bc = act[:, D_INNER + g * D_STATE:D_INNER + (g + 1) * D_STATE]
        cc = act[:, D_INNER + (N_GROUPS_S + g) * D_STATE:D_INNER + (N_GROUPS_S + g + 1) * D_STATE]
        b_pad = jnp.concatenate([bc, zpad], axis=0).astype(BF16)
        cb = _dot_nt(cc.astype(BF16), b_pad)
        s_g = st_ref[0, gc:gc + GROUP_W, :]
        y = _dot_nt(cc.astype(BF16), s_g.astype(BF16)) * ea_x[:, gc:gc + GROUP_W]
        x_g = xs[:, gc:gc + GROUP_W]
        a_g = a_x[:, gc:gc + GROUP_W]
        dt_g = dt_x[:, gc:gc + GROUP_W]
        for s in range(l):
            dec = jnp.exp(jnp.where(trow_x >= s, a_g - a_g[s:s + 1, :], NEG_INF))
            y = y + cb[:, s:s + 1] * dec * (dt_g[s:s + 1, :] * x_g[s:s + 1, :])
        y = y + x_g * dskip_ref[:, gc:gc + GROUP_W]
        gp = y * _silu(z_ref[0, :, gc:gc + GROUP_W])
        rs = lax.rsqrt(jnp.sum(gp * gp, axis=-1, keepdims=True) / GROUP_W + EPS)
        y_ref[0, :, gc:gc + GROUP_W] = gp * rs * nw_ref[:, gc:gc + GROUP_W]
        for pp in range(HEADS_PER_GROUP // 2):
            h0 = g * HEADS_PER_GROUP + 2 * pp
            col = h0 * HEAD_DIM_S
            li = DT_LANE0 + h0
            u = _dot(xw_pad[:, col:col + pair_w].T.astype(BF16), b_pad)
            e_col = jnp.where(rowsel, ea_last[:, li:li + 1], ea_last[:, li + 1:li + 2])
            so_ref[0, col:col + pair_w, :] = st_ref[0, col:col + pair_w, :] * e_col + u


def _ssd_sample(xbc, conv0, z, small, state, conv_w, conv_b, alog_lane, dskip_exp, norm_w):
    b, l, _ = xbc.shape
    full = lambda shape: pl.BlockSpec(shape, lambda bi: (0,) * len(shape))
    seq = lambda shape: pl.BlockSpec(shape, lambda bi: (bi, 0, 0))
    return pl.pallas_call(
        _ssd_sample_body,
        grid=(b,),
        in_specs=[seq((1, l, CONV_DIM)), seq((1, CONV_K - 1, CONV_DIM)), seq((1, l, D_INNER)),
                  seq((1, l, LANES)), seq((1, D_INNER, D_STATE)),
                  full((CONV_K, CONV_DIM)), full((1, CONV_DIM)), full((1, LANES)),
                  full((1, D_INNER)), full((1, D_INNER))],
        out_specs=[seq((1, l, D_INNER)), seq((1, D_INNER, D_STATE))],
        out_shape=[jax.ShapeDtypeStruct((b, l, D_INNER), F32),
                   jax.ShapeDtypeStruct((b, D_INNER, D_STATE), F32)],
        scratch_shapes=[pltpu.VMEM((8 + l, CONV_DIM), F32)],
        compiler_params=_cparams(("arbitrary",)),
        name="ssd_sample",
    )(xbc, conv0, z, small, state, conv_w, conv_b, alog_lane, dskip_exp, norm_w)


def _post_mixer_body(x_ref, oa_ref, ys_ref, gate_ref, g1_ref, sh2_ref, sc2_ref, wa_ref, ws_ref, wo_ref,
                     n2_ref, wr_ref, br_ref, x2_ref, h_ref, te_ref, tw_ref):
    bb, bl, d = x_ref.shape
    rows = bb * bl
    oa = oa_ref[...].reshape(rows, d).astype(BF16)
    ys = ys_ref[...].reshape(rows, D_INNER).astype(BF16)
    y_attn = _dot(oa, wa_ref[...])
    y_ssm = _dot(ys, ws_ref[...])
    gate = gate_ref[...]
    merged = _sigmoid(gate[:, 0:d]) * y_attn + _sigmoid(gate[:, d:2 * d]) * y_ssm
    mix = _dot(merged.astype(BF16), wo_ref[...])
    x2 = x_ref[...] + g1_ref[...] * mix.reshape(bb, bl, d)
    x2_ref[...] = x2
    ms = jnp.mean(x2 * x2, axis=-1, keepdims=True)
    hf = (x2 * lax.rsqrt(ms + EPS) * n2_ref[...]) * (1.0 + sc2_ref[...]) + sh2_ref[...]
    hf = hf.reshape(rows, d)
    h_ref[...] = hf.astype(BF16)
    lane = lax.broadcasted_iota(jnp.int32, (rows, LANES), 1)
    logits = jnp.where(lane < N_EXPERTS, _dot3(hf, wr_ref[...]) + br_ref[...], NEG_INF)
    vals, idxs = [], []
    for _ in range(TOP_K):
        mx = jnp.max(logits, axis=-1, keepdims=True)
        idx = jnp.min(jnp.where(logits == mx, lane, LANES), axis=-1, keepdims=True)
        vals.append(mx)
        idxs.append(idx)
        logits = jnp.where(lane == idx, NEG_INF, logits)
    ex = [jnp.exp(v - vals[0]) for v in vals]
    den = ex[0] + ex[1] + ex[2] + ex[3]
    te = jnp.zeros((rows, LANES), jnp.int32)
    tw = jnp.zeros((rows, LANES), F32)
    for k in range(TOP_K):
        te = jnp.where(lane == k, idxs[k], te)
        tw = jnp.where(lane == k, ex[k] / den, tw)
    te_ref[...] = te
    tw_ref[...] = tw


def _post_mixer(x, o_attn, y_ssm, gate, mod, w_attn_out, w_ssm_out, w_o, norm2_w, w_router, b_router,
                rows=256):
    b, l, d = x.shape
    bb, bl, rows = _row_blocks(b, l, rows)
    nl = l // bl
    tok = lambda w: pl.BlockSpec((bb, bl, w), lambda i, j: (i, j, 0))
    flat = lambda w: pl.BlockSpec((rows, w), lambda i, j: (i * nl + j, 0))
    modc = lambda c: pl.BlockSpec((bb, 1, d), lambda i, j: (i, 0, c))
    full = lambda shape: pl.BlockSpec(shape, lambda i, j: (0,) * len(shape))
    return pl.pallas_call(
        _post_mixer_body,
        grid=(b // bb, nl),
        in_specs=[tok(d), tok(d), tok(D_INNER), flat(2 * d), modc(2), modc(3), modc(4),
                  full((d, d)), full((D_INNER, d)), full((d, d)), full((1, d)),
                  full((d, LANES)), full((1, LANES))],
        out_specs=[tok(d), flat(d), flat(LANES), flat(LANES)],
        out_shape=[jax.ShapeDtypeStruct((b, l, d), F32), jax.ShapeDtypeStruct((b * l, d), BF16),
                   jax.ShapeDtypeStruct((b * l, LANES), jnp.int32), jax.ShapeDtypeStruct((b * l, LANES), F32)],
        compiler_params=_cparams(("arbitrary", "arbitrary")),
        name="post_mixer",
    )(x, o_attn, y_ssm, gate, mod, mod, mod, w_attn_out, w_ssm_out, w_o, norm2_w.reshape(1, d),
      w_router, b_router)


MOE_ROWS = 256


def _experts_body(be_ref, x_ref, wg_ref, wu_ref, bg_ref, bu_ref, wd_ref, bd_ref, o_ref):
    x = x_ref[...]
    gate = jnp.minimum(_dot(x, wg_ref[0]) + bg_ref[0], SWIGLU_LIMIT)
    up = jnp.clip(_dot(x, wu_ref[0]) + bu_ref[0], -SWIGLU_LIMIT, SWIGLU_LIMIT)
    act = (up + 1.0) * (gate * _sigmoid(SWIGLU_ALPHA * gate))
    o_ref[...] = _dot(act.astype(BF16), wd_ref[0]) + bd_ref[0]


def _experts(block_e, xg, w_g, w_u, b_g, b_u, w_d, b_d):
    p, d = xg.shape
    ff = w_g.shape[2]
    wspec = lambda shape: pl.BlockSpec(shape, lambda i, be: (be[i], 0, 0))
    grid_spec = pltpu.PrefetchScalarGridSpec(
        num_scalar_prefetch=1,
        grid=(p // MOE_ROWS,),
        in_specs=[pl.BlockSpec((MOE_ROWS, d), lambda i, be: (i, 0)),
                  wspec((1, d, ff)), wspec((1, d, ff)), wspec((1, 1, ff)), wspec((1, 1, ff)),
                  wspec((1, ff, d)), wspec((1, 1, d))],
        out_specs=pl.BlockSpec((MOE_ROWS, d), lambda i, be: (i, 0)),
    )
    return pl.pallas_call(
        _experts_body,
        grid_spec=grid_spec,
        out_shape=jax.ShapeDtypeStruct((p, d), F32),
        compiler_params=_cparams(("arbitrary",)),
        name="moe_experts",
    )(block_e, xg, w_g, w_u, b_g, b_u, w_d, b_d)


def _route(top_e):
    t, k = top_e.shape
    s = t * k
    e_flat = top_e.reshape(-1)
    onehot = (e_flat[:, None] == jnp.arange(N_EXPERTS, dtype=jnp.int32)[None, :]).astype(jnp.int32)
    csum = jnp.cumsum(onehot, axis=0)
    rank = jnp.sum(onehot * csum, axis=1) - 1
    counts = csum[-1]
    padded = (counts + MOE_ROWS - 1) // MOE_ROWS * MOE_ROWS
    pend = jnp.cumsum(padded)
    dest = (pend - padded)[e_flat] + rank
    n_blocks = -(-s // MOE_ROWS) + N_EXPERTS
    row_tok = jnp.zeros((n_blocks * MOE_ROWS,), jnp.int32).at[dest].set(
        jnp.arange(s, dtype=jnp.int32) // k)
    block_e = jnp.minimum(
        jnp.searchsorted(pend, jnp.arange(n_blocks, dtype=jnp.int32) * MOE_ROWS, side="right"),
        N_EXPERTS - 1).astype(jnp.int32)
    return dest.reshape(t, k), row_tok, block_e


def _final_body(x_ref, moe_ref, g2_ref, nf_ref, y_ref):
    bb, bl, d = x_ref.shape
    x = x_ref[...] + g2_ref[...] * moe_ref[...].reshape(bb, bl, d)
    ms = jnp.mean(x * x, axis=-1, keepdims=True)
    y_ref[...] = x * lax.rsqrt(ms + EPS) * nf_ref[...]


def _final(x2, moe_out, row0, mod, norm_f_w, rows=512):
    b, l, d = x2.shape
    bb, bl, rows = _row_blocks(b, l, rows)
    nl = l // bl
    assert row0 % rows == 0
    blk0 = row0 // rows
    return pl.pallas_call(
        _final_body,
        grid=(b // bb, nl),
        in_specs=[pl.BlockSpec((bb, bl, d), lambda i, j: (i, j, 0)),
                  pl.BlockSpec((rows, d), lambda i, j: (blk0 + i * nl + j, 0)),
                  pl.BlockSpec((bb, 1, d), lambda i, j: (i, 0, 5)),
                  pl.BlockSpec((1, d), lambda i, j: (0, 0))],
        out_specs=pl.BlockSpec((bb, bl, d), lambda i, j: (i, j, 0)),
        out_shape=jax.ShapeDtypeStruct((b, l, d), F32),
        compiler_params=_cparams(("arbitrary", "arbitrary")),
        name="final_norm",
    )(x2, moe_out, mod, norm_f_w.reshape(1, d))


def _prep_in_weights(w_in, b_fgate, dt_bias):
    aw = N_HEADS_A * HEAD_DIM_A
    o = np.cumsum([0, aw, aw, aw, N_HEADS_A, D_INNER, CONV_DIM, N_HEADS_S, D_MODEL, D_MODEL])
    w_t = w_in.T
    rows = lambda i: w_t[o[i]:o[i + 1]]
    w_small = jnp.concatenate(
        [rows(3), rows(6), jnp.zeros((LANES - N_HEADS_A - N_HEADS_S, D_MODEL), F32)], axis=0).T
    b_small = jnp.concatenate(
        [b_fgate, dt_bias, jnp.zeros((LANES - N_HEADS_A - N_HEADS_S,), F32)]).reshape(1, LANES)
    big = {"q": rows(0), "k": rows(1), "v": rows(2), "z": rows(4), "xbc": rows(5),
           "gate": w_t[o[7]:o[9]]}
    return {n: w.astype(BF16) for n, w in big.items()}, w_small, b_small


def kernel(x_prompt, x_sample, cache_k, cache_v, cache_logf, state_ssm, state_conv, page_table, c_prompt, c_sample, w_cond, b_cond, norm1_w, w_in, b_fgate, conv_w, conv_b, dt_bias, a_log, d_skip, ssm_norm_w, w_attn_out, w_ssm_out, w_o, norm2_w, w_router, b_router, w_gate_up, b_gate_up, w_down, b_down, norm_f_w):
    assert w_in.shape[0] == 1, "single-layer trunk"
    bp, lp, d = x_prompt.shape
    bs, ls, _ = x_sample.shape
    tp, ts = bp * lp, bs * ls
    aw = N_HEADS_A * HEAD_DIM_A
    c_all = jnp.concatenate([c_prompt, c_sample, jnp.zeros((-(bp + bs) % 8, d), F32)], axis=0)
    mod = _adaln_mod(c_all, w_cond[0], b_cond[0])
    mod_p = mod[:bp].reshape(bp, 1, 6 * d)
    mod_s = mod[bp:bp + bs].reshape(bs, 1, 6 * d)

    w_big, w_small, b_small = _prep_in_weights(w_in[0], b_fgate[0], dt_bias[0])
    pad_lanes = LANES - N_HEADS_A - N_HEADS_S
    alog_lane = jnp.concatenate([jnp.zeros((N_HEADS_A,), F32), a_log[0], jnp.zeros((pad_lanes,), F32)]).reshape(1, LANES)
    dskip_exp = jnp.repeat(d_skip[0], HEAD_DIM_S).reshape(1, D_INNER)
    ssm_nw = ssm_norm_w[0].reshape(1, D_INNER)
    conv_b2 = conv_b[0].reshape(1, CONV_DIM)
    wa, ws, wo = w_attn_out[0].astype(BF16), w_ssm_out[0].astype(BF16), w_o[0].astype(BF16)
    w_router_pad = jnp.pad(w_router[0], ((0, 0), (0, LANES - N_EXPERTS)))
    b_router_pad = jnp.pad(b_router[0], (0, LANES - N_EXPERTS)).reshape(1, LANES)

    def in_proj(x, m):
        h, small = _norm_in(x, m, norm1_w[0], w_small, b_small)
        return {n: _matmul_nt(h, w) for n, w in w_big.items()}, small

    proj_p, small_p = in_proj(x_prompt, mod_p)
    proj_s, small_s = in_proj(x_sample, mod_s)

    fcum = _cumsum_t(small_p.reshape(tp // PAGE_SIZE, PAGE_SIZE, LANES), lp // PAGE_SIZE, True)
    fcum = fcum.reshape(bp, N_HEADS_A // 2, 2, lp)
    o_p = _fox_prompt(proj_p["q"].reshape(bp, lp, aw), proj_p["k"].reshape(bp, lp, aw),
                      proj_p["v"].reshape(bp, lp, aw), fcum, tile=min(512, lp))
    xbc_p = proj_p["xbc"].reshape(bp, lp, CONV_DIM)
    y_p, st_p = _ssd_prompt(xbc_p, proj_p["z"].reshape(bp, lp, D_INNER), small_p.reshape(bp, lp, LANES),
                            conv_w[0], conv_b2, alog_lane, dskip_exp, ssm_nw)

    n_phys = cache_k.shape[1]
    k_t = jnp.transpose(cache_k[0], (0, 2, 3, 1)).reshape(n_phys, aw, PAGE_SIZE)
    v_t = jnp.transpose(cache_v[0], (0, 2, 3, 1)).reshape(n_phys, aw, PAGE_SIZE)
    lf_t = jnp.swapaxes(cache_logf[0], 1, 2).reshape(n_phys * N_HEADS_A, PAGE_SIZE)
    c_pages = _cumsum_lanes(lf_t, rows=512).reshape(n_phys, N_HEADS_A, PAGE_SIZE)
    lf_new = jnp.pad(small_s.reshape(bs, ls, LANES), ((0, 0), (0, PAGE_SIZE - ls), (0, 0)))
    c_new = _cumsum_t(lf_new, 1, False)
    n_pages = page_table.shape[1]
    o_s = _fox_sample(proj_s["q"].reshape(bs, ls, aw), proj_s["k"].reshape(bs, ls, aw),
                      proj_s["v"].reshape(bs, ls, aw), c_new, k_t, v_t, c_pages, page_table,
                      pages_per_step=min(8, n_pages))
    xbc_s = proj_s["xbc"].reshape(bs, ls, CONV_DIM)
    y_s, st_s = _ssd_sample(xbc_s, state_conv[0], proj_s["z"].reshape(bs, ls, D_INNER),
                            small_s.reshape(bs, ls, LANES), state_ssm[0].reshape(bs, D_INNER, D_STATE),
                            conv_w[0], conv_b2, alog_lane, dskip_exp, ssm_nw)

    post = lambda x, o, y, gate, m: _post_mixer(x, o, y, gate, m, wa, ws, wo, norm2_w[0],
                                                w_router_pad, b_router_pad)
    x2_p, hf_p, te_p, tw_p = post(x_prompt, o_p, y_p, proj_p["gate"], mod_p)
    x2_s, hf_s, te_s, tw_s = post(x_sample, o_s, y_s, proj_s["gate"], mod_s)

    hf = jnp.concatenate([hf_p, hf_s], axis=0)
    top_e = jnp.concatenate([te_p[:, :TOP_K], te_s[:, :TOP_K]], axis=0)
    top_w = jnp.concatenate([tw_p[:, :TOP_K], tw_s[:, :TOP_K]], axis=0)
    dest, row_tok, block_e = _route(top_e)
    w_gu = w_gate_up[0]
    rows = _experts(block_e, hf[row_tok],
                    w_gu[:, :, 0::2].astype(BF16), w_gu[:, :, 1::2].astype(BF16),
                    b_gate_up[0][:, None, 0::2], b_gate_up[0][:, None, 1::2],
                    w_down[0].astype(BF16), b_down[0][:, None, :])
    moe_out = jnp.sum(rows[dest] * top_w[:, :, None], axis=1)

    y_prompt = _final(x2_p, moe_out, 0, mod_p, norm_f_w)
    y_sample = _final(x2_s, moe_out, tp, mod_s, norm_f_w)

    def state_rows(k, v, small, st, xbc, conv0, b, l):
        conv_new = jnp.concatenate([conv0, xbc], axis=1)[:, -(CONV_K - 1):]
        return (k.reshape(1, b, l, N_HEADS_A, HEAD_DIM_A), v.reshape(1, b, l, N_HEADS_A, HEAD_DIM_A),
                small[:, :N_HEADS_A].reshape(1, b, l, N_HEADS_A),
                st.reshape(1, b, N_HEADS_S, HEAD_DIM_S, D_STATE), conv_new[None])

    sp = state_rows(proj_p["k"], proj_p["v"], small_p, st_p, xbc_p,
                    jnp.zeros((bp, CONV_K - 1, CONV_DIM), F32), bp, lp)
    ss = state_rows(proj_s["k"], proj_s["v"], small_s, st_s, xbc_s, state_conv[0], bs, ls)
    return (y_prompt, y_sample) + sp + ss
```

```python
import functools

import numpy as np
import jax
import jax.numpy as jnp
from jax import lax
from jax.experimental import pallas as pl
from jax.experimental.pallas import tpu as pltpu

F32 = jnp.float32
BF16 = jnp.bfloat16

D_MODEL = 1024
N_HEADS_A = 16
HEAD_DIM_A = 64
PAGE_SIZE = 128
D_INNER = 2048
HEAD_DIM_S = 64
N_HEADS_S = 32
N_GROUPS_S = 4
HEADS_PER_GROUP = N_HEADS_S // N_GROUPS_S
GROUP_W = D_INNER // N_GROUPS_S
D_STATE = 128
CONV_K = 4
CONV_DIM = D_INNER + 2 * N_GROUPS_S * D_STATE
SSD_CHUNK = 128
N_EXPERTS = 32
TOP_K = 4
D_FF = D_MODEL
SWIGLU_LIMIT = 7.0
SWIGLU_ALPHA = 1.702
EPS = 1e-6

LANES = 128
DT_LANE0 = N_HEADS_A
NEG_INF = float("-inf")
LOG2E = 1.4426950408889634
VMEM_LIMIT = 56 * 1024 * 1024


def _cparams(sem):
    return pltpu.CompilerParams(dimension_semantics=sem, vmem_limit_bytes=VMEM_LIMIT)


def _split3(x):
    hi = x.astype(BF16)
    r = x - hi.astype(F32)
    mid = r.astype(BF16)
    lo = (r - mid.astype(F32)).astype(BF16)
    return hi, mid, lo


def _dot(a, b):
    return jnp.dot(a, b, preferred_element_type=F32)


def _dot_nt(a, b):
    return lax.dot_general(a, b, (((1,), (1,)), ((), ())), preferred_element_type=F32)


def _dot_sel_left(sel_bf16, x):
    hi, mid, lo = _split3(x)
    return _dot(sel_bf16, hi) + _dot(sel_bf16, mid) + _dot(sel_bf16, lo)


def _dot3(a, b):
    ah = a.astype(BF16)
    am = (a - ah.astype(F32)).astype(BF16)
    bh = b.astype(BF16)
    bm = (b - bh.astype(F32)).astype(BF16)
    return _dot(ah, bh) + _dot(ah, bm) + _dot(am, bh)


def _tri_lower(n):
    r = lax.broadcasted_iota(jnp.int32, (n, n), 0)
    c = lax.broadcasted_iota(jnp.int32, (n, n), 1)
    return r >= c


def _sigmoid(x):
    return 1.0 / (1.0 + jnp.exp(-x))


def _silu(x):
    return x * _sigmoid(x)


def _mod_body(c_ref, w_ref, b_ref, o_ref):
    c = c_ref[...]
    o_ref[...] = _dot3(_silu(c), w_ref[...]) + b_ref[...]


def _adaln_mod(c_all, w_cond, b_cond):
    rows, d = c_all.shape
    n = w_cond.shape[1]
    tn = 1024
    return pl.pallas_call(
        _mod_body,
        grid=(n // tn,),
        in_specs=[pl.BlockSpec((rows, d), lambda j: (0, 0)),
                  pl.BlockSpec((d, tn), lambda j: (0, j)),
                  pl.BlockSpec((1, tn), lambda j: (0, j))],
        out_specs=pl.BlockSpec((rows, tn), lambda j: (0, j)),
        out_shape=jax.ShapeDtypeStruct((rows, n), F32),
        compiler_params=_cparams(("arbitrary",)),
        name="adaln_mod",
    )(c_all, w_cond, b_cond.reshape(1, n))


def _norm_in_body(x_ref, sh_ref, sc_ref, nw_ref, ws_ref, bs_ref, h_ref, sm_ref):
    x = x_ref[...]
    bb, bl, d = x.shape
    ms = jnp.mean(x * x, axis=-1, keepdims=True)
    y = x * lax.rsqrt(ms + EPS) * nw_ref[...]
    h = (y * (1.0 + sc_ref[...]) + sh_ref[...]).reshape(bb * bl, d)
    h_ref[...] = h.astype(BF16)
    sm = _dot3(h, ws_ref[...]) + bs_ref[...]
    lane = lax.broadcasted_iota(jnp.int32, sm.shape, 1)
    t = jnp.log(1.0 + jnp.exp(-jnp.abs(sm)))
    sm_ref[...] = jnp.where(lane < N_HEADS_A, jnp.minimum(sm, 0.0) - t, jnp.maximum(sm, 0.0) + t)


def _row_blocks(b, l, rows):
    rows = min(rows, b * l)
    if l >= rows:
        assert l % rows == 0
        return 1, rows, rows
    assert rows % l == 0 and b % (rows // l) == 0
    return rows // l, l, rows


def _norm_in(x, mod, norm_w, w_small, b_small, rows=512):
    b, l, d = x.shape
    bb, bl, rows = _row_blocks(b, l, rows)
    nl = l // bl
    grid = (b // bb, nl)
    return pl.pallas_call(
        _norm_in_body,
        grid=grid,
        in_specs=[pl.BlockSpec((bb, bl, d), lambda i, j: (i, j, 0)),
                  pl.BlockSpec((bb, 1, d), lambda i, j: (i, 0, 0)),
                  pl.BlockSpec((bb, 1, d), lambda i, j: (i, 0, 1)),
                  pl.BlockSpec((1, d), lambda i, j: (0, 0)),
                  pl.BlockSpec((d, LANES), lambda i, j: (0, 0)),
                  pl.BlockSpec((1, LANES), lambda i, j: (0, 0))],
        out_specs=[pl.BlockSpec((rows, d), lambda i, j: (i * nl + j, 0)),
                   pl.BlockSpec((rows, LANES), lambda i, j: (i * nl + j, 0))],
        out_shape=[jax.ShapeDtypeStruct((b * l, d), BF16),
                   jax.ShapeDtypeStruct((b * l, LANES), F32)],
        compiler_params=_cparams(("arbitrary", "arbitrary")),
        name="norm_in",
    )(x, mod, mod, norm_w.reshape(1, d), w_small, b_small)


def _mm_body(a_ref, w_ref, o_ref):
    o_ref[...] = _dot_nt(a_ref[...], w_ref[...])


def _matmul_nt(a, w_t, tm=512, tn=1024):
    m, k = a.shape
    n = w_t.shape[0]
    tn = min(tn, n)
    return pl.pallas_call(
        _mm_body,
        grid=(n // tn, m // tm),
        in_specs=[pl.BlockSpec((tm, k), lambda j, i: (i, 0)),
                  pl.BlockSpec((tn, k), lambda j, i: (j, 0))],
        out_specs=pl.BlockSpec((tm, tn), lambda j, i: (i, j)),
        out_shape=jax.ShapeDtypeStruct((m, n), F32),
        compiler_params=_cparams(("arbitrary", "arbitrary")),
        name="proj_matmul",
    )(a, w_t)


def _cumsum_t_body(x_ref, o_ref, pad_scr, carry_scr, *, width, carry):
    j = pl.program_id(1)

    @pl.when(j == 0)
    def _():
        pad_scr[...] = jnp.zeros_like(pad_scr)
        carry_scr[...] = jnp.zeros_like(carry_scr)

    if width == LANES:
        blk = x_ref[0]
    else:
        pad_scr[:, 0:width] = x_ref[0]
        blk = pad_scr[...]
    tri = _tri_lower(PAGE_SIZE).astype(BF16)
    cs = _dot_sel_left(tri, blk) + carry_scr[...]
    if carry:
        carry_scr[...] = cs[PAGE_SIZE - 1:PAGE_SIZE, :]
    o_ref[0] = cs.T[0:N_HEADS_A, :]


def _cumsum_t(x, nper, carry):
    n, p, width = x.shape
    assert p == PAGE_SIZE and n % nper == 0
    return pl.pallas_call(
        functools.partial(_cumsum_t_body, width=width, carry=carry),
        grid=(n // nper, nper),
        in_specs=[pl.BlockSpec((1, p, width), lambda i, j: (i * nper + j, 0, 0))],
        out_specs=pl.BlockSpec((1, N_HEADS_A, p), lambda i, j: (i, 0, j)),
        out_shape=jax.ShapeDtypeStruct((n // nper, N_HEADS_A, nper * p), F32),
        scratch_shapes=[pltpu.VMEM((p, LANES), F32), pltpu.VMEM((1, LANES), F32)],
        compiler_params=_cparams(("arbitrary", "arbitrary")),
        name="logf_cumsum",
    )(x)


def _cumsum_lanes_body(x_ref, o_ref):
    n = x_ref.shape[1]
    r = lax.broadcasted_iota(jnp.int32, (n, n), 0)
    c = lax.broadcasted_iota(jnp.int32, (n, n), 1)
    tri = (r <= c).astype(BF16)
    hi, mid, lo = _split3(x_ref[...])
    o_ref[...] = _dot(hi, tri) + _dot(mid, tri) + _dot(lo, tri)


def _cumsum_lanes(x, rows=1024):
    r, n = x.shape
    assert r % rows == 0
    return pl.pallas_call(
        _cumsum_lanes_body,
        grid=(r // rows,),
        in_specs=[pl.BlockSpec((rows, n), lambda i: (i, 0))],
        out_specs=pl.BlockSpec((rows, n), lambda i: (i, 0)),
        out_shape=jax.ShapeDtypeStruct((r, n), F32),
        compiler_params=_cparams(("arbitrary",)),
        name="page_logf_cumsum",
    )(x)


def _fox_prompt_body(it_ref, jt_ref, q_ref, k_ref, v_ref, f_ref, o_ref, qm_scr, m_scr, acc_scr, *, tile):
    t = pl.program_id(2)
    i = it_ref[t]
    j = jt_ref[t]
    lane = lax.broadcasted_iota(jnp.int32, (tile, LANES), 1)
    first = lane < HEAD_DIM_A

    @pl.when(j == 0)
    def _():
        q = q_ref[0] * (HEAD_DIM_A ** -0.5 * LOG2E)
        qm_scr[0] = jnp.where(first, q, 0.0).astype(BF16)
        qm_scr[1] = jnp.where(first, 0.0, q).astype(BF16)
        m_scr[...] = jnp.full_like(m_scr, -1e30)
        acc_scr[...] = jnp.zeros_like(acc_scr)

    def step(diagonal):
        kb = k_ref[0]
        vb = v_ref[0]
        one = jnp.ones((), BF16)
        for h in range(2):
            s = _dot_nt(qm_scr[h], kb) - f_ref[0, 0, h:h + 1, :] * LOG2E
            if diagonal:
                row = lax.broadcasted_iota(jnp.int32, (tile, tile), 0)
                col = lax.broadcasted_iota(jnp.int32, (tile, tile), 1)
                s = jnp.where(col <= row, s, NEG_INF)
            m_old = m_scr[h]
            m_new = jnp.maximum(m_old, jnp.max(s, axis=-1, keepdims=True))
            alpha = jnp.exp2(m_old - m_new)
            p = jnp.exp2(s - jnp.concatenate([m_new] * (tile // LANES), axis=1))
            v_aug = jnp.where(first, vb, one) if h == 0 else jnp.where(first, one, vb)
            acc_scr[h] = alpha * acc_scr[h] + _dot(p.astype(BF16), v_aug)
            m_scr[h] = m_new

    @pl.when(j < i)
    def _():
        step(False)

    @pl.when(j == i)
    def _():
        step(True)
        a0 = acc_scr[0]
        a1 = acc_scr[1]
        den = jnp.where(first, pltpu.roll(a0, HEAD_DIM_A, 1), pltpu.roll(a1, HEAD_DIM_A, 1))
        o_ref[0] = jnp.where(first, a0, a1) / den


def _fox_prompt(q, k, v, fcum, tile=512):
    b, l, w = q.shape
    npair = w // LANES
    nt = l // tile
    it = np.concatenate([np.full(i + 1, i, np.int32) for i in range(nt)])
    jt = np.concatenate([np.arange(i + 1, dtype=np.int32) for i in range(nt)])
    grid_spec = pltpu.PrefetchScalarGridSpec(
        num_scalar_prefetch=2,
        grid=(b, npair, len(it)),
        in_specs=[pl.BlockSpec((1, tile, LANES), lambda bi, p, t, it, jt: (bi, it[t], p)),
                  pl.BlockSpec((1, tile, LANES), lambda bi, p, t, it, jt: (bi, jt[t], p)),
                  pl.BlockSpec((1, tile, LANES), lambda bi, p, t, it, jt: (bi, jt[t], p)),
                  pl.BlockSpec((1, 1, 2, tile), lambda bi, p, t, it, jt: (bi, p, 0, jt[t]))],
        out_specs=pl.BlockSpec((1, tile, LANES), lambda bi, p, t, it, jt: (bi, it[t], p)),
        scratch_shapes=[pltpu.VMEM((2, tile, LANES), BF16), pltpu.VMEM((2, tile, LANES), F32),
                        pltpu.VMEM((2, tile, LANES), F32)],
    )
    return pl.pallas_call(
        functools.partial(_fox_prompt_body, tile=tile),
        grid_spec=grid_spec,
        out_shape=jax.ShapeDtypeStruct((b, l, w), F32),
        compiler_params=_cparams(("arbitrary", "arbitrary", "arbitrary")),
        name="fox_prompt",
    )(jnp.asarray(it), jnp.asarray(jt), q, k, v, fcum)


def _fox_sample_body(pt_ref, q_ref, kn_ref, vn_ref, cn_ref, *rest, pages_per_step):
    pg = pages_per_step
    k_refs = rest[0:pg]
    v_refs = rest[pg:2 * pg]
    c_refs = rest[2 * pg:3 * pg]
    o_ref, qbd_scr, m_scr, l_scr, carry_scr, acc_scr = rest[3 * pg:]
    j = pl.program_id(1)
    nq = q_ref.shape[1]
    rows = N_HEADS_A * nq
    width = N_HEADS_A * HEAD_DIM_A

    @pl.when(j == 0)
    def _():
        q = q_ref[0] * (HEAD_DIM_A ** -0.5)
        qt = jnp.broadcast_to(q[None], (N_HEADS_A, nq, width)).reshape(rows, width)
        row = lax.broadcasted_iota(jnp.int32, (rows, width), 0)
        lane = lax.broadcasted_iota(jnp.int32, (rows, width), 1)
        own = (lane // HEAD_DIM_A) == (row // nq)
        qbd_scr[...] = jnp.where(own, qt, 0.0).astype(BF16)
        m_scr[...] = jnp.full_like(m_scr, -1e30)
        l_scr[...] = jnp.zeros_like(l_scr)
        carry_scr[...] = jnp.zeros_like(carry_scr)
        acc_scr[...] = jnp.zeros_like(acc_scr)

    def attend(blocks, valid, transposed):
        qbd = qbd_scr[...]
        carry = carry_scr[...]
        scores = []
        for kf, _, c16 in blocks:
            s = _dot(qbd, kf.astype(BF16)) if transposed else _dot_nt(qbd, kf.astype(BF16))
            nk = s.shape[1]
            cexp = jnp.broadcast_to(c16[:, None, :], (N_HEADS_A, nq, nk)).reshape(rows, nk)
            s = s - (cexp + carry)
            if valid is not None:
                s = jnp.where(valid, s, NEG_INF)
            scores.append(s)
            carry = carry + cexp[:, nk - 1:nk]
        carry_scr[...] = carry
        smax = scores[0]
        for s in scores[1:]:
            smax = jnp.maximum(smax, s)
        m_old = m_scr[...]
        m_new = jnp.maximum(m_old, jnp.max(smax, axis=-1, keepdims=True))
        alpha = jnp.exp(m_old - m_new)
        m_scr[...] = m_new
        psum = None
        pv = None
        for s, (_, vf, _) in zip(scores, blocks):
            p = jnp.exp(s - m_new)
            psum = p if psum is None else psum + p
            pb = p.astype(BF16)
            d = _dot_nt(pb, vf.astype(BF16)) if transposed else _dot(pb, vf.astype(BF16))
            pv = d if pv is None else pv + d
        l_scr[...] = alpha * l_scr[...] + jnp.sum(psum, axis=-1, keepdims=True)
        acc_scr[...] = acc_scr[...] * alpha + pv

    attend([(k_refs[g][0], v_refs[g][0], c_refs[g][0]) for g in range(pg)], None, True)

    @pl.when(j == pl.num_programs(1) - 1)
    def _():
        zpad = jnp.zeros((PAGE_SIZE - nq, width), F32)
        kpad = jnp.concatenate([kn_ref[0], zpad], axis=0)
        vpad = jnp.concatenate([vn_ref[0], zpad], axis=0)
        row = lax.broadcasted_iota(jnp.int32, (rows, PAGE_SIZE), 0)
        key = lax.broadcasted_iota(jnp.int32, (rows, PAGE_SIZE), 1)
        attend([(kpad, vpad, cn_ref[0])], key <= (row % nq), False)
        o = acc_scr[...] / l_scr[...]
        for h in range(N_HEADS_A):
            o_ref[0, :, h * HEAD_DIM_A:(h + 1) * HEAD_DIM_A] = (
                o[h * nq:(h + 1) * nq, h * HEAD_DIM_A:(h + 1) * HEAD_DIM_A])


def _fox_sample(q, k_new, v_new, c_new, cache_k, cache_v, c_pages, page_table, pages_per_step=8):
    b, nq, w = q.shape
    n_pages = page_table.shape[1]
    pg = pages_per_step
    assert n_pages % pg == 0
    rows = N_HEADS_A * nq

    def page_map(g):
        return lambda bi, j, pt: (pt[bi * n_pages + j * pg + g], 0, 0)

    seq_map = lambda bi, j, pt: (bi, 0, 0)
    in_specs = [pl.BlockSpec((1, nq, w), seq_map), pl.BlockSpec((1, nq, w), seq_map),
                pl.BlockSpec((1, nq, w), seq_map), pl.BlockSpec((1, N_HEADS_A, PAGE_SIZE), seq_map)]
    in_specs += [pl.BlockSpec((1, w, PAGE_SIZE), page_map(g)) for g in range(pg)]
    in_specs += [pl.BlockSpec((1, w, PAGE_SIZE), page_map(g)) for g in range(pg)]
    in_specs += [pl.BlockSpec((1, N_HEADS_A, PAGE_SIZE), page_map(g)) for g in range(pg)]
    grid_spec = pltpu.PrefetchScalarGridSpec(
        num_scalar_prefetch=1,
        grid=(b, n_pages // pg),
        in_specs=in_specs,
        out_specs=pl.BlockSpec((1, nq, w), seq_map),
        scratch_shapes=[pltpu.VMEM((rows, w), BF16), pltpu.VMEM((rows, 1), F32), pltpu.VMEM((rows, 1), F32),
                        pltpu.VMEM((rows, 1), F32), pltpu.VMEM((rows, w), F32)],
    )
    args = [q, k_new, v_new, c_new] + [cache_k] * pg + [cache_v] * pg + [c_pages] * pg
    return pl.pallas_call(
        functools.partial(_fox_sample_body, pages_per_step=pg),
        grid_spec=grid_spec,
        out_shape=jax.ShapeDtypeStruct((b, nq, w), F32),
        compiler_params=_cparams(("arbitrary", "arbitrary")),
        name="fox_sample",
    )(page_table.reshape(-1), *args)


def _causal_conv_silu(ext_scr, cw_ref, cb_ref, n):
    acc = cb_ref[...] + cw_ref[0:1, :] * ext_scr[8 - (CONV_K - 1):8 - (CONV_K - 1) + n, :]
    for i in range(1, CONV_K):
        acc = acc + cw_ref[i:i + 1, :] * ext_scr[8 - (CONV_K - 1) + i:8 - (CONV_K - 1) + i + n, :]
    return _silu(acc)


def _pair_select(first, a, b):
    return jnp.where(first, a, b)


def _ssd_prompt_body(xbc_ref, z_ref, sm_ref, cw_ref, cb_ref, alog_ref, dskip_ref, nw_ref,
                     y_ref, st_ref, st_scr, ext_scr):
    c = pl.program_id(1)
    q = SSD_CHUNK
    pair_w = 2 * HEAD_DIM_S

    @pl.when(c == 0)
    def _():
        st_scr[...] = jnp.zeros_like(st_scr)
        ext_scr[0:8, :] = jnp.zeros((8, CONV_DIM), F32)

    cur = xbc_ref[0]
    ext_scr[8:8 + q, :] = cur
    act = _causal_conv_silu(ext_scr, cw_ref, cb_ref, q)
    ext_scr[0:8, :] = cur[q - 8:q, :]

    sm = sm_ref[0]
    a_lane = -jnp.exp(alog_ref[...])
    tri_mask = _tri_lower(q)
    a_all = _dot_sel_left(tri_mask.astype(BF16), sm * a_lane)
    a_t = a_all.T
    dt_t = sm.T
    w_t = jnp.exp(a_t[:, q - 1:q] - a_t) * dt_t
    ea_all = jnp.exp(a_all)
    lane = lax.broadcasted_iota(jnp.int32, (1, pair_w), 1)
    first = lane < HEAD_DIM_S

    for g in range(N_GROUPS_S):
        bc = act[:, D_INNER + g * D_STATE:D_INNER + (g + 1) * D_STATE]
        cc = act[:, D_INNER + (N_GROUPS_S + g) * D_STATE:D_INNER + (N_GROUPS_S + g + 1) * D_STATE]
        cb = _dot_nt(cc.astype(BF16), bc.astype(BF16))
        bc_t = bc.T
        gated = []
        ssq = jnp.zeros((q, 1), F32)
        for pp in range(HEADS_PER_GROUP // 2):
            h0 = g * HEADS_PER_GROUP + 2 * pp
            col = h0 * HEAD_DIM_S
            x_pair = act[:, col:col + pair_w]
            s_pair = st_scr[:, col:col + pair_w]
            rhs = jnp.concatenate([x_pair.astype(BF16), s_pair.astype(BF16)], axis=0)
            ys, us, el = [], [], []
            for h in (h0, h0 + 1):
                li = DT_LANE0 + h
                seg = a_all[:, li:li + 1] - a_t[li:li + 1, :]
                dec = jnp.exp(jnp.where(tri_mask, seg, NEG_INF))
                m_h = cb * dec * dt_t[li:li + 1, :]
                lhs = jnp.concatenate([m_h, cc * ea_all[:, li:li + 1]], axis=1).astype(BF16)
                ys.append(_dot(lhs, rhs))
                us.append(_dot((bc_t * w_t[li:li + 1, :]).astype(BF16), x_pair.astype(BF16)))
                el.append(ea_all[q - 1:q, li:li + 1])
            y_pair = _pair_select(first, ys[0], ys[1]) + x_pair * dskip_ref[:, col:col + pair_w]
            st_scr[:, col:col + pair_w] = (s_pair * _pair_select(first, el[0], el[1])
                                           + _pair_select(first, us[0], us[1]))
            gp = y_pair * _silu(z_ref[0, :, col:col + pair_w])
            ssq = ssq + jnp.sum(gp * gp, axis=-1, keepdims=True)
            gated.append((col, gp))
        rs = lax.rsqrt(ssq / GROUP_W + EPS)
        for col, gp in gated:
            y_ref[0, :, col:col + pair_w] = (gp * rs * nw_ref[:, col:col + pair_w]).astype(y_ref.dtype)

    @pl.when(c == pl.num_programs(1) - 1)
    def _():
        for blk in range(D_INNER // pair_w):
            st_ref[0, blk * pair_w:(blk + 1) * pair_w, :] = st_scr[:, blk * pair_w:(blk + 1) * pair_w].T


def _ssd_prompt(xbc, z, small, conv_w, conv_b, alog_lane, dskip_exp, norm_w):
    b, l, _ = xbc.shape
    q = SSD_CHUNK
    full = lambda shape: pl.BlockSpec(shape, lambda bi, c: (0,) * len(shape))
    return pl.pallas_call(
        _ssd_prompt_body,
        grid=(b, l // q),
        in_specs=[pl.BlockSpec((1, q, CONV_DIM), lambda bi, c: (bi, c, 0)),
                  pl.BlockSpec((1, q, D_INNER), lambda bi, c: (bi, c, 0)),
                  pl.BlockSpec((1, q, LANES), lambda bi, c: (bi, c, 0)),
                  full((CONV_K, CONV_DIM)), full((1, CONV_DIM)), full((1, LANES)),
                  full((1, D_INNER)), full((1, D_INNER))],
        out_specs=[pl.BlockSpec((1, q, D_INNER), lambda bi, c: (bi, c, 0)),
                   pl.BlockSpec((1, D_INNER, D_STATE), lambda bi, c: (bi, 0, 0))],
        out_shape=[jax.ShapeDtypeStruct((b, l, D_INNER), BF16),
                   jax.ShapeDtypeStruct((b, D_INNER, D_STATE), F32)],
        scratch_shapes=[pltpu.VMEM((D_STATE, D_INNER), F32), pltpu.VMEM((8 + q, CONV_DIM), F32)],
        compiler_params=_cparams(("arbitrary", "arbitrary")),
        name="ssd_prompt",
    )(xbc, z, small, conv_w, conv_b, alog_lane, dskip_exp, norm_w)


def _ssd_sample_body(xbc_ref, c0_ref, z_ref, sm_ref, st_ref, cw_ref, cb_ref, alog_ref, dskip_ref, nw_ref,
                     y_ref, so_ref, ext_scr):
    l = xbc_ref.shape[1]
    pair_w = 2 * HEAD_DIM_S
    ext_scr[0:8, :] = jnp.zeros((8, CONV_DIM), F32)
    ext_scr[8 - (CONV_K - 1):8, :] = c0_ref[0]
    ext_scr[8:8 + l, :] = xbc_ref[0]
    act = _causal_conv_silu(ext_scr, cw_ref, cb_ref, l)

    sm = sm_ref[0]
    dta = sm * (-jnp.exp(alog_ref[...]))
    trow = lax.broadcasted_iota(jnp.int32, (l, LANES), 0)
    a_c = jnp.zeros((l, LANES), F32)
    for s in range(l):
        a_c = a_c + jnp.where(trow >= s, dta[s:s + 1, :], 0.0)
    lane = lax.broadcasted_iota(jnp.int32, (1, pair_w), 1)
    first = lane < HEAD_DIM_S

    def expand(v):
        cols = []
        for pp in range(N_HEADS_S // 2):
            li = DT_LANE0 + 2 * pp
            cols.append(_pair_select(first, v[:, li:li + 1], v[:, li + 1:li + 2]))
        return jnp.concatenate(cols, axis=1)

    a_x = expand(a_c)
    dt_x = expand(sm)
    ea_x = jnp.exp(a_x)
    w_x = jnp.exp(a_x[l - 1:l, :] - a_x) * dt_x
    xs = act[:, 0:D_INNER]
    zpad = jnp.zeros((PAGE_SIZE - l, D_STATE), F32)
    trow_x = lax.broadcasted_iota(jnp.int32, (l, GROUP_W), 0)

    xw_pad = jnp.concatenate([xs * w_x, jnp.zeros((PAGE_SIZE - l, D_INNER), F32)], axis=0)
    ea_last = jnp.exp(a_c[l - 1:l, :])
    rowsel = lax.broadcasted_iota(jnp.int32, (pair_w, 1), 0) < HEAD_DIM_S

    for g in range(N_GROUPS_S):
        gc = g * GROUP_W
        bc = act[:, D_INNER + g * D_STATE:D_INNER + (g + 1) * D_STATE]
        cc = act[:, D_INNER + (N_GROUPS_S + g) * D_STATE:D_INNER + (N_GROUPS_S + g + 1) * D_STATE]
        b_pad = jnp.concatenate([bc, zpad], axis=0).astype(BF16)
        cb = _dot_nt(cc.astype(BF16), b_pad)
        s_g = st_ref[0, gc:gc + GROUP_W, :]
        y = _dot_nt(cc.astype(BF16), s_g.astype(BF16)) * ea_x[:, gc:gc + GROUP_W]
        x_g = xs[:, gc:gc + GROUP_W]
        a_g = a_x[:, gc:gc + GROUP_W]
        dt_g = dt_x[:, gc:gc + GROUP_W]
        for s in range(l):
            dec = jnp.exp(jnp.where(trow_x >= s, a_g - a_g[s:s + 1, :], NEG_INF))
            y = y + cb[:, s:s + 1] * dec * (dt_g[s:s + 1, :] * x_g[s:s + 1, :])
        y = y + x_g * dskip_ref[:, gc:gc + GROUP_W]
        gp = y * _silu(z_ref[0, :, gc:gc + GROUP_W])
        rs = lax.rsqrt(jnp.sum(gp * gp, axis=-1, keepdims=True) / GROUP_W + EPS)
        y_ref[0, :, gc:gc + GROUP_W] = gp * rs * nw_ref[:, gc:gc + GROUP_W]
        for pp in range(HEADS_PER_GROUP // 2):
            h0 = g * HEADS_PER_GROUP + 2 * pp
            col = h0 * HEAD_DIM_S
            li = DT_LANE0 + h0
            u = _dot(xw_pad[:, col:col + pair_w].T.astype(BF16), b_pad)
            e_col = jnp.where(rowsel, ea_last[:, li:li + 1], ea_last[:, li + 1:li + 2])
            so_ref[0, col:col + pair_w, :] = st_ref[0, col:col + pair_w, :] * e_col + u


def _ssd_sample(xbc, conv0, z, small, state, conv_w, conv_b, alog_lane, dskip_exp, norm_w):
    b, l, _ = xbc.shape
    full = lambda shape: pl.BlockSpec(shape, lambda bi: (0,) * len(shape))
    seq = lambda shape: pl.BlockSpec(shape, lambda bi: (bi, 0, 0))
    return pl.pallas_call(
        _ssd_sample_body,
        grid=(b,),
        in_specs=[seq((1, l, CONV_DIM)), seq((1, CONV_K - 1, CONV_DIM)), seq((1, l, D_INNER)),
                  seq((1, l, LANES)), seq((1, D_INNER, D_STATE)),
                  full((CONV_K, CONV_DIM)), full((1, CONV_DIM)), full((1, LANES)),
                  full((1, D_INNER)), full((1, D_INNER))],
        out_specs=[seq((1, l, D_INNER)), seq((1, D_INNER, D_STATE))],
        out_shape=[jax.ShapeDtypeStruct((b, l, D_INNER), F32),
                   jax.ShapeDtypeStruct((b, D_INNER, D_STATE), F32)],
        scratch_shapes=[pltpu.VMEM((8 + l, CONV_DIM), F32)],
        compiler_params=_cparams(("arbitrary",)),
        name="ssd_sample",
    )(xbc, conv0, z, small, state, conv_w, conv_b, alog_lane, dskip_exp, norm_w)


def _post_mixer_body(x_ref, oa_ref, ys_ref, gate_ref, g1_ref, sh2_ref, sc2_ref, wa_ref, ws_ref, wo_ref,
                     n2_ref, wr_ref, br_ref, x2_ref, h_ref, te_ref, tw_ref):
    bb, bl, d = x_ref.shape
    rows = bb * bl
    oa = oa_ref[...].reshape(rows, d).astype(BF16)
    ys = ys_ref[...].reshape(rows, D_INNER).astype(BF16)
    y_attn = _dot(oa, wa_ref[...])
    y_ssm = _dot(ys, ws_ref[...])
    gate = gate_ref[...]
    merged = _sigmoid(gate[:, 0:d]) * y_attn + _sigmoid(gate[:, d:2 * d]) * y_ssm
    mix = _dot(merged.astype(BF16), wo_ref[...])
    x2 = x_ref[...] + g1_ref[...] * mix.reshape(bb, bl, d)
    x2_ref[...] = x2
    ms = jnp.mean(x2 * x2, axis=-1, keepdims=True)
    hf = (x2 * lax.rsqrt(ms + EPS) * n2_ref[...]) * (1.0 + sc2_ref[...]) + sh2_ref[...]
    hf = hf.reshape(rows, d)
    h_ref[...] = hf.astype(BF16)
    lane = lax.broadcasted_iota(jnp.int32, (rows, LANES), 1)
    logits = jnp.where(lane < N_EXPERTS, _dot3(hf, wr_ref[...]) + br_ref[...], NEG_INF)
    vals, idxs = [], []
    for _ in range(TOP_K):
        mx = jnp.max(logits, axis=-1, keepdims=True)
        idx = jnp.min(jnp.where(logits == mx, lane, LANES), axis=-1, keepdims=True)
        vals.append(mx)
        idxs.append(idx)
        logits = jnp.where(lane == idx, NEG_INF, logits)
    ex = [jnp.exp(v - vals[0]) for v in vals]
    den = ex[0] + ex[1] + ex[2] + ex[3]
    te = jnp.zeros((rows, LANES), jnp.int32)
    tw = jnp.zeros((rows, LANES), F32)
    for k in range(TOP_K):
        te = jnp.where(lane == k, idxs[k], te)
        tw = jnp.where(lane == k, ex[k] / den, tw)
    te_ref[...] = te
    tw_ref[...] = tw


def _post_mixer(x, o_attn, y_ssm, gate, mod, w_attn_out, w_ssm_out, w_o, norm2_w, w_router, b_router,
                rows=256):
    b, l, d = x.shape
    bb, bl, rows = _row_blocks(b, l, rows)
    nl = l // bl
    tok = lambda w: pl.BlockSpec((bb, bl, w), lambda i, j: (i, j, 0))
    flat = lambda w: pl.BlockSpec((rows, w), lambda i, j: (i * nl + j, 0))
    modc = lambda c: pl.BlockSpec((bb, 1, d), lambda i, j: (i, 0, c))
    full = lambda shape: pl.BlockSpec(shape, lambda i, j: (0,) * len(shape))
    return pl.pallas_call(
        _post_mixer_body,
        grid=(b // bb, nl),
        in_specs=[tok(d), tok(d), tok(D_INNER), flat(2 * d), modc(2), modc(3), modc(4),
                  full((d, d)), full((D_INNER, d)), full((d, d)), full((1, d)),
                  full((d, LANES)), full((1, LANES))],
        out_specs=[tok(d), flat(d), flat(LANES), flat(LANES)],
        out_shape=[jax.ShapeDtypeStruct((b, l, d), F32), jax.ShapeDtypeStruct((b * l, d), BF16),
                   jax.ShapeDtypeStruct((b * l, LANES), jnp.int32), jax.ShapeDtypeStruct((b * l, LANES), F32)],
        compiler_params=_cparams(("arbitrary", "arbitrary")),
        name="post_mixer",
    )(x, o_attn, y_ssm, gate, mod, mod, mod, w_attn_out, w_ssm_out, w_o, norm2_w.reshape(1, d),
      w_router, b_router)


MOE_ROWS = 256
SPLIT_COLS = 512


def _split_gate_up_body(w_ref, g_ref, u_ref, t_scr):
    half = SPLIT_COLS // 2
    for kc in range(w_ref.shape[1] // LANES):
        ks = slice(kc * LANES, (kc + 1) * LANES)
        t_scr[...] = w_ref[0, ks, :].T
        g_ref[0, :, ks] = t_scr[pl.ds(0, half, stride=2), :].astype(BF16)
        u_ref[0, :, ks] = t_scr[pl.ds(1, half, stride=2), :].astype(BF16)


def _split_gate_up(w_gu):
    e, k, n2 = w_gu.shape
    half = SPLIT_COLS // 2
    out = jax.ShapeDtypeStruct((e, n2 // 2, k), BF16)
    return pl.pallas_call(
        _split_gate_up_body,
        grid=(e, n2 // SPLIT_COLS),
        in_specs=[pl.BlockSpec((1, k, SPLIT_COLS), lambda i, j: (i, 0, j))],
        out_specs=[pl.BlockSpec((1, half, k), lambda i, j: (i, j, 0)),
                   pl.BlockSpec((1, half, k), lambda i, j: (i, j, 0))],
        out_shape=[out, out],
        scratch_shapes=[pltpu.VMEM((SPLIT_COLS, LANES), F32)],
        compiler_params=_cparams(("arbitrary", "arbitrary")),
        name="split_gate_up",
    )(w_gu)


def _experts_body(be_ref, x_ref, wg_ref, wu_ref, bg_ref, bu_ref, wd_ref, bd_ref, o_ref):
    x = x_ref[...]
    gate = jnp.minimum(_dot_nt(x, wg_ref[0]) + bg_ref[0], SWIGLU_LIMIT)
    up = jnp.clip(_dot_nt(x, wu_ref[0]) + bu_ref[0], -SWIGLU_LIMIT, SWIGLU_LIMIT)
    act = (up + 1.0) * (gate * _sigmoid(SWIGLU_ALPHA * gate))
    o_ref[...] = _dot(act.astype(BF16), wd_ref[0]) + bd_ref[0]


def _experts(block_e, xg, w_g, w_u, b_g, b_u, w_d, b_d):
    p, d = xg.shape
    ff = w_g.shape[1]
    wspec = lambda shape: pl.BlockSpec(shape, lambda i, be: (be[i], 0, 0))
    grid_spec = pltpu.PrefetchScalarGridSpec(
        num_scalar_prefetch=1,
        grid=(p // MOE_ROWS,),
        in_specs=[pl.BlockSpec((MOE_ROWS, d), lambda i, be: (i, 0)),
                  wspec((1, ff, d)), wspec((1, ff, d)), wspec((1, 1, ff)), wspec((1, 1, ff)),
                  wspec((1, ff, d)), wspec((1, 1, d))],
        out_specs=pl.BlockSpec((MOE_ROWS, d), lambda i, be: (i, 0)),
    )
    return pl.pallas_call(
        _experts_body,
        grid_spec=grid_spec,
        out_shape=jax.ShapeDtypeStruct((p, d), F32),
        compiler_params=_cparams(("arbitrary",)),
        name="moe_experts",
    )(block_e, xg, w_g, w_u, b_g, b_u, w_d, b_d)


def _route(top_e):
    t, k = top_e.shape
    s = t * k
    e_flat = top_e.reshape(-1)
    onehot = (e_flat[:, None] == jnp.arange(N_EXPERTS, dtype=jnp.int32)[None, :]).astype(jnp.int32)
    csum = jnp.cumsum(onehot, axis=0)
    rank = jnp.sum(onehot * csum, axis=1) - 1
    counts = csum[-1]
    padded = (counts + MOE_ROWS - 1) // MOE_ROWS * MOE_ROWS
    pend = jnp.cumsum(padded)
    dest = (pend - padded)[e_flat] + rank
    n_blocks = -(-s // MOE_ROWS) + N_EXPERTS
    row_tok = jnp.zeros((n_blocks * MOE_ROWS,), jnp.int32).at[dest].set(
        jnp.arange(s, dtype=jnp.int32) // k)
    block_row0 = jnp.arange(n_blocks, dtype=jnp.int32) * MOE_ROWS
    block_e = jnp.minimum(jnp.sum((pend[None, :] <= block_row0[:, None]).astype(jnp.int32), axis=1),
                          N_EXPERTS - 1)
    return dest.reshape(t, k), row_tok, block_e


def _final_body(x_ref, moe_ref, g2_ref, nf_ref, y_ref):
    bb, bl, d = x_ref.shape
    x = x_ref[...] + g2_ref[...] * moe_ref[...].reshape(bb, bl, d)
    ms = jnp.mean(x * x, axis=-1, keepdims=True)
    y_ref[...] = x * lax.rsqrt(ms + EPS) * nf_ref[...]


def _final(x2, moe_out, row0, mod, norm_f_w, rows=512):
    b, l, d = x2.shape
    bb, bl, rows = _row_blocks(b, l, rows)
    nl = l // bl
    assert row0 % rows == 0
    blk0 = row0 // rows
    return pl.pallas_call(
        _final_body,
        grid=(b // bb, nl),
        in_specs=[pl.BlockSpec((bb, bl, d), lambda i, j: (i, j, 0)),
                  pl.BlockSpec((rows, d), lambda i, j: (blk0 + i * nl + j, 0)),
                  pl.BlockSpec((bb, 1, d), lambda i, j: (i, 0, 5)),
                  pl.BlockSpec((1, d), lambda i, j: (0, 0))],
        out_specs=pl.BlockSpec((bb, bl, d), lambda i, j: (i, j, 0)),
        out_shape=jax.ShapeDtypeStruct((b, l, d), F32),
        compiler_params=_cparams(("arbitrary", "arbitrary")),
        name="final_norm",
    )(x2, moe_out, mod, norm_f_w.reshape(1, d))


def _prep_in_weights(w_in, b_fgate, dt_bias):
    aw = N_HEADS_A * HEAD_DIM_A
    o = np.cumsum([0, aw, aw, aw, N_HEADS_A, D_INNER, CONV_DIM, N_HEADS_S, D_MODEL, D_MODEL])
    w_t = w_in.T
    rows = lambda i: w_t[o[i]:o[i + 1]]
    w_small = jnp.concatenate(
        [rows(3), rows(6), jnp.zeros((LANES - N_HEADS_A - N_HEADS_S, D_MODEL), F32)], axis=0).T
    b_small = jnp.concatenate(
        [b_fgate, dt_bias, jnp.zeros((LANES - N_HEADS_A - N_HEADS_S,), F32)]).reshape(1, LANES)
    big = {"q": rows(0), "k": rows(1), "v": rows(2), "z": rows(4), "xbc": rows(5),
           "gate": w_t[o[7]:o[9]]}
    return {n: w.astype(BF16) for n, w in big.items()}, w_small, b_small


def kernel(x_prompt, x_sample, cache_k, cache_v, cache_logf, state_ssm, state_conv, page_table, c_prompt, c_sample, w_cond, b_cond, norm1_w, w_in, b_fgate, conv_w, conv_b, dt_bias, a_log, d_skip, ssm_norm_w, w_attn_out, w_ssm_out, w_o, norm2_w, w_router, b_router, w_gate_up, b_gate_up, w_down, b_down, norm_f_w):
    assert w_in.shape[0] == 1, "single-layer trunk"
    bp, lp, d = x_prompt.shape
    bs, ls, _ = x_sample.shape
    tp, ts = bp * lp, bs * ls
    aw = N_HEADS_A * HEAD_DIM_A
    c_all = jnp.concatenate([c_prompt, c_sample, jnp.zeros((-(bp + bs) % 8, d), F32)], axis=0)
    mod = _adaln_mod(c_all, w_cond[0], b_cond[0])
    mod_p = mod[:bp].reshape(bp, 1, 6 * d)
    mod_s = mod[bp:bp + bs].reshape(bs, 1, 6 * d)

    w_big, w_small, b_small = _prep_in_weights(w_in[0], b_fgate[0], dt_bias[0])
    pad_lanes = LANES - N_HEADS_A - N_HEADS_S
    alog_lane = jnp.concatenate([jnp.zeros((N_HEADS_A,), F32), a_log[0], jnp.zeros((pad_lanes,), F32)]).reshape(1, LANES)
    dskip_exp = jnp.repeat(d_skip[0], HEAD_DIM_S).reshape(1, D_INNER)
    ssm_nw = ssm_norm_w[0].reshape(1, D_INNER)
    conv_b2 = conv_b[0].reshape(1, CONV_DIM)
    wa, ws, wo = w_attn_out[0].astype(BF16), w_ssm_out[0].astype(BF16), w_o[0].astype(BF16)
    w_router_pad = jnp.pad(w_router[0], ((0, 0), (0, LANES - N_EXPERTS)))
    b_router_pad = jnp.pad(b_router[0], (0, LANES - N_EXPERTS)).reshape(1, LANES)

    def in_proj(x, m):
        h, small = _norm_in(x, m, norm1_w[0], w_small, b_small)
        return {n: _matmul_nt(h, w) for n, w in w_big.items()}, small

    proj_p, small_p = in_proj(x_prompt, mod_p)
    proj_s, small_s = in_proj(x_sample, mod_s)

    fcum = _cumsum_t(small_p.reshape(tp // PAGE_SIZE, PAGE_SIZE, LANES), lp // PAGE_SIZE, True)
    fcum = fcum.reshape(bp, N_HEADS_A // 2, 2, lp)
    o_p = _fox_prompt(proj_p["q"].reshape(bp, lp, aw), proj_p["k"].astype(BF16).reshape(bp, lp, aw),
                      proj_p["v"].astype(BF16).reshape(bp, lp, aw), fcum, tile=min(512, lp))
    xbc_p = proj_p["xbc"].reshape(bp, lp, CONV_DIM)
    y_p, st_p = _ssd_prompt(xbc_p, proj_p["z"].reshape(bp, lp, D_INNER), small_p.reshape(bp, lp, LANES),
                            conv_w[0], conv_b2, alog_lane, dskip_exp, ssm_nw)

    n_phys = cache_k.shape[1]
    k_t = jnp.transpose(cache_k[0], (0, 2, 3, 1)).reshape(n_phys, aw, PAGE_SIZE)
    v_t = jnp.transpose(cache_v[0], (0, 2, 3, 1)).reshape(n_phys, aw, PAGE_SIZE)
    lf_t = jnp.swapaxes(cache_logf[0], 1, 2).reshape(n_phys * N_HEADS_A, PAGE_SIZE)
    c_pages = _cumsum_lanes(lf_t, rows=512).reshape(n_phys, N_HEADS_A, PAGE_SIZE)
    lf_new = jnp.pad(small_s.reshape(bs, ls, LANES), ((0, 0), (0, PAGE_SIZE - ls), (0, 0)))
    c_new = _cumsum_t(lf_new, 1, False)
    n_pages = page_table.shape[1]
    o_s = _fox_sample(proj_s["q"].reshape(bs, ls, aw), proj_s["k"].reshape(bs, ls, aw),
                      proj_s["v"].reshape(bs, ls, aw), c_new, k_t, v_t, c_pages, page_table,
                      pages_per_step=min(8, n_pages))
    xbc_s = proj_s["xbc"].reshape(bs, ls, CONV_DIM)
    y_s, st_s = _ssd_sample(xbc_s, state_conv[0], proj_s["z"].reshape(bs, ls, D_INNER),
                            small_s.reshape(bs, ls, LANES), state_ssm[0].reshape(bs, D_INNER, D_STATE),
                            conv_w[0], conv_b2, alog_lane, dskip_exp, ssm_nw)

    post = lambda x, o, y, gate, m: _post_mixer(x, o, y, gate, m, wa, ws, wo, norm2_w[0],
                                                w_router_pad, b_router_pad)
    x2_p, hf_p, te_p, tw_p = post(x_prompt, o_p, y_p, proj_p["gate"], mod_p)
    x2_s, hf_s, te_s, tw_s = post(x_sample, o_s, y_s, proj_s["gate"], mod_s)

    hf = jnp.concatenate([hf_p, hf_s], axis=0)
    top_e = jnp.concatenate([te_p[:, :TOP_K], te_s[:, :TOP_K]], axis=0)
    top_w = jnp.concatenate([tw_p[:, :TOP_K], tw_s[:, :TOP_K]], axis=0)
    dest, row_tok, block_e = _route(top_e)
    w_gate_t, w_up_t = _split_gate_up(w_gate_up[0])
    rows = _experts(block_e, hf[row_tok], w_gate_t, w_up_t,
                    b_gate_up[0][:, None, 0::2], b_gate_up[0][:, None, 1::2],
                    w_down[0].astype(BF16), b_down[0][:, None, :])
    moe_out = jnp.sum(rows[dest] * top_w[:, :, None], axis=1)

    y_prompt = _final(x2_p, moe_out, 0, mod_p, norm_f_w)
    y_sample = _final(x2_s, moe_out, tp, mod_s, norm_f_w)

    def state_rows(k, v, small, st, xbc, conv0, b, l):
        conv_new = jnp.concatenate([conv0, xbc], axis=1)[:, -(CONV_K - 1):]
        return (k.reshape(1, b, l, N_HEADS_A, HEAD_DIM_A), v.reshape(1, b, l, N_HEADS_A, HEAD_DIM_A),
                small[:, :N_HEADS_A].reshape(1, b, l, N_HEADS_A),
                st.reshape(1, b, N_HEADS_S, HEAD_DIM_S, D_STATE), conv_new[None])

    sp = state_rows(proj_p["k"], proj_p["v"], small_p, st_p, xbc_p,
                    jnp.zeros((bp, CONV_K - 1, CONV_DIM), F32), bp, lp)
    ss = state_rows(proj_s["k"], proj_s["v"], small_s, st_s, xbc_s, state_conv[0], bs, ls)
    return (y_prompt, y_sample) + sp + ss
```

```python
import functools

import numpy as np
import jax
import jax.numpy as jnp
from jax import lax
from jax.experimental import pallas as pl
from jax.experimental.pallas import tpu as pltpu
from jax.experimental.pallas import tpu_sc as plsc

F32 = jnp.float32
BF16 = jnp.bfloat16

D_MODEL = 1024
N_HEADS_A = 16
HEAD_DIM_A = 64
PAGE_SIZE = 128
D_INNER = 2048
HEAD_DIM_S = 64
N_HEADS_S = 32
N_GROUPS_S = 4
HEADS_PER_GROUP = N_HEADS_S // N_GROUPS_S
GROUP_W = D_INNER // N_GROUPS_S
D_STATE = 128
CONV_K = 4
CONV_DIM = D_INNER + 2 * N_GROUPS_S * D_STATE
SSD_CHUNK = 128
N_EXPERTS = 32
TOP_K = 4
D_FF = D_MODEL
SWIGLU_LIMIT = 7.0
SWIGLU_ALPHA = 1.702
EPS = 1e-6

LANES = 128
DT_LANE0 = N_HEADS_A
NEG_INF = float("-inf")
LOG2E = 1.4426950408889634
VMEM_LIMIT = 56 * 1024 * 1024


def _cparams(sem):
    return pltpu.CompilerParams(dimension_semantics=sem, vmem_limit_bytes=VMEM_LIMIT)


def _split3(x):
    hi = x.astype(BF16)
    r = x - hi.astype(F32)
    mid = r.astype(BF16)
    lo = (r - mid.astype(F32)).astype(BF16)
    return hi, mid, lo


def _dot(a, b):
    return jnp.dot(a, b, preferred_element_type=F32)


def _dot_nt(a, b):
    return lax.dot_general(a, b, (((1,), (1,)), ((), ())), preferred_element_type=F32)


def _dot_sel_left(sel_bf16, x):
    hi, mid, lo = _split3(x)
    return _dot(sel_bf16, hi) + _dot(sel_bf16, mid) + _dot(sel_bf16, lo)


def _dot3(a, b):
    ah = a.astype(BF16)
    am = (a - ah.astype(F32)).astype(BF16)
    bh = b.astype(BF16)
    bm = (b - bh.astype(F32)).astype(BF16)
    return _dot(ah, bh) + _dot(ah, bm) + _dot(am, bh)


def _tri_lower(n):
    r = lax.broadcasted_iota(jnp.int32, (n, n), 0)
    c = lax.broadcasted_iota(jnp.int32, (n, n), 1)
    return r >= c


def _sigmoid(x):
    return 1.0 / (1.0 + jnp.exp(-x))


def _silu(x):
    return x * _sigmoid(x)


def _mod_body(c_ref, w_ref, b_ref, o_ref):
    c = c_ref[...]
    o_ref[...] = _dot3(_silu(c), w_ref[...]) + b_ref[...]


def _adaln_mod(c_all, w_cond, b_cond):
    rows, d = c_all.shape
    n = w_cond.shape[1]
    tn = 1024
    return pl.pallas_call(
        _mod_body,
        grid=(n // tn,),
        in_specs=[pl.BlockSpec((rows, d), lambda j: (0, 0)),
                  pl.BlockSpec((d, tn), lambda j: (0, j)),
                  pl.BlockSpec((1, tn), lambda j: (0, j))],
        out_specs=pl.BlockSpec((rows, tn), lambda j: (0, j)),
        out_shape=jax.ShapeDtypeStruct((rows, n), F32),
        compiler_params=_cparams(("arbitrary",)),
        name="adaln_mod",
    )(c_all, w_cond, b_cond.reshape(1, n))


def _norm_in_body(x_ref, sh_ref, sc_ref, nw_ref, ws_ref, bs_ref, h_ref, sm_ref):
    x = x_ref[...]
    bb, bl, d = x.shape
    ms = jnp.mean(x * x, axis=-1, keepdims=True)
    y = x * lax.rsqrt(ms + EPS) * nw_ref[...]
    h = (y * (1.0 + sc_ref[...]) + sh_ref[...]).reshape(bb * bl, d)
    h_ref[...] = h.astype(BF16)
    sm = _dot3(h, ws_ref[...]) + bs_ref[...]
    lane = lax.broadcasted_iota(jnp.int32, sm.shape, 1)
    t = jnp.log(1.0 + jnp.exp(-jnp.abs(sm)))
    sm_ref[...] = jnp.where(lane < N_HEADS_A, jnp.minimum(sm, 0.0) - t, jnp.maximum(sm, 0.0) + t)


def _row_blocks(b, l, rows):
    rows = min(rows, b * l)
    if l >= rows:
        assert l % rows == 0
        return 1, rows, rows
    assert rows % l == 0 and b % (rows // l) == 0
    return rows // l, l, rows


def _norm_in(x, mod, norm_w, w_small, b_small, rows=512):
    b, l, d = x.shape
    bb, bl, rows = _row_blocks(b, l, rows)
    nl = l // bl
    grid = (b // bb, nl)
    return pl.pallas_call(
        _norm_in_body,
        grid=grid,
        in_specs=[pl.BlockSpec((bb, bl, d), lambda i, j: (i, j, 0)),
                  pl.BlockSpec((bb, 1, d), lambda i, j: (i, 0, 0)),
                  pl.BlockSpec((bb, 1, d), lambda i, j: (i, 0, 1)),
                  pl.BlockSpec((1, d), lambda i, j: (0, 0)),
                  pl.BlockSpec((d, LANES), lambda i, j: (0, 0)),
                  pl.BlockSpec((1, LANES), lambda i, j: (0, 0))],
        out_specs=[pl.BlockSpec((rows, d), lambda i, j: (i * nl + j, 0)),
                   pl.BlockSpec((rows, LANES), lambda i, j: (i * nl + j, 0))],
        out_shape=[jax.ShapeDtypeStruct((b * l, d), BF16),
                   jax.ShapeDtypeStruct((b * l, LANES), F32)],
        compiler_params=_cparams(("arbitrary", "arbitrary")),
        name="norm_in",
    )(x, mod, mod, norm_w.reshape(1, d), w_small, b_small)


def _mm_body(a_ref, w_ref, o_ref):
    o_ref[...] = _dot_nt(a_ref[...], w_ref[...])


def _matmul_nt(a, w_t, tm=512, tn=1024):
    m, k = a.shape
    n = w_t.shape[0]
    tn = min(tn, n)
    return pl.pallas_call(
        _mm_body,
        grid=(n // tn, m // tm),
        in_specs=[pl.BlockSpec((tm, k), lambda j, i: (i, 0)),
                  pl.BlockSpec((tn, k), lambda j, i: (j, 0))],
        out_specs=pl.BlockSpec((tm, tn), lambda j, i: (i, j)),
        out_shape=jax.ShapeDtypeStruct((m, n), F32),
        compiler_params=_cparams(("arbitrary", "arbitrary")),
        name="proj_matmul",
    )(a, w_t)


def _cumsum_t_body(x_ref, o_ref, pad_scr, carry_scr, *, width, carry):
    j = pl.program_id(1)

    @pl.when(j == 0)
    def _():
        pad_scr[...] = jnp.zeros_like(pad_scr)
        carry_scr[...] = jnp.zeros_like(carry_scr)

    if width == LANES:
        blk = x_ref[0]
    else:
        pad_scr[:, 0:width] = x_ref[0]
        blk = pad_scr[...]
    tri = _tri_lower(PAGE_SIZE).astype(BF16)
    cs = _dot_sel_left(tri, blk) + carry_scr[...]
    if carry:
        carry_scr[...] = cs[PAGE_SIZE - 1:PAGE_SIZE, :]
    o_ref[0] = cs.T[0:N_HEADS_A, :]


def _cumsum_t(x, nper, carry):
    n, p, width = x.shape
    assert p == PAGE_SIZE and n % nper == 0
    return pl.pallas_call(
        functools.partial(_cumsum_t_body, width=width, carry=carry),
        grid=(n // nper, nper),
        in_specs=[pl.BlockSpec((1, p, width), lambda i, j: (i * nper + j, 0, 0))],
        out_specs=pl.BlockSpec((1, N_HEADS_A, p), lambda i, j: (i, 0, j)),
        out_shape=jax.ShapeDtypeStruct((n // nper, N_HEADS_A, nper * p), F32),
        scratch_shapes=[pltpu.VMEM((p, LANES), F32), pltpu.VMEM((1, LANES), F32)],
        compiler_params=_cparams(("arbitrary", "arbitrary")),
        name="logf_cumsum",
    )(x)


def _cumsum_lanes_body(x_ref, o_ref):
    n = x_ref.shape[1]
    r = lax.broadcasted_iota(jnp.int32, (n, n), 0)
    c = lax.broadcasted_iota(jnp.int32, (n, n), 1)
    tri = (r <= c).astype(BF16)
    hi, mid, lo = _split3(x_ref[...])
    o_ref[...] = _dot(hi, tri) + _dot(mid, tri) + _dot(lo, tri)


def _cumsum_lanes(x, rows=1024):
    r, n = x.shape
    assert r % rows == 0
    return pl.pallas_call(
        _cumsum_lanes_body,
        grid=(r // rows,),
        in_specs=[pl.BlockSpec((rows, n), lambda i: (i, 0))],
        out_specs=pl.BlockSpec((rows, n), lambda i: (i, 0)),
        out_shape=jax.ShapeDtypeStruct((r, n), F32),
        compiler_params=_cparams(("arbitrary",)),
        name="page_logf_cumsum",
    )(x)


def _fox_prompt_body(it_ref, jt_ref, q_ref, k_ref, v_ref, f_ref, o_ref, qm_scr, m_scr, acc_scr, *, tile):
    t = pl.program_id(2)
    i = it_ref[t]
    j = jt_ref[t]
    lane = lax.broadcasted_iota(jnp.int32, (tile, LANES), 1)
    first = lane < HEAD_DIM_A

    @pl.when(j == 0)
    def _():
        q = q_ref[0] * (HEAD_DIM_A ** -0.5 * LOG2E)
        qm_scr[0] = jnp.where(first, q, 0.0).astype(BF16)
        qm_scr[1] = jnp.where(first, 0.0, q).astype(BF16)
        m_scr[...] = jnp.full_like(m_scr, -1e30)
        acc_scr[...] = jnp.zeros_like(acc_scr)

    def step(diagonal):
        kb = k_ref[0]
        vb = v_ref[0]
        one = jnp.ones((), BF16)
        for h in range(2):
            s = _dot_nt(qm_scr[h], kb) - f_ref[0, 0, h:h + 1, :] * LOG2E
            if diagonal:
                row = lax.broadcasted_iota(jnp.int32, (tile, tile), 0)
                col = lax.broadcasted_iota(jnp.int32, (tile, tile), 1)
                s = jnp.where(col <= row, s, NEG_INF)
            m_old = m_scr[h]
            m_new = jnp.maximum(m_old, jnp.max(s, axis=-1, keepdims=True))
            alpha = jnp.exp2(m_old - m_new)
            p = jnp.exp2(s - jnp.concatenate([m_new] * (tile // LANES), axis=1))
            v_aug = jnp.where(first, vb, one) if h == 0 else jnp.where(first, one, vb)
            acc_scr[h] = alpha * acc_scr[h] + _dot(p.astype(BF16), v_aug)
            m_scr[h] = m_new

    @pl.when(j < i)
    def _():
        step(False)

    @pl.when(j == i)
    def _():
        step(True)
        a0 = acc_scr[0]
        a1 = acc_scr[1]
        den = jnp.where(first, pltpu.roll(a0, HEAD_DIM_A, 1), pltpu.roll(a1, HEAD_DIM_A, 1))
        o_ref[0] = jnp.where(first, a0, a1) / den


def _fox_prompt(q, k, v, fcum, tile=512):
    b, l, w = q.shape
    npair = w // LANES
    nt = l // tile
    it = np.concatenate([np.full(i + 1, i, np.int32) for i in range(nt)])
    jt = np.concatenate([np.arange(i + 1, dtype=np.int32) for i in range(nt)])
    grid_spec = pltpu.PrefetchScalarGridSpec(
        num_scalar_prefetch=2,
        grid=(b, npair, len(it)),
        in_specs=[pl.BlockSpec((1, tile, LANES), lambda bi, p, t, it, jt: (bi, it[t], p)),
                  pl.BlockSpec((1, tile, LANES), lambda bi, p, t, it, jt: (bi, jt[t], p)),
                  pl.BlockSpec((1, tile, LANES), lambda bi, p, t, it, jt: (bi, jt[t], p)),
                  pl.BlockSpec((1, 1, 2, tile), lambda bi, p, t, it, jt: (bi, p, 0, jt[t]))],
        out_specs=pl.BlockSpec((1, tile, LANES), lambda bi, p, t, it, jt: (bi, it[t], p)),
        scratch_shapes=[pltpu.VMEM((2, tile, LANES), BF16), pltpu.VMEM((2, tile, LANES), F32),
                        pltpu.VMEM((2, tile, LANES), F32)],
    )
    return pl.pallas_call(
        functools.partial(_fox_prompt_body, tile=tile),
        grid_spec=grid_spec,
        out_shape=jax.ShapeDtypeStruct((b, l, w), F32),
        compiler_params=_cparams(("arbitrary", "arbitrary", "arbitrary")),
        name="fox_prompt",
    )(jnp.asarray(it), jnp.asarray(jt), q, k, v, fcum)


def _fox_sample_body(pt_ref, q_ref, kn_ref, vn_ref, cn_ref, *rest, pages_per_step):
    pg = pages_per_step
    k_refs = rest[0:pg]
    v_refs = rest[pg:2 * pg]
    c_refs = rest[2 * pg:3 * pg]
    o_ref, qbd_scr, m_scr, l_scr, carry_scr, acc_scr = rest[3 * pg:]
    j = pl.program_id(1)
    nq = q_ref.shape[1]
    rows = N_HEADS_A * nq
    width = N_HEADS_A * HEAD_DIM_A

    @pl.when(j == 0)
    def _():
        q = q_ref[0] * (HEAD_DIM_A ** -0.5)
        qt = jnp.broadcast_to(q[None], (N_HEADS_A, nq, width)).reshape(rows, width)
        row = lax.broadcasted_iota(jnp.int32, (rows, width), 0)
        lane = lax.broadcasted_iota(jnp.int32, (rows, width), 1)
        own = (lane // HEAD_DIM_A) == (row // nq)
        qbd_scr[...] = jnp.where(own, qt, 0.0).astype(BF16)
        m_scr[...] = jnp.full_like(m_scr, -1e30)
        l_scr[...] = jnp.zeros_like(l_scr)
        carry_scr[...] = jnp.zeros_like(carry_scr)
        acc_scr[...] = jnp.zeros_like(acc_scr)

    def attend(blocks, valid, transposed):
        qbd = qbd_scr[...]
        carry = carry_scr[...]
        scores = []
        for kf, _, c16 in blocks:
            s = _dot(qbd, kf.astype(BF16)) if transposed else _dot_nt(qbd, kf.astype(BF16))
            nk = s.shape[1]
            cexp = jnp.broadcast_to(c16[:, None, :], (N_HEADS_A, nq, nk)).reshape(rows, nk)
            s = s - (cexp + carry)
            if valid is not None:
                s = jnp.where(valid, s, NEG_INF)
            scores.append(s)
            carry = carry + cexp[:, nk - 1:nk]
        carry_scr[...] = carry
        smax = scores[0]
        for s in scores[1:]:
            smax = jnp.maximum(smax, s)
        m_old = m_scr[...]
        m_new = jnp.maximum(m_old, jnp.max(smax, axis=-1, keepdims=True))
        alpha = jnp.exp(m_old - m_new)
        m_scr[...] = m_new
        psum = None
        pv = None
        for s, (_, vf, _) in zip(scores, blocks):
            p = jnp.exp(s - m_new)
            psum = p if psum is None else psum + p
            pb = p.astype(BF16)
            d = _dot_nt(pb, vf.astype(BF16)) if transposed else _dot(pb, vf.astype(BF16))
            pv = d if pv is None else pv + d
        l_scr[...] = alpha * l_scr[...] + jnp.sum(psum, axis=-1, keepdims=True)
        acc_scr[...] = acc_scr[...] * alpha + pv

    attend([(k_refs[g][0], v_refs[g][0], c_refs[g][0]) for g in range(pg)], None, True)

    @pl.when(j == pl.num_programs(1) - 1)
    def _():
        zpad = jnp.zeros((PAGE_SIZE - nq, width), F32)
        kpad = jnp.concatenate([kn_ref[0], zpad], axis=0)
        vpad = jnp.concatenate([vn_ref[0], zpad], axis=0)
        row = lax.broadcasted_iota(jnp.int32, (rows, PAGE_SIZE), 0)
        key = lax.broadcasted_iota(jnp.int32, (rows, PAGE_SIZE), 1)
        attend([(kpad, vpad, cn_ref[0])], key <= (row % nq), False)
        o = acc_scr[...] / l_scr[...]
        for h in range(N_HEADS_A):
            o_ref[0, :, h * HEAD_DIM_A:(h + 1) * HEAD_DIM_A] = (
                o[h * nq:(h + 1) * nq, h * HEAD_DIM_A:(h + 1) * HEAD_DIM_A])


def _fox_sample(q, k_new, v_new, c_new, cache_k, cache_v, c_pages, page_table, pages_per_step=8):
    b, nq, w = q.shape
    n_pages = page_table.shape[1]
    pg = pages_per_step
    assert n_pages % pg == 0
    rows = N_HEADS_A * nq

    def page_map(g):
        return lambda bi, j, pt: (pt[bi * n_pages + j * pg + g], 0, 0)

    seq_map = lambda bi, j, pt: (bi, 0, 0)
    in_specs = [pl.BlockSpec((1, nq, w), seq_map), pl.BlockSpec((1, nq, w), seq_map),
                pl.BlockSpec((1, nq, w), seq_map), pl.BlockSpec((1, N_HEADS_A, PAGE_SIZE), seq_map)]
    in_specs += [pl.BlockSpec((1, w, PAGE_SIZE), page_map(g)) for g in range(pg)]
    in_specs += [pl.BlockSpec((1, w, PAGE_SIZE), page_map(g)) for g in range(pg)]
    in_specs += [pl.BlockSpec((1, N_HEADS_A, PAGE_SIZE), page_map(g)) for g in range(pg)]
    grid_spec = pltpu.PrefetchScalarGridSpec(
        num_scalar_prefetch=1,
        grid=(b, n_pages // pg),
        in_specs=in_specs,
        out_specs=pl.BlockSpec((1, nq, w), seq_map),
        scratch_shapes=[pltpu.VMEM((rows, w), BF16), pltpu.VMEM((rows, 1), F32), pltpu.VMEM((rows, 1), F32),
                        pltpu.VMEM((rows, 1), F32), pltpu.VMEM((rows, w), F32)],
    )
    args = [q, k_new, v_new, c_new] + [cache_k] * pg + [cache_v] * pg + [c_pages] * pg
    return pl.pallas_call(
        functools.partial(_fox_sample_body, pages_per_step=pg),
        grid_spec=grid_spec,
        out_shape=jax.ShapeDtypeStruct((b, nq, w), F32),
        compiler_params=_cparams(("arbitrary", "arbitrary")),
        name="fox_sample",
    )(page_table.reshape(-1), *args)


def _causal_conv_silu(ext_scr, cw_ref, cb_ref, n):
    acc = cb_ref[...] + cw_ref[0:1, :] * ext_scr[8 - (CONV_K - 1):8 - (CONV_K - 1) + n, :]
    for i in range(1, CONV_K):
        acc = acc + cw_ref[i:i + 1, :] * ext_scr[8 - (CONV_K - 1) + i:8 - (CONV_K - 1) + i + n, :]
    return _silu(acc)


def _pair_select(first, a, b):
    return jnp.where(first, a, b)


def _ssd_prompt_body(xbc_ref, z_ref, sm_ref, cw_ref, cb_ref, alog_ref, dskip_ref, nw_ref,
                     y_ref, st_ref, st_scr, ext_scr):
    c = pl.program_id(1)
    q = SSD_CHUNK
    pair_w = 2 * HEAD_DIM_S

    @pl.when(c == 0)
    def _():
        st_scr[...] = jnp.zeros_like(st_scr)
        ext_scr[0:8, :] = jnp.zeros((8, CONV_DIM), F32)

    cur = xbc_ref[0]
    ext_scr[8:8 + q, :] = cur
    act = _causal_conv_silu(ext_scr, cw_ref, cb_ref, q)
    ext_scr[0:8, :] = cur[q - 8:q, :]

    sm = sm_ref[0]
    a_lane = -jnp.exp(alog_ref[...])
    tri_mask = _tri_lower(q)
    a_all = _dot_sel_left(tri_mask.astype(BF16), sm * a_lane)
    a_t = a_all.T
    dt_t = sm.T
    w_t = jnp.exp(a_t[:, q - 1:q] - a_t) * dt_t
    ea_all = jnp.exp(a_all)
    lane = lax.broadcasted_iota(jnp.int32, (1, pair_w), 1)
    first = lane < HEAD_DIM_S

    for g in range(N_GROUPS_S):
        bc = act[:, D_INNER + g * D_STATE:D_INNER + (g + 1) * D_STATE]
        cc = act[:, D_INNER + (N_GROUPS_S + g) * D_STATE:D_INNER + (N_GROUPS_S + g + 1) * D_STATE]
        cb = _dot_nt(cc.astype(BF16), bc.astype(BF16))
        bc_t = bc.T
        gated = []
        ssq = jnp.zeros((q, 1), F32)
        for pp in range(HEADS_PER_GROUP // 2):
            h0 = g * HEADS_PER_GROUP + 2 * pp
            col = h0 * HEAD_DIM_S
            x_pair = act[:, col:col + pair_w]
            s_pair = st_scr[:, col:col + pair_w]
            rhs = jnp.concatenate([x_pair.astype(BF16), s_pair.astype(BF16)], axis=0)
            ys, us, el = [], [], []
            for h in (h0, h0 + 1):
                li = DT_LANE0 + h
                seg = a_all[:, li:li + 1] - a_t[li:li + 1, :]
                dec = jnp.exp(jnp.where(tri_mask, seg, NEG_INF))
                m_h = cb * dec * dt_t[li:li + 1, :]
                lhs = jnp.concatenate([m_h, cc * ea_all[:, li:li + 1]], axis=1).astype(BF16)
                ys.append(_dot(lhs, rhs))
                us.append(_dot((bc_t * w_t[li:li + 1, :]).astype(BF16), x_pair.astype(BF16)))
                el.append(ea_all[q - 1:q, li:li + 1])
            y_pair = _pair_select(first, ys[0], ys[1]) + x_pair * dskip_ref[:, col:col + pair_w]
            st_scr[:, col:col + pair_w] = (s_pair * _pair_select(first, el[0], el[1])
                                           + _pair_select(first, us[0], us[1]))
            gp = y_pair * _silu(z_ref[0, :, col:col + pair_w])
            ssq = ssq + jnp.sum(gp * gp, axis=-1, keepdims=True)
            gated.append((col, gp))
        rs = lax.rsqrt(ssq / GROUP_W + EPS)
        for col, gp in gated:
            y_ref[0, :, col:col + pair_w] = (gp * rs * nw_ref[:, col:col + pair_w]).astype(y_ref.dtype)

    @pl.when(c == pl.num_programs(1) - 1)
    def _():
        for blk in range(D_INNER // pair_w):
            st_ref[0, blk * pair_w:(blk + 1) * pair_w, :] = st_scr[:, blk * pair_w:(blk + 1) * pair_w].T


def _ssd_prompt(xbc, z, small, conv_w, conv_b, alog_lane, dskip_exp, norm_w):
    b, l, _ = xbc.shape
    q = SSD_CHUNK
    full = lambda shape: pl.BlockSpec(shape, lambda bi, c: (0,) * len(shape))
    return pl.pallas_call(
        _ssd_prompt_body,
        grid=(b, l // q),
        in_specs=[pl.BlockSpec((1, q, CONV_DIM), lambda bi, c: (bi, c, 0)),
                  pl.BlockSpec((1, q, D_INNER), lambda bi, c: (bi, c, 0)),
                  pl.BlockSpec((1, q, LANES), lambda bi, c: (bi, c, 0)),
                  full((CONV_K, CONV_DIM)), full((1, CONV_DIM)), full((1, LANES)),
                  full((1, D_INNER)), full((1, D_INNER))],
        out_specs=[pl.BlockSpec((1, q, D_INNER), lambda bi, c: (bi, c, 0)),
                   pl.BlockSpec((1, D_INNER, D_STATE), lambda bi, c: (bi, 0, 0))],
        out_shape=[jax.ShapeDtypeStruct((b, l, D_INNER), BF16),
                   jax.ShapeDtypeStruct((b, D_INNER, D_STATE), F32)],
        scratch_shapes=[pltpu.VMEM((D_STATE, D_INNER), F32), pltpu.VMEM((8 + q, CONV_DIM), F32)],
        compiler_params=_cparams(("arbitrary", "arbitrary")),
        name="ssd_prompt",
    )(xbc, z, small, conv_w, conv_b, alog_lane, dskip_exp, norm_w)


def _ssd_sample_body(xbc_ref, c0_ref, z_ref, sm_ref, st_ref, cw_ref, cb_ref, alog_ref, dskip_ref, nw_ref,
                     y_ref, so_ref, ext_scr):
    l = xbc_ref.shape[1]
    pair_w = 2 * HEAD_DIM_S
    ext_scr[0:8, :] = jnp.zeros((8, CONV_DIM), F32)
    ext_scr[8 - (CONV_K - 1):8, :] = c0_ref[0]
    ext_scr[8:8 + l, :] = xbc_ref[0]
    act = _causal_conv_silu(ext_scr, cw_ref, cb_ref, l)

    sm = sm_ref[0]
    dta = sm * (-jnp.exp(alog_ref[...]))
    trow = lax.broadcasted_iota(jnp.int32, (l, LANES), 0)
    a_c = jnp.zeros((l, LANES), F32)
    for s in range(l):
        a_c = a_c + jnp.where(trow >= s, dta[s:s + 1, :], 0.0)
    lane = lax.broadcasted_iota(jnp.int32, (1, pair_w), 1)
    first = lane < HEAD_DIM_S

    def expand(v):
        cols = []
        for pp in range(N_HEADS_S // 2):
            li = DT_LANE0 + 2 * pp
            cols.append(_pair_select(first, v[:, li:li + 1], v[:, li + 1:li + 2]))
        return jnp.concatenate(cols, axis=1)

    a_x = expand(a_c)
    dt_x = expand(sm)
    ea_x = jnp.exp(a_x)
    w_x = jnp.exp(a_x[l - 1:l, :] - a_x) * dt_x
    xs = act[:, 0:D_INNER]
    zpad = jnp.zeros((PAGE_SIZE - l, D_STATE), F32)
    trow_x = lax.broadcasted_iota(jnp.int32, (l, GROUP_W), 0)

    xw_pad = jnp.concatenate([xs * w_x, jnp.zeros((PAGE_SIZE - l, D_INNER), F32)], axis=0)
    ea_last = jnp.exp(a_c[l - 1:l, :])
    rowsel = lax.broadcasted_iota(jnp.int32, (pair_w, 1), 0) < HEAD_DIM_S

    for g in range(N_GROUPS_S):
        gc = g * GROUP_W
        bc = act[:, D_INNER + g * D_STATE:D_INNER + (g + 1) * D_STATE]
        cc = act[:, D_INNER + (N_GROUPS_S + g) * D_STATE:D_INNER + (N_GROUPS_S + g + 1) * D_STATE]
        b_pad = jnp.concatenate([bc, zpad], axis=0).astype(BF16)
        cb = _dot_nt(cc.astype(BF16), b_pad)
        s_g = st_ref[0, gc:gc + GROUP_W, :]
        y = _dot_nt(cc.astype(BF16), s_g.astype(BF16)) * ea_x[:, gc:gc + GROUP_W]
        x_g = xs[:, gc:gc + GROUP_W]
        a_g = a_x[:, gc:gc + GROUP_W]
        dt_g = dt_x[:, gc:gc + GROUP_W]
        for s in range(l):
            dec = jnp.exp(jnp.where(trow_x >= s, a_g - a_g[s:s + 1, :], NEG_INF))
            y = y + cb[:, s:s + 1] * dec * (dt_g[s:s + 1, :] * x_g[s:s + 1, :])
        y = y + x_g * dskip_ref[:, gc:gc + GROUP_W]
        gp = y * _silu(z_ref[0, :, gc:gc + GROUP_W])
        rs = lax.rsqrt(jnp.sum(gp * gp, axis=-1, keepdims=True) / GROUP_W + EPS)
        y_ref[0, :, gc:gc + GROUP_W] = gp * rs * nw_ref[:, gc:gc + GROUP_W]
        for pp in range(HEADS_PER_GROUP // 2):
            h0 = g * HEADS_PER_GROUP + 2 * pp
            col = h0 * HEAD_DIM_S
            li = DT_LANE0 + h0
            u = _dot(xw_pad[:, col:col + pair_w].T.astype(BF16), b_pad)
            e_col = jnp.where(rowsel, ea_last[:, li:li + 1], ea_last[:, li + 1:li + 2])
            so_ref[0, col:col + pair_w, :] = st_ref[0, col:col + pair_w, :] * e_col + u


def _ssd_sample(xbc, conv0, z, small, state, conv_w, conv_b, alog_lane, dskip_exp, norm_w):
    b, l, _ = xbc.shape
    full = lambda shape: pl.BlockSpec(shape, lambda bi: (0,) * len(shape))
    seq = lambda shape: pl.BlockSpec(shape, lambda bi: (bi, 0, 0))
    return pl.pallas_call(
        _ssd_sample_body,
        grid=(b,),
        in_specs=[seq((1, l, CONV_DIM)), seq((1, CONV_K - 1, CONV_DIM)), seq((1, l, D_INNER)),
                  seq((1, l, LANES)), seq((1, D_INNER, D_STATE)),
                  full((CONV_K, CONV_DIM)), full((1, CONV_DIM)), full((1, LANES)),
                  full((1, D_INNER)), full((1, D_INNER))],
        out_specs=[seq((1, l, D_INNER)), seq((1, D_INNER, D_STATE))],
        out_shape=[jax.ShapeDtypeStruct((b, l, D_INNER), F32),
                   jax.ShapeDtypeStruct((b, D_INNER, D_STATE), F32)],
        scratch_shapes=[pltpu.VMEM((8 + l, CONV_DIM), F32)],
        compiler_params=_cparams(("arbitrary",)),
        name="ssd_sample",
    )(xbc, conv0, z, small, state, conv_w, conv_b, alog_lane, dskip_exp, norm_w)


def _post_mixer_body(x_ref, oa_ref, ys_ref, gate_ref, g1_ref, sh2_ref, sc2_ref, wa_ref, ws_ref, wo_ref,
                     n2_ref, wr_ref, br_ref, x2_ref, h_ref, te_ref, tw_ref):
    bb, bl, d = x_ref.shape
    rows = bb * bl
    oa = oa_ref[...].reshape(rows, d).astype(BF16)
    ys = ys_ref[...].reshape(rows, D_INNER).astype(BF16)
    y_attn = _dot(oa, wa_ref[...])
    y_ssm = _dot(ys, ws_ref[...])
    gate = gate_ref[...]
    merged = _sigmoid(gate[:, 0:d]) * y_attn + _sigmoid(gate[:, d:2 * d]) * y_ssm
    mix = _dot(merged.astype(BF16), wo_ref[...])
    x2 = x_ref[...] + g1_ref[...] * mix.reshape(bb, bl, d)
    x2_ref[...] = x2
    ms = jnp.mean(x2 * x2, axis=-1, keepdims=True)
    hf = (x2 * lax.rsqrt(ms + EPS) * n2_ref[...]) * (1.0 + sc2_ref[...]) + sh2_ref[...]
    hf = hf.reshape(rows, d)
    h_ref[...] = hf
    lane = lax.broadcasted_iota(jnp.int32, (rows, LANES), 1)
    logits = jnp.where(lane < N_EXPERTS, _dot3(hf, wr_ref[...]) + br_ref[...], NEG_INF)
    vals, idxs = [], []
    for _ in range(TOP_K):
        mx = jnp.max(logits, axis=-1, keepdims=True)
        idx = jnp.min(jnp.where(logits == mx, lane, LANES), axis=-1, keepdims=True)
        vals.append(mx)
        idxs.append(idx)
        logits = jnp.where(lane == idx, NEG_INF, logits)
    ex = [jnp.exp(v - vals[0]) for v in vals]
    den = ex[0] + ex[1] + ex[2] + ex[3]
    te = jnp.zeros((rows, LANES), jnp.int32)
    tw = jnp.zeros((rows, LANES), F32)
    for k in range(TOP_K):
        te = jnp.where(lane == k, idxs[k], te)
        tw = jnp.where(lane == k, ex[k] / den, tw)
    te_ref[...] = te
    tw_ref[...] = tw


def _post_mixer(x, o_attn, y_ssm, gate, mod, w_attn_out, w_ssm_out, w_o, norm2_w, w_router, b_router,
                rows=256):
    b, l, d = x.shape
    bb, bl, rows = _row_blocks(b, l, rows)
    nl = l // bl
    tok = lambda w: pl.BlockSpec((bb, bl, w), lambda i, j: (i, j, 0))
    flat = lambda w: pl.BlockSpec((rows, w), lambda i, j: (i * nl + j, 0))
    modc = lambda c: pl.BlockSpec((bb, 1, d), lambda i, j: (i, 0, c))
    full = lambda shape: pl.BlockSpec(shape, lambda i, j: (0,) * len(shape))
    return pl.pallas_call(
        _post_mixer_body,
        grid=(b // bb, nl),
        in_specs=[tok(d), tok(d), tok(D_INNER), flat(2 * d), modc(2), modc(3), modc(4),
                  full((d, d)), full((D_INNER, d)), full((d, d)), full((1, d)),
                  full((d, LANES)), full((1, LANES))],
        out_specs=[tok(d), flat(d), flat(LANES), flat(LANES)],
        out_shape=[jax.ShapeDtypeStruct((b, l, d), F32), jax.ShapeDtypeStruct((b * l, d), F32),
                   jax.ShapeDtypeStruct((b * l, LANES), jnp.int32), jax.ShapeDtypeStruct((b * l, LANES), F32)],
        compiler_params=_cparams(("arbitrary", "arbitrary")),
        name="post_mixer",
    )(x, o_attn, y_ssm, gate, mod, mod, mod, w_attn_out, w_ssm_out, w_o, norm2_w.reshape(1, d),
      w_router, b_router)


MOE_ROWS = 256
SPLIT_COLS = 512


def _split_gate_up_body(w_ref, g_ref, u_ref, t_scr):
    half = SPLIT_COLS // 2
    for kc in range(w_ref.shape[1] // LANES):
        ks = slice(kc * LANES, (kc + 1) * LANES)
        t_scr[...] = w_ref[0, ks, :].T
        g_ref[0, :, ks] = t_scr[pl.ds(0, half, stride=2), :].astype(BF16)
        u_ref[0, :, ks] = t_scr[pl.ds(1, half, stride=2), :].astype(BF16)


def _split_gate_up(w_gu):
    e, k, n2 = w_gu.shape
    half = SPLIT_COLS // 2
    out = jax.ShapeDtypeStruct((e, n2 // 2, k), BF16)
    return pl.pallas_call(
        _split_gate_up_body,
        grid=(e, n2 // SPLIT_COLS),
        in_specs=[pl.BlockSpec((1, k, SPLIT_COLS), lambda i, j: (i, 0, j))],
        out_specs=[pl.BlockSpec((1, half, k), lambda i, j: (i, j, 0)),
                   pl.BlockSpec((1, half, k), lambda i, j: (i, j, 0))],
        out_shape=[out, out],
        scratch_shapes=[pltpu.VMEM((SPLIT_COLS, LANES), F32)],
        compiler_params=_cparams(("arbitrary", "arbitrary")),
        name="split_gate_up",
    )(w_gu)


def _experts_body(be_ref, x_ref, wg_ref, wu_ref, bg_ref, bu_ref, wd_ref, bd_ref, o_ref):
    x = x_ref[...].astype(BF16)
    gate = jnp.minimum(_dot_nt(x, wg_ref[0]) + bg_ref[0], SWIGLU_LIMIT)
    up = jnp.clip(_dot_nt(x, wu_ref[0]) + bu_ref[0], -SWIGLU_LIMIT, SWIGLU_LIMIT)
    act = (up + 1.0) * (gate * _sigmoid(SWIGLU_ALPHA * gate))
    o_ref[...] = _dot(act.astype(BF16), wd_ref[0]) + bd_ref[0]


def _experts(block_e, xg, w_g, w_u, b_g, b_u, w_d, b_d):
    p, d = xg.shape
    ff = w_g.shape[1]
    wspec = lambda shape: pl.BlockSpec(shape, lambda i, be: (be[i], 0, 0))
    grid_spec = pltpu.PrefetchScalarGridSpec(
        num_scalar_prefetch=1,
        grid=(p // MOE_ROWS,),
        in_specs=[pl.BlockSpec((MOE_ROWS, d), lambda i, be: (i, 0)),
                  wspec((1, ff, d)), wspec((1, ff, d)), wspec((1, 1, ff)), wspec((1, 1, ff)),
                  wspec((1, ff, d)), wspec((1, 1, d))],
        out_specs=pl.BlockSpec((MOE_ROWS, d), lambda i, be: (i, 0)),
    )
    return pl.pallas_call(
        _experts_body,
        grid_spec=grid_spec,
        out_shape=jax.ShapeDtypeStruct((p, d), F32),
        compiler_params=_cparams(("arbitrary",)),
        name="moe_experts",
    )(block_e, xg, w_g, w_u, b_g, b_u, w_d, b_d)


SC_WORKERS = 32
SC_CHUNK = 64


def _sc_gather_rows(table, idx):
    b = idx.shape[0]
    d = table.shape[1]
    per_w = b // SC_WORKERS
    assert b % (SC_WORKERS * SC_CHUNK) == 0
    mesh = plsc.VectorSubcoreMesh(core_axis_name="c", subcore_axis_name="s")

    @functools.partial(
        pl.kernel, mesh=mesh, out_type=jax.ShapeDtypeStruct((b, d), table.dtype),
        scratch_types=[pltpu.VMEM((SC_CHUNK,), jnp.int32), pltpu.VMEM((SC_CHUNK, d), table.dtype),
                       pltpu.SemaphoreType.DMA],
        name="sc_gather_rows")
    def gather(table_hbm, idx_hbm, out_hbm, idx_v, rows_v, sem):
        wid = lax.axis_index("s") * 2 + lax.axis_index("c")
        base = wid * per_w

        @pl.loop(0, per_w // SC_CHUNK)
        def _(ci):
            off = pl.multiple_of(base + ci * SC_CHUNK, SC_CHUNK)
            pltpu.sync_copy(idx_hbm.at[pl.ds(off, SC_CHUNK)], idx_v)
            pltpu.async_copy(table_hbm.at[idx_v], rows_v, sem).wait()
            pltpu.sync_copy(rows_v, out_hbm.at[pl.ds(off, SC_CHUNK)])

    return gather(table, idx)


def _route(top_e):
    t, k = top_e.shape
    s = t * k
    e_flat = top_e.reshape(-1)
    onehot = (e_flat[:, None] == jnp.arange(N_EXPERTS, dtype=jnp.int32)[None, :]).astype(jnp.int32)
    csum = jnp.cumsum(onehot, axis=0)
    rank = jnp.sum(onehot * csum, axis=1) - 1
    counts = csum[-1]
    padded = (counts + MOE_ROWS - 1) // MOE_ROWS * MOE_ROWS
    pend = jnp.cumsum(padded)
    dest = (pend - padded)[e_flat] + rank
    n_blocks = -(-s // MOE_ROWS) + N_EXPERTS
    row_tok = jnp.zeros((n_blocks * MOE_ROWS,), jnp.int32).at[dest].set(
        jnp.arange(s, dtype=jnp.int32) // k)
    block_row0 = jnp.arange(n_blocks, dtype=jnp.int32) * MOE_ROWS
    block_e = jnp.minimum(jnp.sum((pend[None, :] <= block_row0[:, None]).astype(jnp.int32), axis=1),
                          N_EXPERTS - 1)
    return dest.reshape(t, k), row_tok, block_e


def _final_body(x_ref, ex_ref, tw_ref, g2_ref, nf_ref, y_ref):
    bb, bl, d = x_ref.shape
    tw = tw_ref[...]
    moe = ex_ref[0] * tw[:, 0:1]
    for k in range(1, TOP_K):
        moe = moe + ex_ref[k] * tw[:, k:k + 1]
    x = x_ref[...] + g2_ref[...] * moe.reshape(bb, bl, d)
    ms = jnp.mean(x * x, axis=-1, keepdims=True)
    y_ref[...] = x * lax.rsqrt(ms + EPS) * nf_ref[...]


def _final(x2, expert_out, top_w, row0, mod, norm_f_w, rows=512):
    b, l, d = x2.shape
    bb, bl, rows = _row_blocks(b, l, rows)
    nl = l // bl
    assert row0 % rows == 0
    blk0 = row0 // rows
    return pl.pallas_call(
        _final_body,
        grid=(b // bb, nl),
        in_specs=[pl.BlockSpec((bb, bl, d), lambda i, j: (i, j, 0)),
                  pl.BlockSpec((TOP_K, rows, d), lambda i, j: (0, blk0 + i * nl + j, 0)),
                  pl.BlockSpec((rows, LANES), lambda i, j: (blk0 + i * nl + j, 0)),
                  pl.BlockSpec((bb, 1, d), lambda i, j: (i, 0, 5)),
                  pl.BlockSpec((1, d), lambda i, j: (0, 0))],
        out_specs=pl.BlockSpec((bb, bl, d), lambda i, j: (i, j, 0)),
        out_shape=jax.ShapeDtypeStruct((b, l, d), F32),
        compiler_params=_cparams(("arbitrary", "arbitrary")),
        name="final_norm",
    )(x2, expert_out, top_w, mod, norm_f_w.reshape(1, d))


def _prep_in_weights(w_in, b_fgate, dt_bias):
    aw = N_HEADS_A * HEAD_DIM_A
    o = np.cumsum([0, aw, aw, aw, N_HEADS_A, D_INNER, CONV_DIM, N_HEADS_S, D_MODEL, D_MODEL])
    w_t = w_in.T
    rows = lambda i: w_t[o[i]:o[i + 1]]
    w_small = jnp.concatenate(
        [rows(3), rows(6), jnp.zeros((LANES - N_HEADS_A - N_HEADS_S, D_MODEL), F32)], axis=0).T
    b_small = jnp.concatenate(
        [b_fgate, dt_bias, jnp.zeros((LANES - N_HEADS_A - N_HEADS_S,), F32)]).reshape(1, LANES)
    big = {"q": rows(0), "k": rows(1), "v": rows(2), "z": rows(4), "xbc": rows(5),
           "gate": w_t[o[7]:o[9]]}
    return {n: w.astype(BF16) for n, w in big.items()}, w_small, b_small


def kernel(x_prompt, x_sample, cache_k, cache_v, cache_logf, state_ssm, state_conv, page_table, c_prompt, c_sample, w_cond, b_cond, norm1_w, w_in, b_fgate, conv_w, conv_b, dt_bias, a_log, d_skip, ssm_norm_w, w_attn_out, w_ssm_out, w_o, norm2_w, w_router, b_router, w_gate_up, b_gate_up, w_down, b_down, norm_f_w):
    assert w_in.shape[0] == 1, "single-layer trunk"
    bp, lp, d = x_prompt.shape
    bs, ls, _ = x_sample.shape
    tp, ts = bp * lp, bs * ls
    aw = N_HEADS_A * HEAD_DIM_A
    c_all = jnp.concatenate([c_prompt, c_sample, jnp.zeros((-(bp + bs) % 8, d), F32)], axis=0)
    mod = _adaln_mod(c_all, w_cond[0], b_cond[0])
    mod_p = mod[:bp].reshape(bp, 1, 6 * d)
    mod_s = mod[bp:bp + bs].reshape(bs, 1, 6 * d)

    w_big, w_small, b_small = _prep_in_weights(w_in[0], b_fgate[0], dt_bias[0])
    pad_lanes = LANES - N_HEADS_A - N_HEADS_S
    alog_lane = jnp.concatenate([jnp.zeros((N_HEADS_A,), F32), a_log[0], jnp.zeros((pad_lanes,), F32)]).reshape(1, LANES)
    dskip_exp = jnp.repeat(d_skip[0], HEAD_DIM_S).reshape(1, D_INNER)
    ssm_nw = ssm_norm_w[0].reshape(1, D_INNER)
    conv_b2 = conv_b[0].reshape(1, CONV_DIM)
    wa, ws, wo = w_attn_out[0].astype(BF16), w_ssm_out[0].astype(BF16), w_o[0].astype(BF16)
    w_router_pad = jnp.pad(w_router[0], ((0, 0), (0, LANES - N_EXPERTS)))
    b_router_pad = jnp.pad(b_router[0], (0, LANES - N_EXPERTS)).reshape(1, LANES)

    def in_proj(x, m):
        h, small = _norm_in(x, m, norm1_w[0], w_small, b_small)
        return {n: _matmul_nt(h, w) for n, w in w_big.items()}, small

    proj_p, small_p = in_proj(x_prompt, mod_p)
    proj_s, small_s = in_proj(x_sample, mod_s)

    fcum = _cumsum_t(small_p.reshape(tp // PAGE_SIZE, PAGE_SIZE, LANES), lp // PAGE_SIZE, True)
    fcum = fcum.reshape(bp, N_HEADS_A // 2, 2, lp)
    o_p = _fox_prompt(proj_p["q"].reshape(bp, lp, aw), proj_p["k"].astype(BF16).reshape(bp, lp, aw),
                      proj_p["v"].astype(BF16).reshape(bp, lp, aw), fcum, tile=min(512, lp))
    xbc_p = proj_p["xbc"].reshape(bp, lp, CONV_DIM)
    y_p, st_p = _ssd_prompt(xbc_p, proj_p["z"].reshape(bp, lp, D_INNER), small_p.reshape(bp, lp, LANES),
                            conv_w[0], conv_b2, alog_lane, dskip_exp, ssm_nw)

    n_phys = cache_k.shape[1]
    k_t = jnp.transpose(cache_k[0], (0, 2, 3, 1)).reshape(n_phys, aw, PAGE_SIZE)
    v_t = jnp.transpose(cache_v[0], (0, 2, 3, 1)).reshape(n_phys, aw, PAGE_SIZE)
    lf_t = jnp.swapaxes(cache_logf[0], 1, 2).reshape(n_phys * N_HEADS_A, PAGE_SIZE)
    c_pages = _cumsum_lanes(lf_t, rows=512).reshape(n_phys, N_HEADS_A, PAGE_SIZE)
    lf_new = jnp.pad(small_s.reshape(bs, ls, LANES), ((0, 0), (0, PAGE_SIZE - ls), (0, 0)))
    c_new = _cumsum_t(lf_new, 1, False)
    n_pages = page_table.shape[1]
    o_s = _fox_sample(proj_s["q"].reshape(bs, ls, aw), proj_s["k"].reshape(bs, ls, aw),
                      proj_s["v"].reshape(bs, ls, aw), c_new, k_t, v_t, c_pages, page_table,
                      pages_per_step=min(8, n_pages))
    xbc_s = proj_s["xbc"].reshape(bs, ls, CONV_DIM)
    y_s, st_s = _ssd_sample(xbc_s, state_conv[0], proj_s["z"].reshape(bs, ls, D_INNER),
                            small_s.reshape(bs, ls, LANES), state_ssm[0].reshape(bs, D_INNER, D_STATE),
                            conv_w[0], conv_b2, alog_lane, dskip_exp, ssm_nw)

    post = lambda x, o, y, gate, m: _post_mixer(x, o, y, gate, m, wa, ws, wo, norm2_w[0],
                                                w_router_pad, b_router_pad)
    x2_p, hf_p, te_p, tw_p = post(x_prompt, o_p, y_p, proj_p["gate"], mod_p)
    x2_s, hf_s, te_s, tw_s = post(x_sample, o_s, y_s, proj_s["gate"], mod_s)

    hf = jnp.concatenate([hf_p, hf_s], axis=0)
    top_e = jnp.concatenate([te_p[:, :TOP_K], te_s[:, :TOP_K]], axis=0)
    top_w = jnp.concatenate([tw_p, tw_s], axis=0)
    dest, row_tok, block_e = _route(top_e)
    w_gate_t, w_up_t = _split_gate_up(w_gate_up[0])
    rows = _experts(block_e, _sc_gather_rows(hf, row_tok), w_gate_t, w_up_t,
                    b_gate_up[0][:, None, 0::2], b_gate_up[0][:, None, 1::2],
                    w_down[0].astype(BF16), b_down[0][:, None, :])
    expert_out = _sc_gather_rows(rows, dest.T.reshape(-1)).reshape(TOP_K, tp + ts, d)

    y_prompt = _final(x2_p, expert_out, top_w, 0, mod_p, norm_f_w)
    y_sample = _final(x2_s, expert_out, top_w, tp, mod_s, norm_f_w)

    def state_rows(k, v, small, st, xbc, conv0, b, l):
        conv_new = jnp.concatenate([conv0, xbc], axis=1)[:, -(CONV_K - 1):]
        return (k.reshape(1, b, l, N_HEADS_A, HEAD_DIM_A), v.reshape(1, b, l, N_HEADS_A, HEAD_DIM_A),
                small[:, :N_HEADS_A].reshape(1, b, l, N_HEADS_A),
                st.reshape(1, b, N_HEADS_S, HEAD_DIM_S, D_STATE), conv_new[None])

    sp = state_rows(proj_p["k"], proj_p["v"], small_p, st_p, xbc_p,
                    jnp.zeros((bp, CONV_K - 1, CONV_DIM), F32), bp, lp)
    ss = state_rows(proj_s["k"], proj_s["v"], small_s, st_s, xbc_s, state_conv[0], bs, ls)
    return (y_prompt, y_sample) + sp + ss
```

```python
import functools

import numpy as np
import jax
import jax.numpy as jnp
from jax import lax
from jax.experimental import pallas as pl
from jax.experimental.pallas import tpu as pltpu
from jax.experimental.pallas import tpu_sc as plsc

F32 = jnp.float32
BF16 = jnp.bfloat16

D_MODEL = 1024
N_HEADS_A = 16
HEAD_DIM_A = 64
PAGE_SIZE = 128
D_INNER = 2048
HEAD_DIM_S = 64
N_HEADS_S = 32
N_GROUPS_S = 4
HEADS_PER_GROUP = N_HEADS_S // N_GROUPS_S
GROUP_W = D_INNER // N_GROUPS_S
D_STATE = 128
CONV_K = 4
CONV_DIM = D_INNER + 2 * N_GROUPS_S * D_STATE
SSD_CHUNK = 128
N_EXPERTS = 32
TOP_K = 4
D_FF = D_MODEL
SWIGLU_LIMIT = 7.0
SWIGLU_ALPHA = 1.702
EPS = 1e-6

LANES = 128
DT_LANE0 = N_HEADS_A
NEG_INF = float("-inf")
LOG2E = 1.4426950408889634
VMEM_LIMIT = 56 * 1024 * 1024


def _cparams(sem):
    return pltpu.CompilerParams(dimension_semantics=sem, vmem_limit_bytes=VMEM_LIMIT)


def _split3(x):
    hi = x.astype(BF16)
    r = x - hi.astype(F32)
    mid = r.astype(BF16)
    lo = (r - mid.astype(F32)).astype(BF16)
    return hi, mid, lo


def _dot(a, b):
    return jnp.dot(a, b, preferred_element_type=F32)


def _dot_nt(a, b):
    return lax.dot_general(a, b, (((1,), (1,)), ((), ())), preferred_element_type=F32)


def _dot_sel_left(sel_bf16, x):
    hi, mid, lo = _split3(x)
    return _dot(sel_bf16, hi) + _dot(sel_bf16, mid) + _dot(sel_bf16, lo)


def _dot3(a, b):
    ah = a.astype(BF16)
    am = (a - ah.astype(F32)).astype(BF16)
    bh = b.astype(BF16)
    bm = (b - bh.astype(F32)).astype(BF16)
    return _dot(ah, bh) + _dot(ah, bm) + _dot(am, bh)


def _tri_lower(n):
    r = lax.broadcasted_iota(jnp.int32, (n, n), 0)
    c = lax.broadcasted_iota(jnp.int32, (n, n), 1)
    return r >= c


def _sigmoid(x):
    return 1.0 / (1.0 + jnp.exp(-x))


def _silu(x):
    return x * _sigmoid(x)


def _mod_body(c_ref, w_ref, b_ref, o_ref):
    c = c_ref[...]
    o_ref[...] = _dot3(_silu(c), w_ref[...]) + b_ref[...]


def _adaln_mod(c_all, w_cond, b_cond):
    rows, d = c_all.shape
    n = w_cond.shape[1]
    tn = 1024
    return pl.pallas_call(
        _mod_body,
        grid=(n // tn,),
        in_specs=[pl.BlockSpec((rows, d), lambda j: (0, 0)),
                  pl.BlockSpec((d, tn), lambda j: (0, j)),
                  pl.BlockSpec((1, tn), lambda j: (0, j))],
        out_specs=pl.BlockSpec((rows, tn), lambda j: (0, j)),
        out_shape=jax.ShapeDtypeStruct((rows, n), F32),
        compiler_params=_cparams(("arbitrary",)),
        name="adaln_mod",
    )(c_all, w_cond, b_cond.reshape(1, n))


def _norm_in_body(x_ref, sh_ref, sc_ref, nw_ref, ws_ref, bs_ref, h_ref, sm_ref):
    x = x_ref[...]
    bb, bl, d = x.shape
    ms = jnp.mean(x * x, axis=-1, keepdims=True)
    y = x * lax.rsqrt(ms + EPS) * nw_ref[...]
    h = (y * (1.0 + sc_ref[...]) + sh_ref[...]).reshape(bb * bl, d)
    h_ref[...] = h.astype(BF16)
    sm = _dot3(h, ws_ref[...]) + bs_ref[...]
    lane = lax.broadcasted_iota(jnp.int32, sm.shape, 1)
    t = jnp.log(1.0 + jnp.exp(-jnp.abs(sm)))
    sm_ref[...] = jnp.where(lane < N_HEADS_A, jnp.minimum(sm, 0.0) - t, jnp.maximum(sm, 0.0) + t)


def _row_blocks(b, l, rows):
    rows = min(rows, b * l)
    if l >= rows:
        assert l % rows == 0
        return 1, rows, rows
    assert rows % l == 0 and b % (rows // l) == 0
    return rows // l, l, rows


def _norm_in(x, mod, norm_w, w_small, b_small, rows=512):
    b, l, d = x.shape
    bb, bl, rows = _row_blocks(b, l, rows)
    nl = l // bl
    grid = (b // bb, nl)
    return pl.pallas_call(
        _norm_in_body,
        grid=grid,
        in_specs=[pl.BlockSpec((bb, bl, d), lambda i, j: (i, j, 0)),
                  pl.BlockSpec((bb, 1, d), lambda i, j: (i, 0, 0)),
                  pl.BlockSpec((bb, 1, d), lambda i, j: (i, 0, 1)),
                  pl.BlockSpec((1, d), lambda i, j: (0, 0)),
                  pl.BlockSpec((d, LANES), lambda i, j: (0, 0)),
                  pl.BlockSpec((1, LANES), lambda i, j: (0, 0))],
        out_specs=[pl.BlockSpec((rows, d), lambda i, j: (i * nl + j, 0)),
                   pl.BlockSpec((rows, LANES), lambda i, j: (i * nl + j, 0))],
        out_shape=[jax.ShapeDtypeStruct((b * l, d), BF16),
                   jax.ShapeDtypeStruct((b * l, LANES), F32)],
        compiler_params=_cparams(("arbitrary", "arbitrary")),
        name="norm_in",
    )(x, mod, mod, norm_w.reshape(1, d), w_small, b_small)


def _mm_body(a_ref, w_ref, o_ref):
    o_ref[...] = _dot_nt(a_ref[...], w_ref[...])


def _matmul_nt(a, w_t, tm=512, tn=1024):
    m, k = a.shape
    n = w_t.shape[0]
    tn = min(tn, n)
    return pl.pallas_call(
        _mm_body,
        grid=(n // tn, m // tm),
        in_specs=[pl.BlockSpec((tm, k), lambda j, i: (i, 0)),
                  pl.BlockSpec((tn, k), lambda j, i: (j, 0))],
        out_specs=pl.BlockSpec((tm, tn), lambda j, i: (i, j)),
        out_shape=jax.ShapeDtypeStruct((m, n), F32),
        compiler_params=_cparams(("arbitrary", "arbitrary")),
        name="proj_matmul",
    )(a, w_t)


def _cumsum_t_body(x_ref, o_ref, pad_scr, carry_scr, *, width, carry):
    j = pl.program_id(1)

    @pl.when(j == 0)
    def _():
        pad_scr[...] = jnp.zeros_like(pad_scr)
        carry_scr[...] = jnp.zeros_like(carry_scr)

    if width == LANES:
        blk = x_ref[0]
    else:
        pad_scr[:, 0:width] = x_ref[0]
        blk = pad_scr[...]
    tri = _tri_lower(PAGE_SIZE).astype(BF16)
    cs = _dot_sel_left(tri, blk) + carry_scr[...]
    if carry:
        carry_scr[...] = cs[PAGE_SIZE - 1:PAGE_SIZE, :]
    o_ref[0] = cs.T[0:N_HEADS_A, :]


def _cumsum_t(x, nper, carry):
    n, p, width = x.shape
    assert p == PAGE_SIZE and n % nper == 0
    return pl.pallas_call(
        functools.partial(_cumsum_t_body, width=width, carry=carry),
        grid=(n // nper, nper),
        in_specs=[pl.BlockSpec((1, p, width), lambda i, j: (i * nper + j, 0, 0))],
        out_specs=pl.BlockSpec((1, N_HEADS_A, p), lambda i, j: (i, 0, j)),
        out_shape=jax.ShapeDtypeStruct((n // nper, N_HEADS_A, nper * p), F32),
        scratch_shapes=[pltpu.VMEM((p, LANES), F32), pltpu.VMEM((1, LANES), F32)],
        compiler_params=_cparams(("arbitrary", "arbitrary")),
        name="logf_cumsum",
    )(x)


def _cumsum_lanes_body(x_ref, o_ref):
    n = x_ref.shape[1]
    r = lax.broadcasted_iota(jnp.int32, (n, n), 0)
    c = lax.broadcasted_iota(jnp.int32, (n, n), 1)
    tri = (r <= c).astype(BF16)
    hi, mid, lo = _split3(x_ref[...])
    o_ref[...] = _dot(hi, tri) + _dot(mid, tri) + _dot(lo, tri)


def _cumsum_lanes(x, rows=1024):
    r, n = x.shape
    assert r % rows == 0
    return pl.pallas_call(
        _cumsum_lanes_body,
        grid=(r // rows,),
        in_specs=[pl.BlockSpec((rows, n), lambda i: (i, 0))],
        out_specs=pl.BlockSpec((rows, n), lambda i: (i, 0)),
        out_shape=jax.ShapeDtypeStruct((r, n), F32),
        compiler_params=_cparams(("arbitrary",)),
        name="page_logf_cumsum",
    )(x)


def _fox_prompt_body(it_ref, jt_ref, q_ref, k_ref, v_ref, f_ref, o_ref, qm_scr, m_scr, acc_scr, *,
                     tile, sub):
    t = pl.program_id(2)
    i = it_ref[t]
    j = jt_ref[t]
    lane = lax.broadcasted_iota(jnp.int32, (tile, LANES), 1)
    first = lane < HEAD_DIM_A

    @pl.when(j == 0)
    def _():
        q = q_ref[0] * (HEAD_DIM_A ** -0.5 * LOG2E)
        qm_scr[0] = jnp.where(first, q, 0.0).astype(BF16)
        qm_scr[1] = jnp.where(first, 0.0, q).astype(BF16)
        m_scr[...] = jnp.full_like(m_scr, -1e30)
        acc_scr[...] = jnp.zeros_like(acc_scr)

    def step(diagonal):
        one = jnp.ones((), BF16)
        first_s = lax.broadcasted_iota(jnp.int32, (sub, LANES), 1) < HEAD_DIM_A
        for kj in range(tile // sub):
            ks = slice(kj * sub, (kj + 1) * sub)
            kb = k_ref[0, ks, :]
            vb = v_ref[0, ks, :]
            v_augs = (jnp.where(first_s, vb, one), jnp.where(first_s, one, vb))
            for qi in range(tile // sub):
                if diagonal and kj > qi:
                    continue
                qs = slice(qi * sub, (qi + 1) * sub)
                for h in range(2):
                    s = _dot_nt(qm_scr[h, qs, :], kb) - f_ref[0, 0, h:h + 1, ks] * LOG2E
                    if diagonal and kj == qi:
                        row = lax.broadcasted_iota(jnp.int32, (sub, sub), 0)
                        col = lax.broadcasted_iota(jnp.int32, (sub, sub), 1)
                        s = jnp.where(col <= row, s, NEG_INF)
                    m_old = m_scr[h, qs, :]
                    m_new = jnp.maximum(m_old, jnp.max(s, axis=-1, keepdims=True))
                    alpha = jnp.exp2(m_old - m_new)
                    p = jnp.exp2(s - jnp.concatenate([m_new] * (sub // LANES), axis=1))
                    acc_scr[h, qs, :] = alpha * acc_scr[h, qs, :] + _dot(p.astype(BF16), v_augs[h])
                    m_scr[h, qs, :] = m_new

    @pl.when(j < i)
    def _():
        step(False)

    @pl.when(j == i)
    def _():
        step(True)
        a0 = acc_scr[0]
        a1 = acc_scr[1]
        den = jnp.where(first, pltpu.roll(a0, HEAD_DIM_A, 1), pltpu.roll(a1, HEAD_DIM_A, 1))
        o_ref[0] = jnp.where(first, a0, a1) / den


def _fox_prompt(q, k, v, fcum, tile=1024, sub=512):
    b, l, w = q.shape
    npair = w // LANES
    tile = min(tile, l)
    sub = min(sub, tile)
    assert l % tile == 0 and tile % sub == 0
    nt = l // tile
    it = np.concatenate([np.full(i + 1, i, np.int32) for i in range(nt)])
    jt = np.concatenate([np.arange(i + 1, dtype=np.int32) for i in range(nt)])
    grid_spec = pltpu.PrefetchScalarGridSpec(
        num_scalar_prefetch=2,
        grid=(b, npair, len(it)),
        in_specs=[pl.BlockSpec((1, tile, LANES), lambda bi, p, t, it, jt: (bi, it[t], p)),
                  pl.BlockSpec((1, tile, LANES), lambda bi, p, t, it, jt: (bi, jt[t], p)),
                  pl.BlockSpec((1, tile, LANES), lambda bi, p, t, it, jt: (bi, jt[t], p)),
                  pl.BlockSpec((1, 1, 2, tile), lambda bi, p, t, it, jt: (bi, p, 0, jt[t]))],
        out_specs=pl.BlockSpec((1, tile, LANES), lambda bi, p, t, it, jt: (bi, it[t], p)),
        scratch_shapes=[pltpu.VMEM((2, tile, LANES), BF16), pltpu.VMEM((2, tile, LANES), F32),
                        pltpu.VMEM((2, tile, LANES), F32)],
    )
    return pl.pallas_call(
        functools.partial(_fox_prompt_body, tile=tile, sub=sub),
        grid_spec=grid_spec,
        out_shape=jax.ShapeDtypeStruct((b, l, w), F32),
        compiler_params=_cparams(("arbitrary", "arbitrary", "arbitrary")),
        name="fox_prompt",
    )(jnp.asarray(it), jnp.asarray(jt), q, k, v, fcum)


def _fox_sample_body(pt_ref, q_ref, kn_ref, vn_ref, cn_ref, *rest, pages_per_step):
    pg = pages_per_step
    k_refs = rest[0:pg]
    v_refs = rest[pg:2 * pg]
    c_refs = rest[2 * pg:3 * pg]
    o_ref, qbd_scr, m_scr, l_scr, carry_scr, acc_scr = rest[3 * pg:]
    j = pl.program_id(1)
    nq = q_ref.shape[1]
    rows = N_HEADS_A * nq
    width = N_HEADS_A * HEAD_DIM_A

    @pl.when(j == 0)
    def _():
        q = q_ref[0] * (HEAD_DIM_A ** -0.5)
        qt = jnp.broadcast_to(q[None], (N_HEADS_A, nq, width)).reshape(rows, width)
        row = lax.broadcasted_iota(jnp.int32, (rows, width), 0)
        lane = lax.broadcasted_iota(jnp.int32, (rows, width), 1)
        own = (lane // HEAD_DIM_A) == (row // nq)
        qbd_scr[...] = jnp.where(own, qt, 0.0).astype(BF16)
        m_scr[...] = jnp.full_like(m_scr, -1e30)
        l_scr[...] = jnp.zeros_like(l_scr)
        carry_scr[...] = jnp.zeros_like(carry_scr)
        acc_scr[...] = jnp.zeros_like(acc_scr)

    def attend(blocks, valid, transposed):
        qbd = qbd_scr[...]
        carry = carry_scr[...]
        scores = []
        for kf, _, c16 in blocks:
            s = _dot(qbd, kf.astype(BF16)) if transposed else _dot_nt(qbd, kf.astype(BF16))
            nk = s.shape[1]
            cexp = jnp.broadcast_to(c16[:, None, :], (N_HEADS_A, nq, nk)).reshape(rows, nk)
            s = s - (cexp + carry)
            if valid is not None:
                s = jnp.where(valid, s, NEG_INF)
            scores.append(s)
            carry = carry + cexp[:, nk - 1:nk]
        carry_scr[...] = carry
        smax = scores[0]
        for s in scores[1:]:
            smax = jnp.maximum(smax, s)
        m_old = m_scr[...]
        m_new = jnp.maximum(m_old, jnp.max(smax, axis=-1, keepdims=True))
        alpha = jnp.exp(m_old - m_new)
        m_scr[...] = m_new
        psum = None
        pv = None
        for s, (_, vf, _) in zip(scores, blocks):
            p = jnp.exp(s - m_new)
            psum = p if psum is None else psum + p
            pb = p.astype(BF16)
            d = _dot_nt(pb, vf.astype(BF16)) if transposed else _dot(pb, vf.astype(BF16))
            pv = d if pv is None else pv + d
        l_scr[...] = alpha * l_scr[...] + jnp.sum(psum, axis=-1, keepdims=True)
        acc_scr[...] = acc_scr[...] * alpha + pv

    attend([(k_refs[g][0], v_refs[g][0], c_refs[g][0]) for g in range(pg)], None, True)

    @pl.when(j == pl.num_programs(1) - 1)
    def _():
        zpad = jnp.zeros((PAGE_SIZE - nq, width), F32)
        kpad = jnp.concatenate([kn_ref[0], zpad], axis=0)
        vpad = jnp.concatenate([vn_ref[0], zpad], axis=0)
        row = lax.broadcasted_iota(jnp.int32, (rows, PAGE_SIZE), 0)
        key = lax.broadcasted_iota(jnp.int32, (rows, PAGE_SIZE), 1)
        attend([(kpad, vpad, cn_ref[0])], key <= (row % nq), False)
        o = acc_scr[...] / l_scr[...]
        for h in range(N_HEADS_A):
            o_ref[0, :, h * HEAD_DIM_A:(h + 1) * HEAD_DIM_A] = (
                o[h * nq:(h + 1) * nq, h * HEAD_DIM_A:(h + 1) * HEAD_DIM_A])


def _fox_sample(q, k_new, v_new, c_new, cache_k, cache_v, c_pages, page_table, pages_per_step=8):
    b, nq, w = q.shape
    n_pages = page_table.shape[1]
    pg = pages_per_step
    assert n_pages % pg == 0
    rows = N_HEADS_A * nq

    def page_map(g):
        return lambda bi, j, pt: (pt[bi * n_pages + j * pg + g], 0, 0)

    seq_map = lambda bi, j, pt: (bi, 0, 0)
    in_specs = [pl.BlockSpec((1, nq, w), seq_map), pl.BlockSpec((1, nq, w), seq_map),
                pl.BlockSpec((1, nq, w), seq_map), pl.BlockSpec((1, N_HEADS_A, PAGE_SIZE), seq_map)]
    in_specs += [pl.BlockSpec((1, w, PAGE_SIZE), page_map(g)) for g in range(pg)]
    in_specs += [pl.BlockSpec((1, w, PAGE_SIZE), page_map(g)) for g in range(pg)]
    in_specs += [pl.BlockSpec((1, N_HEADS_A, PAGE_SIZE), page_map(g)) for g in range(pg)]
    grid_spec = pltpu.PrefetchScalarGridSpec(
        num_scalar_prefetch=1,
        grid=(b, n_pages // pg),
        in_specs=in_specs,
        out_specs=pl.BlockSpec((1, nq, w), seq_map),
        scratch_shapes=[pltpu.VMEM((rows, w), BF16), pltpu.VMEM((rows, 1), F32), pltpu.VMEM((rows, 1), F32),
                        pltpu.VMEM((rows, 1), F32), pltpu.VMEM((rows, w), F32)],
    )
    args = [q, k_new, v_new, c_new] + [cache_k] * pg + [cache_v] * pg + [c_pages] * pg
    return pl.pallas_call(
        functools.partial(_fox_sample_body, pages_per_step=pg),
        grid_spec=grid_spec,
        out_shape=jax.ShapeDtypeStruct((b, nq, w), F32),
        compiler_params=_cparams(("arbitrary", "arbitrary")),
        name="fox_sample",
    )(page_table.reshape(-1), *args)


def _causal_conv_silu(ext_scr, cw_ref, cb_ref, n):
    acc = cb_ref[...] + cw_ref[0:1, :] * ext_scr[8 - (CONV_K - 1):8 - (CONV_K - 1) + n, :]
    for i in range(1, CONV_K):
        acc = acc + cw_ref[i:i + 1, :] * ext_scr[8 - (CONV_K - 1) + i:8 - (CONV_K - 1) + i + n, :]
    return _silu(acc)


def _pair_select(first, a, b):
    return jnp.where(first, a, b)


def _ssd_prompt_body(xbc_ref, z_ref, sm_ref, cw_ref, cb_ref, alog_ref, dskip_ref, nw_ref,
                     y_ref, st_ref, st_scr, ext_scr):
    c = pl.program_id(1)
    q = SSD_CHUNK
    pair_w = 2 * HEAD_DIM_S

    @pl.when(c == 0)
    def _():
        st_scr[...] = jnp.zeros_like(st_scr)
        ext_scr[0:8, :] = jnp.zeros((8, CONV_DIM), F32)

    cur = xbc_ref[0]
    ext_scr[8:8 + q, :] = cur
    act = _causal_conv_silu(ext_scr, cw_ref, cb_ref, q)
    ext_scr[0:8, :] = cur[q - 8:q, :]

    sm = sm_ref[0]
    a_lane = -jnp.exp(alog_ref[...])
    tri_mask = _tri_lower(q)
    a_all = _dot_sel_left(tri_mask.astype(BF16), sm * a_lane)
    a_t = a_all.T
    dt_t = sm.T
    w_t = jnp.exp(a_t[:, q - 1:q] - a_t) * dt_t
    ea_all = jnp.exp(a_all)
    lane = lax.broadcasted_iota(jnp.int32, (1, pair_w), 1)
    first = lane < HEAD_DIM_S

    for g in range(N_GROUPS_S):
        bc = act[:, D_INNER + g * D_STATE:D_INNER + (g + 1) * D_STATE]
        cc = act[:, D_INNER + (N_GROUPS_S + g) * D_STATE:D_INNER + (N_GROUPS_S + g + 1) * D_STATE]
        cb = _dot_nt(cc.astype(BF16), bc.astype(BF16))
        bc_t = bc.T
        gated = []
        ssq = jnp.zeros((q, 1), F32)
        for pp in range(HEADS_PER_GROUP // 2):
            h0 = g * HEADS_PER_GROUP + 2 * pp
            col = h0 * HEAD_DIM_S
            x_pair = act[:, col:col + pair_w]
            s_pair = st_scr[:, col:col + pair_w]
            rhs = jnp.concatenate([x_pair.astype(BF16), s_pair.astype(BF16)], axis=0)
            ys, us, el = [], [], []
            for h in (h0, h0 + 1):
                li = DT_LANE0 + h
                seg = a_all[:, li:li + 1] - a_t[li:li + 1, :]
                dec = jnp.exp(jnp.where(tri_mask, seg, NEG_INF))
                m_h = cb * dec * dt_t[li:li + 1, :]
                lhs = jnp.concatenate([m_h, cc * ea_all[:, li:li + 1]], axis=1).astype(BF16)
                ys.append(_dot(lhs, rhs))
                us.append(_dot((bc_t * w_t[li:li + 1, :]).astype(BF16), x_pair.astype(BF16)))
                el.append(ea_all[q - 1:q, li:li + 1])
            y_pair = _pair_select(first, ys[0], ys[1]) + x_pair * dskip_ref[:, col:col + pair_w]
            st_scr[:, col:col + pair_w] = (s_pair * _pair_select(first, el[0], el[1])
                                           + _pair_select(first, us[0], us[1]))
            gp = y_pair * _silu(z_ref[0, :, col:col + pair_w])
            ssq = ssq + jnp.sum(gp * gp, axis=-1, keepdims=True)
            gated.append((col, gp))
        rs = lax.rsqrt(ssq / GROUP_W + EPS)
        for col, gp in gated:
            y_ref[0, :, col:col + pair_w] = (gp * rs * nw_ref[:, col:col + pair_w]).astype(y_ref.dtype)

    @pl.when(c == pl.num_programs(1) - 1)
    def _():
        for blk in range(D_INNER // pair_w):
            st_ref[0, blk * pair_w:(blk + 1) * pair_w, :] = st_scr[:, blk * pair_w:(blk + 1) * pair_w].T


def _ssd_prompt(xbc, z, small, conv_w, conv_b, alog_lane, dskip_exp, norm_w):
    b, l, _ = xbc.shape
    q = SSD_CHUNK
    full = lambda shape: pl.BlockSpec(shape, lambda bi, c: (0,) * len(shape))
    return pl.pallas_call(
        _ssd_prompt_body,
        grid=(b, l // q),
        in_specs=[pl.BlockSpec((1, q, CONV_DIM), lambda bi, c: (bi, c, 0)),
                  pl.BlockSpec((1, q, D_INNER), lambda bi, c: (bi, c, 0)),
                  pl.BlockSpec((1, q, LANES), lambda bi, c: (bi, c, 0)),
                  full((CONV_K, CONV_DIM)), full((1, CONV_DIM)), full((1, LANES)),
                  full((1, D_INNER)), full((1, D_INNER))],
        out_specs=[pl.BlockSpec((1, q, D_INNER), lambda bi, c: (bi, c, 0)),
                   pl.BlockSpec((1, D_INNER, D_STATE), lambda bi, c: (bi, 0, 0))],
        out_shape=[jax.ShapeDtypeStruct((b, l, D_INNER), BF16),
                   jax.ShapeDtypeStruct((b, D_INNER, D_STATE), F32)],
        scratch_shapes=[pltpu.VMEM((D_STATE, D_INNER), F32), pltpu.VMEM((8 + q, CONV_DIM), F32)],
        compiler_params=_cparams(("arbitrary", "arbitrary")),
        name="ssd_prompt",
    )(xbc, z, small, conv_w, conv_b, alog_lane, dskip_exp, norm_w)


def _ssd_sample_body(xbc_ref, c0_ref, z_ref, sm_ref, st_ref, cw_ref, cb_ref, alog_ref, dskip_ref, nw_ref,
                     y_ref, so_ref, ext_scr):
    l = xbc_ref.shape[1]
    pair_w = 2 * HEAD_DIM_S
    ext_scr[0:8, :] = jnp.zeros((8, CONV_DIM), F32)
    ext_scr[8 - (CONV_K - 1):8, :] = c0_ref[0]
    ext_scr[8:8 + l, :] = xbc_ref[0]
    act = _causal_conv_silu(ext_scr, cw_ref, cb_ref, l)

    sm = sm_ref[0]
    dta = sm * (-jnp.exp(alog_ref[...]))
    trow = lax.broadcasted_iota(jnp.int32, (l, LANES), 0)
    a_c = jnp.zeros((l, LANES), F32)
    for s in range(l):
        a_c = a_c + jnp.where(trow >= s, dta[s:s + 1, :], 0.0)
    lane = lax.broadcasted_iota(jnp.int32, (1, pair_w), 1)
    first = lane < HEAD_DIM_S

    def expand(v):
        cols = []
        for pp in range(N_HEADS_S // 2):
            li = DT_LANE0 + 2 * pp
            cols.append(_pair_select(first, v[:, li:li + 1], v[:, li + 1:li + 2]))
        return jnp.concatenate(cols, axis=1)

    a_x = expand(a_c)
    dt_x = expand(sm)
    ea_x = jnp.exp(a_x)
    w_x = jnp.exp(a_x[l - 1:l, :] - a_x) * dt_x
    xs = act[:, 0:D_INNER]
    zpad = jnp.zeros((PAGE_SIZE - l, D_STATE), F32)
    trow_x = lax.broadcasted_iota(jnp.int32, (l, GROUP_W), 0)

    xw_pad = jnp.concatenate([xs * w_x, jnp.zeros((PAGE_SIZE - l, D_INNER), F32)], axis=0)
    ea_last = jnp.exp(a_c[l - 1:l, :])
    rowsel = lax.broadcasted_iota(jnp.int32, (pair_w, 1), 0) < HEAD_DIM_S

    for g in range(N_GROUPS_S):
        gc = g * GROUP_W
        bc = act[:, D_INNER + g * D_STATE:D_INNER + (g + 1) * D_STATE]
        cc = act[:, D_INNER + (N_GROUPS_S + g) * D_STATE:D_INNER + (N_GROUPS_S + g + 1) * D_STATE]
        b_pad = jnp.concatenate([bc, zpad], axis=0).astype(BF16)
        cb = _dot_nt(cc.astype(BF16), b_pad)
        s_g = st_ref[0, gc:gc + GROUP_W, :]
        y = _dot_nt(cc.astype(BF16), s_g.astype(BF16)) * ea_x[:, gc:gc + GROUP_W]
        x_g = xs[:, gc:gc + GROUP_W]
        a_g = a_x[:, gc:gc + GROUP_W]
        dt_g = dt_x[:, gc:gc + GROUP_W]
        for s in range(l):
            dec = jnp.exp(jnp.where(trow_x >= s, a_g - a_g[s:s + 1, :], NEG_INF))
            y = y + cb[:, s:s + 1] * dec * (dt_g[s:s + 1, :] * x_g[s:s + 1, :])
        y = y + x_g * dskip_ref[:, gc:gc + GROUP_W]
        gp = y * _silu(z_ref[0, :, gc:gc + GROUP_W])
        rs = lax.rsqrt(jnp.sum(gp * gp, axis=-1, keepdims=True) / GROUP_W + EPS)
        y_ref[0, :, gc:gc + GROUP_W] = gp * rs * nw_ref[:, gc:gc + GROUP_W]
        for pp in range(HEADS_PER_GROUP // 2):
            h0 = g * HEADS_PER_GROUP + 2 * pp
            col = h0 * HEAD_DIM_S
            li = DT_LANE0 + h0
            u = _dot(xw_pad[:, col:col + pair_w].T.astype(BF16), b_pad)
            e_col = jnp.where(rowsel, ea_last[:, li:li + 1], ea_last[:, li + 1:li + 2])
            so_ref[0, col:col + pair_w, :] = st_ref[0, col:col + pair_w, :] * e_col + u


def _ssd_sample(xbc, conv0, z, small, state, conv_w, conv_b, alog_lane, dskip_exp, norm_w):
    b, l, _ = xbc.shape
    full = lambda shape: pl.BlockSpec(shape, lambda bi: (0,) * len(shape))
    seq = lambda shape: pl.BlockSpec(shape, lambda bi: (bi, 0, 0))
    return pl.pallas_call(
        _ssd_sample_body,
        grid=(b,),
        in_specs=[seq((1, l, CONV_DIM)), seq((1, CONV_K - 1, CONV_DIM)), seq((1, l, D_INNER)),
                  seq((1, l, LANES)), seq((1, D_INNER, D_STATE)),
                  full((CONV_K, CONV_DIM)), full((1, CONV_DIM)), full((1, LANES)),
                  full((1, D_INNER)), full((1, D_INNER))],
        out_specs=[seq((1, l, D_INNER)), seq((1, D_INNER, D_STATE))],
        out_shape=[jax.ShapeDtypeStruct((b, l, D_INNER), F32),
                   jax.ShapeDtypeStruct((b, D_INNER, D_STATE), F32)],
        scratch_shapes=[pltpu.VMEM((8 + l, CONV_DIM), F32)],
        compiler_params=_cparams(("arbitrary",)),
        name="ssd_sample",
    )(xbc, conv0, z, small, state, conv_w, conv_b, alog_lane, dskip_exp, norm_w)


def _post_mixer_body(x_ref, oa_ref, ys_ref, gate_ref, g1_ref, sh2_ref, sc2_ref, wa_ref, ws_ref, wo_ref,
                     n2_ref, wr_ref, br_ref, x2_ref, h_ref, te_ref, tw_ref):
    bb, bl, d = x_ref.shape
    rows = bb * bl
    oa = oa_ref[...].reshape(rows, d).astype(BF16)
    ys = ys_ref[...].reshape(rows, D_INNER).astype(BF16)
    y_attn = _dot(oa, wa_ref[...])
    y_ssm = _dot(ys, ws_ref[...])
    gate = gate_ref[...]
    merged = _sigmoid(gate[:, 0:d]) * y_attn + _sigmoid(gate[:, d:2 * d]) * y_ssm
    mix = _dot(merged.astype(BF16), wo_ref[...])
    x2 = x_ref[...] + g1_ref[...] * mix.reshape(bb, bl, d)
    x2_ref[...] = x2
    ms = jnp.mean(x2 * x2, axis=-1, keepdims=True)
    hf = (x2 * lax.rsqrt(ms + EPS) * n2_ref[...]) * (1.0 + sc2_ref[...]) + sh2_ref[...]
    hf = hf.reshape(rows, d)
    h_ref[...] = hf
    lane = lax.broadcasted_iota(jnp.int32, (rows, LANES), 1)
    logits = jnp.where(lane < N_EXPERTS, _dot3(hf, wr_ref[...]) + br_ref[...], NEG_INF)
    vals, idxs = [], []
    for _ in range(TOP_K):
        mx = jnp.max(logits, axis=-1, keepdims=True)
        idx = jnp.min(jnp.where(logits == mx, lane, LANES), axis=-1, keepdims=True)
        vals.append(mx)
        idxs.append(idx)
        logits = jnp.where(lane == idx, NEG_INF, logits)
    ex = [jnp.exp(v - vals[0]) for v in vals]
    den = ex[0] + ex[1] + ex[2] + ex[3]
    te = jnp.zeros((rows, LANES), jnp.int32)
    tw = jnp.zeros((rows, LANES), F32)
    for k in range(TOP_K):
        te = jnp.where(lane == k, idxs[k], te)
        tw = jnp.where(lane == k, ex[k] / den, tw)
    te_ref[...] = te
    tw_ref[...] = tw


def _post_mixer(x, o_attn, y_ssm, gate, mod, w_attn_out, w_ssm_out, w_o, norm2_w, w_router, b_router,
                rows=256):
    b, l, d = x.shape
    bb, bl, rows = _row_blocks(b, l, rows)
    nl = l // bl
    tok = lambda w: pl.BlockSpec((bb, bl, w), lambda i, j: (i, j, 0))
    flat = lambda w: pl.BlockSpec((rows, w), lambda i, j: (i * nl + j, 0))
    modc = lambda c: pl.BlockSpec((bb, 1, d), lambda i, j: (i, 0, c))
    full = lambda shape: pl.BlockSpec(shape, lambda i, j: (0,) * len(shape))
    return pl.pallas_call(
        _post_mixer_body,
        grid=(b // bb, nl),
        in_specs=[tok(d), tok(d), tok(D_INNER), flat(2 * d), modc(2), modc(3), modc(4),
                  full((d, d)), full((D_INNER, d)), full((d, d)), full((1, d)),
                  full((d, LANES)), full((1, LANES))],
        out_specs=[tok(d), flat(d), flat(LANES), flat(LANES)],
        out_shape=[jax.ShapeDtypeStruct((b, l, d), F32), jax.ShapeDtypeStruct((b * l, d), F32),
                   jax.ShapeDtypeStruct((b * l, LANES), jnp.int32), jax.ShapeDtypeStruct((b * l, LANES), F32)],
        compiler_params=_cparams(("arbitrary", "arbitrary")),
        name="post_mixer",
    )(x, o_attn, y_ssm, gate, mod, mod, mod, w_attn_out, w_ssm_out, w_o, norm2_w.reshape(1, d),
      w_router, b_router)


MOE_ROWS = 256
SPLIT_COLS = 512


def _experts_body(be_ref, nb_ref, x_ref, wgu_ref, bg_ref, bu_ref, wdn_ref, bd_ref, o_ref,
                  wg_scr, wu_scr, wd_scr, t_scr):
    i = pl.program_id(0)
    d = x_ref.shape[1]
    half = SPLIT_COLS // 2

    @pl.when(jnp.logical_or(i == 0, be_ref[i] != be_ref[jnp.maximum(i - 1, 0)]))
    def _():
        for cb in range(wgu_ref.shape[2] // SPLIT_COLS):
            for kc in range(d // LANES):
                ks = slice(kc * LANES, (kc + 1) * LANES)
                t_scr[kc] = wgu_ref[0, ks, cb * SPLIT_COLS:(cb + 1) * SPLIT_COLS].T
                rows = slice(cb * half, (cb + 1) * half)
                wg_scr[rows, ks] = t_scr[kc, pl.ds(0, half, stride=2), :].astype(BF16)
                wu_scr[rows, ks] = t_scr[kc, pl.ds(1, half, stride=2), :].astype(BF16)
        wd_scr[...] = wdn_ref[0].astype(BF16)

    @pl.when(i < nb_ref[0])
    def _():
        x = x_ref[...].astype(BF16)
        gate = jnp.minimum(_dot_nt(x, wg_scr[...]) + bg_ref[0], SWIGLU_LIMIT)
        up = jnp.clip(_dot_nt(x, wu_scr[...]) + bu_ref[0], -SWIGLU_LIMIT, SWIGLU_LIMIT)
        act = (up + 1.0) * (gate * _sigmoid(SWIGLU_ALPHA * gate))
        o_ref[...] = _dot(act.astype(BF16), wd_scr[...]) + bd_ref[0]

    @pl.when(i >= nb_ref[0])
    def _():
        o_ref[...] = jnp.zeros_like(o_ref)


def _experts(block_e, n_used, xg, w_gu, b_g, b_u, w_d, b_d):
    p, d = xg.shape
    ff = w_d.shape[1]
    wspec = lambda shape: pl.BlockSpec(shape, lambda i, be, nb: (be[i], 0, 0))
    grid_spec = pltpu.PrefetchScalarGridSpec(
        num_scalar_prefetch=2,
        grid=(p // MOE_ROWS,),
        in_specs=[pl.BlockSpec((MOE_ROWS, d), lambda i, be, nb: (i, 0)),
                  wspec((1, d, 2 * ff)), wspec((1, 1, ff)), wspec((1, 1, ff)),
                  wspec((1, ff, d)), wspec((1, 1, d))],
        out_specs=pl.BlockSpec((MOE_ROWS, d), lambda i, be, nb: (i, 0)),
        scratch_shapes=[pltpu.VMEM((ff, d), BF16), pltpu.VMEM((ff, d), BF16), pltpu.VMEM((ff, d), BF16),
                        pltpu.VMEM((d // LANES, SPLIT_COLS, LANES), F32)],
    )
    return pl.pallas_call(
        _experts_body,
        grid_spec=grid_spec,
        out_shape=jax.ShapeDtypeStruct((p, d), F32),
        compiler_params=_cparams(("arbitrary",)),
        name="moe_experts",
    )(block_e, n_used, xg, w_gu, b_g, b_u, w_d, b_d)


SC_WORKERS = 32
SC_CHUNK = 64


def _sc_gather_rows(table, idx):
    b = idx.shape[0]
    d = table.shape[1]
    per_w = b // SC_WORKERS
    assert b % (SC_WORKERS * SC_CHUNK) == 0
    mesh = plsc.VectorSubcoreMesh(core_axis_name="c", subcore_axis_name="s")

    @functools.partial(
        pl.kernel, mesh=mesh, out_type=jax.ShapeDtypeStruct((b, d), table.dtype),
        scratch_types=[pltpu.VMEM((SC_CHUNK,), jnp.int32), pltpu.VMEM((SC_CHUNK, d), table.dtype),
                       pltpu.SemaphoreType.DMA],
        name="sc_gather_rows")
    def gather(table_hbm, idx_hbm, out_hbm, idx_v, rows_v, sem):
        wid = lax.axis_index("s") * 2 + lax.axis_index("c")
        base = wid * per_w

        @pl.loop(0, per_w // SC_CHUNK)
        def _(ci):
            off = pl.multiple_of(base + ci * SC_CHUNK, SC_CHUNK)
            pltpu.sync_copy(idx_hbm.at[pl.ds(off, SC_CHUNK)], idx_v)
            pltpu.async_copy(table_hbm.at[idx_v], rows_v, sem).wait()
            pltpu.sync_copy(rows_v, out_hbm.at[pl.ds(off, SC_CHUNK)])

    return gather(table, idx)


def _route(top_e):
    t, k = top_e.shape
    s = t * k
    e_flat = top_e.reshape(-1)
    onehot = (e_flat[:, None] == jnp.arange(N_EXPERTS, dtype=jnp.int32)[None, :]).astype(jnp.int32)
    csum = jnp.cumsum(onehot, axis=0)
    rank = jnp.sum(onehot * csum, axis=1) - 1
    counts = csum[-1]
    padded = (counts + MOE_ROWS - 1) // MOE_ROWS * MOE_ROWS
    pend = jnp.cumsum(padded)
    dest = (pend - padded)[e_flat] + rank
    n_blocks = -(-s // MOE_ROWS) + N_EXPERTS
    row_tok = jnp.zeros((n_blocks * MOE_ROWS,), jnp.int32).at[dest].set(
        jnp.arange(s, dtype=jnp.int32) // k)
    block_row0 = jnp.arange(n_blocks, dtype=jnp.int32) * MOE_ROWS
    block_e = jnp.minimum(jnp.sum((pend[None, :] <= block_row0[:, None]).astype(jnp.int32), axis=1),
                          N_EXPERTS - 1)
    n_used = (pend[-1:] // MOE_ROWS).astype(jnp.int32)
    return dest.reshape(t, k), row_tok, block_e, n_used


def _final_body(x_ref, ex_ref, tw_ref, g2_ref, nf_ref, y_ref):
    bb, bl, d = x_ref.shape
    tw = tw_ref[...]
    moe = ex_ref[0] * tw[:, 0:1]
    for k in range(1, TOP_K):
        moe = moe + ex_ref[k] * tw[:, k:k + 1]
    x = x_ref[...] + g2_ref[...] * moe.reshape(bb, bl, d)
    ms = jnp.mean(x * x, axis=-1, keepdims=True)
    y_ref[...] = x * lax.rsqrt(ms + EPS) * nf_ref[...]


def _final(x2, expert_out, top_w, row0, mod, norm_f_w, rows=512):
    b, l, d = x2.shape
    bb, bl, rows = _row_blocks(b, l, rows)
    nl = l // bl
    assert row0 % rows == 0
    blk0 = row0 // rows
    return pl.pallas_call(
        _final_body,
        grid=(b // bb, nl),
        in_specs=[pl.BlockSpec((bb, bl, d), lambda i, j: (i, j, 0)),
                  pl.BlockSpec((TOP_K, rows, d), lambda i, j: (0, blk0 + i * nl + j, 0)),
                  pl.BlockSpec((rows, LANES), lambda i, j: (blk0 + i * nl + j, 0)),
                  pl.BlockSpec((bb, 1, d), lambda i, j: (i, 0, 5)),
                  pl.BlockSpec((1, d), lambda i, j: (0, 0))],
        out_specs=pl.BlockSpec((bb, bl, d), lambda i, j: (i, j, 0)),
        out_shape=jax.ShapeDtypeStruct((b, l, d), F32),
        compiler_params=_cparams(("arbitrary", "arbitrary")),
        name="final_norm",
    )(x2, expert_out, top_w, mod, norm_f_w.reshape(1, d))


def _prep_in_weights(w_in, b_fgate, dt_bias):
    aw = N_HEADS_A * HEAD_DIM_A
    o = np.cumsum([0, aw, aw, aw, N_HEADS_A, D_INNER, CONV_DIM, N_HEADS_S, D_MODEL, D_MODEL])
    w_t = w_in.T
    rows = lambda i: w_t[o[i]:o[i + 1]]
    w_small = jnp.concatenate(
        [rows(3), rows(6), jnp.zeros((LANES - N_HEADS_A - N_HEADS_S, D_MODEL), F32)], axis=0).T
    b_small = jnp.concatenate(
        [b_fgate, dt_bias, jnp.zeros((LANES - N_HEADS_A - N_HEADS_S,), F32)]).reshape(1, LANES)
    big = {"q": rows(0), "k": rows(1), "v": rows(2), "z": rows(4), "xbc": rows(5),
           "gate": w_t[o[7]:o[9]]}
    return {n: w.astype(BF16) for n, w in big.items()}, w_small, b_small


def kernel(x_prompt, x_sample, cache_k, cache_v, cache_logf, state_ssm, state_conv, page_table, c_prompt, c_sample, w_cond, b_cond, norm1_w, w_in, b_fgate, conv_w, conv_b, dt_bias, a_log, d_skip, ssm_norm_w, w_attn_out, w_ssm_out, w_o, norm2_w, w_router, b_router, w_gate_up, b_gate_up, w_down, b_down, norm_f_w):
    assert w_in.shape[0] == 1, "single-layer trunk"
    bp, lp, d = x_prompt.shape
    bs, ls, _ = x_sample.shape
    tp, ts = bp * lp, bs * ls
    aw = N_HEADS_A * HEAD_DIM_A
    c_all = jnp.concatenate([c_prompt, c_sample, jnp.zeros((-(bp + bs) % 8, d), F32)], axis=0)
    mod = _adaln_mod(c_all, w_cond[0], b_cond[0])
    mod_p = mod[:bp].reshape(bp, 1, 6 * d)
    mod_s = mod[bp:bp + bs].reshape(bs, 1, 6 * d)

    w_big, w_small, b_small = _prep_in_weights(w_in[0], b_fgate[0], dt_bias[0])
    pad_lanes = LANES - N_HEADS_A - N_HEADS_S
    alog_lane = jnp.concatenate([jnp.zeros((N_HEADS_A,), F32), a_log[0], jnp.zeros((pad_lanes,), F32)]).reshape(1, LANES)
    dskip_exp = jnp.repeat(d_skip[0], HEAD_DIM_S).reshape(1, D_INNER)
    ssm_nw = ssm_norm_w[0].reshape(1, D_INNER)
    conv_b2 = conv_b[0].reshape(1, CONV_DIM)
    wa, ws, wo = w_attn_out[0].astype(BF16), w_ssm_out[0].astype(BF16), w_o[0].astype(BF16)
    w_router_pad = jnp.pad(w_router[0], ((0, 0), (0, LANES - N_EXPERTS)))
    b_router_pad = jnp.pad(b_router[0], (0, LANES - N_EXPERTS)).reshape(1, LANES)

    def in_proj(x, m):
        h, small = _norm_in(x, m, norm1_w[0], w_small, b_small)
        return {n: _matmul_nt(h, w) for n, w in w_big.items()}, small

    proj_p, small_p = in_proj(x_prompt, mod_p)
    proj_s, small_s = in_proj(x_sample, mod_s)

    fcum = _cumsum_t(small_p.reshape(tp // PAGE_SIZE, PAGE_SIZE, LANES), lp // PAGE_SIZE, True)
    fcum = fcum.reshape(bp, N_HEADS_A // 2, 2, lp)
    o_p = _fox_prompt(proj_p["q"].reshape(bp, lp, aw), proj_p["k"].astype(BF16).reshape(bp, lp, aw),
                      proj_p["v"].astype(BF16).reshape(bp, lp, aw), fcum)
    xbc_p = proj_p["xbc"].reshape(bp, lp, CONV_DIM)
    y_p, st_p = _ssd_prompt(xbc_p, proj_p["z"].reshape(bp, lp, D_INNER), small_p.reshape(bp, lp, LANES),
                            conv_w[0], conv_b2, alog_lane, dskip_exp, ssm_nw)

    n_phys = cache_k.shape[1]
    k_t = jnp.transpose(cache_k[0], (0, 2, 3, 1)).reshape(n_phys, aw, PAGE_SIZE)
    v_t = jnp.transpose(cache_v[0], (0, 2, 3, 1)).reshape(n_phys, aw, PAGE_SIZE)
    lf_t = jnp.swapaxes(cache_logf[0], 1, 2).reshape(n_phys * N_HEADS_A, PAGE_SIZE)
    c_pages = _cumsum_lanes(lf_t, rows=512).reshape(n_phys, N_HEADS_A, PAGE_SIZE)
    lf_new = jnp.pad(small_s.reshape(bs, ls, LANES), ((0, 0), (0, PAGE_SIZE - ls), (0, 0)))
    c_new = _cumsum_t(lf_new, 1, False)
    n_pages = page_table.shape[1]
    o_s = _fox_sample(proj_s["q"].reshape(bs, ls, aw), proj_s["k"].reshape(bs, ls, aw),
                      proj_s["v"].reshape(bs, ls, aw), c_new, k_t, v_t, c_pages, page_table,
                      pages_per_step=min(8, n_pages))
    xbc_s = proj_s["xbc"].reshape(bs, ls, CONV_DIM)
    y_s, st_s = _ssd_sample(xbc_s, state_conv[0], proj_s["z"].reshape(bs, ls, D_INNER),
                            small_s.reshape(bs, ls, LANES), state_ssm[0].reshape(bs, D_INNER, D_STATE),
                            conv_w[0], conv_b2, alog_lane, dskip_exp, ssm_nw)

    post = lambda x, o, y, gate, m: _post_mixer(x, o, y, gate, m, wa, ws, wo, norm2_w[0],
                                                w_router_pad, b_router_pad)
    x2_p, hf_p, te_p, tw_p = post(x_prompt, o_p, y_p, proj_p["gate"], mod_p)
    x2_s, hf_s, te_s, tw_s = post(x_sample, o_s, y_s, proj_s["gate"], mod_s)

    hf = jnp.concatenate([hf_p, hf_s], axis=0)
    top_e = jnp.concatenate([te_p[:, :TOP_K], te_s[:, :TOP_K]], axis=0)
    top_w = jnp.concatenate([tw_p, tw_s], axis=0)
    dest, row_tok, block_e, n_used = _route(top_e)
    rows = _experts(block_e, n_used, _sc_gather_rows(hf, row_tok), w_gate_up[0],
                    b_gate_up[0][:, None, 0::2], b_gate_up[0][:, None, 1::2],
                    w_down[0], b_down[0][:, None, :])
    expert_out = _sc_gather_rows(rows, dest.T.reshape(-1)).reshape(TOP_K, tp + ts, d)

    y_prompt = _final(x2_p, expert_out, top_w, 0, mod_p, norm_f_w)
    y_sample = _final(x2_s, expert_out, top_w, tp, mod_s, norm_f_w)

    def state_rows(k, v, small, st, xbc, conv0, b, l):
        conv_new = jnp.concatenate([conv0, xbc], axis=1)[:, -(CONV_K - 1):]
        return (k.reshape(1, b, l, N_HEADS_A, HEAD_DIM_A), v.reshape(1, b, l, N_HEADS_A, HEAD_DIM_A),
                small[:, :N_HEADS_A].reshape(1, b, l, N_HEADS_A),
                st.reshape(1, b, N_HEADS_S, HEAD_DIM_S, D_STATE), conv_new[None])

    sp = state_rows(proj_p["k"], proj_p["v"], small_p, st_p, xbc_p,
                    jnp.zeros((bp, CONV_K - 1, CONV_DIM), F32), bp, lp)
    ss = state_rows(proj_s["k"], proj_s["v"], small_s, st_s, xbc_s, state_conv[0], bs, ls)
    return (y_prompt, y_sample) + sp + ss
```

```python
import functools

import numpy as np
import jax
import jax.numpy as jnp
from jax import lax
from jax.experimental import pallas as pl
from jax.experimental.pallas import tpu as pltpu
from jax.experimental.pallas import tpu_sc as plsc

F32 = jnp.float32
BF16 = jnp.bfloat16

D_MODEL = 1024
N_HEADS_A = 16
HEAD_DIM_A = 64
PAGE_SIZE = 128
D_INNER = 2048
HEAD_DIM_S = 64
N_HEADS_S = 32
N_GROUPS_S = 4
HEADS_PER_GROUP = N_HEADS_S // N_GROUPS_S
GROUP_W = D_INNER // N_GROUPS_S
D_STATE = 128
CONV_K = 4
CONV_DIM = D_INNER + 2 * N_GROUPS_S * D_STATE
SSD_CHUNK = 128
N_EXPERTS = 32
TOP_K = 4
D_FF = D_MODEL
SWIGLU_LIMIT = 7.0
SWIGLU_ALPHA = 1.702
EPS = 1e-6

LANES = 128
DT_LANE0 = N_HEADS_A
NEG_INF = float("-inf")
LOG2E = 1.4426950408889634
VMEM_LIMIT = 56 * 1024 * 1024


def _cparams(sem):
    return pltpu.CompilerParams(dimension_semantics=sem, vmem_limit_bytes=VMEM_LIMIT)


def _split3(x):
    hi = x.astype(BF16)
    r = x - hi.astype(F32)
    mid = r.astype(BF16)
    lo = (r - mid.astype(F32)).astype(BF16)
    return hi, mid, lo


def _dot(a, b):
    return jnp.dot(a, b, preferred_element_type=F32)


def _dot_nt(a, b):
    return lax.dot_general(a, b, (((1,), (1,)), ((), ())), preferred_element_type=F32)


def _dot_sel_left(sel_bf16, x):
    hi, mid, lo = _split3(x)
    return _dot(sel_bf16, hi) + _dot(sel_bf16, mid) + _dot(sel_bf16, lo)


def _dot3(a, b):
    ah = a.astype(BF16)
    am = (a - ah.astype(F32)).astype(BF16)
    bh = b.astype(BF16)
    bm = (b - bh.astype(F32)).astype(BF16)
    return _dot(ah, bh) + _dot(ah, bm) + _dot(am, bh)


def _tri_lower(n):
    r = lax.broadcasted_iota(jnp.int32, (n, n), 0)
    c = lax.broadcasted_iota(jnp.int32, (n, n), 1)
    return r >= c


def _sigmoid(x):
    return 1.0 / (1.0 + jnp.exp(-x))


def _silu(x):
    return x * _sigmoid(x)


def _mod_body(c_ref, w_ref, b_ref, o_ref):
    c = c_ref[...]
    o_ref[...] = _dot3(_silu(c), w_ref[...]) + b_ref[...]


def _adaln_mod(c_all, w_cond, b_cond):
    rows, d = c_all.shape
    n = w_cond.shape[1]
    tn = 1024
    return pl.pallas_call(
        _mod_body,
        grid=(n // tn,),
        in_specs=[pl.BlockSpec((rows, d), lambda j: (0, 0)),
                  pl.BlockSpec((d, tn), lambda j: (0, j)),
                  pl.BlockSpec((1, tn), lambda j: (0, j))],
        out_specs=pl.BlockSpec((rows, tn), lambda j: (0, j)),
        out_shape=jax.ShapeDtypeStruct((rows, n), F32),
        compiler_params=_cparams(("arbitrary",)),
        name="adaln_mod",
    )(c_all, w_cond, b_cond.reshape(1, n))


def _norm_in_body(x_ref, sh_ref, sc_ref, nw_ref, ws_ref, bs_ref, h_ref, sm_ref):
    x = x_ref[...]
    bb, bl, d = x.shape
    ms = jnp.mean(x * x, axis=-1, keepdims=True)
    y = x * lax.rsqrt(ms + EPS) * nw_ref[...]
    h = (y * (1.0 + sc_ref[...]) + sh_ref[...]).reshape(bb * bl, d)
    h_ref[...] = h.astype(BF16)
    sm = _dot3(h, ws_ref[...]) + bs_ref[...]
    lane = lax.broadcasted_iota(jnp.int32, sm.shape, 1)
    t = jnp.log(1.0 + jnp.exp(-jnp.abs(sm)))
    sm_ref[...] = jnp.where(lane < N_HEADS_A, jnp.minimum(sm, 0.0) - t, jnp.maximum(sm, 0.0) + t)


def _row_blocks(b, l, rows):
    rows = min(rows, b * l)
    if l >= rows:
        assert l % rows == 0
        return 1, rows, rows
    assert rows % l == 0 and b % (rows // l) == 0
    return rows // l, l, rows


def _norm_in(x, mod, norm_w, w_small, b_small, rows=512):
    b, l, d = x.shape
    bb, bl, rows = _row_blocks(b, l, rows)
    nl = l // bl
    grid = (b // bb, nl)
    return pl.pallas_call(
        _norm_in_body,
        grid=grid,
        in_specs=[pl.BlockSpec((bb, bl, d), lambda i, j: (i, j, 0)),
                  pl.BlockSpec((bb, 1, d), lambda i, j: (i, 0, 0)),
                  pl.BlockSpec((bb, 1, d), lambda i, j: (i, 0, 1)),
                  pl.BlockSpec((1, d), lambda i, j: (0, 0)),
                  pl.BlockSpec((d, LANES), lambda i, j: (0, 0)),
                  pl.BlockSpec((1, LANES), lambda i, j: (0, 0))],
        out_specs=[pl.BlockSpec((rows, d), lambda i, j: (i * nl + j, 0)),
                   pl.BlockSpec((rows, LANES), lambda i, j: (i * nl + j, 0))],
        out_shape=[jax.ShapeDtypeStruct((b * l, d), BF16),
                   jax.ShapeDtypeStruct((b * l, LANES), F32)],
        compiler_params=_cparams(("arbitrary", "arbitrary")),
        name="norm_in",
    )(x, mod, mod, norm_w.reshape(1, d), w_small, b_small)


def _mm_body(a_ref, w_ref, o_ref):
    o_ref[...] = _dot_nt(a_ref[...], w_ref[...])


def _matmul_nt(a, w_t, tm=512, tn=1024):
    m, k = a.shape
    n = w_t.shape[0]
    tn = min(tn, n)
    return pl.pallas_call(
        _mm_body,
        grid=(n // tn, m // tm),
        in_specs=[pl.BlockSpec((tm, k), lambda j, i: (i, 0)),
                  pl.BlockSpec((tn, k), lambda j, i: (j, 0))],
        out_specs=pl.BlockSpec((tm, tn), lambda j, i: (i, j)),
        out_shape=jax.ShapeDtypeStruct((m, n), F32),
        compiler_params=_cparams(("arbitrary", "arbitrary")),
        name="proj_matmul",
    )(a, w_t)


def _cumsum_t_body(x_ref, o_ref, pad_scr, carry_scr, *, width, carry):
    j = pl.program_id(1)

    @pl.when(j == 0)
    def _():
        pad_scr[...] = jnp.zeros_like(pad_scr)
        carry_scr[...] = jnp.zeros_like(carry_scr)

    if width == LANES:
        blk = x_ref[0]
    else:
        pad_scr[:, 0:width] = x_ref[0]
        blk = pad_scr[...]
    tri = _tri_lower(PAGE_SIZE).astype(BF16)
    cs = _dot_sel_left(tri, blk) + carry_scr[...]
    if carry:
        carry_scr[...] = cs[PAGE_SIZE - 1:PAGE_SIZE, :]
    o_ref[0] = cs.T[0:N_HEADS_A, :]


def _cumsum_t(x, nper, carry):
    n, p, width = x.shape
    assert p == PAGE_SIZE and n % nper == 0
    return pl.pallas_call(
        functools.partial(_cumsum_t_body, width=width, carry=carry),
        grid=(n // nper, nper),
        in_specs=[pl.BlockSpec((1, p, width), lambda i, j: (i * nper + j, 0, 0))],
        out_specs=pl.BlockSpec((1, N_HEADS_A, p), lambda i, j: (i, 0, j)),
        out_shape=jax.ShapeDtypeStruct((n // nper, N_HEADS_A, nper * p), F32),
        scratch_shapes=[pltpu.VMEM((p, LANES), F32), pltpu.VMEM((1, LANES), F32)],
        compiler_params=_cparams(("arbitrary", "arbitrary")),
        name="logf_cumsum",
    )(x)


def _cumsum_lanes_body(x_ref, o_ref):
    n = x_ref.shape[1]
    r = lax.broadcasted_iota(jnp.int32, (n, n), 0)
    c = lax.broadcasted_iota(jnp.int32, (n, n), 1)
    tri = (r <= c).astype(BF16)
    hi, mid, lo = _split3(x_ref[...])
    o_ref[...] = _dot(hi, tri) + _dot(mid, tri) + _dot(lo, tri)


def _cumsum_lanes(x, rows=1024):
    r, n = x.shape
    assert r % rows == 0
    return pl.pallas_call(
        _cumsum_lanes_body,
        grid=(r // rows,),
        in_specs=[pl.BlockSpec((rows, n), lambda i: (i, 0))],
        out_specs=pl.BlockSpec((rows, n), lambda i: (i, 0)),
        out_shape=jax.ShapeDtypeStruct((r, n), F32),
        compiler_params=_cparams(("arbitrary",)),
        name="page_logf_cumsum",
    )(x)


def _fox_prompt_body(it_ref, jt_ref, q_ref, k_ref, v_ref, f_ref, o_ref, qm_scr, m_scr, acc_scr, *,
                     tile, sub):
    t = pl.program_id(2)
    i = it_ref[t]
    j = jt_ref[t]
    lane = lax.broadcasted_iota(jnp.int32, (tile, LANES), 1)
    first = lane < HEAD_DIM_A

    @pl.when(j == 0)
    def _():
        q = q_ref[0] * (HEAD_DIM_A ** -0.5 * LOG2E)
        qm_scr[0] = jnp.where(first, q, 0.0).astype(BF16)
        qm_scr[1] = jnp.where(first, 0.0, q).astype(BF16)
        m_scr[...] = jnp.full_like(m_scr, -1e30)
        acc_scr[...] = jnp.zeros_like(acc_scr)

    def step(diagonal):
        one = jnp.ones((), BF16)
        first_s = lax.broadcasted_iota(jnp.int32, (sub, LANES), 1) < HEAD_DIM_A
        for kj in range(tile // sub):
            ks = slice(kj * sub, (kj + 1) * sub)
            kb = k_ref[0, ks, :]
            vb = v_ref[0, ks, :]
            v_augs = (jnp.where(first_s, vb, one), jnp.where(first_s, one, vb))
            for qi in range(tile // sub):
                if diagonal and kj > qi:
                    continue
                qs = slice(qi * sub, (qi + 1) * sub)
                for h in range(2):
                    s = _dot_nt(qm_scr[h, qs, :], kb) - f_ref[0, 0, h:h + 1, ks] * LOG2E
                    if diagonal and kj == qi:
                        row = lax.broadcasted_iota(jnp.int32, (sub, sub), 0)
                        col = lax.broadcasted_iota(jnp.int32, (sub, sub), 1)
                        s = jnp.where(col <= row, s, NEG_INF)
                    m_old = m_scr[h, qs, :]
                    m_new = jnp.maximum(m_old, jnp.max(s, axis=-1, keepdims=True))
                    alpha = jnp.exp2(m_old - m_new)
                    p = jnp.exp2(s - jnp.concatenate([m_new] * (sub // LANES), axis=1))
                    acc_scr[h, qs, :] = alpha * acc_scr[h, qs, :] + _dot(p.astype(BF16), v_augs[h])
                    m_scr[h, qs, :] = m_new

    @pl.when(j < i)
    def _():
        step(False)

    @pl.when(j == i)
    def _():
        step(True)
        a0 = acc_scr[0]
        a1 = acc_scr[1]
        den = jnp.where(first, pltpu.roll(a0, HEAD_DIM_A, 1), pltpu.roll(a1, HEAD_DIM_A, 1))
        o_ref[0] = jnp.where(first, a0, a1) / den


def _fox_prompt(q, k, v, fcum, tile=1024, sub=512):
    b, l, w = q.shape
    npair = w // LANES
    tile = min(tile, l)
    sub = min(sub, tile)
    assert l % tile == 0 and tile % sub == 0
    nt = l // tile
    it = np.concatenate([np.full(i + 1, i, np.int32) for i in range(nt)])
    jt = np.concatenate([np.arange(i + 1, dtype=np.int32) for i in range(nt)])
    grid_spec = pltpu.PrefetchScalarGridSpec(
        num_scalar_prefetch=2,
        grid=(b, npair, len(it)),
        in_specs=[pl.BlockSpec((1, tile, LANES), lambda bi, p, t, it, jt: (bi, it[t], p)),
                  pl.BlockSpec((1, tile, LANES), lambda bi, p, t, it, jt: (bi, jt[t], p)),
                  pl.BlockSpec((1, tile, LANES), lambda bi, p, t, it, jt: (bi, jt[t], p)),
                  pl.BlockSpec((1, 1, 2, tile), lambda bi, p, t, it, jt: (bi, p, 0, jt[t]))],
        out_specs=pl.BlockSpec((1, tile, LANES), lambda bi, p, t, it, jt: (bi, it[t], p)),
        scratch_shapes=[pltpu.VMEM((2, tile, LANES), BF16), pltpu.VMEM((2, tile, LANES), F32),
                        pltpu.VMEM((2, tile, LANES), F32)],
    )
    return pl.pallas_call(
        functools.partial(_fox_prompt_body, tile=tile, sub=sub),
        grid_spec=grid_spec,
        out_shape=jax.ShapeDtypeStruct((b, l, w), F32),
        compiler_params=_cparams(("arbitrary", "arbitrary", "arbitrary")),
        name="fox_prompt",
    )(jnp.asarray(it), jnp.asarray(jt), q, k, v, fcum)


def _fox_sample_body(pt_ref, q_ref, kn_ref, vn_ref, cn_ref, *rest, pages_per_step):
    pg = pages_per_step
    k_refs = rest[0:pg]
    v_refs = rest[pg:2 * pg]
    c_refs = rest[2 * pg:3 * pg]
    o_ref, qbd_scr, m_scr, l_scr, carry_scr, acc_scr = rest[3 * pg:]
    j = pl.program_id(1)
    nq = q_ref.shape[1]
    rows = N_HEADS_A * nq
    width = N_HEADS_A * HEAD_DIM_A

    @pl.when(j == 0)
    def _():
        q = q_ref[0] * (HEAD_DIM_A ** -0.5)
        qt = jnp.broadcast_to(q[None], (N_HEADS_A, nq, width)).reshape(rows, width)
        row = lax.broadcasted_iota(jnp.int32, (rows, width), 0)
        lane = lax.broadcasted_iota(jnp.int32, (rows, width), 1)
        own = (lane // HEAD_DIM_A) == (row // nq)
        qbd_scr[...] = jnp.where(own, qt, 0.0).astype(BF16)
        m_scr[...] = jnp.full_like(m_scr, -1e30)
        l_scr[...] = jnp.zeros_like(l_scr)
        carry_scr[...] = jnp.zeros_like(carry_scr)
        acc_scr[...] = jnp.zeros_like(acc_scr)

    def attend(blocks, valid, transposed):
        qbd = qbd_scr[...]
        carry = carry_scr[...]
        scores = []
        for kf, _, c16 in blocks:
            s = _dot(qbd, kf.astype(BF16)) if transposed else _dot_nt(qbd, kf.astype(BF16))
            nk = s.shape[1]
            cexp = jnp.broadcast_to(c16[:, None, :], (N_HEADS_A, nq, nk)).reshape(rows, nk)
            s = s - (cexp + carry)
            if valid is not None:
                s = jnp.where(valid, s, NEG_INF)
            scores.append(s)
            carry = carry + cexp[:, nk - 1:nk]
        carry_scr[...] = carry
        smax = scores[0]
        for s in scores[1:]:
            smax = jnp.maximum(smax, s)
        m_old = m_scr[...]
        m_new = jnp.maximum(m_old, jnp.max(smax, axis=-1, keepdims=True))
        alpha = jnp.exp(m_old - m_new)
        m_scr[...] = m_new
        psum = None
        pv = None
        for s, (_, vf, _) in zip(scores, blocks):
            p = jnp.exp(s - m_new)
            psum = p if psum is None else psum + p
            pb = p.astype(BF16)
            d = _dot_nt(pb, vf.astype(BF16)) if transposed else _dot(pb, vf.astype(BF16))
            pv = d if pv is None else pv + d
        l_scr[...] = alpha * l_scr[...] + jnp.sum(psum, axis=-1, keepdims=True)
        acc_scr[...] = acc_scr[...] * alpha + pv

    attend([(k_refs[g][0], v_refs[g][0], c_refs[g][0]) for g in range(pg)], None, True)

    @pl.when(j == pl.num_programs(1) - 1)
    def _():
        zpad = jnp.zeros((PAGE_SIZE - nq, width), F32)
        kpad = jnp.concatenate([kn_ref[0], zpad], axis=0)
        vpad = jnp.concatenate([vn_ref[0], zpad], axis=0)
        row = lax.broadcasted_iota(jnp.int32, (rows, PAGE_SIZE), 0)
        key = lax.broadcasted_iota(jnp.int32, (rows, PAGE_SIZE), 1)
        attend([(kpad, vpad, cn_ref[0])], key <= (row % nq), False)
        o = acc_scr[...] / l_scr[...]
        for h in range(N_HEADS_A):
            o_ref[0, :, h * HEAD_DIM_A:(h + 1) * HEAD_DIM_A] = (
                o[h * nq:(h + 1) * nq, h * HEAD_DIM_A:(h + 1) * HEAD_DIM_A])


def _fox_sample(q, k_new, v_new, c_new, cache_k, cache_v, c_pages, page_table, pages_per_step=8):
    b, nq, w = q.shape
    n_pages = page_table.shape[1]
    pg = pages_per_step
    assert n_pages % pg == 0
    rows = N_HEADS_A * nq

    def page_map(g):
        return lambda bi, j, pt: (pt[bi * n_pages + j * pg + g], 0, 0)

    seq_map = lambda bi, j, pt: (bi, 0, 0)
    in_specs = [pl.BlockSpec((1, nq, w), seq_map), pl.BlockSpec((1, nq, w), seq_map),
                pl.BlockSpec((1, nq, w), seq_map), pl.BlockSpec((1, N_HEADS_A, PAGE_SIZE), seq_map)]
    in_specs += [pl.BlockSpec((1, w, PAGE_SIZE), page_map(g)) for g in range(pg)]
    in_specs += [pl.BlockSpec((1, w, PAGE_SIZE), page_map(g)) for g in range(pg)]
    in_specs += [pl.BlockSpec((1, N_HEADS_A, PAGE_SIZE), page_map(g)) for g in range(pg)]
    grid_spec = pltpu.PrefetchScalarGridSpec(
        num_scalar_prefetch=1,
        grid=(b, n_pages // pg),
        in_specs=in_specs,
        out_specs=pl.BlockSpec((1, nq, w), seq_map),
        scratch_shapes=[pltpu.VMEM((rows, w), BF16), pltpu.VMEM((rows, 1), F32), pltpu.VMEM((rows, 1), F32),
                        pltpu.VMEM((rows, 1), F32), pltpu.VMEM((rows, w), F32)],
    )
    args = [q, k_new, v_new, c_new] + [cache_k] * pg + [cache_v] * pg + [c_pages] * pg
    return pl.pallas_call(
        functools.partial(_fox_sample_body, pages_per_step=pg),
        grid_spec=grid_spec,
        out_shape=jax.ShapeDtypeStruct((b, nq, w), F32),
        compiler_params=_cparams(("arbitrary", "arbitrary")),
        name="fox_sample",
    )(page_table.reshape(-1), *args)


def _causal_conv_silu(ext_scr, cw_ref, cb_ref, n):
    acc = cb_ref[...] + cw_ref[0:1, :] * ext_scr[8 - (CONV_K - 1):8 - (CONV_K - 1) + n, :]
    for i in range(1, CONV_K):
        acc = acc + cw_ref[i:i + 1, :] * ext_scr[8 - (CONV_K - 1) + i:8 - (CONV_K - 1) + i + n, :]
    return _silu(acc)


def _pair_select(first, a, b):
    return jnp.where(first, a, b)


def _ssd_prompt_body(xbc_ref, z_ref, sm_ref, cw_ref, cb_ref, alog_ref, dskip_ref, nw_ref,
                     y_ref, st_ref, st_scr, ext_scr):
    c = pl.program_id(1)
    q = SSD_CHUNK
    pair_w = 2 * HEAD_DIM_S

    @pl.when(c == 0)
    def _():
        st_scr[...] = jnp.zeros_like(st_scr)
        ext_scr[0:8, :] = jnp.zeros((8, CONV_DIM), F32)

    cur = xbc_ref[0]
    ext_scr[8:8 + q, :] = cur
    act = _causal_conv_silu(ext_scr, cw_ref, cb_ref, q)
    ext_scr[0:8, :] = cur[q - 8:q, :]

    sm = sm_ref[0]
    a_lane = -jnp.exp(alog_ref[...])
    tri_mask = _tri_lower(q)
    a_all = _dot_sel_left(tri_mask.astype(BF16), sm * a_lane)
    a_t = a_all.T
    dt_t = sm.T
    w_t = jnp.exp(a_t[:, q - 1:q] - a_t) * dt_t
    ea_all = jnp.exp(a_all)
    lane = lax.broadcasted_iota(jnp.int32, (1, pair_w), 1)
    first = lane < HEAD_DIM_S

    for g in range(N_GROUPS_S):
        bc = act[:, D_INNER + g * D_STATE:D_INNER + (g + 1) * D_STATE]
        cc = act[:, D_INNER + (N_GROUPS_S + g) * D_STATE:D_INNER + (N_GROUPS_S + g + 1) * D_STATE]
        cb = _dot_nt(cc.astype(BF16), bc.astype(BF16))
        bc_t = bc.T
        gated = []
        ssq = jnp.zeros((q, 1), F32)
        for pp in range(HEADS_PER_GROUP // 2):
            h0 = g * HEADS_PER_GROUP + 2 * pp
            col = h0 * HEAD_DIM_S
            x_pair = act[:, col:col + pair_w]
            s_pair = st_scr[:, col:col + pair_w]
            rhs = jnp.concatenate([x_pair.astype(BF16), s_pair.astype(BF16)], axis=0)
            ys, us, el = [], [], []
            for h in (h0, h0 + 1):
                li = DT_LANE0 + h
                seg = a_all[:, li:li + 1] - a_t[li:li + 1, :]
                dec = jnp.exp(jnp.where(tri_mask, seg, NEG_INF))
                m_h = cb * dec * dt_t[li:li + 1, :]
                lhs = jnp.concatenate([m_h, cc * ea_all[:, li:li + 1]], axis=1).astype(BF16)
                ys.append(_dot(lhs, rhs))
                us.append(_dot((bc_t * w_t[li:li + 1, :]).astype(BF16), x_pair.astype(BF16)))
                el.append(ea_all[q - 1:q, li:li + 1])
            y_pair = _pair_select(first, ys[0], ys[1]) + x_pair * dskip_ref[:, col:col + pair_w]
            st_scr[:, col:col + pair_w] = (s_pair * _pair_select(first, el[0], el[1])
                                           + _pair_select(first, us[0], us[1]))
            gp = y_pair * _silu(z_ref[0, :, col:col + pair_w])
            ssq = ssq + jnp.sum(gp * gp, axis=-1, keepdims=True)
            gated.append((col, gp))
        rs = lax.rsqrt(ssq / GROUP_W + EPS)
        for col, gp in gated:
            y_ref[0, :, col:col + pair_w] = (gp * rs * nw_ref[:, col:col + pair_w]).astype(y_ref.dtype)

    @pl.when(c == pl.num_programs(1) - 1)
    def _():
        for blk in range(D_INNER // pair_w):
            st_ref[0, blk * pair_w:(blk + 1) * pair_w, :] = st_scr[:, blk * pair_w:(blk + 1) * pair_w].T


def _ssd_prompt(xbc, z, small, conv_w, conv_b, alog_lane, dskip_exp, norm_w):
    b, l, _ = xbc.shape
    q = SSD_CHUNK
    full = lambda shape: pl.BlockSpec(shape, lambda bi, c: (0,) * len(shape))
    return pl.pallas_call(
        _ssd_prompt_body,
        grid=(b, l // q),
        in_specs=[pl.BlockSpec((1, q, CONV_DIM), lambda bi, c: (bi, c, 0)),
                  pl.BlockSpec((1, q, D_INNER), lambda bi, c: (bi, c, 0)),
                  pl.BlockSpec((1, q, LANES), lambda bi, c: (bi, c, 0)),
                  full((CONV_K, CONV_DIM)), full((1, CONV_DIM)), full((1, LANES)),
                  full((1, D_INNER)), full((1, D_INNER))],
        out_specs=[pl.BlockSpec((1, q, D_INNER), lambda bi, c: (bi, c, 0)),
                   pl.BlockSpec((1, D_INNER, D_STATE), lambda bi, c: (bi, 0, 0))],
        out_shape=[jax.ShapeDtypeStruct((b, l, D_INNER), BF16),
                   jax.ShapeDtypeStruct((b, D_INNER, D_STATE), F32)],
        scratch_shapes=[pltpu.VMEM((D_STATE, D_INNER), F32), pltpu.VMEM((8 + q, CONV_DIM), F32)],
        compiler_params=_cparams(("arbitrary", "arbitrary")),
        name="ssd_prompt",
    )(xbc, z, small, conv_w, conv_b, alog_lane, dskip_exp, norm_w)


def _ssd_sample_body(xbc_ref, c0_ref, z_ref, sm_ref, st_ref, cw_ref, cb_ref, alog_ref, dskip_ref, nw_ref,
                     y_ref, so_ref, ext_scr):
    l = xbc_ref.shape[1]
    pair_w = 2 * HEAD_DIM_S
    ext_scr[0:8, :] = jnp.zeros((8, CONV_DIM), F32)
    ext_scr[8 - (CONV_K - 1):8, :] = c0_ref[0]
    ext_scr[8:8 + l, :] = xbc_ref[0]
    act = _causal_conv_silu(ext_scr, cw_ref, cb_ref, l)

    sm = sm_ref[0]
    dta = sm * (-jnp.exp(alog_ref[...]))
    trow = lax.broadcasted_iota(jnp.int32, (l, LANES), 0)
    a_c = jnp.zeros((l, LANES), F32)
    for s in range(l):
        a_c = a_c + jnp.where(trow >= s, dta[s:s + 1, :], 0.0)
    lane = lax.broadcasted_iota(jnp.int32, (1, pair_w), 1)
    first = lane < HEAD_DIM_S

    def expand(v):
        cols = []
        for pp in range(N_HEADS_S // 2):
            li = DT_LANE0 + 2 * pp
            cols.append(_pair_select(first, v[:, li:li + 1], v[:, li + 1:li + 2]))
        return jnp.concatenate(cols, axis=1)

    a_x = expand(a_c)
    dt_x = expand(sm)
    ea_x = jnp.exp(a_x)
    w_x = jnp.exp(a_x[l - 1:l, :] - a_x) * dt_x
    xs = act[:, 0:D_INNER]
    zpad = jnp.zeros((PAGE_SIZE - l, D_STATE), F32)
    trow_x = lax.broadcasted_iota(jnp.int32, (l, GROUP_W), 0)

    xw_pad = jnp.concatenate([xs * w_x, jnp.zeros((PAGE_SIZE - l, D_INNER), F32)], axis=0)
    ea_last = jnp.exp(a_c[l - 1:l, :])
    rowsel = lax.broadcasted_iota(jnp.int32, (pair_w, 1), 0) < HEAD_DIM_S

    for g in range(N_GROUPS_S):
        gc = g * GROUP_W
        bc = act[:, D_INNER + g * D_STATE:D_INNER + (g + 1) * D_STATE]
        cc = act[:, D_INNER + (N_GROUPS_S + g) * D_STATE:D_INNER + (N_GROUPS_S + g + 1) * D_STATE]
        b_pad = jnp.concatenate([bc, zpad], axis=0).astype(BF16)
        cb = _dot_nt(cc.astype(BF16), b_pad)
        s_g = st_ref[0, gc:gc + GROUP_W, :]
        y = _dot_nt(cc.astype(BF16), s_g.astype(BF16)) * ea_x[:, gc:gc + GROUP_W]
        x_g = xs[:, gc:gc + GROUP_W]
        a_g = a_x[:, gc:gc + GROUP_W]
        dt_g = dt_x[:, gc:gc + GROUP_W]
        for s in range(l):
            dec = jnp.exp(jnp.where(trow_x >= s, a_g - a_g[s:s + 1, :], NEG_INF))
            y = y + cb[:, s:s + 1] * dec * (dt_g[s:s + 1, :] * x_g[s:s + 1, :])
        y = y + x_g * dskip_ref[:, gc:gc + GROUP_W]
        gp = y * _silu(z_ref[0, :, gc:gc + GROUP_W])
        rs = lax.rsqrt(jnp.sum(gp * gp, axis=-1, keepdims=True) / GROUP_W + EPS)
        y_ref[0, :, gc:gc + GROUP_W] = gp * rs * nw_ref[:, gc:gc + GROUP_W]
        for pp in range(HEADS_PER_GROUP // 2):
            h0 = g * HEADS_PER_GROUP + 2 * pp
            col = h0 * HEAD_DIM_S
            li = DT_LANE0 + h0
            u = _dot(xw_pad[:, col:col + pair_w].T.astype(BF16), b_pad)
            e_col = jnp.where(rowsel, ea_last[:, li:li + 1], ea_last[:, li + 1:li + 2])
            so_ref[0, col:col + pair_w, :] = st_ref[0, col:col + pair_w, :] * e_col + u


def _ssd_sample(xbc, conv0, z, small, state, conv_w, conv_b, alog_lane, dskip_exp, norm_w):
    b, l, _ = xbc.shape
    full = lambda shape: pl.BlockSpec(shape, lambda bi: (0,) * len(shape))
    seq = lambda shape: pl.BlockSpec(shape, lambda bi: (bi, 0, 0))
    return pl.pallas_call(
        _ssd_sample_body,
        grid=(b,),
        in_specs=[seq((1, l, CONV_DIM)), seq((1, CONV_K - 1, CONV_DIM)), seq((1, l, D_INNER)),
                  seq((1, l, LANES)), seq((1, D_INNER, D_STATE)),
                  full((CONV_K, CONV_DIM)), full((1, CONV_DIM)), full((1, LANES)),
                  full((1, D_INNER)), full((1, D_INNER))],
        out_specs=[seq((1, l, D_INNER)), seq((1, D_INNER, D_STATE))],
        out_shape=[jax.ShapeDtypeStruct((b, l, D_INNER), F32),
                   jax.ShapeDtypeStruct((b, D_INNER, D_STATE), F32)],
        scratch_shapes=[pltpu.VMEM((8 + l, CONV_DIM), F32)],
        compiler_params=_cparams(("arbitrary",)),
        name="ssd_sample",
    )(xbc, conv0, z, small, state, conv_w, conv_b, alog_lane, dskip_exp, norm_w)


def _post_mixer_body(x_ref, oa_ref, ys_ref, gate_ref, g1_ref, sh2_ref, sc2_ref, wa_ref, ws_ref, wo_ref,
                     n2_ref, wr_ref, br_ref, x2_ref, h_ref, te_ref, tw_ref):
    bb, bl, d = x_ref.shape
    rows = bb * bl
    oa = oa_ref[...].reshape(rows, d).astype(BF16)
    ys = ys_ref[...].reshape(rows, D_INNER).astype(BF16)
    y_attn = _dot(oa, wa_ref[...])
    y_ssm = _dot(ys, ws_ref[...])
    gate = gate_ref[...]
    merged = _sigmoid(gate[:, 0:d]) * y_attn + _sigmoid(gate[:, d:2 * d]) * y_ssm
    mix = _dot(merged.astype(BF16), wo_ref[...])
    x2 = x_ref[...] + g1_ref[...] * mix.reshape(bb, bl, d)
    x2_ref[...] = x2
    ms = jnp.mean(x2 * x2, axis=-1, keepdims=True)
    hf = (x2 * lax.rsqrt(ms + EPS) * n2_ref[...]) * (1.0 + sc2_ref[...]) + sh2_ref[...]
    hf = hf.reshape(rows, d)
    h_ref[...] = hf
    lane = lax.broadcasted_iota(jnp.int32, (rows, LANES), 1)
    logits = jnp.where(lane < N_EXPERTS, _dot3(hf, wr_ref[...]) + br_ref[...], NEG_INF)
    vals, idxs = [], []
    for _ in range(TOP_K):
        mx = jnp.max(logits, axis=-1, keepdims=True)
        idx = jnp.min(jnp.where(logits == mx, lane, LANES), axis=-1, keepdims=True)
        vals.append(mx)
        idxs.append(idx)
        logits = jnp.where(lane == idx, NEG_INF, logits)
    ex = [jnp.exp(v - vals[0]) for v in vals]
    den = ex[0] + ex[1] + ex[2] + ex[3]
    te = jnp.zeros((rows, LANES), jnp.int32)
    tw = jnp.zeros((rows, LANES), F32)
    for k in range(TOP_K):
        te = jnp.where(lane == k, idxs[k], te)
        tw = jnp.where(lane == k, ex[k] / den, tw)
    te_ref[...] = te
    tw_ref[...] = tw


def _post_mixer(x, o_attn, y_ssm, gate, mod, w_attn_out, w_ssm_out, w_o, norm2_w, w_router, b_router,
                rows=256):
    b, l, d = x.shape
    bb, bl, rows = _row_blocks(b, l, rows)
    nl = l // bl
    tok = lambda w: pl.BlockSpec((bb, bl, w), lambda i, j: (i, j, 0))
    flat = lambda w: pl.BlockSpec((rows, w), lambda i, j: (i * nl + j, 0))
    modc = lambda c: pl.BlockSpec((bb, 1, d), lambda i, j: (i, 0, c))
    full = lambda shape: pl.BlockSpec(shape, lambda i, j: (0,) * len(shape))
    return pl.pallas_call(
        _post_mixer_body,
        grid=(b // bb, nl),
        in_specs=[tok(d), tok(d), tok(D_INNER), flat(2 * d), modc(2), modc(3), modc(4),
                  full((d, d)), full((D_INNER, d)), full((d, d)), full((1, d)),
                  full((d, LANES)), full((1, LANES))],
        out_specs=[tok(d), flat(d), flat(LANES), flat(LANES)],
        out_shape=[jax.ShapeDtypeStruct((b, l, d), F32), jax.ShapeDtypeStruct((b * l, d), F32),
                   jax.ShapeDtypeStruct((b * l, LANES), jnp.int32), jax.ShapeDtypeStruct((b * l, LANES), F32)],
        compiler_params=_cparams(("arbitrary", "arbitrary")),
        name="post_mixer",
    )(x, o_attn, y_ssm, gate, mod, mod, mod, w_attn_out, w_ssm_out, w_o, norm2_w.reshape(1, d),
      w_router, b_router)


MOE_ROWS = 256
SPLIT_COLS = 512


def _experts_body(be_ref, nb_ref, x_ref, wgu_ref, bg_ref, bu_ref, wdn_ref, bd_ref, o_ref,
                  wg_scr, wu_scr, wd_scr, t_scr):
    i = pl.program_id(0)
    d = x_ref.shape[1]
    half = SPLIT_COLS // 2

    @pl.when(jnp.logical_or(i == 0, be_ref[i] != be_ref[jnp.maximum(i - 1, 0)]))
    def _():
        for cb in range(wgu_ref.shape[2] // SPLIT_COLS):
            for kc in range(d // LANES):
                ks = slice(kc * LANES, (kc + 1) * LANES)
                t_scr[kc] = wgu_ref[0, ks, cb * SPLIT_COLS:(cb + 1) * SPLIT_COLS].T
                rows = slice(cb * half, (cb + 1) * half)
                wg_scr[rows, ks] = t_scr[kc, pl.ds(0, half, stride=2), :].astype(BF16)
                wu_scr[rows, ks] = t_scr[kc, pl.ds(1, half, stride=2), :].astype(BF16)
        wd_scr[...] = wdn_ref[0].astype(BF16)

    @pl.when(i < nb_ref[0])
    def _():
        x = x_ref[...].astype(BF16)
        gate = jnp.minimum(_dot_nt(x, wg_scr[...]) + bg_ref[0], SWIGLU_LIMIT)
        up = jnp.clip(_dot_nt(x, wu_scr[...]) + bu_ref[0], -SWIGLU_LIMIT, SWIGLU_LIMIT)
        act = (up + 1.0) * (gate * _sigmoid(SWIGLU_ALPHA * gate))
        o_ref[...] = _dot(act.astype(BF16), wd_scr[...]) + bd_ref[0]

    @pl.when(i >= nb_ref[0])
    def _():
        o_ref[...] = jnp.zeros_like(o_ref)


def _experts(block_e, n_used, xg, w_gu, b_g, b_u, w_d, b_d):
    p, d = xg.shape
    ff = w_d.shape[1]
    wspec = lambda shape: pl.BlockSpec(shape, lambda i, be, nb: (be[i], 0, 0))
    grid_spec = pltpu.PrefetchScalarGridSpec(
        num_scalar_prefetch=2,
        grid=(p // MOE_ROWS,),
        in_specs=[pl.BlockSpec((MOE_ROWS, d), lambda i, be, nb: (i, 0)),
                  wspec((1, d, 2 * ff)), wspec((1, 1, ff)), wspec((1, 1, ff)),
                  wspec((1, ff, d)), wspec((1, 1, d))],
        out_specs=pl.BlockSpec((MOE_ROWS, d), lambda i, be, nb: (i, 0)),
        scratch_shapes=[pltpu.VMEM((ff, d), BF16), pltpu.VMEM((ff, d), BF16), pltpu.VMEM((ff, d), BF16),
                        pltpu.VMEM((d // LANES, SPLIT_COLS, LANES), F32)],
    )
    return pl.pallas_call(
        _experts_body,
        grid_spec=grid_spec,
        out_shape=jax.ShapeDtypeStruct((p, d), F32),
        compiler_params=_cparams(("arbitrary",)),
        name="moe_experts",
    )(block_e, n_used, xg, w_gu, b_g, b_u, w_d, b_d)


SC_WORKERS = 32
SC_CHUNK = 32


def _sc_gather_rows(table, idx):
    b = idx.shape[0]
    d = table.shape[1]
    per_w = b // SC_WORKERS
    n = per_w // SC_CHUNK
    assert b % (SC_WORKERS * SC_CHUNK) == 0 and n % 2 == 0
    mesh = plsc.VectorSubcoreMesh(core_axis_name="c", subcore_axis_name="s")

    @functools.partial(
        pl.kernel, mesh=mesh, out_type=jax.ShapeDtypeStruct((b, d), table.dtype),
        scratch_types=[pltpu.VMEM((n, SC_CHUNK), jnp.int32), pltpu.VMEM((2, SC_CHUNK, d), table.dtype),
                       pltpu.SemaphoreType.DMA((2,)), pltpu.SemaphoreType.DMA((2,))],
        name="sc_gather_rows")
    def gather(table_hbm, idx_hbm, out_hbm, idx_v, rows_v, gsem, wsem):
        wid = lax.axis_index("s") * 2 + lax.axis_index("c")
        base = wid * per_w
        pltpu.sync_copy(idx_hbm.at[wid], idx_v)

        def fetch(c, slot):
            return pltpu.make_async_copy(table_hbm.at[idx_v.at[c]], rows_v.at[slot], gsem.at[slot])

        def write(c, slot):
            off = pl.multiple_of(base + c * SC_CHUNK, SC_CHUNK)
            return pltpu.make_async_copy(rows_v.at[slot], out_hbm.at[pl.ds(off, SC_CHUNK)], wsem.at[slot])

        fetch(0, 0).start()

        @pl.loop(0, n, step=2)
        def _(c0):
            for slot in (0, 1):
                c = c0 + slot
                fetch(c, slot).wait()

                @pl.when(c >= 1)
                def _():
                    write(c - 1, 1 - slot).wait()

                @pl.when(c + 1 < n)
                def _():
                    fetch(c + 1, 1 - slot).start()

                write(c, slot).start()

        write(n - 1, 1).wait()

    return gather(table, idx.reshape(SC_WORKERS, n, SC_CHUNK))


def _route(top_e):
    t, k = top_e.shape
    s = t * k
    e_flat = top_e.reshape(-1)
    onehot = (e_flat[:, None] == jnp.arange(N_EXPERTS, dtype=jnp.int32)[None, :]).astype(jnp.int32)
    csum = jnp.cumsum(onehot, axis=0)
    rank = jnp.sum(onehot * csum, axis=1) - 1
    counts = csum[-1]
    padded = (counts + MOE_ROWS - 1) // MOE_ROWS * MOE_ROWS
    pend = jnp.cumsum(padded)
    dest = (pend - padded)[e_flat] + rank
    n_blocks = -(-s // MOE_ROWS) + N_EXPERTS
    row_tok = jnp.zeros((n_blocks * MOE_ROWS,), jnp.int32).at[dest].set(
        jnp.arange(s, dtype=jnp.int32) // k)
    block_row0 = jnp.arange(n_blocks, dtype=jnp.int32) * MOE_ROWS
    block_e = jnp.minimum(jnp.sum((pend[None, :] <= block_row0[:, None]).astype(jnp.int32), axis=1),
                          N_EXPERTS - 1)
    n_used = (pend[-1:] // MOE_ROWS).astype(jnp.int32)
    return dest.reshape(t, k), row_tok, block_e, n_used


def _final_body(x_ref, ex_ref, tw_ref, g2_ref, nf_ref, y_ref):
    bb, bl, d = x_ref.shape
    tw = tw_ref[...]
    moe = ex_ref[0] * tw[:, 0:1]
    for k in range(1, TOP_K):
        moe = moe + ex_ref[k] * tw[:, k:k + 1]
    x = x_ref[...] + g2_ref[...] * moe.reshape(bb, bl, d)
    ms = jnp.mean(x * x, axis=-1, keepdims=True)
    y_ref[...] = x * lax.rsqrt(ms + EPS) * nf_ref[...]


def _final(x2, expert_out, top_w, row0, mod, norm_f_w, rows=512):
    b, l, d = x2.shape
    bb, bl, rows = _row_blocks(b, l, rows)
    nl = l // bl
    assert row0 % rows == 0
    blk0 = row0 // rows
    return pl.pallas_call(
        _final_body,
        grid=(b // bb, nl),
        in_specs=[pl.BlockSpec((bb, bl, d), lambda i, j: (i, j, 0)),
                  pl.BlockSpec((TOP_K, rows, d), lambda i, j: (0, blk0 + i * nl + j, 0)),
                  pl.BlockSpec((rows, LANES), lambda i, j: (blk0 + i * nl + j, 0)),
                  pl.BlockSpec((bb, 1, d), lambda i, j: (i, 0, 5)),
                  pl.BlockSpec((1, d), lambda i, j: (0, 0))],
        out_specs=pl.BlockSpec((bb, bl, d), lambda i, j: (i, j, 0)),
        out_shape=jax.ShapeDtypeStruct((b, l, d), F32),
        compiler_params=_cparams(("arbitrary", "arbitrary")),
        name="final_norm",
    )(x2, expert_out, top_w, mod, norm_f_w.reshape(1, d))


def _prep_in_weights(w_in, b_fgate, dt_bias):
    aw = N_HEADS_A * HEAD_DIM_A
    o = np.cumsum([0, aw, aw, aw, N_HEADS_A, D_INNER, CONV_DIM, N_HEADS_S, D_MODEL, D_MODEL])
    w_t = w_in.T
    rows = lambda i: w_t[o[i]:o[i + 1]]
    w_small = jnp.concatenate(
        [rows(3), rows(6), jnp.zeros((LANES - N_HEADS_A - N_HEADS_S, D_MODEL), F32)], axis=0).T
    b_small = jnp.concatenate(
        [b_fgate, dt_bias, jnp.zeros((LANES - N_HEADS_A - N_HEADS_S,), F32)]).reshape(1, LANES)
    big = {"q": rows(0), "k": rows(1), "v": rows(2), "z": rows(4), "xbc": rows(5),
           "gate": w_t[o[7]:o[9]]}
    return {n: w.astype(BF16) for n, w in big.items()}, w_small, b_small


def kernel(x_prompt, x_sample, cache_k, cache_v, cache_logf, state_ssm, state_conv, page_table, c_prompt, c_sample, w_cond, b_cond, norm1_w, w_in, b_fgate, conv_w, conv_b, dt_bias, a_log, d_skip, ssm_norm_w, w_attn_out, w_ssm_out, w_o, norm2_w, w_router, b_router, w_gate_up, b_gate_up, w_down, b_down, norm_f_w):
    assert w_in.shape[0] == 1, "single-layer trunk"
    bp, lp, d = x_prompt.shape
    bs, ls, _ = x_sample.shape
    tp, ts = bp * lp, bs * ls
    aw = N_HEADS_A * HEAD_DIM_A
    c_all = jnp.concatenate([c_prompt, c_sample, jnp.zeros((-(bp + bs) % 8, d), F32)], axis=0)
    mod = _adaln_mod(c_all, w_cond[0], b_cond[0])
    mod_p = mod[:bp].reshape(bp, 1, 6 * d)
    mod_s = mod[bp:bp + bs].reshape(bs, 1, 6 * d)

    w_big, w_small, b_small = _prep_in_weights(w_in[0], b_fgate[0], dt_bias[0])
    pad_lanes = LANES - N_HEADS_A - N_HEADS_S
    alog_lane = jnp.concatenate([jnp.zeros((N_HEADS_A,), F32), a_log[0], jnp.zeros((pad_lanes,), F32)]).reshape(1, LANES)
    dskip_exp = jnp.repeat(d_skip[0], HEAD_DIM_S).reshape(1, D_INNER)
    ssm_nw = ssm_norm_w[0].reshape(1, D_INNER)
    conv_b2 = conv_b[0].reshape(1, CONV_DIM)
    wa, ws, wo = w_attn_out[0].astype(BF16), w_ssm_out[0].astype(BF16), w_o[0].astype(BF16)
    w_router_pad = jnp.pad(w_router[0], ((0, 0), (0, LANES - N_EXPERTS)))
    b_router_pad = jnp.pad(b_router[0], (0, LANES - N_EXPERTS)).reshape(1, LANES)

    def in_proj(x, m):
        h, small = _norm_in(x, m, norm1_w[0], w_small, b_small)
        return {n: _matmul_nt(h, w) for n, w in w_big.items()}, small

    proj_p, small_p = in_proj(x_prompt, mod_p)
    proj_s, small_s = in_proj(x_sample, mod_s)

    fcum = _cumsum_t(small_p.reshape(tp // PAGE_SIZE, PAGE_SIZE, LANES), lp // PAGE_SIZE, True)
    fcum = fcum.reshape(bp, N_HEADS_A // 2, 2, lp)
    o_p = _fox_prompt(proj_p["q"].reshape(bp, lp, aw), proj_p["k"].astype(BF16).reshape(bp, lp, aw),
                      proj_p["v"].astype(BF16).reshape(bp, lp, aw), fcum)
    xbc_p = proj_p["xbc"].reshape(bp, lp, CONV_DIM)
    y_p, st_p = _ssd_prompt(xbc_p, proj_p["z"].reshape(bp, lp, D_INNER), small_p.reshape(bp, lp, LANES),
                            conv_w[0], conv_b2, alog_lane, dskip_exp, ssm_nw)

    n_phys = cache_k.shape[1]
    k_t = jnp.transpose(cache_k[0], (0, 2, 3, 1)).reshape(n_phys, aw, PAGE_SIZE)
    v_t = jnp.transpose(cache_v[0], (0, 2, 3, 1)).reshape(n_phys, aw, PAGE_SIZE)
    lf_t = jnp.swapaxes(cache_logf[0], 1, 2).reshape(n_phys * N_HEADS_A, PAGE_SIZE)
    c_pages = _cumsum_lanes(lf_t, rows=512).reshape(n_phys, N_HEADS_A, PAGE_SIZE)
    lf_new = jnp.pad(small_s.reshape(bs, ls, LANES), ((0, 0), (0, PAGE_SIZE - ls), (0, 0)))
    c_new = _cumsum_t(lf_new, 1, False)
    n_pages = page_table.shape[1]
    o_s = _fox_sample(proj_s["q"].reshape(bs, ls, aw), proj_s["k"].reshape(bs, ls, aw),
                      proj_s["v"].reshape(bs, ls, aw), c_new, k_t, v_t, c_pages, page_table,
                      pages_per_step=min(16, n_pages))
    xbc_s = proj_s["xbc"].reshape(bs, ls, CONV_DIM)
    y_s, st_s = _ssd_sample(xbc_s, state_conv[0], proj_s["z"].reshape(bs, ls, D_INNER),
                            small_s.reshape(bs, ls, LANES), state_ssm[0].reshape(bs, D_INNER, D_STATE),
                            conv_w[0], conv_b2, alog_lane, dskip_exp, ssm_nw)

    post = lambda x, o, y, gate, m: _post_mixer(x, o, y, gate, m, wa, ws, wo, norm2_w[0],
                                                w_router_pad, b_router_pad)
    x2_p, hf_p, te_p, tw_p = post(x_prompt, o_p, y_p, proj_p["gate"], mod_p)
    x2_s, hf_s, te_s, tw_s = post(x_sample, o_s, y_s, proj_s["gate"], mod_s)

    hf = jnp.concatenate([hf_p, hf_s], axis=0)
    top_e = jnp.concatenate([te_p[:, :TOP_K], te_s[:, :TOP_K]], axis=0)
    top_w = jnp.concatenate([tw_p, tw_s], axis=0)
    dest, row_tok, block_e, n_used = _route(top_e)
    rows = _experts(block_e, n_used, _sc_gather_rows(hf, row_tok), w_gate_up[0],
                    b_gate_up[0][:, None, 0::2], b_gate_up[0][:, None, 1::2],
                    w_down[0], b_down[0][:, None, :])
    expert_out = _sc_gather_rows(rows, dest.T.reshape(-1)).reshape(TOP_K, tp + ts, d)

    y_prompt = _final(x2_p, expert_out, top_w, 0, mod_p, norm_f_w)
    y_sample = _final(x2_s, expert_out, top_w, tp, mod_s, norm_f_w)

    def state_rows(k, v, small, st, xbc, conv0, b, l):
        conv_new = jnp.concatenate([conv0, xbc], axis=1)[:, -(CONV_K - 1):]
        return (k.reshape(1, b, l, N_HEADS_A, HEAD_DIM_A), v.reshape(1, b, l, N_HEADS_A, HEAD_DIM_A),
                small[:, :N_HEADS_A].reshape(1, b, l, N_HEADS_A),
                st.reshape(1, b, N_HEADS_S, HEAD_DIM_S, D_STATE), conv_new[None])

    sp = state_rows(proj_p["k"], proj_p["v"], small_p, st_p, xbc_p,
                    jnp.zeros((bp, CONV_K - 1, CONV_DIM), F32), bp, lp)
    ss = state_rows(proj_s["k"], proj_s["v"], small_s, st_s, xbc_s, state_conv[0], bs, ls)
    return (y_prompt, y_sample) + sp + ss
```

```python
import functools

import numpy as np
import jax
import jax.numpy as jnp
from jax import lax
from jax.experimental import pallas as pl
from jax.experimental.pallas import tpu as pltpu
from jax.experimental.pallas import tpu_sc as plsc

F32 = jnp.float32
BF16 = jnp.bfloat16

D_MODEL = 1024
N_HEADS_A = 16
HEAD_DIM_A = 64
PAGE_SIZE = 128
D_INNER = 2048
HEAD_DIM_S = 64
N_HEADS_S = 32
N_GROUPS_S = 4
HEADS_PER_GROUP = N_HEADS_S // N_GROUPS_S
GROUP_W = D_INNER // N_GROUPS_S
D_STATE = 128
CONV_K = 4
CONV_DIM = D_INNER + 2 * N_GROUPS_S * D_STATE
SSD_CHUNK = 128
N_EXPERTS = 32
TOP_K = 4
D_FF = D_MODEL
SWIGLU_LIMIT = 7.0
SWIGLU_ALPHA = 1.702
EPS = 1e-6

LANES = 128
DT_LANE0 = N_HEADS_A
NEG_INF = float("-inf")
LOG2E = 1.4426950408889634
VMEM_LIMIT = 56 * 1024 * 1024


def _cparams(sem):
    return pltpu.CompilerParams(dimension_semantics=sem, vmem_limit_bytes=VMEM_LIMIT)


def _split3(x):
    hi = x.astype(BF16)
    r = x - hi.astype(F32)
    mid = r.astype(BF16)
    lo = (r - mid.astype(F32)).astype(BF16)
    return hi, mid, lo


def _dot(a, b):
    return jnp.dot(a, b, preferred_element_type=F32)


def _dot_nt(a, b):
    return lax.dot_general(a, b, (((1,), (1,)), ((), ())), preferred_element_type=F32)


def _dot_sel_left(sel_bf16, x):
    hi, mid, lo = _split3(x)
    return _dot(sel_bf16, hi) + _dot(sel_bf16, mid) + _dot(sel_bf16, lo)


def _dot3(a, b):
    ah = a.astype(BF16)
    am = (a - ah.astype(F32)).astype(BF16)
    bh = b.astype(BF16)
    bm = (b - bh.astype(F32)).astype(BF16)
    return _dot(ah, bh) + _dot(ah, bm) + _dot(am, bh)


def _tri_lower(n):
    r = lax.broadcasted_iota(jnp.int32, (n, n), 0)
    c = lax.broadcasted_iota(jnp.int32, (n, n), 1)
    return r >= c


def _sigmoid(x):
    return 1.0 / (1.0 + jnp.exp(-x))


def _silu(x):
    return x * _sigmoid(x)


def _mod_body(c_ref, w_ref, b_ref, o_ref):
    c = c_ref[...]
    o_ref[...] = _dot3(_silu(c), w_ref[...]) + b_ref[...]


def _adaln_mod(c_all, w_cond, b_cond):
    rows, d = c_all.shape
    n = w_cond.shape[1]
    tn = 1024
    return pl.pallas_call(
        _mod_body,
        grid=(n // tn,),
        in_specs=[pl.BlockSpec((rows, d), lambda j: (0, 0)),
                  pl.BlockSpec((d, tn), lambda j: (0, j)),
                  pl.BlockSpec((1, tn), lambda j: (0, j))],
        out_specs=pl.BlockSpec((rows, tn), lambda j: (0, j)),
        out_shape=jax.ShapeDtypeStruct((rows, n), F32),
        compiler_params=_cparams(("arbitrary",)),
        name="adaln_mod",
    )(c_all, w_cond, b_cond.reshape(1, n))


def _norm_in_body(x_ref, sh_ref, sc_ref, nw_ref, ws_ref, bs_ref, h_ref, sm_ref):
    x = x_ref[...]
    bb, bl, d = x.shape
    ms = jnp.mean(x * x, axis=-1, keepdims=True)
    y = x * lax.rsqrt(ms + EPS) * nw_ref[...]
    h = (y * (1.0 + sc_ref[...]) + sh_ref[...]).reshape(bb * bl, d)
    h_ref[...] = h.astype(BF16)
    sm = _dot3(h, ws_ref[...]) + bs_ref[...]
    lane = lax.broadcasted_iota(jnp.int32, sm.shape, 1)
    t = jnp.log(1.0 + jnp.exp(-jnp.abs(sm)))
    sm_ref[...] = jnp.where(lane < N_HEADS_A, jnp.minimum(sm, 0.0) - t, jnp.maximum(sm, 0.0) + t)


def _row_blocks(b, l, rows):
    rows = min(rows, b * l)
    if l >= rows:
        assert l % rows == 0
        return 1, rows, rows
    assert rows % l == 0 and b % (rows // l) == 0
    return rows // l, l, rows


def _norm_in(x, mod, norm_w, w_small, b_small, rows=512):
    b, l, d = x.shape
    bb, bl, rows = _row_blocks(b, l, rows)
    nl = l // bl
    grid = (b // bb, nl)
    return pl.pallas_call(
        _norm_in_body,
        grid=grid,
        in_specs=[pl.BlockSpec((bb, bl, d), lambda i, j: (i, j, 0)),
                  pl.BlockSpec((bb, 1, d), lambda i, j: (i, 0, 0)),
                  pl.BlockSpec((bb, 1, d), lambda i, j: (i, 0, 1)),
                  pl.BlockSpec((1, d), lambda i, j: (0, 0)),
                  pl.BlockSpec((d, LANES), lambda i, j: (0, 0)),
                  pl.BlockSpec((1, LANES), lambda i, j: (0, 0))],
        out_specs=[pl.BlockSpec((rows, d), lambda i, j: (i * nl + j, 0)),
                   pl.BlockSpec((rows, LANES), lambda i, j: (i * nl + j, 0))],
        out_shape=[jax.ShapeDtypeStruct((b * l, d), BF16),
                   jax.ShapeDtypeStruct((b * l, LANES), F32)],
        compiler_params=_cparams(("arbitrary", "arbitrary")),
        name="norm_in",
    )(x, mod, mod, norm_w.reshape(1, d), w_small, b_small)


def _mm_body(a_ref, w_ref, o_ref):
    o_ref[...] = _dot_nt(a_ref[...], w_ref[...])


def _matmul_nt(a, w_t, tm=1024, tn=1024):
    m, k = a.shape
    n = w_t.shape[0]
    tn = min(tn, n)
    tm = min(tm, m)
    return pl.pallas_call(
        _mm_body,
        grid=(n // tn, m // tm),
        in_specs=[pl.BlockSpec((tm, k), lambda j, i: (i, 0)),
                  pl.BlockSpec((tn, k), lambda j, i: (j, 0))],
        out_specs=pl.BlockSpec((tm, tn), lambda j, i: (i, j)),
        out_shape=jax.ShapeDtypeStruct((m, n), F32),
        compiler_params=_cparams(("arbitrary", "arbitrary")),
        name="proj_matmul",
    )(a, w_t)


def _cumsum_t_body(x_ref, o_ref, pad_scr, carry_scr, *, width, carry):
    j = pl.program_id(1)

    @pl.when(j == 0)
    def _():
        pad_scr[...] = jnp.zeros_like(pad_scr)
        carry_scr[...] = jnp.zeros_like(carry_scr)

    if width == LANES:
        blk = x_ref[0]
    else:
        pad_scr[:, 0:width] = x_ref[0]
        blk = pad_scr[...]
    tri = _tri_lower(PAGE_SIZE).astype(BF16)
    cs = _dot_sel_left(tri, blk) + carry_scr[...]
    if carry:
        carry_scr[...] = cs[PAGE_SIZE - 1:PAGE_SIZE, :]
    o_ref[0] = cs.T[0:N_HEADS_A, :]


def _cumsum_t(x, nper, carry):
    n, p, width = x.shape
    assert p == PAGE_SIZE and n % nper == 0
    return pl.pallas_call(
        functools.partial(_cumsum_t_body, width=width, carry=carry),
        grid=(n // nper, nper),
        in_specs=[pl.BlockSpec((1, p, width), lambda i, j: (i * nper + j, 0, 0))],
        out_specs=pl.BlockSpec((1, N_HEADS_A, p), lambda i, j: (i, 0, j)),
        out_shape=jax.ShapeDtypeStruct((n // nper, N_HEADS_A, nper * p), F32),
        scratch_shapes=[pltpu.VMEM((p, LANES), F32), pltpu.VMEM((1, LANES), F32)],
        compiler_params=_cparams(("arbitrary", "arbitrary")),
        name="logf_cumsum",
    )(x)


def _cumsum_lanes_body(x_ref, o_ref):
    n = x_ref.shape[1]
    r = lax.broadcasted_iota(jnp.int32, (n, n), 0)
    c = lax.broadcasted_iota(jnp.int32, (n, n), 1)
    tri = (r <= c).astype(BF16)
    hi, mid, lo = _split3(x_ref[...])
    o_ref[...] = _dot(hi, tri) + _dot(mid, tri) + _dot(lo, tri)


def _cumsum_lanes(x, rows=1024):
    r, n = x.shape
    assert r % rows == 0
    return pl.pallas_call(
        _cumsum_lanes_body,
        grid=(r // rows,),
        in_specs=[pl.BlockSpec((rows, n), lambda i: (i, 0))],
        out_specs=pl.BlockSpec((rows, n), lambda i: (i, 0)),
        out_shape=jax.ShapeDtypeStruct((r, n), F32),
        compiler_params=_cparams(("arbitrary",)),
        name="page_logf_cumsum",
    )(x)


def _fox_prompt_body(it_ref, jt_ref, q_ref, k_ref, v_ref, f_ref, o_ref, qm_scr, m_scr, acc_scr, *,
                     tile, sub):
    t = pl.program_id(2)
    i = it_ref[t]
    j = jt_ref[t]
    lane = lax.broadcasted_iota(jnp.int32, (tile, LANES), 1)
    first = lane < HEAD_DIM_A

    @pl.when(j == 0)
    def _():
        q = q_ref[0] * (HEAD_DIM_A ** -0.5 * LOG2E)
        qm_scr[0] = jnp.where(first, q, 0.0).astype(BF16)
        qm_scr[1] = jnp.where(first, 0.0, q).astype(BF16)
        m_scr[...] = jnp.full_like(m_scr, -1e30)
        acc_scr[...] = jnp.zeros_like(acc_scr)

    def step(diagonal):
        one = jnp.ones((), BF16)
        first_s = lax.broadcasted_iota(jnp.int32, (sub, LANES), 1) < HEAD_DIM_A
        for kj in range(tile // sub):
            ks = slice(kj * sub, (kj + 1) * sub)
            kb = k_ref[0, ks, :]
            vb = v_ref[0, ks, :]
            v_augs = (jnp.where(first_s, vb, one), jnp.where(first_s, one, vb))
            for qi in range(tile // sub):
                if diagonal and kj > qi:
                    continue
                qs = slice(qi * sub, (qi + 1) * sub)
                for h in range(2):
                    s = _dot_nt(qm_scr[h, qs, :], kb) - f_ref[0, 0, h:h + 1, ks] * LOG2E
                    if diagonal and kj == qi:
                        row = lax.broadcasted_iota(jnp.int32, (sub, sub), 0)
                        col = lax.broadcasted_iota(jnp.int32, (sub, sub), 1)
                        s = jnp.where(col <= row, s, NEG_INF)
                    m_old = m_scr[h, qs, :]
                    m_new = jnp.maximum(m_old, jnp.max(s, axis=-1, keepdims=True))
                    alpha = jnp.exp2(m_old - m_new)
                    p = jnp.exp2(s - jnp.concatenate([m_new] * (sub // LANES), axis=1))
                    acc_scr[h, qs, :] = alpha * acc_scr[h, qs, :] + _dot(p.astype(BF16), v_augs[h])
                    m_scr[h, qs, :] = m_new

    @pl.when(j < i)
    def _():
        step(False)

    @pl.when(j == i)
    def _():
        step(True)
        a0 = acc_scr[0]
        a1 = acc_scr[1]
        den = jnp.where(first, pltpu.roll(a0, HEAD_DIM_A, 1), pltpu.roll(a1, HEAD_DIM_A, 1))
        o_ref[0] = jnp.where(first, a0, a1) / den


def _fox_prompt(q, k, v, fcum, tile=1024, sub=512):
    b, l, w = q.shape
    npair = w // LANES
    tile = min(tile, l)
    sub = min(sub, tile)
    assert l % tile == 0 and tile % sub == 0
    nt = l // tile
    it = np.concatenate([np.full(i + 1, i, np.int32) for i in range(nt)])
    jt = np.concatenate([np.arange(i + 1, dtype=np.int32) for i in range(nt)])
    grid_spec = pltpu.PrefetchScalarGridSpec(
        num_scalar_prefetch=2,
        grid=(b, npair, len(it)),
        in_specs=[pl.BlockSpec((1, tile, LANES), lambda bi, p, t, it, jt: (bi, it[t], p)),
                  pl.BlockSpec((1, tile, LANES), lambda bi, p, t, it, jt: (bi, jt[t], p)),
                  pl.BlockSpec((1, tile, LANES), lambda bi, p, t, it, jt: (bi, jt[t], p)),
                  pl.BlockSpec((1, 1, 2, tile), lambda bi, p, t, it, jt: (bi, p, 0, jt[t]))],
        out_specs=pl.BlockSpec((1, tile, LANES), lambda bi, p, t, it, jt: (bi, it[t], p)),
        scratch_shapes=[pltpu.VMEM((2, tile, LANES), BF16), pltpu.VMEM((2, tile, LANES), F32),
                        pltpu.VMEM((2, tile, LANES), F32)],
    )
    return pl.pallas_call(
        functools.partial(_fox_prompt_body, tile=tile, sub=sub),
        grid_spec=grid_spec,
        out_shape=jax.ShapeDtypeStruct((b, l, w), F32),
        compiler_params=_cparams(("arbitrary", "arbitrary", "arbitrary")),
        name="fox_prompt",
    )(jnp.asarray(it), jnp.asarray(jt), q, k, v, fcum)


def _fox_sample_body(pt_ref, q_ref, kn_ref, vn_ref, cn_ref, *rest, pages_per_step):
    pg = pages_per_step
    k_refs = rest[0:pg]
    v_refs = rest[pg:2 * pg]
    c_refs = rest[2 * pg:3 * pg]
    o_ref, qbd_scr, m_scr, l_scr, carry_scr, acc_scr = rest[3 * pg:]
    j = pl.program_id(1)
    nq = q_ref.shape[1]
    rows = N_HEADS_A * nq
    width = N_HEADS_A * HEAD_DIM_A

    @pl.when(j == 0)
    def _():
        q = q_ref[0] * (HEAD_DIM_A ** -0.5)
        qt = jnp.broadcast_to(q[None], (N_HEADS_A, nq, width)).reshape(rows, width)
        row = lax.broadcasted_iota(jnp.int32, (rows, width), 0)
        lane = lax.broadcasted_iota(jnp.int32, (rows, width), 1)
        own = (lane // HEAD_DIM_A) == (row // nq)
        qbd_scr[...] = jnp.where(own, qt, 0.0).astype(BF16)
        m_scr[...] = jnp.full_like(m_scr, -1e30)
        l_scr[...] = jnp.zeros_like(l_scr)
        carry_scr[...] = jnp.zeros_like(carry_scr)
        acc_scr[...] = jnp.zeros_like(acc_scr)

    def attend(blocks, valid, transposed):
        qbd = qbd_scr[...]
        carry = carry_scr[...]
        scores = []
        for kf, _, c16 in blocks:
            s = _dot(qbd, kf.astype(BF16)) if transposed else _dot_nt(qbd, kf.astype(BF16))
            nk = s.shape[1]
            cexp = jnp.broadcast_to(c16[:, None, :], (N_HEADS_A, nq, nk)).reshape(rows, nk)
            s = s - (cexp + carry)
            if valid is not None:
                s = jnp.where(valid, s, NEG_INF)
            scores.append(s)
            carry = carry + cexp[:, nk - 1:nk]
        carry_scr[...] = carry
        smax = scores[0]
        for s in scores[1:]:
            smax = jnp.maximum(smax, s)
        m_old = m_scr[...]
        m_new = jnp.maximum(m_old, jnp.max(smax, axis=-1, keepdims=True))
        alpha = jnp.exp(m_old - m_new)
        m_scr[...] = m_new
        psum = None
        pv = None
        for s, (_, vf, _) in zip(scores, blocks):
            p = jnp.exp(s - m_new)
            psum = p if psum is None else psum + p
            pb = p.astype(BF16)
            d = _dot_nt(pb, vf.astype(BF16)) if transposed else _dot(pb, vf.astype(BF16))
            pv = d if pv is None else pv + d
        l_scr[...] = alpha * l_scr[...] + jnp.sum(psum, axis=-1, keepdims=True)
        acc_scr[...] = acc_scr[...] * alpha + pv

    attend([(k_refs[g][0], v_refs[g][0], c_refs[g][0]) for g in range(pg)], None, True)

    @pl.when(j == pl.num_programs(1) - 1)
    def _():
        zpad = jnp.zeros((PAGE_SIZE - nq, width), F32)
        kpad = jnp.concatenate([kn_ref[0], zpad], axis=0)
        vpad = jnp.concatenate([vn_ref[0], zpad], axis=0)
        row = lax.broadcasted_iota(jnp.int32, (rows, PAGE_SIZE), 0)
        key = lax.broadcasted_iota(jnp.int32, (rows, PAGE_SIZE), 1)
        attend([(kpad, vpad, cn_ref[0])], key <= (row % nq), False)
        o = acc_scr[...] / l_scr[...]
        for h in range(N_HEADS_A):
            o_ref[0, :, h * HEAD_DIM_A:(h + 1) * HEAD_DIM_A] = (
                o[h * nq:(h + 1) * nq, h * HEAD_DIM_A:(h + 1) * HEAD_DIM_A])


def _fox_sample(q, k_new, v_new, c_new, cache_k, cache_v, c_pages, page_table, pages_per_step=8):
    b, nq, w = q.shape
    n_pages = page_table.shape[1]
    pg = pages_per_step
    assert n_pages % pg == 0
    rows = N_HEADS_A * nq

    def page_map(g):
        return lambda bi, j, pt: (pt[bi * n_pages + j * pg + g], 0, 0)

    seq_map = lambda bi, j, pt: (bi, 0, 0)
    in_specs = [pl.BlockSpec((1, nq, w), seq_map), pl.BlockSpec((1, nq, w), seq_map),
                pl.BlockSpec((1, nq, w), seq_map), pl.BlockSpec((1, N_HEADS_A, PAGE_SIZE), seq_map)]
    in_specs += [pl.BlockSpec((1, w, PAGE_SIZE), page_map(g)) for g in range(pg)]
    in_specs += [pl.BlockSpec((1, w, PAGE_SIZE), page_map(g)) for g in range(pg)]
    in_specs += [pl.BlockSpec((1, N_HEADS_A, PAGE_SIZE), page_map(g)) for g in range(pg)]
    grid_spec = pltpu.PrefetchScalarGridSpec(
        num_scalar_prefetch=1,
        grid=(b, n_pages // pg),
        in_specs=in_specs,
        out_specs=pl.BlockSpec((1, nq, w), seq_map),
        scratch_shapes=[pltpu.VMEM((rows, w), BF16), pltpu.VMEM((rows, 1), F32), pltpu.VMEM((rows, 1), F32),
                        pltpu.VMEM((rows, 1), F32), pltpu.VMEM((rows, w), F32)],
    )
    args = [q, k_new, v_new, c_new] + [cache_k] * pg + [cache_v] * pg + [c_pages] * pg
    return pl.pallas_call(
        functools.partial(_fox_sample_body, pages_per_step=pg),
        grid_spec=grid_spec,
        out_shape=jax.ShapeDtypeStruct((b, nq, w), F32),
        compiler_params=_cparams(("arbitrary", "arbitrary")),
        name="fox_sample",
    )(page_table.reshape(-1), *args)


def _causal_conv_silu(ext_scr, cw_ref, cb_ref, n):
    acc = cb_ref[...] + cw_ref[0:1, :] * ext_scr[8 - (CONV_K - 1):8 - (CONV_K - 1) + n, :]
    for i in range(1, CONV_K):
        acc = acc + cw_ref[i:i + 1, :] * ext_scr[8 - (CONV_K - 1) + i:8 - (CONV_K - 1) + i + n, :]
    return _silu(acc)


def _pair_select(first, a, b):
    return jnp.where(first, a, b)


def _ssd_prompt_body(xbc_ref, z_ref, sm_ref, cw_ref, cb_ref, alog_ref, dskip_ref, nw_ref,
                     y_ref, st_ref, st_scr, ext_scr):
    c = pl.program_id(1)
    q = SSD_CHUNK
    pair_w = 2 * HEAD_DIM_S

    @pl.when(c == 0)
    def _():
        st_scr[...] = jnp.zeros_like(st_scr)
        ext_scr[0:8, :] = jnp.zeros((8, CONV_DIM), F32)

    cur = xbc_ref[0]
    ext_scr[8:8 + q, :] = cur
    act = _causal_conv_silu(ext_scr, cw_ref, cb_ref, q)
    ext_scr[0:8, :] = cur[q - 8:q, :]

    sm = sm_ref[0]
    a_lane = -jnp.exp(alog_ref[...])
    tri_mask = _tri_lower(q)
    a_all = _dot_sel_left(tri_mask.astype(BF16), sm * a_lane)
    a_t = a_all.T
    dt_t = sm.T
    w_t = jnp.exp(a_t[:, q - 1:q] - a_t) * dt_t
    ea_all = jnp.exp(a_all)
    lane = lax.broadcasted_iota(jnp.int32, (1, pair_w), 1)
    first = lane < HEAD_DIM_S

    for g in range(N_GROUPS_S):
        bc = act[:, D_INNER + g * D_STATE:D_INNER + (g + 1) * D_STATE]
        cc = act[:, D_INNER + (N_GROUPS_S + g) * D_STATE:D_INNER + (N_GROUPS_S + g + 1) * D_STATE]
        cb = _dot_nt(cc.astype(BF16), bc.astype(BF16))
        bc_t = bc.T
        gated = []
        ssq = jnp.zeros((q, 1), F32)
        for pp in range(HEADS_PER_GROUP // 2):
            h0 = g * HEADS_PER_GROUP + 2 * pp
            col = h0 * HEAD_DIM_S
            x_pair = act[:, col:col + pair_w]
            s_pair = st_scr[:, col:col + pair_w]
            rhs = jnp.concatenate([x_pair.astype(BF16), s_pair.astype(BF16)], axis=0)
            ys, us, el = [], [], []
            for h in (h0, h0 + 1):
                li = DT_LANE0 + h
                seg = a_all[:, li:li + 1] - a_t[li:li + 1, :]
                dec = jnp.exp(jnp.where(tri_mask, seg, NEG_INF))
                m_h = cb * dec * dt_t[li:li + 1, :]
                lhs = jnp.concatenate([m_h, cc * ea_all[:, li:li + 1]], axis=1).astype(BF16)
                ys.append(_dot(lhs, rhs))
                us.append(_dot((bc_t * w_t[li:li + 1, :]).astype(BF16), x_pair.astype(BF16)))
                el.append(ea_all[q - 1:q, li:li + 1])
            y_pair = _pair_select(first, ys[0], ys[1]) + x_pair * dskip_ref[:, col:col + pair_w]
            st_scr[:, col:col + pair_w] = (s_pair * _pair_select(first, el[0], el[1])
                                           + _pair_select(first, us[0], us[1]))
            gp = y_pair * _silu(z_ref[0, :, col:col + pair_w])
            ssq = ssq + jnp.sum(gp * gp, axis=-1, keepdims=True)
            gated.append((col, gp))
        rs = lax.rsqrt(ssq / GROUP_W + EPS)
        for col, gp in gated:
            y_ref[0, :, col:col + pair_w] = (gp * rs * nw_ref[:, col:col + pair_w]).astype(y_ref.dtype)

    @pl.when(c == pl.num_programs(1) - 1)
    def _():
        for blk in range(D_INNER // pair_w):
            st_ref[0, blk * pair_w:(blk + 1) * pair_w, :] = st_scr[:, blk * pair_w:(blk + 1) * pair_w].T


def _ssd_prompt(xbc, z, small, conv_w, conv_b, alog_lane, dskip_exp, norm_w):
    b, l, _ = xbc.shape
    q = SSD_CHUNK
    full = lambda shape: pl.BlockSpec(shape, lambda bi, c: (0,) * len(shape))
    return pl.pallas_call(
        _ssd_prompt_body,
        grid=(b, l // q),
        in_specs=[pl.BlockSpec((1, q, CONV_DIM), lambda bi, c: (bi, c, 0)),
                  pl.BlockSpec((1, q, D_INNER), lambda bi, c: (bi, c, 0)),
                  pl.BlockSpec((1, q, LANES), lambda bi, c: (bi, c, 0)),
                  full((CONV_K, CONV_DIM)), full((1, CONV_DIM)), full((1, LANES)),
                  full((1, D_INNER)), full((1, D_INNER))],
        out_specs=[pl.BlockSpec((1, q, D_INNER), lambda bi, c: (bi, c, 0)),
                   pl.BlockSpec((1, D_INNER, D_STATE), lambda bi, c: (bi, 0, 0))],
        out_shape=[jax.ShapeDtypeStruct((b, l, D_INNER), BF16),
                   jax.ShapeDtypeStruct((b, D_INNER, D_STATE), F32)],
        scratch_shapes=[pltpu.VMEM((D_STATE, D_INNER), F32), pltpu.VMEM((8 + q, CONV_DIM), F32)],
        compiler_params=_cparams(("arbitrary", "arbitrary")),
        name="ssd_prompt",
    )(xbc, z, small, conv_w, conv_b, alog_lane, dskip_exp, norm_w)


def _ssd_sample_body(xbc_ref, c0_ref, z_ref, sm_ref, st_ref, cw_ref, cb_ref, alog_ref, dskip_ref, nw_ref,
                     y_ref, so_ref, ext_scr):
    l = xbc_ref.shape[1]
    pair_w = 2 * HEAD_DIM_S
    ext_scr[0:8, :] = jnp.zeros((8, CONV_DIM), F32)
    ext_scr[8 - (CONV_K - 1):8, :] = c0_ref[0]
    ext_scr[8:8 + l, :] = xbc_ref[0]
    act = _causal_conv_silu(ext_scr, cw_ref, cb_ref, l)

    sm = sm_ref[0]
    dta = sm * (-jnp.exp(alog_ref[...]))
    trow = lax.broadcasted_iota(jnp.int32, (l, LANES), 0)
    a_c = jnp.zeros((l, LANES), F32)
    for s in range(l):
        a_c = a_c + jnp.where(trow >= s, dta[s:s + 1, :], 0.0)
    lane = lax.broadcasted_iota(jnp.int32, (1, pair_w), 1)
    first = lane < HEAD_DIM_S

    def expand(v):
        cols = []
        for pp in range(N_HEADS_S // 2):
            li = DT_LANE0 + 2 * pp
            cols.append(_pair_select(first, v[:, li:li + 1], v[:, li + 1:li + 2]))
        return jnp.concatenate(cols, axis=1)

    a_x = expand(a_c)
    dt_x = expand(sm)
    ea_x = jnp.exp(a_x)
    w_x = jnp.exp(a_x[l - 1:l, :] - a_x) * dt_x
    xs = act[:, 0:D_INNER]
    zpad = jnp.zeros((PAGE_SIZE - l, D_STATE), F32)
    trow_x = lax.broadcasted_iota(jnp.int32, (l, GROUP_W), 0)

    xw_pad = jnp.concatenate([xs * w_x, jnp.zeros((PAGE_SIZE - l, D_INNER), F32)], axis=0)
    ea_last = jnp.exp(a_c[l - 1:l, :])
    rowsel = lax.broadcasted_iota(jnp.int32, (pair_w, 1), 0) < HEAD_DIM_S

    for g in range(N_GROUPS_S):
        gc = g * GROUP_W
        bc = act[:, D_INNER + g * D_STATE:D_INNER + (g + 1) * D_STATE]
        cc = act[:, D_INNER + (N_GROUPS_S + g) * D_STATE:D_INNER + (N_GROUPS_S + g + 1) * D_STATE]
        b_pad = jnp.concatenate([bc, zpad], axis=0).astype(BF16)
        cb = _dot_nt(cc.astype(BF16), b_pad)
        s_g = st_ref[0, gc:gc + GROUP_W, :]
        y = _dot_nt(cc.astype(BF16), s_g.astype(BF16)) * ea_x[:, gc:gc + GROUP_W]
        x_g = xs[:, gc:gc + GROUP_W]
        a_g = a_x[:, gc:gc + GROUP_W]
        dt_g = dt_x[:, gc:gc + GROUP_W]
        for s in range(l):
            dec = jnp.exp(jnp.where(trow_x >= s, a_g - a_g[s:s + 1, :], NEG_INF))
            y = y + cb[:, s:s + 1] * dec * (dt_g[s:s + 1, :] * x_g[s:s + 1, :])
        y = y + x_g * dskip_ref[:, gc:gc + GROUP_W]
        gp = y * _silu(z_ref[0, :, gc:gc + GROUP_W])
        rs = lax.rsqrt(jnp.sum(gp * gp, axis=-1, keepdims=True) / GROUP_W + EPS)
        y_ref[0, :, gc:gc + GROUP_W] = gp * rs * nw_ref[:, gc:gc + GROUP_W]
        for pp in range(HEADS_PER_GROUP // 2):
            h0 = g * HEADS_PER_GROUP + 2 * pp
            col = h0 * HEAD_DIM_S
            li = DT_LANE0 + h0
            u = _dot(xw_pad[:, col:col + pair_w].T.astype(BF16), b_pad)
            e_col = jnp.where(rowsel, ea_last[:, li:li + 1], ea_last[:, li + 1:li + 2])
            so_ref[0, col:col + pair_w, :] = st_ref[0, col:col + pair_w, :] * e_col + u


def _ssd_sample(xbc, conv0, z, small, state, conv_w, conv_b, alog_lane, dskip_exp, norm_w):
    b, l, _ = xbc.shape
    full = lambda shape: pl.BlockSpec(shape, lambda bi: (0,) * len(shape))
    seq = lambda shape: pl.BlockSpec(shape, lambda bi: (bi, 0, 0))
    return pl.pallas_call(
        _ssd_sample_body,
        grid=(b,),
        in_specs=[seq((1, l, CONV_DIM)), seq((1, CONV_K - 1, CONV_DIM)), seq((1, l, D_INNER)),
                  seq((1, l, LANES)), seq((1, D_INNER, D_STATE)),
                  full((CONV_K, CONV_DIM)), full((1, CONV_DIM)), full((1, LANES)),
                  full((1, D_INNER)), full((1, D_INNER))],
        out_specs=[seq((1, l, D_INNER)), seq((1, D_INNER, D_STATE))],
        out_shape=[jax.ShapeDtypeStruct((b, l, D_INNER), F32),
                   jax.ShapeDtypeStruct((b, D_INNER, D_STATE), F32)],
        scratch_shapes=[pltpu.VMEM((8 + l, CONV_DIM), F32)],
        compiler_params=_cparams(("arbitrary",)),
        name="ssd_sample",
    )(xbc, conv0, z, small, state, conv_w, conv_b, alog_lane, dskip_exp, norm_w)


def _post_mixer_body(x_ref, oa_ref, ys_ref, gate_ref, g1_ref, sh2_ref, sc2_ref, wa_ref, ws_ref, wo_ref,
                     n2_ref, wr_ref, br_ref, x2_ref, h_ref, te_ref, tw_ref):
    bb, bl, d = x_ref.shape
    rows = bb * bl
    oa = oa_ref[...].reshape(rows, d).astype(BF16)
    ys = ys_ref[...].reshape(rows, D_INNER).astype(BF16)
    y_attn = _dot(oa, wa_ref[...])
    y_ssm = _dot(ys, ws_ref[...])
    gate = gate_ref[...]
    merged = _sigmoid(gate[:, 0:d]) * y_attn + _sigmoid(gate[:, d:2 * d]) * y_ssm
    mix = _dot(merged.astype(BF16), wo_ref[...])
    x2 = x_ref[...] + g1_ref[...] * mix.reshape(bb, bl, d)
    x2_ref[...] = x2
    ms = jnp.mean(x2 * x2, axis=-1, keepdims=True)
    hf = (x2 * lax.rsqrt(ms + EPS) * n2_ref[...]) * (1.0 + sc2_ref[...]) + sh2_ref[...]
    hf = hf.reshape(rows, d)
    h_ref[...] = hf
    lane = lax.broadcasted_iota(jnp.int32, (rows, LANES), 1)
    logits = jnp.where(lane < N_EXPERTS, _dot3(hf, wr_ref[...]) + br_ref[...], NEG_INF)
    vals, idxs = [], []
    for _ in range(TOP_K):
        mx = jnp.max(logits, axis=-1, keepdims=True)
        idx = jnp.min(jnp.where(logits == mx, lane, LANES), axis=-1, keepdims=True)
        vals.append(mx)
        idxs.append(idx)
        logits = jnp.where(lane == idx, NEG_INF, logits)
    ex = [jnp.exp(v - vals[0]) for v in vals]
    den = ex[0] + ex[1] + ex[2] + ex[3]
    te = jnp.zeros((rows, LANES), jnp.int32)
    tw = jnp.zeros((rows, LANES), F32)
    for k in range(TOP_K):
        te = jnp.where(lane == k, idxs[k], te)
        tw = jnp.where(lane == k, ex[k] / den, tw)
    te_ref[...] = te
    tw_ref[...] = tw


def _post_mixer(x, o_attn, y_ssm, gate, mod, w_attn_out, w_ssm_out, w_o, norm2_w, w_router, b_router,
                rows=256):
    b, l, d = x.shape
    bb, bl, rows = _row_blocks(b, l, rows)
    nl = l // bl
    tok = lambda w: pl.BlockSpec((bb, bl, w), lambda i, j: (i, j, 0))
    flat = lambda w: pl.BlockSpec((rows, w), lambda i, j: (i * nl + j, 0))
    modc = lambda c: pl.BlockSpec((bb, 1, d), lambda i, j: (i, 0, c))
    full = lambda shape: pl.BlockSpec(shape, lambda i, j: (0,) * len(shape))
    return pl.pallas_call(
        _post_mixer_body,
        grid=(b // bb, nl),
        in_specs=[tok(d), tok(d), tok(D_INNER), flat(2 * d), modc(2), modc(3), modc(4),
                  full((d, d)), full((D_INNER, d)), full((d, d)), full((1, d)),
                  full((d, LANES)), full((1, LANES))],
        out_specs=[tok(d), flat(d), flat(LANES), flat(LANES)],
        out_shape=[jax.ShapeDtypeStruct((b, l, d), F32), jax.ShapeDtypeStruct((b * l, d), F32),
                   jax.ShapeDtypeStruct((b * l, LANES), jnp.int32), jax.ShapeDtypeStruct((b * l, LANES), F32)],
        compiler_params=_cparams(("arbitrary", "arbitrary")),
        name="post_mixer",
    )(x, o_attn, y_ssm, gate, mod, mod, mod, w_attn_out, w_ssm_out, w_o, norm2_w.reshape(1, d),
      w_router, b_router)


MOE_ROWS = 256
SPLIT_COLS = 512


def _experts_body(be_ref, nb_ref, x_ref, wgu_ref, bg_ref, bu_ref, wdn_ref, bd_ref, o_ref,
                  wg_scr, wu_scr, wd_scr, t_scr):
    i = pl.program_id(0)
    d = x_ref.shape[1]
    half = SPLIT_COLS // 2

    @pl.when(jnp.logical_or(i == 0, be_ref[i] != be_ref[jnp.maximum(i - 1, 0)]))
    def _():
        for cb in range(wgu_ref.shape[2] // SPLIT_COLS):
            for kc in range(d // LANES):
                ks = slice(kc * LANES, (kc + 1) * LANES)
                t_scr[kc] = wgu_ref[0, ks, cb * SPLIT_COLS:(cb + 1) * SPLIT_COLS].T
                rows = slice(cb * half, (cb + 1) * half)
                wg_scr[rows, ks] = t_scr[kc, pl.ds(0, half, stride=2), :].astype(BF16)
                wu_scr[rows, ks] = t_scr[kc, pl.ds(1, half, stride=2), :].astype(BF16)
        wd_scr[...] = wdn_ref[0].astype(BF16)

    @pl.when(i < nb_ref[0])
    def _():
        x = x_ref[...].astype(BF16)
        gate = jnp.minimum(_dot_nt(x, wg_scr[...]) + bg_ref[0], SWIGLU_LIMIT)
        up = jnp.clip(_dot_nt(x, wu_scr[...]) + bu_ref[0], -SWIGLU_LIMIT, SWIGLU_LIMIT)
        act = (up + 1.0) * (gate * _sigmoid(SWIGLU_ALPHA * gate))
        o_ref[...] = _dot(act.astype(BF16), wd_scr[...]) + bd_ref[0]

    @pl.when(i >= nb_ref[0])
    def _():
        o_ref[...] = jnp.zeros_like(o_ref)


def _experts(block_e, n_used, xg, w_gu, b_g, b_u, w_d, b_d):
    p, d = xg.shape
    ff = w_d.shape[1]
    wspec = lambda shape: pl.BlockSpec(shape, lambda i, be, nb: (be[i], 0, 0))
    grid_spec = pltpu.PrefetchScalarGridSpec(
        num_scalar_prefetch=2,
        grid=(p // MOE_ROWS,),
        in_specs=[pl.BlockSpec((MOE_ROWS, d), lambda i, be, nb: (i, 0)),
                  wspec((1, d, 2 * ff)), wspec((1, 1, ff)), wspec((1, 1, ff)),
                  wspec((1, ff, d)), wspec((1, 1, d))],
        out_specs=pl.BlockSpec((MOE_ROWS, d), lambda i, be, nb: (i, 0)),
        scratch_shapes=[pltpu.VMEM((ff, d), BF16), pltpu.VMEM((ff, d), BF16), pltpu.VMEM((ff, d), BF16),
                        pltpu.VMEM((d // LANES, SPLIT_COLS, LANES), F32)],
    )
    return pl.pallas_call(
        _experts_body,
        grid_spec=grid_spec,
        out_shape=jax.ShapeDtypeStruct((p, d), F32),
        compiler_params=_cparams(("arbitrary",)),
        name="moe_experts",
    )(block_e, n_used, xg, w_gu, b_g, b_u, w_d, b_d)


SC_WORKERS = 32
SC_CHUNK = 32


def _sc_gather_rows(table, idx):
    b = idx.shape[0]
    d = table.shape[1]
    per_w = b // SC_WORKERS
    n = per_w // SC_CHUNK
    assert b % (SC_WORKERS * SC_CHUNK) == 0 and n % 2 == 0
    mesh = plsc.VectorSubcoreMesh(core_axis_name="c", subcore_axis_name="s")

    @functools.partial(
        pl.kernel, mesh=mesh, out_type=jax.ShapeDtypeStruct((b, d), table.dtype),
        scratch_types=[pltpu.VMEM((n, SC_CHUNK), jnp.int32), pltpu.VMEM((2, SC_CHUNK, d), table.dtype),
                       pltpu.SemaphoreType.DMA((2,)), pltpu.SemaphoreType.DMA((2,))],
        name="sc_gather_rows")
    def gather(table_hbm, idx_hbm, out_hbm, idx_v, rows_v, gsem, wsem):
        wid = lax.axis_index("s") * 2 + lax.axis_index("c")
        base = wid * per_w
        pltpu.sync_copy(idx_hbm.at[wid], idx_v)

        def fetch(c, slot):
            return pltpu.make_async_copy(table_hbm.at[idx_v.at[c]], rows_v.at[slot], gsem.at[slot])

        def write(c, slot):
            off = pl.multiple_of(base + c * SC_CHUNK, SC_CHUNK)
            return pltpu.make_async_copy(rows_v.at[slot], out_hbm.at[pl.ds(off, SC_CHUNK)], wsem.at[slot])

        fetch(0, 0).start()

        @pl.loop(0, n, step=2)
        def _(c0):
            for slot in (0, 1):
                c = c0 + slot
                fetch(c, slot).wait()

                @pl.when(c >= 1)
                def _():
                    write(c - 1, 1 - slot).wait()

                @pl.when(c + 1 < n)
                def _():
                    fetch(c + 1, 1 - slot).start()

                write(c, slot).start()

        write(n - 1, 1).wait()

    return gather(table, idx.reshape(SC_WORKERS, n, SC_CHUNK))


def _route(top_e):
    t, k = top_e.shape
    s = t * k
    e_flat = top_e.reshape(-1)
    onehot = (e_flat[:, None] == jnp.arange(N_EXPERTS, dtype=jnp.int32)[None, :]).astype(jnp.int32)
    csum = jnp.cumsum(onehot, axis=0)
    rank = jnp.sum(onehot * csum, axis=1) - 1
    counts = csum[-1]
    padded = (counts + MOE_ROWS - 1) // MOE_ROWS * MOE_ROWS
    pend = jnp.cumsum(padded)
    dest = (pend - padded)[e_flat] + rank
    n_blocks = -(-s // MOE_ROWS) + N_EXPERTS
    row_tok = (jnp.arange(n_blocks * MOE_ROWS, dtype=jnp.int32) % t).at[dest].set(
        jnp.arange(s, dtype=jnp.int32) // k)
    block_row0 = jnp.arange(n_blocks, dtype=jnp.int32) * MOE_ROWS
    block_e = jnp.minimum(jnp.sum((pend[None, :] <= block_row0[:, None]).astype(jnp.int32), axis=1),
                          N_EXPERTS - 1)
    n_used = (pend[-1:] // MOE_ROWS).astype(jnp.int32)
    return dest.reshape(t, k), row_tok, block_e, n_used


def _final_body(x_ref, ex_ref, tw_ref, g2_ref, nf_ref, y_ref):
    bb, bl, d = x_ref.shape
    tw = tw_ref[...]
    moe = ex_ref[0] * tw[:, 0:1]
    for k in range(1, TOP_K):
        moe = moe + ex_ref[k] * tw[:, k:k + 1]
    x = x_ref[...] + g2_ref[...] * moe.reshape(bb, bl, d)
    ms = jnp.mean(x * x, axis=-1, keepdims=True)
    y_ref[...] = x * lax.rsqrt(ms + EPS) * nf_ref[...]


def _final(x2, expert_out, top_w, row0, mod, norm_f_w, rows=512):
    b, l, d = x2.shape
    bb, bl, rows = _row_blocks(b, l, rows)
    nl = l // bl
    assert row0 % rows == 0
    blk0 = row0 // rows
    return pl.pallas_call(
        _final_body,
        grid=(b // bb, nl),
        in_specs=[pl.BlockSpec((bb, bl, d), lambda i, j: (i, j, 0)),
                  pl.BlockSpec((TOP_K, rows, d), lambda i, j: (0, blk0 + i * nl + j, 0)),
                  pl.BlockSpec((rows, LANES), lambda i, j: (blk0 + i * nl + j, 0)),
                  pl.BlockSpec((bb, 1, d), lambda i, j: (i, 0, 5)),
                  pl.BlockSpec((1, d), lambda i, j: (0, 0))],
        out_specs=pl.BlockSpec((bb, bl, d), lambda i, j: (i, j, 0)),
        out_shape=jax.ShapeDtypeStruct((b, l, d), F32),
        compiler_params=_cparams(("arbitrary", "arbitrary")),
        name="final_norm",
    )(x2, expert_out, top_w, mod, norm_f_w.reshape(1, d))


def _prep_in_weights(w_in, b_fgate, dt_bias):
    aw = N_HEADS_A * HEAD_DIM_A
    o = np.cumsum([0, aw, aw, aw, N_HEADS_A, D_INNER, CONV_DIM, N_HEADS_S, D_MODEL, D_MODEL])
    w_t = w_in.T
    rows = lambda i: w_t[o[i]:o[i + 1]]
    w_small = jnp.concatenate(
        [rows(3), rows(6), jnp.zeros((LANES - N_HEADS_A - N_HEADS_S, D_MODEL), F32)], axis=0).T
    b_small = jnp.concatenate(
        [b_fgate, dt_bias, jnp.zeros((LANES - N_HEADS_A - N_HEADS_S,), F32)]).reshape(1, LANES)
    big = {"q": rows(0), "k": rows(1), "v": rows(2), "z": rows(4), "xbc": rows(5),
           "gate": w_t[o[7]:o[9]]}
    return {n: w.astype(BF16) for n, w in big.items()}, w_small, b_small


def kernel(x_prompt, x_sample, cache_k, cache_v, cache_logf, state_ssm, state_conv, page_table, c_prompt, c_sample, w_cond, b_cond, norm1_w, w_in, b_fgate, conv_w, conv_b, dt_bias, a_log, d_skip, ssm_norm_w, w_attn_out, w_ssm_out, w_o, norm2_w, w_router, b_router, w_gate_up, b_gate_up, w_down, b_down, norm_f_w):
    assert w_in.shape[0] == 1, "single-layer trunk"
    bp, lp, d = x_prompt.shape
    bs, ls, _ = x_sample.shape
    tp, ts = bp * lp, bs * ls
    aw = N_HEADS_A * HEAD_DIM_A
    c_all = jnp.concatenate([c_prompt, c_sample, jnp.zeros((-(bp + bs) % 8, d), F32)], axis=0)
    mod = _adaln_mod(c_all, w_cond[0], b_cond[0])
    mod_p = mod[:bp].reshape(bp, 1, 6 * d)
    mod_s = mod[bp:bp + bs].reshape(bs, 1, 6 * d)

    w_big, w_small, b_small = _prep_in_weights(w_in[0], b_fgate[0], dt_bias[0])
    pad_lanes = LANES - N_HEADS_A - N_HEADS_S
    alog_lane = jnp.concatenate([jnp.zeros((N_HEADS_A,), F32), a_log[0], jnp.zeros((pad_lanes,), F32)]).reshape(1, LANES)
    dskip_exp = jnp.repeat(d_skip[0], HEAD_DIM_S).reshape(1, D_INNER)
    ssm_nw = ssm_norm_w[0].reshape(1, D_INNER)
    conv_b2 = conv_b[0].reshape(1, CONV_DIM)
    wa, ws, wo = w_attn_out[0].astype(BF16), w_ssm_out[0].astype(BF16), w_o[0].astype(BF16)
    w_router_pad = jnp.pad(w_router[0], ((0, 0), (0, LANES - N_EXPERTS)))
    b_router_pad = jnp.pad(b_router[0], (0, LANES - N_EXPERTS)).reshape(1, LANES)

    def in_proj(x, m):
        h, small = _norm_in(x, m, norm1_w[0], w_small, b_small)
        return {n: _matmul_nt(h, w) for n, w in w_big.items()}, small

    proj_p, small_p = in_proj(x_prompt, mod_p)
    proj_s, small_s = in_proj(x_sample, mod_s)

    fcum = _cumsum_t(small_p.reshape(tp // PAGE_SIZE, PAGE_SIZE, LANES), lp // PAGE_SIZE, True)
    fcum = fcum.reshape(bp, N_HEADS_A // 2, 2, lp)
    o_p = _fox_prompt(proj_p["q"].reshape(bp, lp, aw), proj_p["k"].astype(BF16).reshape(bp, lp, aw),
                      proj_p["v"].astype(BF16).reshape(bp, lp, aw), fcum)
    xbc_p = proj_p["xbc"].reshape(bp, lp, CONV_DIM)
    y_p, st_p = _ssd_prompt(xbc_p, proj_p["z"].reshape(bp, lp, D_INNER), small_p.reshape(bp, lp, LANES),
                            conv_w[0], conv_b2, alog_lane, dskip_exp, ssm_nw)

    n_phys = cache_k.shape[1]
    k_t = jnp.transpose(cache_k[0], (0, 2, 3, 1)).reshape(n_phys, aw, PAGE_SIZE)
    v_t = jnp.transpose(cache_v[0], (0, 2, 3, 1)).reshape(n_phys, aw, PAGE_SIZE)
    lf_t = jnp.swapaxes(cache_logf[0], 1, 2).reshape(n_phys * N_HEADS_A, PAGE_SIZE)
    c_pages = _cumsum_lanes(lf_t, rows=512).reshape(n_phys, N_HEADS_A, PAGE_SIZE)
    lf_new = jnp.pad(small_s.reshape(bs, ls, LANES), ((0, 0), (0, PAGE_SIZE - ls), (0, 0)))
    c_new = _cumsum_t(lf_new, 1, False)
    n_pages = page_table.shape[1]
    o_s = _fox_sample(proj_s["q"].reshape(bs, ls, aw), proj_s["k"].reshape(bs, ls, aw),
                      proj_s["v"].reshape(bs, ls, aw), c_new, k_t, v_t, c_pages, page_table,
                      pages_per_step=min(16, n_pages))
    xbc_s = proj_s["xbc"].reshape(bs, ls, CONV_DIM)
    y_s, st_s = _ssd_sample(xbc_s, state_conv[0], proj_s["z"].reshape(bs, ls, D_INNER),
                            small_s.reshape(bs, ls, LANES), state_ssm[0].reshape(bs, D_INNER, D_STATE),
                            conv_w[0], conv_b2, alog_lane, dskip_exp, ssm_nw)

    post = lambda x, o, y, gate, m: _post_mixer(x, o, y, gate, m, wa, ws, wo, norm2_w[0],
                                                w_router_pad, b_router_pad)
    x2_p, hf_p, te_p, tw_p = post(x_prompt, o_p, y_p, proj_p["gate"], mod_p)
    x2_s, hf_s, te_s, tw_s = post(x_sample, o_s, y_s, proj_s["gate"], mod_s)

    hf = jnp.concatenate([hf_p, hf_s], axis=0)
    top_e = jnp.concatenate([te_p[:, :TOP_K], te_s[:, :TOP_K]], axis=0)
    top_w = jnp.concatenate([tw_p, tw_s], axis=0)
    dest, row_tok, block_e, n_used = _route(top_e)
    rows = _experts(block_e, n_used, _sc_gather_rows(hf, row_tok), w_gate_up[0],
                    b_gate_up[0][:, None, 0::2], b_gate_up[0][:, None, 1::2],
                    w_down[0], b_down[0][:, None, :])
    expert_out = _sc_gather_rows(rows, dest.T.reshape(-1)).reshape(TOP_K, tp + ts, d)

    y_prompt = _final(x2_p, expert_out, top_w, 0, mod_p, norm_f_w)
    y_sample = _final(x2_s, expert_out, top_w, tp, mod_s, norm_f_w)

    def state_rows(k, v, small, st, xbc, conv0, b, l):
        conv_new = jnp.concatenate([conv0, xbc], axis=1)[:, -(CONV_K - 1):]
        return (k.reshape(1, b, l, N_HEADS_A, HEAD_DIM_A), v.reshape(1, b, l, N_HEADS_A, HEAD_DIM_A),
                small[:, :N_HEADS_A].reshape(1, b, l, N_HEADS_A),
                st.reshape(1, b, N_HEADS_S, HEAD_DIM_S, D_STATE), conv_new[None])

    sp = state_rows(proj_p["k"], proj_p["v"], small_p, st_p, xbc_p,
                    jnp.zeros((bp, CONV_K - 1, CONV_DIM), F32), bp, lp)
    ss = state_rows(proj_s["k"], proj_s["v"], small_s, st_s, xbc_s, state_conv[0], bs, ls)
    return (y_prompt, y_sample) + sp + ss
```

```python
import functools

import numpy as np
import jax
import jax.numpy as jnp
from jax import lax
from jax.experimental import pallas as pl
from jax.experimental.pallas import tpu as pltpu
from jax.experimental.pallas import tpu_sc as plsc

F32 = jnp.float32
BF16 = jnp.bfloat16

D_MODEL = 1024
N_HEADS_A = 16
HEAD_DIM_A = 64
PAGE_SIZE = 128
D_INNER = 2048
HEAD_DIM_S = 64
N_HEADS_S = 32
N_GROUPS_S = 4
HEADS_PER_GROUP = N_HEADS_S // N_GROUPS_S
GROUP_W = D_INNER // N_GROUPS_S
D_STATE = 128
CONV_K = 4
CONV_DIM = D_INNER + 2 * N_GROUPS_S * D_STATE
SSD_CHUNK = 128
N_EXPERTS = 32
TOP_K = 4
D_FF = D_MODEL
SWIGLU_LIMIT = 7.0
SWIGLU_ALPHA = 1.702
EPS = 1e-6

LANES = 128
DT_LANE0 = N_HEADS_A
NEG_INF = float("-inf")
LOG2E = 1.4426950408889634
VMEM_LIMIT = 56 * 1024 * 1024


def _cparams(sem):
    return pltpu.CompilerParams(dimension_semantics=sem, vmem_limit_bytes=VMEM_LIMIT)


def _split3(x):
    hi = x.astype(BF16)
    r = x - hi.astype(F32)
    mid = r.astype(BF16)
    lo = (r - mid.astype(F32)).astype(BF16)
    return hi, mid, lo


def _dot(a, b):
    return jnp.dot(a, b, preferred_element_type=F32)


def _dot_nt(a, b):
    return lax.dot_general(a, b, (((1,), (1,)), ((), ())), preferred_element_type=F32)


def _dot_sel_left(sel_bf16, x):
    hi, mid, lo = _split3(x)
    return _dot(sel_bf16, hi) + _dot(sel_bf16, mid) + _dot(sel_bf16, lo)


def _dot3(a, b):
    ah = a.astype(BF16)
    am = (a - ah.astype(F32)).astype(BF16)
    bh = b.astype(BF16)
    bm = (b - bh.astype(F32)).astype(BF16)
    return _dot(ah, bh) + _dot(ah, bm) + _dot(am, bh)


def _tri_lower(n):
    r = lax.broadcasted_iota(jnp.int32, (n, n), 0)
    c = lax.broadcasted_iota(jnp.int32, (n, n), 1)
    return r >= c


def _sigmoid(x):
    return 1.0 / (1.0 + jnp.exp(-x))


def _silu(x):
    return x * _sigmoid(x)


def _mod_body(c_ref, w_ref, b_ref, o_ref):
    c = c_ref[...]
    o_ref[...] = _dot3(_silu(c), w_ref[...]) + b_ref[...]


def _adaln_mod(c_all, w_cond, b_cond):
    rows, d = c_all.shape
    n = w_cond.shape[1]
    tn = 1024
    return pl.pallas_call(
        _mod_body,
        grid=(n // tn,),
        in_specs=[pl.BlockSpec((rows, d), lambda j: (0, 0)),
                  pl.BlockSpec((d, tn), lambda j: (0, j)),
                  pl.BlockSpec((1, tn), lambda j: (0, j))],
        out_specs=pl.BlockSpec((rows, tn), lambda j: (0, j)),
        out_shape=jax.ShapeDtypeStruct((rows, n), F32),
        compiler_params=_cparams(("arbitrary",)),
        name="adaln_mod",
    )(c_all, w_cond, b_cond.reshape(1, n))


def _norm_in_body(x_ref, sh_ref, sc_ref, nw_ref, ws_ref, bs_ref, h_ref, sm_ref):
    x = x_ref[...]
    bb, bl, d = x.shape
    ms = jnp.mean(x * x, axis=-1, keepdims=True)
    y = x * lax.rsqrt(ms + EPS) * nw_ref[...]
    h = (y * (1.0 + sc_ref[...]) + sh_ref[...]).reshape(bb * bl, d)
    h_ref[...] = h.astype(BF16)
    sm = _dot3(h, ws_ref[...]) + bs_ref[...]
    lane = lax.broadcasted_iota(jnp.int32, sm.shape, 1)
    t = jnp.log(1.0 + jnp.exp(-jnp.abs(sm)))
    sm_ref[...] = jnp.where(lane < N_HEADS_A, jnp.minimum(sm, 0.0) - t, jnp.maximum(sm, 0.0) + t)


def _row_blocks(b, l, rows):
    rows = min(rows, b * l)
    if l >= rows:
        assert l % rows == 0
        return 1, rows, rows
    assert rows % l == 0 and b % (rows // l) == 0
    return rows // l, l, rows


def _norm_in(x, mod, norm_w, w_small, b_small, rows=512):
    b, l, d = x.shape
    bb, bl, rows = _row_blocks(b, l, rows)
    nl = l // bl
    grid = (b // bb, nl)
    return pl.pallas_call(
        _norm_in_body,
        grid=grid,
        in_specs=[pl.BlockSpec((bb, bl, d), lambda i, j: (i, j, 0)),
                  pl.BlockSpec((bb, 1, d), lambda i, j: (i, 0, 0)),
                  pl.BlockSpec((bb, 1, d), lambda i, j: (i, 0, 1)),
                  pl.BlockSpec((1, d), lambda i, j: (0, 0)),
                  pl.BlockSpec((d, LANES), lambda i, j: (0, 0)),
                  pl.BlockSpec((1, LANES), lambda i, j: (0, 0))],
        out_specs=[pl.BlockSpec((rows, d), lambda i, j: (i * nl + j, 0)),
                   pl.BlockSpec((rows, LANES), lambda i, j: (i * nl + j, 0))],
        out_shape=[jax.ShapeDtypeStruct((b * l, d), BF16),
                   jax.ShapeDtypeStruct((b * l, LANES), F32)],
        compiler_params=_cparams(("arbitrary", "arbitrary")),
        name="norm_in",
    )(x, mod, mod, norm_w.reshape(1, d), w_small, b_small)


def _mm_body(a_ref, w_ref, o_ref, *half_ref):
    r = _dot_nt(a_ref[...], w_ref[...])
    o_ref[...] = r
    for h in half_ref:
        h[...] = r.astype(BF16)


def _matmul_nt(a, w_t, also_bf16=False, tm=1024, tn=1024):
    m, k = a.shape
    n = w_t.shape[0]
    tn = min(tn, n)
    tm = min(tm, m)
    out_spec = pl.BlockSpec((tm, tn), lambda j, i: (i, j))
    n_out = 2 if also_bf16 else 1
    res = pl.pallas_call(
        _mm_body,
        grid=(n // tn, m // tm),
        in_specs=[pl.BlockSpec((tm, k), lambda j, i: (i, 0)),
                  pl.BlockSpec((tn, k), lambda j, i: (j, 0))],
        out_specs=[out_spec] * n_out,
        out_shape=[jax.ShapeDtypeStruct((m, n), F32), jax.ShapeDtypeStruct((m, n), BF16)][:n_out],
        compiler_params=_cparams(("arbitrary", "arbitrary")),
        name="proj_matmul",
    )(a, w_t)
    return res if also_bf16 else res[0]


def _block_cumsum_t(blk, carry_row):
    tri = _tri_lower(PAGE_SIZE).astype(BF16)
    cs = _dot_sel_left(tri, blk) + carry_row
    return cs.T[0:N_HEADS_A, :], cs[PAGE_SIZE - 1:PAGE_SIZE, :]


def _cumsum_t_body(x_ref, o_ref, carry_scr, *, blocks):
    @pl.when(pl.program_id(1) == 0)
    def _():
        carry_scr[...] = jnp.zeros_like(carry_scr)

    carry = carry_scr[...]
    for i in range(blocks):
        ps = slice(i * PAGE_SIZE, (i + 1) * PAGE_SIZE)
        o_ref[0, :, ps], carry = _block_cumsum_t(x_ref[0, ps, :], carry)
    carry_scr[...] = carry


def _cumsum_t(x, blocks=4):
    b, l, width = x.shape
    blocks = min(blocks, l // PAGE_SIZE)
    span = blocks * PAGE_SIZE
    assert width == LANES and l % span == 0
    return pl.pallas_call(
        functools.partial(_cumsum_t_body, blocks=blocks),
        grid=(b, l // span),
        in_specs=[pl.BlockSpec((1, span, width), lambda i, j: (i, j, 0))],
        out_specs=pl.BlockSpec((1, N_HEADS_A, span), lambda i, j: (i, 0, j)),
        out_shape=jax.ShapeDtypeStruct((b, N_HEADS_A, l), F32),
        scratch_shapes=[pltpu.VMEM((1, LANES), F32)],
        compiler_params=_cparams(("arbitrary", "arbitrary")),
        name="logf_cumsum",
    )(x)


def _cumsum_lanes_body(x_ref, o_ref):
    n = x_ref.shape[1]
    r = lax.broadcasted_iota(jnp.int32, (n, n), 0)
    c = lax.broadcasted_iota(jnp.int32, (n, n), 1)
    tri = (r <= c).astype(BF16)
    hi, mid, lo = _split3(x_ref[...])
    o_ref[...] = _dot(hi, tri) + _dot(mid, tri) + _dot(lo, tri)


def _cumsum_lanes(x, rows=1024):
    r, n = x.shape
    assert r % rows == 0
    return pl.pallas_call(
        _cumsum_lanes_body,
        grid=(r // rows,),
        in_specs=[pl.BlockSpec((rows, n), lambda i: (i, 0))],
        out_specs=pl.BlockSpec((rows, n), lambda i: (i, 0)),
        out_shape=jax.ShapeDtypeStruct((r, n), F32),
        compiler_params=_cparams(("arbitrary",)),
        name="page_logf_cumsum",
    )(x)


def _fox_prompt_body(it_ref, jt_ref, q_ref, k_ref, v_ref, f_ref, o_ref, qm_scr, m_scr, acc_scr, *,
                     tile, sub):
    t = pl.program_id(2)
    i = it_ref[t]
    j = jt_ref[t]
    lane = lax.broadcasted_iota(jnp.int32, (tile, LANES), 1)
    first = lane < HEAD_DIM_A

    @pl.when(j == 0)
    def _():
        q = q_ref[0] * (HEAD_DIM_A ** -0.5 * LOG2E)
        qm_scr[0] = jnp.where(first, q, 0.0).astype(BF16)
        qm_scr[1] = jnp.where(first, 0.0, q).astype(BF16)
        m_scr[...] = jnp.full_like(m_scr, -1e30)
        acc_scr[...] = jnp.zeros_like(acc_scr)

    def step(diagonal):
        one = jnp.ones((), BF16)
        first_s = lax.broadcasted_iota(jnp.int32, (sub, LANES), 1) < HEAD_DIM_A
        for kj in range(tile // sub):
            ks = slice(kj * sub, (kj + 1) * sub)
            kb = k_ref[0, ks, :]
            vb = v_ref[0, ks, :]
            v_augs = (jnp.where(first_s, vb, one), jnp.where(first_s, one, vb))
            for qi in range(tile // sub):
                if diagonal and kj > qi:
                    continue
                qs = slice(qi * sub, (qi + 1) * sub)
                for h in range(2):
                    s = _dot_nt(qm_scr[h, qs, :], kb) - f_ref[0, 0, h:h + 1, ks] * LOG2E
                    if diagonal and kj == qi:
                        row = lax.broadcasted_iota(jnp.int32, (sub, sub), 0)
                        col = lax.broadcasted_iota(jnp.int32, (sub, sub), 1)
                        s = jnp.where(col <= row, s, NEG_INF)
                    m_old = m_scr[h, qs, :]
                    m_new = jnp.maximum(m_old, jnp.max(s, axis=-1, keepdims=True))
                    alpha = jnp.exp2(m_old - m_new)
                    p = jnp.exp2(s - jnp.concatenate([m_new] * (sub // LANES), axis=1))
                    acc_scr[h, qs, :] = alpha * acc_scr[h, qs, :] + _dot(p.astype(BF16), v_augs[h])
                    m_scr[h, qs, :] = m_new

    @pl.when(j < i)
    def _():
        step(False)

    @pl.when(j == i)
    def _():
        step(True)
        a0 = acc_scr[0]
        a1 = acc_scr[1]
        den = jnp.where(first, pltpu.roll(a0, HEAD_DIM_A, 1), pltpu.roll(a1, HEAD_DIM_A, 1))
        o_ref[0] = jnp.where(first, a0, a1) / den


def _fox_prompt(q, k, v, fcum, tile=1024, sub=512):
    b, l, w = q.shape
    npair = w // LANES
    tile = min(tile, l)
    sub = min(sub, tile)
    assert l % tile == 0 and tile % sub == 0
    nt = l // tile
    it = np.concatenate([np.full(i + 1, i, np.int32) for i in range(nt)])
    jt = np.concatenate([np.arange(i + 1, dtype=np.int32) for i in range(nt)])
    grid_spec = pltpu.PrefetchScalarGridSpec(
        num_scalar_prefetch=2,
        grid=(b, npair, len(it)),
        in_specs=[pl.BlockSpec((1, tile, LANES), lambda bi, p, t, it, jt: (bi, it[t], p)),
                  pl.BlockSpec((1, tile, LANES), lambda bi, p, t, it, jt: (bi, jt[t], p)),
                  pl.BlockSpec((1, tile, LANES), lambda bi, p, t, it, jt: (bi, jt[t], p)),
                  pl.BlockSpec((1, 1, 2, tile), lambda bi, p, t, it, jt: (bi, p, 0, jt[t]))],
        out_specs=pl.BlockSpec((1, tile, LANES), lambda bi, p, t, it, jt: (bi, it[t], p)),
        scratch_shapes=[pltpu.VMEM((2, tile, LANES), BF16), pltpu.VMEM((2, tile, LANES), F32),
                        pltpu.VMEM((2, tile, LANES), F32)],
    )
    return pl.pallas_call(
        functools.partial(_fox_prompt_body, tile=tile, sub=sub),
        grid_spec=grid_spec,
        out_shape=jax.ShapeDtypeStruct((b, l, w), F32),
        compiler_params=_cparams(("arbitrary", "arbitrary", "arbitrary")),
        name="fox_prompt",
    )(jnp.asarray(it), jnp.asarray(jt), q, k, v, fcum)


def _fox_sample_body(pt_ref, q_ref, kn_ref, vn_ref, lfn_ref, *rest, pages_per_step):
    pg = pages_per_step
    k_refs = rest[0:pg]
    v_refs = rest[pg:2 * pg]
    c_refs = rest[2 * pg:3 * pg]
    o_ref, qbd_scr, m_scr, l_scr, carry_scr, acc_scr = rest[3 * pg:]
    j = pl.program_id(1)
    nq = q_ref.shape[1]
    rows = N_HEADS_A * nq
    width = N_HEADS_A * HEAD_DIM_A

    @pl.when(j == 0)
    def _():
        q = q_ref[0] * (HEAD_DIM_A ** -0.5)
        qt = jnp.broadcast_to(q[None], (N_HEADS_A, nq, width)).reshape(rows, width)
        row = lax.broadcasted_iota(jnp.int32, (rows, width), 0)
        lane = lax.broadcasted_iota(jnp.int32, (rows, width), 1)
        own = (lane // HEAD_DIM_A) == (row // nq)
        qbd_scr[...] = jnp.where(own, qt, 0.0).astype(BF16)
        m_scr[...] = jnp.full_like(m_scr, -1e30)
        l_scr[...] = jnp.zeros_like(l_scr)
        carry_scr[...] = jnp.zeros_like(carry_scr)
        acc_scr[...] = jnp.zeros_like(acc_scr)

    def attend(blocks, valid, transposed):
        qbd = qbd_scr[...]
        carry = carry_scr[...]
        scores = []
        for kf, _, c16 in blocks:
            s = _dot(qbd, kf.astype(BF16)) if transposed else _dot_nt(qbd, kf.astype(BF16))
            nk = s.shape[1]
            cexp = jnp.broadcast_to(c16[:, None, :], (N_HEADS_A, nq, nk)).reshape(rows, nk)
            s = s - (cexp + carry)
            if valid is not None:
                s = jnp.where(valid, s, NEG_INF)
            scores.append(s)
            carry = carry + cexp[:, nk - 1:nk]
        carry_scr[...] = carry
        smax = scores[0]
        for s in scores[1:]:
            smax = jnp.maximum(smax, s)
        m_old = m_scr[...]
        m_new = jnp.maximum(m_old, jnp.max(smax, axis=-1, keepdims=True))
        alpha = jnp.exp(m_old - m_new)
        m_scr[...] = m_new
        psum = None
        pv = None
        for s, (_, vf, _) in zip(scores, blocks):
            p = jnp.exp(s - m_new)
            psum = p if psum is None else psum + p
            pb = p.astype(BF16)
            d = _dot_nt(pb, vf.astype(BF16)) if transposed else _dot(pb, vf.astype(BF16))
            pv = d if pv is None else pv + d
        l_scr[...] = alpha * l_scr[...] + jnp.sum(psum, axis=-1, keepdims=True)
        acc_scr[...] = acc_scr[...] * alpha + pv

    attend([(k_refs[g][0], v_refs[g][0], c_refs[g][0]) for g in range(pg)], None, True)

    @pl.when(j == pl.num_programs(1) - 1)
    def _():
        zpad = jnp.zeros((PAGE_SIZE - nq, width), F32)
        kpad = jnp.concatenate([kn_ref[0], zpad], axis=0)
        vpad = jnp.concatenate([vn_ref[0], zpad], axis=0)
        row = lax.broadcasted_iota(jnp.int32, (rows, PAGE_SIZE), 0)
        key = lax.broadcasted_iota(jnp.int32, (rows, PAGE_SIZE), 1)
        lf_pad = jnp.concatenate([lfn_ref[0], jnp.zeros((PAGE_SIZE - nq, LANES), F32)], axis=0)
        c_new, _ = _block_cumsum_t(lf_pad, jnp.zeros((1, LANES), F32))
        attend([(kpad, vpad, c_new)], key <= (row % nq), False)
        o = acc_scr[...] / l_scr[...]
        for h in range(N_HEADS_A):
            o_ref[0, :, h * HEAD_DIM_A:(h + 1) * HEAD_DIM_A] = (
                o[h * nq:(h + 1) * nq, h * HEAD_DIM_A:(h + 1) * HEAD_DIM_A])


def _fox_sample(q, k_new, v_new, lf_new, cache_k, cache_v, c_pages, page_table, pages_per_step=8):
    b, nq, w = q.shape
    n_pages = page_table.shape[1]
    pg = pages_per_step
    assert n_pages % pg == 0
    rows = N_HEADS_A * nq

    def page_map(g):
        return lambda bi, j, pt: (pt[bi * n_pages + j * pg + g], 0, 0)

    seq_map = lambda bi, j, pt: (bi, 0, 0)
    in_specs = [pl.BlockSpec((1, nq, w), seq_map), pl.BlockSpec((1, nq, w), seq_map),
                pl.BlockSpec((1, nq, w), seq_map), pl.BlockSpec((1, nq, LANES), seq_map)]
    in_specs += [pl.BlockSpec((1, w, PAGE_SIZE), page_map(g)) for g in range(pg)]
    in_specs += [pl.BlockSpec((1, w, PAGE_SIZE), page_map(g)) for g in range(pg)]
    in_specs += [pl.BlockSpec((1, N_HEADS_A, PAGE_SIZE), page_map(g)) for g in range(pg)]
    grid_spec = pltpu.PrefetchScalarGridSpec(
        num_scalar_prefetch=1,
        grid=(b, n_pages // pg),
        in_specs=in_specs,
        out_specs=pl.BlockSpec((1, nq, w), seq_map),
        scratch_shapes=[pltpu.VMEM((rows, w), BF16), pltpu.VMEM((rows, 1), F32), pltpu.VMEM((rows, 1), F32),
                        pltpu.VMEM((rows, 1), F32), pltpu.VMEM((rows, w), F32)],
    )
    args = [q, k_new, v_new, lf_new] + [cache_k] * pg + [cache_v] * pg + [c_pages] * pg
    return pl.pallas_call(
        functools.partial(_fox_sample_body, pages_per_step=pg),
        grid_spec=grid_spec,
        out_shape=jax.ShapeDtypeStruct((b, nq, w), F32),
        compiler_params=_cparams(("arbitrary", "arbitrary")),
        name="fox_sample",
    )(page_table.reshape(-1), *args)


def _causal_conv_silu(ext_scr, cw_ref, cb_ref, n):
    acc = cb_ref[...] + cw_ref[0:1, :] * ext_scr[8 - (CONV_K - 1):8 - (CONV_K - 1) + n, :]
    for i in range(1, CONV_K):
        acc = acc + cw_ref[i:i + 1, :] * ext_scr[8 - (CONV_K - 1) + i:8 - (CONV_K - 1) + i + n, :]
    return _silu(acc)


def _pair_select(first, a, b):
    return jnp.where(first, a, b)


def _ssd_prompt_body(xbc_ref, z_ref, sm_ref, cw_ref, cb_ref, alog_ref, dskip_ref, nw_ref,
                     y_ref, st_ref, st_scr, ext_scr):
    c = pl.program_id(1)
    q = SSD_CHUNK
    pair_w = 2 * HEAD_DIM_S

    @pl.when(c == 0)
    def _():
        st_scr[...] = jnp.zeros_like(st_scr)
        ext_scr[0:8, :] = jnp.zeros((8, CONV_DIM), F32)

    cur = xbc_ref[0]
    ext_scr[8:8 + q, :] = cur
    act = _causal_conv_silu(ext_scr, cw_ref, cb_ref, q)
    ext_scr[0:8, :] = cur[q - 8:q, :]

    sm = sm_ref[0]
    a_lane = -jnp.exp(alog_ref[...])
    tri_mask = _tri_lower(q)
    a_all = _dot_sel_left(tri_mask.astype(BF16), sm * a_lane)
    a_t = a_all.T
    dt_t = sm.T
    w_t = jnp.exp(a_t[:, q - 1:q] - a_t) * dt_t
    ea_all = jnp.exp(a_all)
    lane = lax.broadcasted_iota(jnp.int32, (1, pair_w), 1)
    first = lane < HEAD_DIM_S

    for g in range(N_GROUPS_S):
        bc = act[:, D_INNER + g * D_STATE:D_INNER + (g + 1) * D_STATE]
        cc = act[:, D_INNER + (N_GROUPS_S + g) * D_STATE:D_INNER + (N_GROUPS_S + g + 1) * D_STATE]
        cb = _dot_nt(cc.astype(BF16), bc.astype(BF16))
        bc_t = bc.T
        gated = []
        ssq = jnp.zeros((q, 1), F32)
        for pp in range(HEADS_PER_GROUP // 2):
            h0 = g * HEADS_PER_GROUP + 2 * pp
            col = h0 * HEAD_DIM_S
            x_pair = act[:, col:col + pair_w]
            s_pair = st_scr[:, col:col + pair_w]
            rhs = jnp.concatenate([x_pair.astype(BF16), s_pair.astype(BF16)], axis=0)
            ys, us, el = [], [], []
            for h in (h0, h0 + 1):
                li = DT_LANE0 + h
                seg = a_all[:, li:li + 1] - a_t[li:li + 1, :]
                dec = jnp.exp(jnp.where(tri_mask, seg, NEG_INF))
                m_h = cb * dec * dt_t[li:li + 1, :]
                lhs = jnp.concatenate([m_h, cc * ea_all[:, li:li + 1]], axis=1).astype(BF16)
                ys.append(_dot(lhs, rhs))
                us.append(_dot((bc_t * w_t[li:li + 1, :]).astype(BF16), x_pair.astype(BF16)))
                el.append(ea_all[q - 1:q, li:li + 1])
            y_pair = _pair_select(first, ys[0], ys[1]) + x_pair * dskip_ref[:, col:col + pair_w]
            st_scr[:, col:col + pair_w] = (s_pair * _pair_select(first, el[0], el[1])
                                           + _pair_select(first, us[0], us[1]))
            gp = y_pair * _silu(z_ref[0, :, col:col + pair_w])
            ssq = ssq + jnp.sum(gp * gp, axis=-1, keepdims=True)
            gated.append((col, gp))
        rs = lax.rsqrt(ssq / GROUP_W + EPS)
        for col, gp in gated:
            y_ref[0, :, col:col + pair_w] = (gp * rs * nw_ref[:, col:col + pair_w]).astype(y_ref.dtype)

    @pl.when(c == pl.num_programs(1) - 1)
    def _():
        for blk in range(D_INNER // pair_w):
            st_ref[0, blk * pair_w:(blk + 1) * pair_w, :] = st_scr[:, blk * pair_w:(blk + 1) * pair_w].T


def _ssd_prompt(xbc, z, small, conv_w, conv_b, alog_lane, dskip_exp, norm_w):
    b, l, _ = xbc.shape
    q = SSD_CHUNK
    full = lambda shape: pl.BlockSpec(shape, lambda bi, c: (0,) * len(shape))
    return pl.pallas_call(
        _ssd_prompt_body,
        grid=(b, l // q),
        in_specs=[pl.BlockSpec((1, q, CONV_DIM), lambda bi, c: (bi, c, 0)),
                  pl.BlockSpec((1, q, D_INNER), lambda bi, c: (bi, c, 0)),
                  pl.BlockSpec((1, q, LANES), lambda bi, c: (bi, c, 0)),
                  full((CONV_K, CONV_DIM)), full((1, CONV_DIM)), full((1, LANES)),
                  full((1, D_INNER)), full((1, D_INNER))],
        out_specs=[pl.BlockSpec((1, q, D_INNER), lambda bi, c: (bi, c, 0)),
                   pl.BlockSpec((1, D_INNER, D_STATE), lambda bi, c: (bi, 0, 0))],
        out_shape=[jax.ShapeDtypeStruct((b, l, D_INNER), BF16),
                   jax.ShapeDtypeStruct((b, D_INNER, D_STATE), F32)],
        scratch_shapes=[pltpu.VMEM((D_STATE, D_INNER), F32), pltpu.VMEM((8 + q, CONV_DIM), F32)],
        compiler_params=_cparams(("arbitrary", "arbitrary")),
        name="ssd_prompt",
    )(xbc, z, small, conv_w, conv_b, alog_lane, dskip_exp, norm_w)


def _ssd_sample_body(xbc_ref, c0_ref, z_ref, sm_ref, st_ref, cw_ref, cb_ref, alog_ref, dskip_ref, nw_ref,
                     y_ref, so_ref, ext_scr):
    l = xbc_ref.shape[1]
    pair_w = 2 * HEAD_DIM_S
    ext_scr[0:8, :] = jnp.zeros((8, CONV_DIM), F32)
    ext_scr[8 - (CONV_K - 1):8, :] = c0_ref[0]
    ext_scr[8:8 + l, :] = xbc_ref[0]
    act = _causal_conv_silu(ext_scr, cw_ref, cb_ref, l)

    sm = sm_ref[0]
    dta = sm * (-jnp.exp(alog_ref[...]))
    trow = lax.broadcasted_iota(jnp.int32, (l, LANES), 0)
    a_c = jnp.zeros((l, LANES), F32)
    for s in range(l):
        a_c = a_c + jnp.where(trow >= s, dta[s:s + 1, :], 0.0)
    lane = lax.broadcasted_iota(jnp.int32, (1, pair_w), 1)
    first = lane < HEAD_DIM_S

    def expand(v):
        cols = []
        for pp in range(N_HEADS_S // 2):
            li = DT_LANE0 + 2 * pp
            cols.append(_pair_select(first, v[:, li:li + 1], v[:, li + 1:li + 2]))
        return jnp.concatenate(cols, axis=1)

    a_x = expand(a_c)
    dt_x = expand(sm)
    ea_x = jnp.exp(a_x)
    w_x = jnp.exp(a_x[l - 1:l, :] - a_x) * dt_x
    xs = act[:, 0:D_INNER]
    zpad = jnp.zeros((PAGE_SIZE - l, D_STATE), F32)
    trow_x = lax.broadcasted_iota(jnp.int32, (l, GROUP_W), 0)

    xw_pad = jnp.concatenate([xs * w_x, jnp.zeros((PAGE_SIZE - l, D_INNER), F32)], axis=0)
    ea_last = jnp.exp(a_c[l - 1:l, :])
    rowsel = lax.broadcasted_iota(jnp.int32, (pair_w, 1), 0) < HEAD_DIM_S

    for g in range(N_GROUPS_S):
        gc = g * GROUP_W
        bc = act[:, D_INNER + g * D_STATE:D_INNER + (g + 1) * D_STATE]
        cc = act[:, D_INNER + (N_GROUPS_S + g) * D_STATE:D_INNER + (N_GROUPS_S + g + 1) * D_STATE]
        b_pad = jnp.concatenate([bc, zpad], axis=0).astype(BF16)
        cb = _dot_nt(cc.astype(BF16), b_pad)
        s_g = st_ref[0, gc:gc + GROUP_W, :]
        y = _dot_nt(cc.astype(BF16), s_g.astype(BF16)) * ea_x[:, gc:gc + GROUP_W]
        x_g = xs[:, gc:gc + GROUP_W]
        a_g = a_x[:, gc:gc + GROUP_W]
        dt_g = dt_x[:, gc:gc + GROUP_W]
        for s in range(l):
            dec = jnp.exp(jnp.where(trow_x >= s, a_g - a_g[s:s + 1, :], NEG_INF))
            y = y + cb[:, s:s + 1] * dec * (dt_g[s:s + 1, :] * x_g[s:s + 1, :])
        y = y + x_g * dskip_ref[:, gc:gc + GROUP_W]
        gp = y * _silu(z_ref[0, :, gc:gc + GROUP_W])
        rs = lax.rsqrt(jnp.sum(gp * gp, axis=-1, keepdims=True) / GROUP_W + EPS)
        y_ref[0, :, gc:gc + GROUP_W] = gp * rs * nw_ref[:, gc:gc + GROUP_W]
        for pp in range(HEADS_PER_GROUP // 2):
            h0 = g * HEADS_PER_GROUP + 2 * pp
            col = h0 * HEAD_DIM_S
            li = DT_LANE0 + h0
            u = _dot(xw_pad[:, col:col + pair_w].T.astype(BF16), b_pad)
            e_col = jnp.where(rowsel, ea_last[:, li:li + 1], ea_last[:, li + 1:li + 2])
            so_ref[0, col:col + pair_w, :] = st_ref[0, col:col + pair_w, :] * e_col + u


def _ssd_sample(xbc, conv0, z, small, state, conv_w, conv_b, alog_lane, dskip_exp, norm_w):
    b, l, _ = xbc.shape
    full = lambda shape: pl.BlockSpec(shape, lambda bi: (0,) * len(shape))
    seq = lambda shape: pl.BlockSpec(shape, lambda bi: (bi, 0, 0))
    return pl.pallas_call(
        _ssd_sample_body,
        grid=(b,),
        in_specs=[seq((1, l, CONV_DIM)), seq((1, CONV_K - 1, CONV_DIM)), seq((1, l, D_INNER)),
                  seq((1, l, LANES)), seq((1, D_INNER, D_STATE)),
                  full((CONV_K, CONV_DIM)), full((1, CONV_DIM)), full((1, LANES)),
                  full((1, D_INNER)), full((1, D_INNER))],
        out_specs=[seq((1, l, D_INNER)), seq((1, D_INNER, D_STATE))],
        out_shape=[jax.ShapeDtypeStruct((b, l, D_INNER), F32),
                   jax.ShapeDtypeStruct((b, D_INNER, D_STATE), F32)],
        scratch_shapes=[pltpu.VMEM((8 + l, CONV_DIM), F32)],
        compiler_params=_cparams(("arbitrary",)),
        name="ssd_sample",
    )(xbc, conv0, z, small, state, conv_w, conv_b, alog_lane, dskip_exp, norm_w)


def _post_mixer_body(x_ref, oa_ref, ys_ref, gate_ref, g1_ref, sh2_ref, sc2_ref, wa_ref, ws_ref, wo_ref,
                     n2_ref, wr_ref, br_ref, cin_ref, x2_ref, h_ref, te_ref, tw_ref, rk_ref, cout_ref,
                     count_scr):
    bb, bl, d = x_ref.shape
    rows = bb * bl

    @pl.when(jnp.logical_and(pl.program_id(0) == 0, pl.program_id(1) == 0))
    def _():
        count_scr[...] = cin_ref[...]

    oa = oa_ref[...].reshape(rows, d).astype(BF16)
    ys = ys_ref[...].reshape(rows, D_INNER).astype(BF16)
    y_attn = _dot(oa, wa_ref[...])
    y_ssm = _dot(ys, ws_ref[...])
    gate = gate_ref[...]
    merged = _sigmoid(gate[:, 0:d]) * y_attn + _sigmoid(gate[:, d:2 * d]) * y_ssm
    mix = _dot(merged.astype(BF16), wo_ref[...])
    x2 = x_ref[...] + g1_ref[...] * mix.reshape(bb, bl, d)
    x2_ref[...] = x2
    ms = jnp.mean(x2 * x2, axis=-1, keepdims=True)
    hf = (x2 * lax.rsqrt(ms + EPS) * n2_ref[...]) * (1.0 + sc2_ref[...]) + sh2_ref[...]
    hf = hf.reshape(rows, d)
    h_ref[...] = hf
    lane = lax.broadcasted_iota(jnp.int32, (rows, LANES), 1)
    logits = jnp.where(lane < N_EXPERTS, _dot3(hf, wr_ref[...]) + br_ref[...], NEG_INF)
    vals, idxs = [], []
    for _ in range(TOP_K):
        mx = jnp.max(logits, axis=-1, keepdims=True)
        idx = jnp.min(jnp.where(logits == mx, lane, LANES), axis=-1, keepdims=True)
        vals.append(mx)
        idxs.append(idx)
        logits = jnp.where(lane == idx, NEG_INF, logits)
    ex = [jnp.exp(v - vals[0]) for v in vals]
    den = ex[0] + ex[1] + ex[2] + ex[3]
    chosen = [lane == idxs[k] for k in range(TOP_K)]
    cnt = chosen[0].astype(F32)
    for k in range(1, TOP_K):
        cnt = cnt + chosen[k].astype(F32)
    r_i = lax.broadcasted_iota(jnp.int32, (rows, rows), 0)
    c_i = lax.broadcasted_iota(jnp.int32, (rows, rows), 1)
    before = _dot((c_i < r_i).astype(BF16), cnt.astype(BF16)) + count_scr[...]
    te = jnp.zeros((rows, LANES), jnp.int32)
    tw = jnp.zeros((rows, LANES), F32)
    rk = jnp.zeros((rows, LANES), jnp.int32)
    for k in range(TOP_K):
        rank = jnp.sum(jnp.where(chosen[k], before, 0.0), axis=-1, keepdims=True)
        te = jnp.where(lane == k, idxs[k], te)
        tw = jnp.where(lane == k, ex[k] / den, tw)
        rk = jnp.where(lane == k, rank.astype(jnp.int32), rk)
    te_ref[...] = te
    tw_ref[...] = tw
    rk_ref[...] = rk
    count_scr[...] = count_scr[...] + jnp.sum(cnt, axis=0, keepdims=True)
    cout_ref[...] = count_scr[...]


def _post_mixer(x, o_attn, y_ssm, gate, mod, w_attn_out, w_ssm_out, w_o, norm2_w, w_router, b_router,
                counts_in, rows=256):
    b, l, d = x.shape
    bb, bl, rows = _row_blocks(b, l, rows)
    nl = l // bl
    tok = lambda w: pl.BlockSpec((bb, bl, w), lambda i, j: (i, j, 0))
    flat = lambda w: pl.BlockSpec((rows, w), lambda i, j: (i * nl + j, 0))
    modc = lambda c: pl.BlockSpec((bb, 1, d), lambda i, j: (i, 0, c))
    full = lambda shape: pl.BlockSpec(shape, lambda i, j: (0,) * len(shape))
    return pl.pallas_call(
        _post_mixer_body,
        grid=(b // bb, nl),
        in_specs=[tok(d), tok(d), tok(D_INNER), flat(2 * d), modc(2), modc(3), modc(4),
                  full((d, d)), full((D_INNER, d)), full((d, d)), full((1, d)),
                  full((d, LANES)), full((1, LANES)), full((1, LANES))],
        out_specs=[tok(d), flat(d), flat(LANES), flat(LANES), flat(LANES), full((1, LANES))],
        out_shape=[jax.ShapeDtypeStruct((b, l, d), F32), jax.ShapeDtypeStruct((b * l, d), F32),
                   jax.ShapeDtypeStruct((b * l, LANES), jnp.int32), jax.ShapeDtypeStruct((b * l, LANES), F32),
                   jax.ShapeDtypeStruct((b * l, LANES), jnp.int32), jax.ShapeDtypeStruct((1, LANES), F32)],
        scratch_shapes=[pltpu.VMEM((1, LANES), F32)],
        compiler_params=_cparams(("arbitrary", "arbitrary")),
        name="post_mixer",
    )(x, o_attn, y_ssm, gate, mod, mod, mod, w_attn_out, w_ssm_out, w_o, norm2_w.reshape(1, d),
      w_router, b_router, counts_in)


MOE_ROWS = 256
SPLIT_COLS = 512


def _experts_body(be_ref, nb_ref, x_ref, wgu_ref, bg_ref, bu_ref, wdn_ref, bd_ref, o_ref,
                  wg_scr, wu_scr, wd_scr, t_scr):
    i = pl.program_id(0)
    d = x_ref.shape[1]
    half = SPLIT_COLS // 2

    @pl.when(jnp.logical_or(i == 0, be_ref[i] != be_ref[jnp.maximum(i - 1, 0)]))
    def _():
        for cb in range(wgu_ref.shape[2] // SPLIT_COLS):
            for kc in range(d // LANES):
                ks = slice(kc * LANES, (kc + 1) * LANES)
                t_scr[kc] = wgu_ref[0, ks, cb * SPLIT_COLS:(cb + 1) * SPLIT_COLS].T
                rows = slice(cb * half, (cb + 1) * half)
                wg_scr[rows, ks] = t_scr[kc, pl.ds(0, half, stride=2), :].astype(BF16)
                wu_scr[rows, ks] = t_scr[kc, pl.ds(1, half, stride=2), :].astype(BF16)
        wd_scr[...] = wdn_ref[0].astype(BF16)

    @pl.when(i < nb_ref[0])
    def _():
        x = x_ref[...].astype(BF16)
        gate = jnp.minimum(_dot_nt(x, wg_scr[...]) + bg_ref[0], SWIGLU_LIMIT)
        up = jnp.clip(_dot_nt(x, wu_scr[...]) + bu_ref[0], -SWIGLU_LIMIT, SWIGLU_LIMIT)
        act = (up + 1.0) * (gate * _sigmoid(SWIGLU_ALPHA * gate))
        o_ref[...] = _dot(act.astype(BF16), wd_scr[...]) + bd_ref[0]

    @pl.when(i >= nb_ref[0])
    def _():
        o_ref[...] = jnp.zeros_like(o_ref)


def _experts(block_e, n_used, xg, w_gu, b_g, b_u, w_d, b_d):
    p, d = xg.shape
    ff = w_d.shape[1]
    wspec = lambda shape: pl.BlockSpec(shape, lambda i, be, nb: (be[i], 0, 0))
    grid_spec = pltpu.PrefetchScalarGridSpec(
        num_scalar_prefetch=2,
        grid=(p // MOE_ROWS,),
        in_specs=[pl.BlockSpec((MOE_ROWS, d), lambda i, be, nb: (i, 0)),
                  wspec((1, d, 2 * ff)), wspec((1, 1, ff)), wspec((1, 1, ff)),
                  wspec((1, ff, d)), wspec((1, 1, d))],
        out_specs=pl.BlockSpec((MOE_ROWS, d), lambda i, be, nb: (i, 0)),
        scratch_shapes=[pltpu.VMEM((ff, d), BF16), pltpu.VMEM((ff, d), BF16), pltpu.VMEM((ff, d), BF16),
                        pltpu.VMEM((d // LANES, SPLIT_COLS, LANES), F32)],
    )
    return pl.pallas_call(
        _experts_body,
        grid_spec=grid_spec,
        out_shape=jax.ShapeDtypeStruct((p, d), F32),
        compiler_params=_cparams(("arbitrary",)),
        name="moe_experts",
    )(block_e, n_used, xg, w_gu, b_g, b_u, w_d, b_d)


SC_WORKERS = 32
SC_CHUNK = 32


def _sc_gather_rows(table, idx):
    b = idx.shape[0]
    d = table.shape[1]
    per_w = b // SC_WORKERS
    n = per_w // SC_CHUNK
    assert b % (SC_WORKERS * SC_CHUNK) == 0 and n % 2 == 0
    mesh = plsc.VectorSubcoreMesh(core_axis_name="c", subcore_axis_name="s")

    @functools.partial(
        pl.kernel, mesh=mesh, out_type=jax.ShapeDtypeStruct((b, d), table.dtype),
        scratch_types=[pltpu.VMEM((n, SC_CHUNK), jnp.int32), pltpu.VMEM((2, SC_CHUNK, d), table.dtype),
                       pltpu.SemaphoreType.DMA((2,)), pltpu.SemaphoreType.DMA((2,))],
        name="sc_gather_rows")
    def gather(table_hbm, idx_hbm, out_hbm, idx_v, rows_v, gsem, wsem):
        wid = lax.axis_index("s") * 2 + lax.axis_index("c")
        base = wid * per_w
        pltpu.sync_copy(idx_hbm.at[wid], idx_v)

        def fetch(c, slot):
            return pltpu.make_async_copy(table_hbm.at[idx_v.at[c]], rows_v.at[slot], gsem.at[slot])

        def write(c, slot):
            off = pl.multiple_of(base + c * SC_CHUNK, SC_CHUNK)
            return pltpu.make_async_copy(rows_v.at[slot], out_hbm.at[pl.ds(off, SC_CHUNK)], wsem.at[slot])

        fetch(0, 0).start()

        @pl.loop(0, n, step=2)
        def _(c0):
            for slot in (0, 1):
                c = c0 + slot
                fetch(c, slot).wait()

                @pl.when(c >= 1)
                def _():
                    write(c - 1, 1 - slot).wait()

                @pl.when(c + 1 < n)
                def _():
                    fetch(c + 1, 1 - slot).start()

                write(c, slot).start()

        write(n - 1, 1).wait()

    return gather(table, idx.reshape(SC_WORKERS, n, SC_CHUNK))


def _route(top_e, rank, counts):
    t, k = top_e.shape
    s = t * k
    e_flat = top_e.reshape(-1)
    padded = (counts + MOE_ROWS - 1) // MOE_ROWS * MOE_ROWS
    pend = jnp.cumsum(padded)
    onehot = e_flat[:, None] == jnp.arange(N_EXPERTS, dtype=jnp.int32)[None, :]
    dest = jnp.sum(jnp.where(onehot, (pend - padded)[None, :], 0), axis=1) + rank.reshape(-1)
    n_blocks = -(-s // MOE_ROWS) + N_EXPERTS
    row_tok = (jnp.arange(n_blocks * MOE_ROWS, dtype=jnp.int32) % t).at[dest].set(
        jnp.arange(s, dtype=jnp.int32) // k)
    block_row0 = jnp.arange(n_blocks, dtype=jnp.int32) * MOE_ROWS
    block_e = jnp.minimum(jnp.sum((pend[None, :] <= block_row0[:, None]).astype(jnp.int32), axis=1),
                          N_EXPERTS - 1)
    n_used = (pend[-1:] // MOE_ROWS).astype(jnp.int32)
    return dest.reshape(t, k), row_tok, block_e, n_used


def _final_body(x_ref, ex_ref, tw_ref, g2_ref, nf_ref, y_ref):
    bb, bl, d = x_ref.shape
    tw = tw_ref[...]
    moe = ex_ref[0] * tw[:, 0:1]
    for k in range(1, TOP_K):
        moe = moe + ex_ref[k] * tw[:, k:k + 1]
    x = x_ref[...] + g2_ref[...] * moe.reshape(bb, bl, d)
    ms = jnp.mean(x * x, axis=-1, keepdims=True)
    y_ref[...] = x * lax.rsqrt(ms + EPS) * nf_ref[...]


def _final(x2, expert_out, top_w, row0, mod, norm_f_w, rows=512):
    b, l, d = x2.shape
    bb, bl, rows = _row_blocks(b, l, rows)
    nl = l // bl
    assert row0 % rows == 0
    blk0 = row0 // rows
    return pl.pallas_call(
        _final_body,
        grid=(b // bb, nl),
        in_specs=[pl.BlockSpec((bb, bl, d), lambda i, j: (i, j, 0)),
                  pl.BlockSpec((TOP_K, rows, d), lambda i, j: (0, blk0 + i * nl + j, 0)),
                  pl.BlockSpec((rows, LANES), lambda i, j: (blk0 + i * nl + j, 0)),
                  pl.BlockSpec((bb, 1, d), lambda i, j: (i, 0, 5)),
                  pl.BlockSpec((1, d), lambda i, j: (0, 0))],
        out_specs=pl.BlockSpec((bb, bl, d), lambda i, j: (i, j, 0)),
        out_shape=jax.ShapeDtypeStruct((b, l, d), F32),
        compiler_params=_cparams(("arbitrary", "arbitrary")),
        name="final_norm",
    )(x2, expert_out, top_w, mod, norm_f_w.reshape(1, d))


def _prep_in_weights(w_in, b_fgate, dt_bias):
    aw = N_HEADS_A * HEAD_DIM_A
    o = np.cumsum([0, aw, aw, aw, N_HEADS_A, D_INNER, CONV_DIM, N_HEADS_S, D_MODEL, D_MODEL])
    w_t = w_in.T
    rows = lambda i: w_t[o[i]:o[i + 1]]
    w_small = jnp.concatenate(
        [rows(3), rows(6), jnp.zeros((LANES - N_HEADS_A - N_HEADS_S, D_MODEL), F32)], axis=0).T
    b_small = jnp.concatenate(
        [b_fgate, dt_bias, jnp.zeros((LANES - N_HEADS_A - N_HEADS_S,), F32)]).reshape(1, LANES)
    big = {"q": rows(0), "k": rows(1), "v": rows(2), "z": rows(4), "xbc": rows(5),
           "gate": w_t[o[7]:o[9]]}
    return {n: w.astype(BF16) for n, w in big.items()}, w_small, b_small


def kernel(x_prompt, x_sample, cache_k, cache_v, cache_logf, state_ssm, state_conv, page_table, c_prompt, c_sample, w_cond, b_cond, norm1_w, w_in, b_fgate, conv_w, conv_b, dt_bias, a_log, d_skip, ssm_norm_w, w_attn_out, w_ssm_out, w_o, norm2_w, w_router, b_router, w_gate_up, b_gate_up, w_down, b_down, norm_f_w):
    assert w_in.shape[0] == 1, "single-layer trunk"
    bp, lp, d = x_prompt.shape
    bs, ls, _ = x_sample.shape
    tp, ts = bp * lp, bs * ls
    aw = N_HEADS_A * HEAD_DIM_A
    c_all = jnp.concatenate([c_prompt, c_sample, jnp.zeros((-(bp + bs) % 8, d), F32)], axis=0)
    mod = _adaln_mod(c_all, w_cond[0], b_cond[0])
    mod_p = mod[:bp].reshape(bp, 1, 6 * d)
    mod_s = mod[bp:bp + bs].reshape(bs, 1, 6 * d)

    w_big, w_small, b_small = _prep_in_weights(w_in[0], b_fgate[0], dt_bias[0])
    pad_lanes = LANES - N_HEADS_A - N_HEADS_S
    alog_lane = jnp.concatenate([jnp.zeros((N_HEADS_A,), F32), a_log[0], jnp.zeros((pad_lanes,), F32)]).reshape(1, LANES)
    dskip_exp = jnp.repeat(d_skip[0], HEAD_DIM_S).reshape(1, D_INNER)
    ssm_nw = ssm_norm_w[0].reshape(1, D_INNER)
    conv_b2 = conv_b[0].reshape(1, CONV_DIM)
    wa, ws, wo = w_attn_out[0].astype(BF16), w_ssm_out[0].astype(BF16), w_o[0].astype(BF16)
    w_router_pad = jnp.pad(w_router[0], ((0, 0), (0, LANES - N_EXPERTS)))
    b_router_pad = jnp.pad(b_router[0], (0, LANES - N_EXPERTS)).reshape(1, LANES)

    def in_proj(x, m, half_copies):
        h, small = _norm_in(x, m, norm1_w[0], w_small, b_small)
        return {n: _matmul_nt(h, w, also_bf16=n in half_copies) for n, w in w_big.items()}, small

    proj_p, small_p = in_proj(x_prompt, mod_p, ("k", "v"))
    proj_s, small_s = in_proj(x_sample, mod_s, ())
    (k_p, kb_p), (v_p, vb_p) = proj_p["k"], proj_p["v"]

    fcum = _cumsum_t(small_p.reshape(bp, lp, LANES)).reshape(bp, N_HEADS_A // 2, 2, lp)
    o_p = _fox_prompt(proj_p["q"].reshape(bp, lp, aw), kb_p.reshape(bp, lp, aw),
                      vb_p.reshape(bp, lp, aw), fcum)
    xbc_p = proj_p["xbc"].reshape(bp, lp, CONV_DIM)
    y_p, st_p = _ssd_prompt(xbc_p, proj_p["z"].reshape(bp, lp, D_INNER), small_p.reshape(bp, lp, LANES),
                            conv_w[0], conv_b2, alog_lane, dskip_exp, ssm_nw)

    n_phys = cache_k.shape[1]
    k_t = jnp.transpose(cache_k[0], (0, 2, 3, 1)).reshape(n_phys, aw, PAGE_SIZE)
    v_t = jnp.transpose(cache_v[0], (0, 2, 3, 1)).reshape(n_phys, aw, PAGE_SIZE)
    lf_t = jnp.swapaxes(cache_logf[0], 1, 2).reshape(n_phys * N_HEADS_A, PAGE_SIZE)
    c_pages = _cumsum_lanes(lf_t, rows=512).reshape(n_phys, N_HEADS_A, PAGE_SIZE)
    n_pages = page_table.shape[1]
    o_s = _fox_sample(proj_s["q"].reshape(bs, ls, aw), proj_s["k"].reshape(bs, ls, aw),
                      proj_s["v"].reshape(bs, ls, aw), small_s.reshape(bs, ls, LANES), k_t, v_t,
                      c_pages, page_table, pages_per_step=min(16, n_pages))
    xbc_s = proj_s["xbc"].reshape(bs, ls, CONV_DIM)
    y_s, st_s = _ssd_sample(xbc_s, state_conv[0], proj_s["z"].reshape(bs, ls, D_INNER),
                            small_s.reshape(bs, ls, LANES), state_ssm[0].reshape(bs, D_INNER, D_STATE),
                            conv_w[0], conv_b2, alog_lane, dskip_exp, ssm_nw)

    post = lambda x, o, y, gate, m, counts: _post_mixer(x, o, y, gate, m, wa, ws, wo, norm2_w[0],
                                                        w_router_pad, b_router_pad, counts)
    x2_p, hf_p, te_p, tw_p, rk_p, counts_p = post(x_prompt, o_p, y_p, proj_p["gate"], mod_p,
                                                  jnp.zeros((1, LANES), F32))
    x2_s, hf_s, te_s, tw_s, rk_s, counts = post(x_sample, o_s, y_s, proj_s["gate"], mod_s, counts_p)

    hf = jnp.concatenate([hf_p, hf_s], axis=0)
    top_e = jnp.concatenate([te_p[:, :TOP_K], te_s[:, :TOP_K]], axis=0)
    rank = jnp.concatenate([rk_p[:, :TOP_K], rk_s[:, :TOP_K]], axis=0)
    top_w = jnp.concatenate([tw_p, tw_s], axis=0)
    dest, row_tok, block_e, n_used = _route(top_e, rank, counts[0, :N_EXPERTS].astype(jnp.int32))
    rows = _experts(block_e, n_used, _sc_gather_rows(hf, row_tok), w_gate_up[0],
                    b_gate_up[0][:, None, 0::2], b_gate_up[0][:, None, 1::2],
                    w_down[0], b_down[0][:, None, :])
    expert_out = _sc_gather_rows(rows, dest.T.reshape(-1)).reshape(TOP_K, tp + ts, d)

    y_prompt = _final(x2_p, expert_out, top_w, 0, mod_p, norm_f_w)
    y_sample = _final(x2_s, expert_out, top_w, tp, mod_s, norm_f_w)

    def state_rows(k, v, small, st, xbc, conv0, b, l):
        conv_new = jnp.concatenate([conv0, xbc], axis=1)[:, -(CONV_K - 1):]
        return (k.reshape(1, b, l, N_HEADS_A, HEAD_DIM_A), v.reshape(1, b, l, N_HEADS_A, HEAD_DIM_A),
                small[:, :N_HEADS_A].reshape(1, b, l, N_HEADS_A),
                st.reshape(1, b, N_HEADS_S, HEAD_DIM_S, D_STATE), conv_new[None])

    sp = state_rows(k_p, v_p, small_p, st_p, xbc_p,
                    jnp.zeros((bp, CONV_K - 1, CONV_DIM), F32), bp, lp)
    ss = state_rows(proj_s["k"], proj_s["v"], small_s, st_s, xbc_s, state_conv[0], bs, ls)
    return (y_prompt, y_sample) + sp + ss
```

```python
import functools

import numpy as np
import jax
import jax.numpy as jnp
from jax import lax
from jax.experimental import pallas as pl
from jax.experimental.pallas import tpu as pltpu
from jax.experimental.pallas import tpu_sc as plsc

F32 = jnp.float32
BF16 = jnp.bfloat16

D_MODEL = 1024
N_HEADS_A = 16
HEAD_DIM_A = 64
PAGE_SIZE = 128
D_INNER = 2048
HEAD_DIM_S = 64
N_HEADS_S = 32
N_GROUPS_S = 4
HEADS_PER_GROUP = N_HEADS_S // N_GROUPS_S
GROUP_W = D_INNER // N_GROUPS_S
D_STATE = 128
CONV_K = 4
CONV_DIM = D_INNER + 2 * N_GROUPS_S * D_STATE
SSD_CHUNK = 128
N_EXPERTS = 32
TOP_K = 4
D_FF = D_MODEL
SWIGLU_LIMIT = 7.0
SWIGLU_ALPHA = 1.702
EPS = 1e-6

LANES = 128
DT_LANE0 = N_HEADS_A
NEG_INF = float("-inf")
LOG2E = 1.4426950408889634
VMEM_LIMIT = 56 * 1024 * 1024


def _cparams(sem):
    return pltpu.CompilerParams(dimension_semantics=sem, vmem_limit_bytes=VMEM_LIMIT)


def _split3(x):
    hi = x.astype(BF16)
    r = x - hi.astype(F32)
    mid = r.astype(BF16)
    lo = (r - mid.astype(F32)).astype(BF16)
    return hi, mid, lo


def _dot(a, b):
    return jnp.dot(a, b, preferred_element_type=F32)


def _dot_nt(a, b):
    return lax.dot_general(a, b, (((1,), (1,)), ((), ())), preferred_element_type=F32)


def _dot_sel_left(sel_bf16, x):
    hi, mid, lo = _split3(x)
    return _dot(sel_bf16, hi) + _dot(sel_bf16, mid) + _dot(sel_bf16, lo)


def _dot3(a, b):
    ah = a.astype(BF16)
    am = (a - ah.astype(F32)).astype(BF16)
    bh = b.astype(BF16)
    bm = (b - bh.astype(F32)).astype(BF16)
    return _dot(ah, bh) + _dot(ah, bm) + _dot(am, bh)


def _tri_lower(n):
    r = lax.broadcasted_iota(jnp.int32, (n, n), 0)
    c = lax.broadcasted_iota(jnp.int32, (n, n), 1)
    return r >= c


def _sigmoid(x):
    return 1.0 / (1.0 + jnp.exp(-x))


def _silu(x):
    return x * _sigmoid(x)


def _mod_body(c_ref, w_ref, b_ref, o_ref):
    c = c_ref[...]
    o_ref[...] = _dot3(_silu(c), w_ref[...]) + b_ref[...]


def _adaln_mod(c_all, w_cond, b_cond):
    rows, d = c_all.shape
    n = w_cond.shape[1]
    tn = 1024
    return pl.pallas_call(
        _mod_body,
        grid=(n // tn,),
        in_specs=[pl.BlockSpec((rows, d), lambda j: (0, 0)),
                  pl.BlockSpec((d, tn), lambda j: (0, j)),
                  pl.BlockSpec((1, tn), lambda j: (0, j))],
        out_specs=pl.BlockSpec((rows, tn), lambda j: (0, j)),
        out_shape=jax.ShapeDtypeStruct((rows, n), F32),
        compiler_params=_cparams(("arbitrary",)),
        name="adaln_mod",
    )(c_all, w_cond, b_cond.reshape(1, n))


def _norm_in_body(x_ref, sh_ref, sc_ref, nw_ref, ws_ref, bs_ref, h_ref, sm_ref):
    x = x_ref[...]
    bb, bl, d = x.shape
    ms = jnp.mean(x * x, axis=-1, keepdims=True)
    y = x * lax.rsqrt(ms + EPS) * nw_ref[...]
    h = (y * (1.0 + sc_ref[...]) + sh_ref[...]).reshape(bb * bl, d)
    h_ref[...] = h.astype(BF16)
    sm = _dot3(h, ws_ref[...]) + bs_ref[...]
    lane = lax.broadcasted_iota(jnp.int32, sm.shape, 1)
    t = jnp.log(1.0 + jnp.exp(-jnp.abs(sm)))
    sm_ref[...] = jnp.where(lane < N_HEADS_A, jnp.minimum(sm, 0.0) - t, jnp.maximum(sm, 0.0) + t)


def _row_blocks(b, l, rows):
    rows = min(rows, b * l)
    if l >= rows:
        assert l % rows == 0
        return 1, rows, rows
    assert rows % l == 0 and b % (rows // l) == 0
    return rows // l, l, rows


def _norm_in(x, mod, norm_w, w_small, b_small, rows=512):
    b, l, d = x.shape
    bb, bl, rows = _row_blocks(b, l, rows)
    nl = l // bl
    grid = (b // bb, nl)
    return pl.pallas_call(
        _norm_in_body,
        grid=grid,
        in_specs=[pl.BlockSpec((bb, bl, d), lambda i, j: (i, j, 0)),
                  pl.BlockSpec((bb, 1, d), lambda i, j: (i, 0, 0)),
                  pl.BlockSpec((bb, 1, d), lambda i, j: (i, 0, 1)),
                  pl.BlockSpec((1, d), lambda i, j: (0, 0)),
                  pl.BlockSpec((d, LANES), lambda i, j: (0, 0)),
                  pl.BlockSpec((1, LANES), lambda i, j: (0, 0))],
        out_specs=[pl.BlockSpec((rows, d), lambda i, j: (i * nl + j, 0)),
                   pl.BlockSpec((rows, LANES), lambda i, j: (i * nl + j, 0))],
        out_shape=[jax.ShapeDtypeStruct((b * l, d), BF16),
                   jax.ShapeDtypeStruct((b * l, LANES), F32)],
        compiler_params=_cparams(("arbitrary", "arbitrary")),
        name="norm_in",
    )(x, mod, mod, norm_w.reshape(1, d), w_small, b_small)


def _mm_body(a_ref, w_ref, o_ref, *half_ref):
    r = _dot_nt(a_ref[...], w_ref[...])
    o_ref[...] = r
    for h in half_ref:
        h[...] = r.astype(BF16)


def _matmul_nt(a, w_t, also_bf16=False, tm=1024, tn=1024):
    m, k = a.shape
    n = w_t.shape[0]
    tn = min(tn, n)
    tm = min(tm, m)
    out_spec = pl.BlockSpec((tm, tn), lambda j, i: (i, j))
    n_out = 2 if also_bf16 else 1
    res = pl.pallas_call(
        _mm_body,
        grid=(n // tn, m // tm),
        in_specs=[pl.BlockSpec((tm, k), lambda j, i: (i, 0)),
                  pl.BlockSpec((tn, k), lambda j, i: (j, 0))],
        out_specs=[out_spec] * n_out,
        out_shape=[jax.ShapeDtypeStruct((m, n), F32), jax.ShapeDtypeStruct((m, n), BF16)][:n_out],
        compiler_params=_cparams(("arbitrary", "arbitrary")),
        name="proj_matmul",
    )(a, w_t)
    return res if also_bf16 else res[0]


def _block_cumsum_t(blk, carry_row):
    tri = _tri_lower(PAGE_SIZE).astype(BF16)
    cs = _dot_sel_left(tri, blk) + carry_row
    return cs.T[0:N_HEADS_A, :], cs[PAGE_SIZE - 1:PAGE_SIZE, :]


def _cumsum_t_body(x_ref, o_ref, carry_scr, *, blocks):
    @pl.when(pl.program_id(1) == 0)
    def _():
        carry_scr[...] = jnp.zeros_like(carry_scr)

    carry = carry_scr[...]
    for i in range(blocks):
        ps = slice(i * PAGE_SIZE, (i + 1) * PAGE_SIZE)
        o_ref[0, :, ps], carry = _block_cumsum_t(x_ref[0, ps, :], carry)
    carry_scr[...] = carry


def _cumsum_t(x, blocks=4):
    b, l, width = x.shape
    blocks = min(blocks, l // PAGE_SIZE)
    span = blocks * PAGE_SIZE
    assert width == LANES and l % span == 0
    return pl.pallas_call(
        functools.partial(_cumsum_t_body, blocks=blocks),
        grid=(b, l // span),
        in_specs=[pl.BlockSpec((1, span, width), lambda i, j: (i, j, 0))],
        out_specs=pl.BlockSpec((1, N_HEADS_A, span), lambda i, j: (i, 0, j)),
        out_shape=jax.ShapeDtypeStruct((b, N_HEADS_A, l), F32),
        scratch_shapes=[pltpu.VMEM((1, LANES), F32)],
        compiler_params=_cparams(("arbitrary", "arbitrary")),
        name="logf_cumsum",
    )(x)


def _cumsum_lanes_body(x_ref, o_ref):
    n = x_ref.shape[1]
    r = lax.broadcasted_iota(jnp.int32, (n, n), 0)
    c = lax.broadcasted_iota(jnp.int32, (n, n), 1)
    tri = (r <= c).astype(BF16)
    hi, mid, lo = _split3(x_ref[...])
    o_ref[...] = _dot(hi, tri) + _dot(mid, tri) + _dot(lo, tri)


def _cumsum_lanes(x, rows=1024):
    r, n = x.shape
    assert r % rows == 0
    return pl.pallas_call(
        _cumsum_lanes_body,
        grid=(r // rows,),
        in_specs=[pl.BlockSpec((rows, n), lambda i: (i, 0))],
        out_specs=pl.BlockSpec((rows, n), lambda i: (i, 0)),
        out_shape=jax.ShapeDtypeStruct((r, n), F32),
        compiler_params=_cparams(("arbitrary",)),
        name="page_logf_cumsum",
    )(x)


def _fox_prompt_body(it_ref, jt_ref, q_ref, k_ref, v_ref, f_ref, o_ref, qm_scr, m_scr, acc_scr, *,
                     tile, sub):
    t = pl.program_id(2)
    i = it_ref[t]
    j = jt_ref[t]
    lane = lax.broadcasted_iota(jnp.int32, (tile, LANES), 1)
    first = lane < HEAD_DIM_A

    @pl.when(j == 0)
    def _():
        q = q_ref[0] * (HEAD_DIM_A ** -0.5 * LOG2E)
        qm_scr[0] = jnp.where(first, q, 0.0).astype(BF16)
        qm_scr[1] = jnp.where(first, 0.0, q).astype(BF16)
        m_scr[...] = jnp.full_like(m_scr, -1e30)
        acc_scr[...] = jnp.zeros_like(acc_scr)

    def step(diagonal):
        one = jnp.ones((), BF16)
        first_s = lax.broadcasted_iota(jnp.int32, (sub, LANES), 1) < HEAD_DIM_A
        for kj in range(tile // sub):
            ks = slice(kj * sub, (kj + 1) * sub)
            kb = k_ref[0, ks, :]
            vb = v_ref[0, ks, :]
            v_augs = (jnp.where(first_s, vb, one), jnp.where(first_s, one, vb))
            for qi in range(tile // sub):
                if diagonal and kj > qi:
                    continue
                qs = slice(qi * sub, (qi + 1) * sub)
                for h in range(2):
                    s = _dot_nt(qm_scr[h, qs, :], kb) - f_ref[0, 0, h:h + 1, ks] * LOG2E
                    if diagonal and kj == qi:
                        row = lax.broadcasted_iota(jnp.int32, (sub, sub), 0)
                        col = lax.broadcasted_iota(jnp.int32, (sub, sub), 1)
                        s = jnp.where(col <= row, s, NEG_INF)
                    m_old = m_scr[h, qs, :]
                    m_new = jnp.maximum(m_old, jnp.max(s, axis=-1, keepdims=True))
                    alpha = jnp.exp2(m_old - m_new)
                    p = jnp.exp2(s - jnp.concatenate([m_new] * (sub // LANES), axis=1))
                    acc_scr[h, qs, :] = alpha * acc_scr[h, qs, :] + _dot(p.astype(BF16), v_augs[h])
                    m_scr[h, qs, :] = m_new

    @pl.when(j < i)
    def _():
        step(False)

    @pl.when(j == i)
    def _():
        step(True)
        a0 = acc_scr[0]
        a1 = acc_scr[1]
        den = jnp.where(first, pltpu.roll(a0, HEAD_DIM_A, 1), pltpu.roll(a1, HEAD_DIM_A, 1))
        o_ref[0] = jnp.where(first, a0, a1) / den


def _fox_prompt(q, k, v, fcum, tile=1024, sub=512):
    b, l, w = q.shape
    npair = w // LANES
    tile = min(tile, l)
    sub = min(sub, tile)
    assert l % tile == 0 and tile % sub == 0
    nt = l // tile
    it = np.concatenate([np.full(i + 1, i, np.int32) for i in range(nt)])
    jt = np.concatenate([np.arange(i + 1, dtype=np.int32) for i in range(nt)])
    grid_spec = pltpu.PrefetchScalarGridSpec(
        num_scalar_prefetch=2,
        grid=(b, npair, len(it)),
        in_specs=[pl.BlockSpec((1, tile, LANES), lambda bi, p, t, it, jt: (bi, it[t], p)),
                  pl.BlockSpec((1, tile, LANES), lambda bi, p, t, it, jt: (bi, jt[t], p)),
                  pl.BlockSpec((1, tile, LANES), lambda bi, p, t, it, jt: (bi, jt[t], p)),
                  pl.BlockSpec((1, 1, 2, tile), lambda bi, p, t, it, jt: (bi, p, 0, jt[t]))],
        out_specs=pl.BlockSpec((1, tile, LANES), lambda bi, p, t, it, jt: (bi, it[t], p)),
        scratch_shapes=[pltpu.VMEM((2, tile, LANES), BF16), pltpu.VMEM((2, tile, LANES), F32),
                        pltpu.VMEM((2, tile, LANES), F32)],
    )
    return pl.pallas_call(
        functools.partial(_fox_prompt_body, tile=tile, sub=sub),
        grid_spec=grid_spec,
        out_shape=jax.ShapeDtypeStruct((b, l, w), F32),
        compiler_params=_cparams(("arbitrary", "arbitrary", "arbitrary")),
        name="fox_prompt",
    )(jnp.asarray(it), jnp.asarray(jt), q, k, v, fcum)


def _fox_sample_body(pt_ref, q_ref, kn_ref, vn_ref, lfn_ref, *rest, pages_per_step):
    pg = pages_per_step
    k_refs = rest[0:pg]
    v_refs = rest[pg:2 * pg]
    c_refs = rest[2 * pg:3 * pg]
    o_ref, qbd_scr, m_scr, l_scr, carry_scr, acc_scr = rest[3 * pg:]
    j = pl.program_id(1)
    nq = q_ref.shape[1]
    rows = N_HEADS_A * nq
    width = N_HEADS_A * HEAD_DIM_A

    @pl.when(j == 0)
    def _():
        q = q_ref[0] * (HEAD_DIM_A ** -0.5)
        qt = jnp.broadcast_to(q[None], (N_HEADS_A, nq, width)).reshape(rows, width)
        row = lax.broadcasted_iota(jnp.int32, (rows, width), 0)
        lane = lax.broadcasted_iota(jnp.int32, (rows, width), 1)
        own = (lane // HEAD_DIM_A) == (row // nq)
        qbd_scr[...] = jnp.where(own, qt, 0.0).astype(BF16)
        m_scr[...] = jnp.full_like(m_scr, -1e30)
        l_scr[...] = jnp.zeros_like(l_scr)
        carry_scr[...] = jnp.zeros_like(carry_scr)
        acc_scr[...] = jnp.zeros_like(acc_scr)

    def attend(blocks, valid, transposed):
        qbd = qbd_scr[...]
        carry = carry_scr[...]
        scores = []
        for kf, _, c16 in blocks:
            s = _dot(qbd, kf.astype(BF16)) if transposed else _dot_nt(qbd, kf.astype(BF16))
            nk = s.shape[1]
            cexp = jnp.broadcast_to(c16[:, None, :], (N_HEADS_A, nq, nk)).reshape(rows, nk)
            s = s - (cexp + carry)
            if valid is not None:
                s = jnp.where(valid, s, NEG_INF)
            scores.append(s)
            carry = carry + cexp[:, nk - 1:nk]
        carry_scr[...] = carry
        smax = scores[0]
        for s in scores[1:]:
            smax = jnp.maximum(smax, s)
        m_old = m_scr[...]
        m_new = jnp.maximum(m_old, jnp.max(smax, axis=-1, keepdims=True))
        alpha = jnp.exp(m_old - m_new)
        m_scr[...] = m_new
        psum = None
        pv = None
        for s, (_, vf, _) in zip(scores, blocks):
            p = jnp.exp(s - m_new)
            psum = p if psum is None else psum + p
            pb = p.astype(BF16)
            d = _dot_nt(pb, vf.astype(BF16)) if transposed else _dot(pb, vf.astype(BF16))
            pv = d if pv is None else pv + d
        l_scr[...] = alpha * l_scr[...] + jnp.sum(psum, axis=-1, keepdims=True)
        acc_scr[...] = acc_scr[...] * alpha + pv

    attend([(k_refs[g][0], v_refs[g][0], c_refs[g][0]) for g in range(pg)], None, True)

    @pl.when(j == pl.num_programs(1) - 1)
    def _():
        zpad = jnp.zeros((PAGE_SIZE - nq, width), F32)
        kpad = jnp.concatenate([kn_ref[0], zpad], axis=0)
        vpad = jnp.concatenate([vn_ref[0], zpad], axis=0)
        row = lax.broadcasted_iota(jnp.int32, (rows, PAGE_SIZE), 0)
        key = lax.broadcasted_iota(jnp.int32, (rows, PAGE_SIZE), 1)
        lf_pad = jnp.concatenate([lfn_ref[0], jnp.zeros((PAGE_SIZE - nq, LANES), F32)], axis=0)
        c_new, _ = _block_cumsum_t(lf_pad, jnp.zeros((1, LANES), F32))
        attend([(kpad, vpad, c_new)], key <= (row % nq), False)
        o = acc_scr[...] / l_scr[...]
        for h in range(N_HEADS_A):
            o_ref[0, :, h * HEAD_DIM_A:(h + 1) * HEAD_DIM_A] = (
                o[h * nq:(h + 1) * nq, h * HEAD_DIM_A:(h + 1) * HEAD_DIM_A])


def _fox_sample(q, k_new, v_new, lf_new, cache_k, cache_v, c_pages, page_table, pages_per_step=8):
    b, nq, w = q.shape
    n_pages = page_table.shape[1]
    pg = pages_per_step
    assert n_pages % pg == 0
    rows = N_HEADS_A * nq

    def page_map(g):
        return lambda bi, j, pt: (pt[bi * n_pages + j * pg + g], 0, 0)

    seq_map = lambda bi, j, pt: (bi, 0, 0)
    in_specs = [pl.BlockSpec((1, nq, w), seq_map), pl.BlockSpec((1, nq, w), seq_map),
                pl.BlockSpec((1, nq, w), seq_map), pl.BlockSpec((1, nq, LANES), seq_map)]
    in_specs += [pl.BlockSpec((1, w, PAGE_SIZE), page_map(g)) for g in range(pg)]
    in_specs += [pl.BlockSpec((1, w, PAGE_SIZE), page_map(g)) for g in range(pg)]
    in_specs += [pl.BlockSpec((1, N_HEADS_A, PAGE_SIZE), page_map(g)) for g in range(pg)]
    grid_spec = pltpu.PrefetchScalarGridSpec(
        num_scalar_prefetch=1,
        grid=(b, n_pages // pg),
        in_specs=in_specs,
        out_specs=pl.BlockSpec((1, nq, w), seq_map),
        scratch_shapes=[pltpu.VMEM((rows, w), BF16), pltpu.VMEM((rows, 1), F32), pltpu.VMEM((rows, 1), F32),
                        pltpu.VMEM((rows, 1), F32), pltpu.VMEM((rows, w), F32)],
    )
    args = [q, k_new, v_new, lf_new] + [cache_k] * pg + [cache_v] * pg + [c_pages] * pg
    return pl.pallas_call(
        functools.partial(_fox_sample_body, pages_per_step=pg),
        grid_spec=grid_spec,
        out_shape=jax.ShapeDtypeStruct((b, nq, w), F32),
        compiler_params=_cparams(("arbitrary", "arbitrary")),
        name="fox_sample",
    )(page_table.reshape(-1), *args)


def _causal_conv_silu(ext_scr, cw_ref, cb_ref, n):
    acc = cb_ref[...] + cw_ref[0:1, :] * ext_scr[8 - (CONV_K - 1):8 - (CONV_K - 1) + n, :]
    for i in range(1, CONV_K):
        acc = acc + cw_ref[i:i + 1, :] * ext_scr[8 - (CONV_K - 1) + i:8 - (CONV_K - 1) + i + n, :]
    return _silu(acc)


def _pair_select(first, a, b):
    return jnp.where(first, a, b)


def _ssd_prompt_body(xbc_ref, z_ref, sm_ref, cw_ref, cb_ref, alog_ref, dskip_ref, nw_ref,
                     y_ref, st_ref, st_scr, ext_scr):
    c = pl.program_id(1)
    q = SSD_CHUNK
    pair_w = 2 * HEAD_DIM_S

    @pl.when(c == 0)
    def _():
        st_scr[...] = jnp.zeros_like(st_scr)
        ext_scr[0:8, :] = jnp.zeros((8, CONV_DIM), F32)

    cur = xbc_ref[0]
    ext_scr[8:8 + q, :] = cur
    act = _causal_conv_silu(ext_scr, cw_ref, cb_ref, q)
    ext_scr[0:8, :] = cur[q - 8:q, :]

    sm = sm_ref[0]
    a_lane = -jnp.exp(alog_ref[...])
    tri_mask = _tri_lower(q)
    a_all = _dot_sel_left(tri_mask.astype(BF16), sm * a_lane)
    a_t = a_all.T
    dt_t = sm.T
    w_t = jnp.exp(a_t[:, q - 1:q] - a_t) * dt_t
    ea_all = jnp.exp(a_all)
    lane = lax.broadcasted_iota(jnp.int32, (1, pair_w), 1)
    first = lane < HEAD_DIM_S

    for g in range(N_GROUPS_S):
        bc = act[:, D_INNER + g * D_STATE:D_INNER + (g + 1) * D_STATE]
        cc = act[:, D_INNER + (N_GROUPS_S + g) * D_STATE:D_INNER + (N_GROUPS_S + g + 1) * D_STATE]
        cb = _dot_nt(cc.astype(BF16), bc.astype(BF16))
        bc_t = bc.T
        gated = []
        ssq = jnp.zeros((q, 1), F32)
        for pp in range(HEADS_PER_GROUP // 2):
            h0 = g * HEADS_PER_GROUP + 2 * pp
            col = h0 * HEAD_DIM_S
            x_pair = act[:, col:col + pair_w]
            s_pair = st_scr[:, col:col + pair_w]
            rhs = jnp.concatenate([x_pair.astype(BF16), s_pair.astype(BF16)], axis=0)
            ys, us, el = [], [], []
            for h in (h0, h0 + 1):
                li = DT_LANE0 + h
                seg = a_all[:, li:li + 1] - a_t[li:li + 1, :]
                dec = jnp.exp(jnp.where(tri_mask, seg, NEG_INF))
                m_h = cb * dec * dt_t[li:li + 1, :]
                lhs = jnp.concatenate([m_h, cc * ea_all[:, li:li + 1]], axis=1).astype(BF16)
                ys.append(_dot(lhs, rhs))
                us.append(_dot((bc_t * w_t[li:li + 1, :]).astype(BF16), x_pair.astype(BF16)))
                el.append(ea_all[q - 1:q, li:li + 1])
            y_pair = _pair_select(first, ys[0], ys[1]) + x_pair * dskip_ref[:, col:col + pair_w]
            st_scr[:, col:col + pair_w] = (s_pair * _pair_select(first, el[0], el[1])
                                           + _pair_select(first, us[0], us[1]))
            gp = y_pair * _silu(z_ref[0, :, col:col + pair_w])
            ssq = ssq + jnp.sum(gp * gp, axis=-1, keepdims=True)
            gated.append((col, gp))
        rs = lax.rsqrt(ssq / GROUP_W + EPS)
        for col, gp in gated:
            y_ref[0, :, col:col + pair_w] = (gp * rs * nw_ref[:, col:col + pair_w]).astype(y_ref.dtype)

    @pl.when(c == pl.num_programs(1) - 1)
    def _():
        for blk in range(D_INNER // pair_w):
            st_ref[0, blk * pair_w:(blk + 1) * pair_w, :] = st_scr[:, blk * pair_w:(blk + 1) * pair_w].T


def _ssd_prompt(xbc, z, small, conv_w, conv_b, alog_lane, dskip_exp, norm_w):
    b, l, _ = xbc.shape
    q = SSD_CHUNK
    full = lambda shape: pl.BlockSpec(shape, lambda bi, c: (0,) * len(shape))
    return pl.pallas_call(
        _ssd_prompt_body,
        grid=(b, l // q),
        in_specs=[pl.BlockSpec((1, q, CONV_DIM), lambda bi, c: (bi, c, 0)),
                  pl.BlockSpec((1, q, D_INNER), lambda bi, c: (bi, c, 0)),
                  pl.BlockSpec((1, q, LANES), lambda bi, c: (bi, c, 0)),
                  full((CONV_K, CONV_DIM)), full((1, CONV_DIM)), full((1, LANES)),
                  full((1, D_INNER)), full((1, D_INNER))],
        out_specs=[pl.BlockSpec((1, q, D_INNER), lambda bi, c: (bi, c, 0)),
                   pl.BlockSpec((1, D_INNER, D_STATE), lambda bi, c: (bi, 0, 0))],
        out_shape=[jax.ShapeDtypeStruct((b, l, D_INNER), BF16),
                   jax.ShapeDtypeStruct((b, D_INNER, D_STATE), F32)],
        scratch_shapes=[pltpu.VMEM((D_STATE, D_INNER), F32), pltpu.VMEM((8 + q, CONV_DIM), F32)],
        compiler_params=_cparams(("arbitrary", "arbitrary")),
        name="ssd_prompt",
    )(xbc, z, small, conv_w, conv_b, alog_lane, dskip_exp, norm_w)


def _ssd_sample_body(xbc_ref, c0_ref, z_ref, sm_ref, st_ref, cw_ref, cb_ref, alog_ref, dskip_ref, nw_ref,
                     y_ref, so_ref, ext_scr):
    l = xbc_ref.shape[1]
    pair_w = 2 * HEAD_DIM_S
    ext_scr[0:8, :] = jnp.zeros((8, CONV_DIM), F32)
    ext_scr[8 - (CONV_K - 1):8, :] = c0_ref[0]
    ext_scr[8:8 + l, :] = xbc_ref[0]
    act = _causal_conv_silu(ext_scr, cw_ref, cb_ref, l)

    sm = sm_ref[0]
    dta = sm * (-jnp.exp(alog_ref[...]))
    trow = lax.broadcasted_iota(jnp.int32, (l, LANES), 0)
    a_c = jnp.zeros((l, LANES), F32)
    for s in range(l):
        a_c = a_c + jnp.where(trow >= s, dta[s:s + 1, :], 0.0)
    lane = lax.broadcasted_iota(jnp.int32, (1, pair_w), 1)
    first = lane < HEAD_DIM_S

    def expand(v):
        cols = []
        for pp in range(N_HEADS_S // 2):
            li = DT_LANE0 + 2 * pp
            cols.append(_pair_select(first, v[:, li:li + 1], v[:, li + 1:li + 2]))
        return jnp.concatenate(cols, axis=1)

    a_x = expand(a_c)
    dt_x = expand(sm)
    ea_x = jnp.exp(a_x)
    w_x = jnp.exp(a_x[l - 1:l, :] - a_x) * dt_x
    xs = act[:, 0:D_INNER]
    zpad = jnp.zeros((PAGE_SIZE - l, D_STATE), F32)
    trow_x = lax.broadcasted_iota(jnp.int32, (l, GROUP_W), 0)

    xw_pad = jnp.concatenate([xs * w_x, jnp.zeros((PAGE_SIZE - l, D_INNER), F32)], axis=0)
    ea_last = jnp.exp(a_c[l - 1:l, :])
    rowsel = lax.broadcasted_iota(jnp.int32, (pair_w, 1), 0) < HEAD_DIM_S

    for g in range(N_GROUPS_S):
        gc = g * GROUP_W
        bc = act[:, D_INNER + g * D_STATE:D_INNER + (g + 1) * D_STATE]
        cc = act[:, D_INNER + (N_GROUPS_S + g) * D_STATE:D_INNER + (N_GROUPS_S + g + 1) * D_STATE]
        b_pad = jnp.concatenate([bc, zpad], axis=0).astype(BF16)
        cb = _dot_nt(cc.astype(BF16), b_pad)
        s_g = st_ref[0, gc:gc + GROUP_W, :]
        y = _dot_nt(cc.astype(BF16), s_g.astype(BF16)) * ea_x[:, gc:gc + GROUP_W]
        x_g = xs[:, gc:gc + GROUP_W]
        a_g = a_x[:, gc:gc + GROUP_W]
        dt_g = dt_x[:, gc:gc + GROUP_W]
        for s in range(l):
            dec = jnp.exp(jnp.where(trow_x >= s, a_g - a_g[s:s + 1, :], NEG_INF))
            y = y + cb[:, s:s + 1] * dec * (dt_g[s:s + 1, :] * x_g[s:s + 1, :])
        y = y + x_g * dskip_ref[:, gc:gc + GROUP_W]
        gp = y * _silu(z_ref[0, :, gc:gc + GROUP_W])
        rs = lax.rsqrt(jnp.sum(gp * gp, axis=-1, keepdims=True) / GROUP_W + EPS)
        y_ref[0, :, gc:gc + GROUP_W] = gp * rs * nw_ref[:, gc:gc + GROUP_W]
        for pp in range(HEADS_PER_GROUP // 2):
            h0 = g * HEADS_PER_GROUP + 2 * pp
            col = h0 * HEAD_DIM_S
            li = DT_LANE0 + h0
            u = _dot(xw_pad[:, col:col + pair_w].T.astype(BF16), b_pad)
            e_col = jnp.where(rowsel, ea_last[:, li:li + 1], ea_last[:, li + 1:li + 2])
            so_ref[0, col:col + pair_w, :] = st_ref[0, col:col + pair_w, :] * e_col + u


def _ssd_sample(xbc, conv0, z, small, state, conv_w, conv_b, alog_lane, dskip_exp, norm_w):
    b, l, _ = xbc.shape
    full = lambda shape: pl.BlockSpec(shape, lambda bi: (0,) * len(shape))
    seq = lambda shape: pl.BlockSpec(shape, lambda bi: (bi, 0, 0))
    return pl.pallas_call(
        _ssd_sample_body,
        grid=(b,),
        in_specs=[seq((1, l, CONV_DIM)), seq((1, CONV_K - 1, CONV_DIM)), seq((1, l, D_INNER)),
                  seq((1, l, LANES)), seq((1, D_INNER, D_STATE)),
                  full((CONV_K, CONV_DIM)), full((1, CONV_DIM)), full((1, LANES)),
                  full((1, D_INNER)), full((1, D_INNER))],
        out_specs=[seq((1, l, D_INNER)), seq((1, D_INNER, D_STATE))],
        out_shape=[jax.ShapeDtypeStruct((b, l, D_INNER), F32),
                   jax.ShapeDtypeStruct((b, D_INNER, D_STATE), F32)],
        scratch_shapes=[pltpu.VMEM((8 + l, CONV_DIM), F32)],
        compiler_params=_cparams(("arbitrary",)),
        name="ssd_sample",
    )(xbc, conv0, z, small, state, conv_w, conv_b, alog_lane, dskip_exp, norm_w)


def _post_mixer_body(x_ref, oa_ref, ys_ref, gate_ref, g1_ref, sh2_ref, sc2_ref, wa_ref, ws_ref, wo_ref,
                     n2_ref, wr_ref, br_ref, cin_ref, x2_ref, h_ref, te_ref, tw_ref, rk_ref, cout_ref,
                     count_scr):
    bb, bl, d = x_ref.shape
    rows = bb * bl

    @pl.when(jnp.logical_and(pl.program_id(0) == 0, pl.program_id(1) == 0))
    def _():
        count_scr[...] = cin_ref[...]

    oa = oa_ref[...].reshape(rows, d).astype(BF16)
    ys = ys_ref[...].reshape(rows, D_INNER).astype(BF16)
    y_attn = _dot(oa, wa_ref[...])
    y_ssm = _dot(ys, ws_ref[...])
    gate = gate_ref[...]
    merged = _sigmoid(gate[:, 0:d]) * y_attn + _sigmoid(gate[:, d:2 * d]) * y_ssm
    mix = _dot(merged.astype(BF16), wo_ref[...])
    x2 = x_ref[...] + g1_ref[...] * mix.reshape(bb, bl, d)
    x2_ref[...] = x2
    ms = jnp.mean(x2 * x2, axis=-1, keepdims=True)
    hf = (x2 * lax.rsqrt(ms + EPS) * n2_ref[...]) * (1.0 + sc2_ref[...]) + sh2_ref[...]
    hf = hf.reshape(rows, d)
    h_ref[...] = hf
    lane = lax.broadcasted_iota(jnp.int32, (rows, LANES), 1)
    logits = jnp.where(lane < N_EXPERTS, _dot3(hf, wr_ref[...]) + br_ref[...], NEG_INF)
    vals, idxs = [], []
    for _ in range(TOP_K):
        mx = jnp.max(logits, axis=-1, keepdims=True)
        idx = jnp.min(jnp.where(logits == mx, lane, LANES), axis=-1, keepdims=True)
        vals.append(mx)
        idxs.append(idx)
        logits = jnp.where(lane == idx, NEG_INF, logits)
    ex = [jnp.exp(v - vals[0]) for v in vals]
    den = ex[0] + ex[1] + ex[2] + ex[3]
    chosen = [lane == idxs[k] for k in range(TOP_K)]
    cnt = chosen[0].astype(F32)
    for k in range(1, TOP_K):
        cnt = cnt + chosen[k].astype(F32)
    r_i = lax.broadcasted_iota(jnp.int32, (rows, rows), 0)
    c_i = lax.broadcasted_iota(jnp.int32, (rows, rows), 1)
    before = _dot((c_i < r_i).astype(BF16), cnt.astype(BF16)) + count_scr[...]
    te = jnp.zeros((rows, LANES), jnp.int32)
    tw = jnp.zeros((rows, LANES), F32)
    rk = jnp.zeros((rows, LANES), jnp.int32)
    for k in range(TOP_K):
        rank = jnp.sum(jnp.where(chosen[k], before, 0.0), axis=-1, keepdims=True)
        te = jnp.where(lane == k, idxs[k], te)
        tw = jnp.where(lane == k, ex[k] / den, tw)
        rk = jnp.where(lane == k, rank.astype(jnp.int32), rk)
    te_ref[...] = te
    tw_ref[...] = tw
    rk_ref[...] = rk
    count_scr[...] = count_scr[...] + jnp.sum(cnt, axis=0, keepdims=True)
    cout_ref[...] = count_scr[...]


def _post_mixer(x, o_attn, y_ssm, gate, mod, w_attn_out, w_ssm_out, w_o, norm2_w, w_router, b_router,
                counts_in, rows=256):
    b, l, d = x.shape
    bb, bl, rows = _row_blocks(b, l, rows)
    nl = l // bl
    tok = lambda w: pl.BlockSpec((bb, bl, w), lambda i, j: (i, j, 0))
    flat = lambda w: pl.BlockSpec((rows, w), lambda i, j: (i * nl + j, 0))
    modc = lambda c: pl.BlockSpec((bb, 1, d), lambda i, j: (i, 0, c))
    full = lambda shape: pl.BlockSpec(shape, lambda i, j: (0,) * len(shape))
    return pl.pallas_call(
        _post_mixer_body,
        grid=(b // bb, nl),
        in_specs=[tok(d), tok(d), tok(D_INNER), flat(2 * d), modc(2), modc(3), modc(4),
                  full((d, d)), full((D_INNER, d)), full((d, d)), full((1, d)),
                  full((d, LANES)), full((1, LANES)), full((1, LANES))],
        out_specs=[tok(d), flat(d), flat(LANES), flat(LANES), flat(LANES), full((1, LANES))],
        out_shape=[jax.ShapeDtypeStruct((b, l, d), F32), jax.ShapeDtypeStruct((b * l, d), F32),
                   jax.ShapeDtypeStruct((b * l, LANES), jnp.int32), jax.ShapeDtypeStruct((b * l, LANES), F32),
                   jax.ShapeDtypeStruct((b * l, LANES), jnp.int32), jax.ShapeDtypeStruct((1, LANES), F32)],
        scratch_shapes=[pltpu.VMEM((1, LANES), F32)],
        compiler_params=_cparams(("arbitrary", "arbitrary")),
        name="post_mixer",
    )(x, o_attn, y_ssm, gate, mod, mod, mod, w_attn_out, w_ssm_out, w_o, norm2_w.reshape(1, d),
      w_router, b_router, counts_in)


MOE_ROWS = 512
SPLIT_COLS = 512


def _experts_body(be_ref, nb_ref, x_ref, wgu_ref, bg_ref, bu_ref, wdn_ref, bd_ref, o_ref,
                  wg_scr, wu_scr, wd_scr, t_scr):
    i = pl.program_id(0)
    d = x_ref.shape[1]
    half = SPLIT_COLS // 2

    @pl.when(jnp.logical_or(i == 0, be_ref[i] != be_ref[jnp.maximum(i - 1, 0)]))
    def _():
        for cb in range(wgu_ref.shape[2] // SPLIT_COLS):
            for kc in range(d // LANES):
                ks = slice(kc * LANES, (kc + 1) * LANES)
                t_scr[kc] = wgu_ref[0, ks, cb * SPLIT_COLS:(cb + 1) * SPLIT_COLS].T
                rows = slice(cb * half, (cb + 1) * half)
                wg_scr[rows, ks] = t_scr[kc, pl.ds(0, half, stride=2), :].astype(BF16)
                wu_scr[rows, ks] = t_scr[kc, pl.ds(1, half, stride=2), :].astype(BF16)
        wd_scr[...] = wdn_ref[0].astype(BF16)

    @pl.when(i < nb_ref[0])
    def _():
        x = x_ref[...].astype(BF16)
        gate = jnp.minimum(_dot_nt(x, wg_scr[...]) + bg_ref[0], SWIGLU_LIMIT)
        up = jnp.clip(_dot_nt(x, wu_scr[...]) + bu_ref[0], -SWIGLU_LIMIT, SWIGLU_LIMIT)
        act = (up + 1.0) * (gate * _sigmoid(SWIGLU_ALPHA * gate))
        o_ref[...] = _dot(act.astype(BF16), wd_scr[...]) + bd_ref[0]

    @pl.when(i >= nb_ref[0])
    def _():
        o_ref[...] = jnp.zeros_like(o_ref)


def _experts(block_e, n_used, xg, w_gu, b_g, b_u, w_d, b_d):
    p, d = xg.shape
    ff = w_d.shape[1]
    wspec = lambda shape: pl.BlockSpec(shape, lambda i, be, nb: (be[i], 0, 0))
    grid_spec = pltpu.PrefetchScalarGridSpec(
        num_scalar_prefetch=2,
        grid=(p // MOE_ROWS,),
        in_specs=[pl.BlockSpec((MOE_ROWS, d), lambda i, be, nb: (i, 0)),
                  wspec((1, d, 2 * ff)), wspec((1, 1, ff)), wspec((1, 1, ff)),
                  wspec((1, ff, d)), wspec((1, 1, d))],
        out_specs=pl.BlockSpec((MOE_ROWS, d), lambda i, be, nb: (i, 0)),
        scratch_shapes=[pltpu.VMEM((ff, d), BF16), pltpu.VMEM((ff, d), BF16), pltpu.VMEM((ff, d), BF16),
                        pltpu.VMEM((d // LANES, SPLIT_COLS, LANES), F32)],
    )
    return pl.pallas_call(
        _experts_body,
        grid_spec=grid_spec,
        out_shape=jax.ShapeDtypeStruct((p, d), F32),
        compiler_params=_cparams(("arbitrary",)),
        name="moe_experts",
    )(block_e, n_used, xg, w_gu, b_g, b_u, w_d, b_d)


SC_WORKERS = 32
SC_CHUNK = 32


def _sc_gather_rows(table, idx):
    b = idx.shape[0]
    d = table.shape[1]
    per_w = b // SC_WORKERS
    n = per_w // SC_CHUNK
    assert b % (SC_WORKERS * SC_CHUNK) == 0 and n % 2 == 0
    mesh = plsc.VectorSubcoreMesh(core_axis_name="c", subcore_axis_name="s")

    @functools.partial(
        pl.kernel, mesh=mesh, out_type=jax.ShapeDtypeStruct((b, d), table.dtype),
        scratch_types=[pltpu.VMEM((n, SC_CHUNK), jnp.int32), pltpu.VMEM((2, SC_CHUNK, d), table.dtype),
                       pltpu.SemaphoreType.DMA((2,)), pltpu.SemaphoreType.DMA((2,))],
        name="sc_gather_rows")
    def gather(table_hbm, idx_hbm, out_hbm, idx_v, rows_v, gsem, wsem):
        wid = lax.axis_index("s") * 2 + lax.axis_index("c")
        base = wid * per_w
        pltpu.sync_copy(idx_hbm.at[wid], idx_v)

        def fetch(c, slot):
            return pltpu.make_async_copy(table_hbm.at[idx_v.at[c]], rows_v.at[slot], gsem.at[slot])

        def write(c, slot):
            off = pl.multiple_of(base + c * SC_CHUNK, SC_CHUNK)
            return pltpu.make_async_copy(rows_v.at[slot], out_hbm.at[pl.ds(off, SC_CHUNK)], wsem.at[slot])

        fetch(0, 0).start()

        @pl.loop(0, n, step=2)
        def _(c0):
            for slot in (0, 1):
                c = c0 + slot
                fetch(c, slot).wait()

                @pl.when(c >= 1)
                def _():
                    write(c - 1, 1 - slot).wait()

                @pl.when(c + 1 < n)
                def _():
                    fetch(c + 1, 1 - slot).start()

                write(c, slot).start()

        write(n - 1, 1).wait()

    return gather(table, idx.reshape(SC_WORKERS, n, SC_CHUNK))


SC_LANES = 16


def _sc_row_tokens(dest, n_rows, n_tokens, slots_per_token):
    s = dest.shape[0]
    per_w = n_rows // SC_WORKERS
    assert n_rows % (SC_WORKERS * SC_LANES) == 0 and s % SC_LANES == 0
    shift = slots_per_token.bit_length() - 1
    assert 1 << shift == slots_per_token
    pad_mask = (1 << (n_tokens.bit_length() - 1)) - 1
    mesh = plsc.VectorSubcoreMesh(core_axis_name="c", subcore_axis_name="s")
    params = pltpu.CompilerParams(needs_layout_passes=False)

    @functools.partial(
        pl.kernel, mesh=mesh, out_type=jax.ShapeDtypeStruct((n_rows,), jnp.int32),
        scratch_types=[pltpu.VMEM((s,), jnp.int32), pltpu.VMEM((per_w,), jnp.int32)],
        compiler_params=params, name="sc_row_tokens")
    def invert(dest_hbm, out_hbm, dest_v, out_v):
        wid = lax.axis_index("s") * 2 + lax.axis_index("c")
        base = wid * per_w
        pltpu.sync_copy(dest_hbm, dest_v)
        lanes = lax.iota(jnp.int32, SC_LANES)

        @pl.loop(0, per_w // SC_LANES)
        def _(i):
            off = pl.multiple_of(i * SC_LANES, SC_LANES)
            out_v[pl.ds(off, SC_LANES)] = (base + off + lanes) & pad_mask

        @pl.loop(0, s // SC_LANES)
        def _(i):
            off = pl.multiple_of(i * SC_LANES, SC_LANES)
            local = dest_v[pl.ds(off, SC_LANES)] - base
            mine = jnp.logical_and(local >= 0, local < per_w)
            tok = lax.shift_right_logical(off + lanes, shift)
            plsc.store_scatter(out_v, [jnp.where(mine, local, 0)], tok, mask=mine)

        pltpu.sync_copy(out_v, out_hbm.at[pl.ds(pl.multiple_of(base, SC_LANES), per_w)])

    return invert(dest)


def _route(top_e, rank, counts):
    t, k = top_e.shape
    s = t * k
    e_flat = top_e.reshape(-1)
    padded = (counts + MOE_ROWS - 1) // MOE_ROWS * MOE_ROWS
    pend = jnp.cumsum(padded)
    onehot = e_flat[:, None] == jnp.arange(N_EXPERTS, dtype=jnp.int32)[None, :]
    dest = jnp.sum(jnp.where(onehot, (pend - padded)[None, :], 0), axis=1) + rank.reshape(-1)
    n_blocks = -(-s // MOE_ROWS) + N_EXPERTS
    row_tok = _sc_row_tokens(dest, n_blocks * MOE_ROWS, t, k)
    block_row0 = jnp.arange(n_blocks, dtype=jnp.int32) * MOE_ROWS
    block_e = jnp.minimum(jnp.sum((pend[None, :] <= block_row0[:, None]).astype(jnp.int32), axis=1),
                          N_EXPERTS - 1)
    n_used = (pend[-1:] // MOE_ROWS).astype(jnp.int32)
    return dest.reshape(t, k), row_tok, block_e, n_used


def _final_body(x_ref, ex_ref, tw_ref, g2_ref, nf_ref, y_ref):
    bb, bl, d = x_ref.shape
    tw = tw_ref[...]
    moe = ex_ref[0] * tw[:, 0:1]
    for k in range(1, TOP_K):
        moe = moe + ex_ref[k] * tw[:, k:k + 1]
    x = x_ref[...] + g2_ref[...] * moe.reshape(bb, bl, d)
    ms = jnp.mean(x * x, axis=-1, keepdims=True)
    y_ref[...] = x * lax.rsqrt(ms + EPS) * nf_ref[...]


def _final(x2, expert_out, top_w, row0, mod, norm_f_w, rows=512):
    b, l, d = x2.shape
    bb, bl, rows = _row_blocks(b, l, rows)
    nl = l // bl
    assert row0 % rows == 0
    blk0 = row0 // rows
    return pl.pallas_call(
        _final_body,
        grid=(b // bb, nl),
        in_specs=[pl.BlockSpec((bb, bl, d), lambda i, j: (i, j, 0)),
                  pl.BlockSpec((TOP_K, rows, d), lambda i, j: (0, blk0 + i * nl + j, 0)),
                  pl.BlockSpec((rows, LANES), lambda i, j: (blk0 + i * nl + j, 0)),
                  pl.BlockSpec((bb, 1, d), lambda i, j: (i, 0, 5)),
                  pl.BlockSpec((1, d), lambda i, j: (0, 0))],
        out_specs=pl.BlockSpec((bb, bl, d), lambda i, j: (i, j, 0)),
        out_shape=jax.ShapeDtypeStruct((b, l, d), F32),
        compiler_params=_cparams(("arbitrary", "arbitrary")),
        name="final_norm",
    )(x2, expert_out, top_w, mod, norm_f_w.reshape(1, d))


def _prep_in_weights(w_in, b_fgate, dt_bias):
    aw = N_HEADS_A * HEAD_DIM_A
    o = np.cumsum([0, aw, aw, aw, N_HEADS_A, D_INNER, CONV_DIM, N_HEADS_S, D_MODEL, D_MODEL])
    w_t = w_in.T
    rows = lambda i: w_t[o[i]:o[i + 1]]
    w_small = jnp.concatenate(
        [rows(3), rows(6), jnp.zeros((LANES - N_HEADS_A - N_HEADS_S, D_MODEL), F32)], axis=0).T
    b_small = jnp.concatenate(
        [b_fgate, dt_bias, jnp.zeros((LANES - N_HEADS_A - N_HEADS_S,), F32)]).reshape(1, LANES)
    big = {"q": rows(0), "k": rows(1), "v": rows(2), "z": rows(4), "xbc": rows(5),
           "gate": w_t[o[7]:o[9]]}
    return {n: w.astype(BF16) for n, w in big.items()}, w_small, b_small


def kernel(x_prompt, x_sample, cache_k, cache_v, cache_logf, state_ssm, state_conv, page_table, c_prompt, c_sample, w_cond, b_cond, norm1_w, w_in, b_fgate, conv_w, conv_b, dt_bias, a_log, d_skip, ssm_norm_w, w_attn_out, w_ssm_out, w_o, norm2_w, w_router, b_router, w_gate_up, b_gate_up, w_down, b_down, norm_f_w):
    assert w_in.shape[0] == 1, "single-layer trunk"
    bp, lp, d = x_prompt.shape
    bs, ls, _ = x_sample.shape
    tp, ts = bp * lp, bs * ls
    aw = N_HEADS_A * HEAD_DIM_A
    c_all = jnp.concatenate([c_prompt, c_sample, jnp.zeros((-(bp + bs) % 8, d), F32)], axis=0)
    mod = _adaln_mod(c_all, w_cond[0], b_cond[0])
    mod_p = mod[:bp].reshape(bp, 1, 6 * d)
    mod_s = mod[bp:bp + bs].reshape(bs, 1, 6 * d)

    w_big, w_small, b_small = _prep_in_weights(w_in[0], b_fgate[0], dt_bias[0])
    pad_lanes = LANES - N_HEADS_A - N_HEADS_S
    alog_lane = jnp.concatenate([jnp.zeros((N_HEADS_A,), F32), a_log[0], jnp.zeros((pad_lanes,), F32)]).reshape(1, LANES)
    dskip_exp = jnp.repeat(d_skip[0], HEAD_DIM_S).reshape(1, D_INNER)
    ssm_nw = ssm_norm_w[0].reshape(1, D_INNER)
    conv_b2 = conv_b[0].reshape(1, CONV_DIM)
    wa, ws, wo = w_attn_out[0].astype(BF16), w_ssm_out[0].astype(BF16), w_o[0].astype(BF16)
    w_router_pad = jnp.pad(w_router[0], ((0, 0), (0, LANES - N_EXPERTS)))
    b_router_pad = jnp.pad(b_router[0], (0, LANES - N_EXPERTS)).reshape(1, LANES)

    def in_proj(x, m, half_copies):
        h, small = _norm_in(x, m, norm1_w[0], w_small, b_small)
        return {n: _matmul_nt(h, w, also_bf16=n in half_copies) for n, w in w_big.items()}, small

    proj_p, small_p = in_proj(x_prompt, mod_p, ("k", "v"))
    proj_s, small_s = in_proj(x_sample, mod_s, ())
    (k_p, kb_p), (v_p, vb_p) = proj_p["k"], proj_p["v"]

    fcum = _cumsum_t(small_p.reshape(bp, lp, LANES)).reshape(bp, N_HEADS_A // 2, 2, lp)
    o_p = _fox_prompt(proj_p["q"].reshape(bp, lp, aw), kb_p.reshape(bp, lp, aw),
                      vb_p.reshape(bp, lp, aw), fcum)
    xbc_p = proj_p["xbc"].reshape(bp, lp, CONV_DIM)
    y_p, st_p = _ssd_prompt(xbc_p, proj_p["z"].reshape(bp, lp, D_INNER), small_p.reshape(bp, lp, LANES),
                            conv_w[0], conv_b2, alog_lane, dskip_exp, ssm_nw)

    n_phys = cache_k.shape[1]
    k_t = jnp.transpose(cache_k[0], (0, 2, 3, 1)).reshape(n_phys, aw, PAGE_SIZE)
    v_t = jnp.transpose(cache_v[0], (0, 2, 3, 1)).reshape(n_phys, aw, PAGE_SIZE)
    lf_t = jnp.swapaxes(cache_logf[0], 1, 2).reshape(n_phys * N_HEADS_A, PAGE_SIZE)
    c_pages = _cumsum_lanes(lf_t, rows=512).reshape(n_phys, N_HEADS_A, PAGE_SIZE)
    n_pages = page_table.shape[1]
    o_s = _fox_sample(proj_s["q"].reshape(bs, ls, aw), proj_s["k"].reshape(bs, ls, aw),
                      proj_s["v"].reshape(bs, ls, aw), small_s.reshape(bs, ls, LANES), k_t, v_t,
                      c_pages, page_table, pages_per_step=min(16, n_pages))
    xbc_s = proj_s["xbc"].reshape(bs, ls, CONV_DIM)
    y_s, st_s = _ssd_sample(xbc_s, state_conv[0], proj_s["z"].reshape(bs, ls, D_INNER),
                            small_s.reshape(bs, ls, LANES), state_ssm[0].reshape(bs, D_INNER, D_STATE),
                            conv_w[0], conv_b2, alog_lane, dskip_exp, ssm_nw)

    post = lambda x, o, y, gate, m, counts: _post_mixer(x, o, y, gate, m, wa, ws, wo, norm2_w[0],
                                                        w_router_pad, b_router_pad, counts)
    x2_p, hf_p, te_p, tw_p, rk_p, counts_p = post(x_prompt, o_p, y_p, proj_p["gate"], mod_p,
                                                  jnp.zeros((1, LANES), F32))
    x2_s, hf_s, te_s, tw_s, rk_s, counts = post(x_sample, o_s, y_s, proj_s["gate"], mod_s, counts_p)

    hf = jnp.concatenate([hf_p, hf_s], axis=0)
    top_e = jnp.concatenate([te_p[:, :TOP_K], te_s[:, :TOP_K]], axis=0)
    rank = jnp.concatenate([rk_p[:, :TOP_K], rk_s[:, :TOP_K]], axis=0)
    top_w = jnp.concatenate([tw_p, tw_s], axis=0)
    dest, row_tok, block_e, n_used = _route(top_e, rank, counts[0, :N_EXPERTS].astype(jnp.int32))
    rows = _experts(block_e, n_used, _sc_gather_rows(hf, row_tok), w_gate_up[0],
                    b_gate_up[0][:, None, 0::2], b_gate_up[0][:, None, 1::2],
                    w_down[0], b_down[0][:, None, :])
    expert_out = _sc_gather_rows(rows, dest.T.reshape(-1)).reshape(TOP_K, tp + ts, d)

    y_prompt = _final(x2_p, expert_out, top_w, 0, mod_p, norm_f_w)
    y_sample = _final(x2_s, expert_out, top_w, tp, mod_s, norm_f_w)

    def state_rows(k, v, small, st, xbc, conv0, b, l):
        conv_new = jnp.concatenate([conv0, xbc], axis=1)[:, -(CONV_K - 1):]
        return (k.reshape(1, b, l, N_HEADS_A, HEAD_DIM_A), v.reshape(1, b, l, N_HEADS_A, HEAD_DIM_A),
                small[:, :N_HEADS_A].reshape(1, b, l, N_HEADS_A),
                st.reshape(1, b, N_HEADS_S, HEAD_DIM_S, D_STATE), conv_new[None])

    sp = state_rows(k_p, v_p, small_p, st_p, xbc_p,
                    jnp.zeros((bp, CONV_K - 1, CONV_DIM), F32), bp, lp)
    ss = state_rows(proj_s["k"], proj_s["v"], small_s, st_s, xbc_s, state_conv[0], bs, ls)
    return (y_prompt, y_sample) + sp + ss
```

```python
import functools

import numpy as np
import jax
import jax.numpy as jnp
from jax import lax
from jax.experimental import pallas as pl
from jax.experimental.pallas import tpu as pltpu
from jax.experimental.pallas import tpu_sc as plsc

F32 = jnp.float32
BF16 = jnp.bfloat16

D_MODEL = 1024
N_HEADS_A = 16
HEAD_DIM_A = 64
PAGE_SIZE = 128
D_INNER = 2048
HEAD_DIM_S = 64
N_HEADS_S = 32
N_GROUPS_S = 4
HEADS_PER_GROUP = N_HEADS_S // N_GROUPS_S
GROUP_W = D_INNER // N_GROUPS_S
D_STATE = 128
CONV_K = 4
CONV_DIM = D_INNER + 2 * N_GROUPS_S * D_STATE
SSD_CHUNK = 128
N_EXPERTS = 32
TOP_K = 4
D_FF = D_MODEL
SWIGLU_LIMIT = 7.0
SWIGLU_ALPHA = 1.702
EPS = 1e-6

LANES = 128
DT_LANE0 = N_HEADS_A
NEG_INF = float("-inf")
LOG2E = 1.4426950408889634
VMEM_LIMIT = 56 * 1024 * 1024


def _cparams(sem):
    return pltpu.CompilerParams(dimension_semantics=sem, vmem_limit_bytes=VMEM_LIMIT)


def _split3(x):
    hi = x.astype(BF16)
    r = x - hi.astype(F32)
    mid = r.astype(BF16)
    lo = (r - mid.astype(F32)).astype(BF16)
    return hi, mid, lo


def _dot(a, b):
    return jnp.dot(a, b, preferred_element_type=F32)


def _dot_nt(a, b):
    return lax.dot_general(a, b, (((1,), (1,)), ((), ())), preferred_element_type=F32)


def _dot_sel_left(sel_bf16, x):
    hi, mid, lo = _split3(x)
    return _dot(sel_bf16, hi) + _dot(sel_bf16, mid) + _dot(sel_bf16, lo)


def _dot3(a, b):
    ah = a.astype(BF16)
    am = (a - ah.astype(F32)).astype(BF16)
    bh = b.astype(BF16)
    bm = (b - bh.astype(F32)).astype(BF16)
    return _dot(ah, bh) + _dot(ah, bm) + _dot(am, bh)


def _tri_lower(n):
    r = lax.broadcasted_iota(jnp.int32, (n, n), 0)
    c = lax.broadcasted_iota(jnp.int32, (n, n), 1)
    return r >= c


def _sigmoid(x):
    return 1.0 / (1.0 + jnp.exp(-x))


def _silu(x):
    return x * _sigmoid(x)


def _mod_body(c_ref, w_ref, b_ref, o_ref):
    c = c_ref[...]
    o_ref[...] = _dot3(_silu(c), w_ref[...]) + b_ref[...]


def _adaln_mod(c_all, w_cond, b_cond):
    rows, d = c_all.shape
    n = w_cond.shape[1]
    tn = 1024
    return pl.pallas_call(
        _mod_body,
        grid=(n // tn,),
        in_specs=[pl.BlockSpec((rows, d), lambda j: (0, 0)),
                  pl.BlockSpec((d, tn), lambda j: (0, j)),
                  pl.BlockSpec((1, tn), lambda j: (0, j))],
        out_specs=pl.BlockSpec((rows, tn), lambda j: (0, j)),
        out_shape=jax.ShapeDtypeStruct((rows, n), F32),
        compiler_params=_cparams(("arbitrary",)),
        name="adaln_mod",
    )(c_all, w_cond, b_cond.reshape(1, n))


def _norm_in_body(x_ref, sh_ref, sc_ref, nw_ref, ws_ref, bs_ref, h_ref, sm_ref):
    x = x_ref[...]
    bb, bl, d = x.shape
    ms = jnp.mean(x * x, axis=-1, keepdims=True)
    y = x * lax.rsqrt(ms + EPS) * nw_ref[...]
    h = (y * (1.0 + sc_ref[...]) + sh_ref[...]).reshape(bb * bl, d)
    h_ref[...] = h.astype(BF16)
    sm = _dot3(h, ws_ref[...]) + bs_ref[...]
    lane = lax.broadcasted_iota(jnp.int32, sm.shape, 1)
    t = jnp.log(1.0 + jnp.exp(-jnp.abs(sm)))
    sm_ref[...] = jnp.where(lane < N_HEADS_A, jnp.minimum(sm, 0.0) - t, jnp.maximum(sm, 0.0) + t)


def _row_blocks(b, l, rows):
    rows = min(rows, b * l)
    if l >= rows:
        assert l % rows == 0
        return 1, rows, rows
    assert rows % l == 0 and b % (rows // l) == 0
    return rows // l, l, rows


def _norm_in(x, mod, norm_w, w_small, b_small, rows=512):
    b, l, d = x.shape
    bb, bl, rows = _row_blocks(b, l, rows)
    nl = l // bl
    grid = (b // bb, nl)
    return pl.pallas_call(
        _norm_in_body,
        grid=grid,
        in_specs=[pl.BlockSpec((bb, bl, d), lambda i, j: (i, j, 0)),
                  pl.BlockSpec((bb, 1, d), lambda i, j: (i, 0, 0)),
                  pl.BlockSpec((bb, 1, d), lambda i, j: (i, 0, 1)),
                  pl.BlockSpec((1, d), lambda i, j: (0, 0)),
                  pl.BlockSpec((d, LANES), lambda i, j: (0, 0)),
                  pl.BlockSpec((1, LANES), lambda i, j: (0, 0))],
        out_specs=[pl.BlockSpec((rows, d), lambda i, j: (i * nl + j, 0)),
                   pl.BlockSpec((rows, LANES), lambda i, j: (i * nl + j, 0))],
        out_shape=[jax.ShapeDtypeStruct((b * l, d), BF16),
                   jax.ShapeDtypeStruct((b * l, LANES), F32)],
        compiler_params=_cparams(("arbitrary", "arbitrary")),
        name="norm_in",
    )(x, mod, mod, norm_w.reshape(1, d), w_small, b_small)


def _mm_body(a_ref, w_ref, o_ref, *half_ref):
    r = _dot_nt(a_ref[...], w_ref[...])
    o_ref[...] = r
    for h in half_ref:
        h[...] = r.astype(BF16)


def _matmul_nt(a, w_t, also_bf16=False, tm=1024, tn=1024):
    m, k = a.shape
    n = w_t.shape[0]
    tn = min(tn, n)
    tm = min(tm, m)
    out_spec = pl.BlockSpec((tm, tn), lambda j, i: (i, j))
    n_out = 2 if also_bf16 else 1
    res = pl.pallas_call(
        _mm_body,
        grid=(n // tn, m // tm),
        in_specs=[pl.BlockSpec((tm, k), lambda j, i: (i, 0)),
                  pl.BlockSpec((tn, k), lambda j, i: (j, 0))],
        out_specs=[out_spec] * n_out,
        out_shape=[jax.ShapeDtypeStruct((m, n), F32), jax.ShapeDtypeStruct((m, n), BF16)][:n_out],
        compiler_params=_cparams(("arbitrary", "arbitrary")),
        name="proj_matmul",
    )(a, w_t)
    return res if also_bf16 else res[0]


def _block_cumsum_t(blk, carry_row):
    tri = _tri_lower(PAGE_SIZE).astype(BF16)
    cs = _dot_sel_left(tri, blk) + carry_row
    return cs.T[0:N_HEADS_A, :], cs[PAGE_SIZE - 1:PAGE_SIZE, :]


def _cumsum_t_body(x_ref, o_ref, carry_scr, *, blocks):
    @pl.when(pl.program_id(1) == 0)
    def _():
        carry_scr[...] = jnp.zeros_like(carry_scr)

    carry = carry_scr[...]
    for i in range(blocks):
        ps = slice(i * PAGE_SIZE, (i + 1) * PAGE_SIZE)
        o_ref[0, :, ps], carry = _block_cumsum_t(x_ref[0, ps, :], carry)
    carry_scr[...] = carry


def _cumsum_t(x, blocks=4):
    b, l, width = x.shape
    blocks = min(blocks, l // PAGE_SIZE)
    span = blocks * PAGE_SIZE
    assert width == LANES and l % span == 0
    return pl.pallas_call(
        functools.partial(_cumsum_t_body, blocks=blocks),
        grid=(b, l // span),
        in_specs=[pl.BlockSpec((1, span, width), lambda i, j: (i, j, 0))],
        out_specs=pl.BlockSpec((1, N_HEADS_A, span), lambda i, j: (i, 0, j)),
        out_shape=jax.ShapeDtypeStruct((b, N_HEADS_A, l), F32),
        scratch_shapes=[pltpu.VMEM((1, LANES), F32)],
        compiler_params=_cparams(("arbitrary", "arbitrary")),
        name="logf_cumsum",
    )(x)


def _cumsum_lanes_body(x_ref, o_ref):
    n = x_ref.shape[1]
    r = lax.broadcasted_iota(jnp.int32, (n, n), 0)
    c = lax.broadcasted_iota(jnp.int32, (n, n), 1)
    tri = (r <= c).astype(BF16)
    hi, mid, lo = _split3(x_ref[...])
    o_ref[...] = _dot(hi, tri) + _dot(mid, tri) + _dot(lo, tri)


def _cumsum_lanes(x, rows=1024):
    r, n = x.shape
    assert r % rows == 0
    return pl.pallas_call(
        _cumsum_lanes_body,
        grid=(r // rows,),
        in_specs=[pl.BlockSpec((rows, n), lambda i: (i, 0))],
        out_specs=pl.BlockSpec((rows, n), lambda i: (i, 0)),
        out_shape=jax.ShapeDtypeStruct((r, n), F32),
        compiler_params=_cparams(("arbitrary",)),
        name="page_logf_cumsum",
    )(x)


def _fox_prompt_body(it_ref, jt_ref, q_ref, k_ref, v_ref, f_ref, o_ref, qm_scr, m_scr, acc_scr, *,
                     tile, sub):
    t = pl.program_id(2)
    i = it_ref[t]
    j = jt_ref[t]
    lane = lax.broadcasted_iota(jnp.int32, (tile, LANES), 1)
    first = lane < HEAD_DIM_A

    @pl.when(j == 0)
    def _():
        q = q_ref[0] * (HEAD_DIM_A ** -0.5 * LOG2E)
        qm_scr[0] = jnp.where(first, q, 0.0).astype(BF16)
        qm_scr[1] = jnp.where(first, 0.0, q).astype(BF16)
        m_scr[...] = jnp.full_like(m_scr, -1e30)
        acc_scr[...] = jnp.zeros_like(acc_scr)

    def step(diagonal):
        one = jnp.ones((), BF16)
        first_s = lax.broadcasted_iota(jnp.int32, (sub, LANES), 1) < HEAD_DIM_A
        for kj in range(tile // sub):
            ks = slice(kj * sub, (kj + 1) * sub)
            kb = k_ref[0, ks, :]
            vb = v_ref[0, ks, :]
            v_augs = (jnp.where(first_s, vb, one), jnp.where(first_s, one, vb))
            for qi in range(tile // sub):
                if diagonal and kj > qi:
                    continue
                qs = slice(qi * sub, (qi + 1) * sub)
                for h in range(2):
                    s = _dot_nt(qm_scr[h, qs, :], kb) - f_ref[0, 0, h:h + 1, ks] * LOG2E
                    if diagonal and kj == qi:
                        row = lax.broadcasted_iota(jnp.int32, (sub, sub), 0)
                        col = lax.broadcasted_iota(jnp.int32, (sub, sub), 1)
                        s = jnp.where(col <= row, s, NEG_INF)
                    m_old = m_scr[h, qs, :]
                    m_new = jnp.maximum(m_old, jnp.max(s, axis=-1, keepdims=True))
                    alpha = jnp.exp2(m_old - m_new)
                    p = jnp.exp2(s - jnp.concatenate([m_new] * (sub // LANES), axis=1))
                    acc_scr[h, qs, :] = alpha * acc_scr[h, qs, :] + _dot(p.astype(BF16), v_augs[h])
                    m_scr[h, qs, :] = m_new

    @pl.when(j < i)
    def _():
        step(False)

    @pl.when(j == i)
    def _():
        step(True)
        a0 = acc_scr[0]
        a1 = acc_scr[1]
        den = jnp.where(first, pltpu.roll(a0, HEAD_DIM_A, 1), pltpu.roll(a1, HEAD_DIM_A, 1))
        o_ref[0] = jnp.where(first, a0, a1) / den


def _fox_prompt(q, k, v, fcum, tile=1024, sub=512):
    b, l, w = q.shape
    npair = w // LANES
    tile = min(tile, l)
    sub = min(sub, tile)
    assert l % tile == 0 and tile % sub == 0
    nt = l // tile
    it = np.concatenate([np.full(i + 1, i, np.int32) for i in range(nt)])
    jt = np.concatenate([np.arange(i + 1, dtype=np.int32) for i in range(nt)])
    grid_spec = pltpu.PrefetchScalarGridSpec(
        num_scalar_prefetch=2,
        grid=(b, npair, len(it)),
        in_specs=[pl.BlockSpec((1, tile, LANES), lambda bi, p, t, it, jt: (bi, it[t], p)),
                  pl.BlockSpec((1, tile, LANES), lambda bi, p, t, it, jt: (bi, jt[t], p)),
                  pl.BlockSpec((1, tile, LANES), lambda bi, p, t, it, jt: (bi, jt[t], p)),
                  pl.BlockSpec((1, 1, 2, tile), lambda bi, p, t, it, jt: (bi, p, 0, jt[t]))],
        out_specs=pl.BlockSpec((1, tile, LANES), lambda bi, p, t, it, jt: (bi, it[t], p)),
        scratch_shapes=[pltpu.VMEM((2, tile, LANES), BF16), pltpu.VMEM((2, tile, LANES), F32),
                        pltpu.VMEM((2, tile, LANES), F32)],
    )
    return pl.pallas_call(
        functools.partial(_fox_prompt_body, tile=tile, sub=sub),
        grid_spec=grid_spec,
        out_shape=jax.ShapeDtypeStruct((b, l, w), F32),
        compiler_params=_cparams(("arbitrary", "arbitrary", "arbitrary")),
        name="fox_prompt",
    )(jnp.asarray(it), jnp.asarray(jt), q, k, v, fcum)


def _fox_sample_body(pt_ref, q_ref, kn_ref, vn_ref, lfn_ref, *rest, pages_per_step):
    pg = pages_per_step
    k_refs = rest[0:pg]
    v_refs = rest[pg:2 * pg]
    c_refs = rest[2 * pg:3 * pg]
    o_ref, qbd_scr, m_scr, l_scr, carry_scr, acc_scr = rest[3 * pg:]
    j = pl.program_id(1)
    nq = q_ref.shape[1]
    rows = N_HEADS_A * nq
    width = N_HEADS_A * HEAD_DIM_A

    @pl.when(j == 0)
    def _():
        q = q_ref[0] * (HEAD_DIM_A ** -0.5)
        qt = jnp.broadcast_to(q[None], (N_HEADS_A, nq, width)).reshape(rows, width)
        row = lax.broadcasted_iota(jnp.int32, (rows, width), 0)
        lane = lax.broadcasted_iota(jnp.int32, (rows, width), 1)
        own = (lane // HEAD_DIM_A) == (row // nq)
        qbd_scr[...] = jnp.where(own, qt, 0.0).astype(BF16)
        m_scr[...] = jnp.full_like(m_scr, -1e30)
        l_scr[...] = jnp.zeros_like(l_scr)
        carry_scr[...] = jnp.zeros_like(carry_scr)
        acc_scr[...] = jnp.zeros_like(acc_scr)

    def attend(blocks, valid, transposed):
        qbd = qbd_scr[...]
        carry = carry_scr[...]
        scores = []
        for kf, _, c16 in blocks:
            s = _dot(qbd, kf.astype(BF16)) if transposed else _dot_nt(qbd, kf.astype(BF16))
            nk = s.shape[1]
            cexp = jnp.broadcast_to(c16[:, None, :], (N_HEADS_A, nq, nk)).reshape(rows, nk)
            s = s - (cexp + carry)
            if valid is not None:
                s = jnp.where(valid, s, NEG_INF)
            scores.append(s)
            carry = carry + cexp[:, nk - 1:nk]
        carry_scr[...] = carry
        smax = scores[0]
        for s in scores[1:]:
            smax = jnp.maximum(smax, s)
        m_old = m_scr[...]
        m_new = jnp.maximum(m_old, jnp.max(smax, axis=-1, keepdims=True))
        alpha = jnp.exp(m_old - m_new)
        m_scr[...] = m_new
        psum = None
        pv = None
        for s, (_, vf, _) in zip(scores, blocks):
            p = jnp.exp(s - m_new)
            psum = p if psum is None else psum + p
            pb = p.astype(BF16)
            d = _dot_nt(pb, vf.astype(BF16)) if transposed else _dot(pb, vf.astype(BF16))
            pv = d if pv is None else pv + d
        l_scr[...] = alpha * l_scr[...] + jnp.sum(psum, axis=-1, keepdims=True)
        acc_scr[...] = acc_scr[...] * alpha + pv

    attend([(k_refs[g][0], v_refs[g][0], c_refs[g][0]) for g in range(pg)], None, True)

    @pl.when(j == pl.num_programs(1) - 1)
    def _():
        zpad = jnp.zeros((PAGE_SIZE - nq, width), F32)
        kpad = jnp.concatenate([kn_ref[0], zpad], axis=0)
        vpad = jnp.concatenate([vn_ref[0], zpad], axis=0)
        row = lax.broadcasted_iota(jnp.int32, (rows, PAGE_SIZE), 0)
        key = lax.broadcasted_iota(jnp.int32, (rows, PAGE_SIZE), 1)
        lf_pad = jnp.concatenate([lfn_ref[0], jnp.zeros((PAGE_SIZE - nq, LANES), F32)], axis=0)
        c_new, _ = _block_cumsum_t(lf_pad, jnp.zeros((1, LANES), F32))
        attend([(kpad, vpad, c_new)], key <= (row % nq), False)
        o = acc_scr[...] / l_scr[...]
        for h in range(N_HEADS_A):
            o_ref[0, :, h * HEAD_DIM_A:(h + 1) * HEAD_DIM_A] = (
                o[h * nq:(h + 1) * nq, h * HEAD_DIM_A:(h + 1) * HEAD_DIM_A])


def _fox_sample(q, k_new, v_new, lf_new, cache_k, cache_v, c_pages, page_table, pages_per_step=8):
    b, nq, w = q.shape
    n_pages = page_table.shape[1]
    pg = pages_per_step
    assert n_pages % pg == 0
    rows = N_HEADS_A * nq

    def page_map(g):
        return lambda bi, j, pt: (pt[bi * n_pages + j * pg + g], 0, 0)

    seq_map = lambda bi, j, pt: (bi, 0, 0)
    in_specs = [pl.BlockSpec((1, nq, w), seq_map), pl.BlockSpec((1, nq, w), seq_map),
                pl.BlockSpec((1, nq, w), seq_map), pl.BlockSpec((1, nq, LANES), seq_map)]
    in_specs += [pl.BlockSpec((1, w, PAGE_SIZE), page_map(g)) for g in range(pg)]
    in_specs += [pl.BlockSpec((1, w, PAGE_SIZE), page_map(g)) for g in range(pg)]
    in_specs += [pl.BlockSpec((1, N_HEADS_A, PAGE_SIZE), page_map(g)) for g in range(pg)]
    grid_spec = pltpu.PrefetchScalarGridSpec(
        num_scalar_prefetch=1,
        grid=(b, n_pages // pg),
        in_specs=in_specs,
        out_specs=pl.BlockSpec((1, nq, w), seq_map),
        scratch_shapes=[pltpu.VMEM((rows, w), BF16), pltpu.VMEM((rows, 1), F32), pltpu.VMEM((rows, 1), F32),
                        pltpu.VMEM((rows, 1), F32), pltpu.VMEM((rows, w), F32)],
    )
    args = [q, k_new, v_new, lf_new] + [cache_k] * pg + [cache_v] * pg + [c_pages] * pg
    return pl.pallas_call(
        functools.partial(_fox_sample_body, pages_per_step=pg),
        grid_spec=grid_spec,
        out_shape=jax.ShapeDtypeStruct((b, nq, w), F32),
        compiler_params=_cparams(("arbitrary", "arbitrary")),
        name="fox_sample",
    )(page_table.reshape(-1), *args)


def _causal_conv_silu(ext_scr, cw_ref, cb_ref, n):
    acc = cb_ref[...] + cw_ref[0:1, :] * ext_scr[8 - (CONV_K - 1):8 - (CONV_K - 1) + n, :]
    for i in range(1, CONV_K):
        acc = acc + cw_ref[i:i + 1, :] * ext_scr[8 - (CONV_K - 1) + i:8 - (CONV_K - 1) + i + n, :]
    return _silu(acc)


def _pair_select(first, a, b):
    return jnp.where(first, a, b)


def _ssd_prompt_body(xbc_ref, z_ref, sm_ref, cw_ref, cb_ref, alog_ref, dskip_ref, nw_ref,
                     y_ref, st_ref, st_scr, ext_scr):
    c = pl.program_id(1)
    q = SSD_CHUNK
    pair_w = 2 * HEAD_DIM_S

    @pl.when(c == 0)
    def _():
        st_scr[...] = jnp.zeros_like(st_scr)
        ext_scr[0:8, :] = jnp.zeros((8, CONV_DIM), F32)

    cur = xbc_ref[0]
    ext_scr[8:8 + q, :] = cur
    act = _causal_conv_silu(ext_scr, cw_ref, cb_ref, q)
    ext_scr[0:8, :] = cur[q - 8:q, :]

    sm = sm_ref[0]
    a_lane = -jnp.exp(alog_ref[...])
    tri_mask = _tri_lower(q)
    a_all = _dot_sel_left(tri_mask.astype(BF16), sm * a_lane)
    a_t = a_all.T
    dt_t = sm.T
    w_t = jnp.exp(a_t[:, q - 1:q] - a_t) * dt_t
    ea_all = jnp.exp(a_all)
    lane = lax.broadcasted_iota(jnp.int32, (1, pair_w), 1)
    first = lane < HEAD_DIM_S

    for g in range(N_GROUPS_S):
        bc = act[:, D_INNER + g * D_STATE:D_INNER + (g + 1) * D_STATE]
        cc = act[:, D_INNER + (N_GROUPS_S + g) * D_STATE:D_INNER + (N_GROUPS_S + g + 1) * D_STATE]
        cb = _dot_nt(cc.astype(BF16), bc.astype(BF16))
        bc_t = bc.T
        gated = []
        ssq = jnp.zeros((q, 1), F32)
        for pp in range(HEADS_PER_GROUP // 2):
            h0 = g * HEADS_PER_GROUP + 2 * pp
            col = h0 * HEAD_DIM_S
            x_pair = act[:, col:col + pair_w]
            s_pair = st_scr[:, col:col + pair_w]
            rhs = jnp.concatenate([x_pair.astype(BF16), s_pair.astype(BF16)], axis=0)
            ys, us, el = [], [], []
            for h in (h0, h0 + 1):
                li = DT_LANE0 + h
                seg = a_all[:, li:li + 1] - a_t[li:li + 1, :]
                dec = jnp.exp(jnp.where(tri_mask, seg, NEG_INF))
                m_h = cb * dec * dt_t[li:li + 1, :]
                lhs = jnp.concatenate([m_h, cc * ea_all[:, li:li + 1]], axis=1).astype(BF16)
                ys.append(_dot(lhs, rhs))
                us.append(_dot((bc_t * w_t[li:li + 1, :]).astype(BF16), x_pair.astype(BF16)))
                el.append(ea_all[q - 1:q, li:li + 1])
            y_pair = _pair_select(first, ys[0], ys[1]) + x_pair * dskip_ref[:, col:col + pair_w]
            st_scr[:, col:col + pair_w] = (s_pair * _pair_select(first, el[0], el[1])
                                           + _pair_select(first, us[0], us[1]))
            gp = y_pair * _silu(z_ref[0, :, col:col + pair_w])
            ssq = ssq + jnp.sum(gp * gp, axis=-1, keepdims=True)
            gated.append((col, gp))
        rs = lax.rsqrt(ssq / GROUP_W + EPS)
        for col, gp in gated:
            y_ref[0, :, col:col + pair_w] = (gp * rs * nw_ref[:, col:col + pair_w]).astype(y_ref.dtype)

    @pl.when(c == pl.num_programs(1) - 1)
    def _():
        for blk in range(D_INNER // pair_w):
            st_ref[0, blk * pair_w:(blk + 1) * pair_w, :] = st_scr[:, blk * pair_w:(blk + 1) * pair_w].T


def _ssd_prompt(xbc, z, small, conv_w, conv_b, alog_lane, dskip_exp, norm_w):
    b, l, _ = xbc.shape
    q = SSD_CHUNK
    full = lambda shape: pl.BlockSpec(shape, lambda bi, c: (0,) * len(shape))
    return pl.pallas_call(
        _ssd_prompt_body,
        grid=(b, l // q),
        in_specs=[pl.BlockSpec((1, q, CONV_DIM), lambda bi, c: (bi, c, 0)),
                  pl.BlockSpec((1, q, D_INNER), lambda bi, c: (bi, c, 0)),
                  pl.BlockSpec((1, q, LANES), lambda bi, c: (bi, c, 0)),
                  full((CONV_K, CONV_DIM)), full((1, CONV_DIM)), full((1, LANES)),
                  full((1, D_INNER)), full((1, D_INNER))],
        out_specs=[pl.BlockSpec((1, q, D_INNER), lambda bi, c: (bi, c, 0)),
                   pl.BlockSpec((1, D_INNER, D_STATE), lambda bi, c: (bi, 0, 0))],
        out_shape=[jax.ShapeDtypeStruct((b, l, D_INNER), BF16),
                   jax.ShapeDtypeStruct((b, D_INNER, D_STATE), F32)],
        scratch_shapes=[pltpu.VMEM((D_STATE, D_INNER), F32), pltpu.VMEM((8 + q, CONV_DIM), F32)],
        compiler_params=_cparams(("arbitrary", "arbitrary")),
        name="ssd_prompt",
    )(xbc, z, small, conv_w, conv_b, alog_lane, dskip_exp, norm_w)


def _ssd_sample_body(xbc_ref, c0_ref, z_ref, sm_ref, st_ref, cw_ref, cb_ref, alog_ref, dskip_ref, nw_ref,
                     y_ref, so_ref, ext_scr):
    l = xbc_ref.shape[1]
    pair_w = 2 * HEAD_DIM_S
    ext_scr[0:8, :] = jnp.zeros((8, CONV_DIM), F32)
    ext_scr[8 - (CONV_K - 1):8, :] = c0_ref[0]
    ext_scr[8:8 + l, :] = xbc_ref[0]
    act = _causal_conv_silu(ext_scr, cw_ref, cb_ref, l)

    sm = sm_ref[0]
    dta = sm * (-jnp.exp(alog_ref[...]))
    trow = lax.broadcasted_iota(jnp.int32, (l, LANES), 0)
    a_c = jnp.zeros((l, LANES), F32)
    for s in range(l):
        a_c = a_c + jnp.where(trow >= s, dta[s:s + 1, :], 0.0)
    lane = lax.broadcasted_iota(jnp.int32, (1, pair_w), 1)
    first = lane < HEAD_DIM_S

    def expand(v):
        cols = []
        for pp in range(N_HEADS_S // 2):
            li = DT_LANE0 + 2 * pp
            cols.append(_pair_select(first, v[:, li:li + 1], v[:, li + 1:li + 2]))
        return jnp.concatenate(cols, axis=1)

    a_x = expand(a_c)
    dt_x = expand(sm)
    ea_x = jnp.exp(a_x)
    w_x = jnp.exp(a_x[l - 1:l, :] - a_x) * dt_x
    xs = act[:, 0:D_INNER]
    zpad = jnp.zeros((PAGE_SIZE - l, D_STATE), F32)
    trow_x = lax.broadcasted_iota(jnp.int32, (l, GROUP_W), 0)

    xw_pad = jnp.concatenate([xs * w_x, jnp.zeros((PAGE_SIZE - l, D_INNER), F32)], axis=0)
    ea_last = jnp.exp(a_c[l - 1:l, :])
    rowsel = lax.broadcasted_iota(jnp.int32, (pair_w, 1), 0) < HEAD_DIM_S

    for g in range(N_GROUPS_S):
        gc = g * GROUP_W
        bc = act[:, D_INNER + g * D_STATE:D_INNER + (g + 1) * D_STATE]
        cc = act[:, D_INNER + (N_GROUPS_S + g) * D_STATE:D_INNER + (N_GROUPS_S + g + 1) * D_STATE]
        b_pad = jnp.concatenate([bc, zpad], axis=0).astype(BF16)
        cb = _dot_nt(cc.astype(BF16), b_pad)
        s_g = st_ref[0, gc:gc + GROUP_W, :]
        y = _dot_nt(cc.astype(BF16), s_g.astype(BF16)) * ea_x[:, gc:gc + GROUP_W]
        x_g = xs[:, gc:gc + GROUP_W]
        a_g = a_x[:, gc:gc + GROUP_W]
        dt_g = dt_x[:, gc:gc + GROUP_W]
        for s in range(l):
            dec = jnp.exp(jnp.where(trow_x >= s, a_g - a_g[s:s + 1, :], NEG_INF))
            y = y + cb[:, s:s + 1] * dec * (dt_g[s:s + 1, :] * x_g[s:s + 1, :])
        y = y + x_g * dskip_ref[:, gc:gc + GROUP_W]
        gp = y * _silu(z_ref[0, :, gc:gc + GROUP_W])
        rs = lax.rsqrt(jnp.sum(gp * gp, axis=-1, keepdims=True) / GROUP_W + EPS)
        y_ref[0, :, gc:gc + GROUP_W] = gp * rs * nw_ref[:, gc:gc + GROUP_W]
        for pp in range(HEADS_PER_GROUP // 2):
            h0 = g * HEADS_PER_GROUP + 2 * pp
            col = h0 * HEAD_DIM_S
            li = DT_LANE0 + h0
            u = _dot(xw_pad[:, col:col + pair_w].T.astype(BF16), b_pad)
            e_col = jnp.where(rowsel, ea_last[:, li:li + 1], ea_last[:, li + 1:li + 2])
            so_ref[0, col:col + pair_w, :] = st_ref[0, col:col + pair_w, :] * e_col + u


def _ssd_sample(xbc, conv0, z, small, state, conv_w, conv_b, alog_lane, dskip_exp, norm_w):
    b, l, _ = xbc.shape
    full = lambda shape: pl.BlockSpec(shape, lambda bi: (0,) * len(shape))
    seq = lambda shape: pl.BlockSpec(shape, lambda bi: (bi, 0, 0))
    return pl.pallas_call(
        _ssd_sample_body,
        grid=(b,),
        in_specs=[seq((1, l, CONV_DIM)), seq((1, CONV_K - 1, CONV_DIM)), seq((1, l, D_INNER)),
                  seq((1, l, LANES)), seq((1, D_INNER, D_STATE)),
                  full((CONV_K, CONV_DIM)), full((1, CONV_DIM)), full((1, LANES)),
                  full((1, D_INNER)), full((1, D_INNER))],
        out_specs=[seq((1, l, D_INNER)), seq((1, D_INNER, D_STATE))],
        out_shape=[jax.ShapeDtypeStruct((b, l, D_INNER), F32),
                   jax.ShapeDtypeStruct((b, D_INNER, D_STATE), F32)],
        scratch_shapes=[pltpu.VMEM((8 + l, CONV_DIM), F32)],
        compiler_params=_cparams(("arbitrary",)),
        name="ssd_sample",
    )(xbc, conv0, z, small, state, conv_w, conv_b, alog_lane, dskip_exp, norm_w)


def _post_mixer_body(x_ref, oa_ref, ys_ref, gate_ref, g1_ref, sh2_ref, sc2_ref, wa_ref, ws_ref, wo_ref,
                     n2_ref, wr_ref, br_ref, cin_ref, h_in, te_in, tw_in, rk_in,
                     x2_ref, h_ref, te_ref, tw_ref, rk_ref, cout_ref, count_scr):
    del h_in, te_in, tw_in, rk_in
    bb, bl, d = x_ref.shape
    rows = bb * bl

    @pl.when(jnp.logical_and(pl.program_id(0) == 0, pl.program_id(1) == 0))
    def _():
        count_scr[...] = cin_ref[...]

    oa = oa_ref[...].reshape(rows, d).astype(BF16)
    ys = ys_ref[...].reshape(rows, D_INNER).astype(BF16)
    y_attn = _dot(oa, wa_ref[...])
    y_ssm = _dot(ys, ws_ref[...])
    gate = gate_ref[...]
    merged = _sigmoid(gate[:, 0:d]) * y_attn + _sigmoid(gate[:, d:2 * d]) * y_ssm
    mix = _dot(merged.astype(BF16), wo_ref[...])
    x2 = x_ref[...] + g1_ref[...] * mix.reshape(bb, bl, d)
    x2_ref[...] = x2
    ms = jnp.mean(x2 * x2, axis=-1, keepdims=True)
    hf = (x2 * lax.rsqrt(ms + EPS) * n2_ref[...]) * (1.0 + sc2_ref[...]) + sh2_ref[...]
    hf = hf.reshape(rows, d)
    bits = lax.bitcast_convert_type(hf.astype(BF16).astype(F32), jnp.uint32)
    h_ref[...] = (bits[:, d // 2:] & jnp.uint32(0xFFFF0000)) | (bits[:, 0:d // 2] >> 16)
    lane = lax.broadcasted_iota(jnp.int32, (rows, LANES), 1)
    logits = jnp.where(lane < N_EXPERTS, _dot3(hf, wr_ref[...]) + br_ref[...], NEG_INF)
    vals, idxs = [], []
    for _ in range(TOP_K):
        mx = jnp.max(logits, axis=-1, keepdims=True)
        idx = jnp.min(jnp.where(logits == mx, lane, LANES), axis=-1, keepdims=True)
        vals.append(mx)
        idxs.append(idx)
        logits = jnp.where(lane == idx, NEG_INF, logits)
    ex = [jnp.exp(v - vals[0]) for v in vals]
    den = ex[0] + ex[1] + ex[2] + ex[3]
    chosen = [lane == idxs[k] for k in range(TOP_K)]
    cnt = chosen[0].astype(F32)
    for k in range(1, TOP_K):
        cnt = cnt + chosen[k].astype(F32)
    r_i = lax.broadcasted_iota(jnp.int32, (rows, rows), 0)
    c_i = lax.broadcasted_iota(jnp.int32, (rows, rows), 1)
    before = _dot((c_i < r_i).astype(BF16), cnt.astype(BF16)) + count_scr[...]
    te = jnp.zeros((rows, LANES), jnp.int32)
    tw = jnp.zeros((rows, LANES), F32)
    rk = jnp.zeros((rows, LANES), jnp.int32)
    for k in range(TOP_K):
        rank = jnp.sum(jnp.where(chosen[k], before, 0.0), axis=-1, keepdims=True)
        te = jnp.where(lane == k, idxs[k], te)
        tw = jnp.where(lane == k, ex[k] / den, tw)
        rk = jnp.where(lane == k, rank.astype(jnp.int32), rk)
    te_ref[...] = te
    tw_ref[...] = tw
    rk_ref[...] = rk
    count_scr[...] = count_scr[...] + jnp.sum(cnt, axis=0, keepdims=True)
    cout_ref[...] = count_scr[...]


def _post_mixer(x, o_attn, y_ssm, gate, mod, w_attn_out, w_ssm_out, w_o, norm2_w, w_router, b_router,
                counts_in, shared, row0=0, rows=256):
    b, l, d = x.shape
    bb, bl, rows = _row_blocks(b, l, rows)
    nl = l // bl
    assert row0 % rows == 0
    blk0 = row0 // rows
    tok = lambda w: pl.BlockSpec((bb, bl, w), lambda i, j: (i, j, 0))
    flat = lambda w: pl.BlockSpec((rows, w), lambda i, j: (i * nl + j, 0))
    buf = lambda w: pl.BlockSpec((rows, w), lambda i, j: (blk0 + i * nl + j, 0))
    modc = lambda c: pl.BlockSpec((bb, 1, d), lambda i, j: (i, 0, c))
    full = lambda shape: pl.BlockSpec(shape, lambda i, j: (0,) * len(shape))
    in_specs = [tok(d), tok(d), tok(D_INNER), flat(2 * d), modc(2), modc(3), modc(4),
                full((d, d)), full((D_INNER, d)), full((d, d)), full((1, d)),
                full((d, LANES)), full((1, LANES)), full((1, LANES))]
    args = [x, o_attn, y_ssm, gate, mod, mod, mod, w_attn_out, w_ssm_out, w_o, norm2_w.reshape(1, d),
            w_router, b_router, counts_in]
    aliases = {len(args) + i: 1 + i for i in range(len(shared))}
    in_specs += [pl.BlockSpec(memory_space=pl.ANY)] * len(shared)
    args += list(shared)
    total_rows = shared[0].shape[0]
    return pl.pallas_call(
        _post_mixer_body,
        grid=(b // bb, nl),
        in_specs=in_specs,
        out_specs=[tok(d), buf(d // 2), buf(LANES), buf(LANES), buf(LANES), full((1, LANES))],
        out_shape=[jax.ShapeDtypeStruct((b, l, d), F32),
                   jax.ShapeDtypeStruct((total_rows, d // 2), jnp.uint32),
                   jax.ShapeDtypeStruct((total_rows, LANES), jnp.int32),
                   jax.ShapeDtypeStruct((total_rows, LANES), F32),
                   jax.ShapeDtypeStruct((total_rows, LANES), jnp.int32),
                   jax.ShapeDtypeStruct((1, LANES), F32)],
        scratch_shapes=[pltpu.VMEM((1, LANES), F32)],
        input_output_aliases=aliases,
        compiler_params=_cparams(("arbitrary", "arbitrary")),
        name="post_mixer",
    )(*args)


MOE_ROWS = 512
SPLIT_COLS = 512


def _experts_body(be_ref, nb_ref, x_ref, wgu_ref, bg_ref, bu_ref, wdn_ref, bd_ref, o_ref,
                  wg_scr, wu_scr, wd_scr, t_scr):
    i = pl.program_id(0)
    d = wdn_ref.shape[2]
    half = SPLIT_COLS // 2

    @pl.when(jnp.logical_or(i == 0, be_ref[i] != be_ref[jnp.maximum(i - 1, 0)]))
    def _():
        for cb in range(wgu_ref.shape[2] // SPLIT_COLS):
            for kc in range(d // LANES):
                ks = slice(kc * LANES, (kc + 1) * LANES)
                t_scr[kc] = wgu_ref[0, ks, cb * SPLIT_COLS:(cb + 1) * SPLIT_COLS].T
                rows = slice(cb * half, (cb + 1) * half)
                wg_scr[rows, ks] = t_scr[kc, pl.ds(0, half, stride=2), :].astype(BF16)
                wu_scr[rows, ks] = t_scr[kc, pl.ds(1, half, stride=2), :].astype(BF16)
        wd_scr[...] = wdn_ref[0].astype(BF16)

    @pl.when(i < nb_ref[0])
    def _():
        packed = x_ref[...]
        lo = lax.bitcast_convert_type(packed << 16, F32)
        hi = lax.bitcast_convert_type(packed & jnp.uint32(0xFFFF0000), F32)
        x = jnp.concatenate([lo, hi], axis=1).astype(BF16)
        gate = jnp.minimum(_dot_nt(x, wg_scr[...]) + bg_ref[0], SWIGLU_LIMIT)
        up = jnp.clip(_dot_nt(x, wu_scr[...]) + bu_ref[0], -SWIGLU_LIMIT, SWIGLU_LIMIT)
        act = (up + 1.0) * (gate * _sigmoid(SWIGLU_ALPHA * gate))
        o_ref[...] = _dot(act.astype(BF16), wd_scr[...]) + bd_ref[0]

    @pl.when(i >= nb_ref[0])
    def _():
        o_ref[...] = jnp.zeros_like(o_ref)


def _experts(block_e, n_used, xg, w_gu, b_g, b_u, w_d, b_d):
    p = xg.shape[0]
    ff, d = w_d.shape[1], w_d.shape[2]
    wspec = lambda shape: pl.BlockSpec(shape, lambda i, be, nb: (be[i], 0, 0))
    grid_spec = pltpu.PrefetchScalarGridSpec(
        num_scalar_prefetch=2,
        grid=(p // MOE_ROWS,),
        in_specs=[pl.BlockSpec((MOE_ROWS, d // 2), lambda i, be, nb: (i, 0)),
                  wspec((1, d, 2 * ff)), wspec((1, 1, ff)), wspec((1, 1, ff)),
                  wspec((1, ff, d)), wspec((1, 1, d))],
        out_specs=pl.BlockSpec((MOE_ROWS, d), lambda i, be, nb: (i, 0)),
        scratch_shapes=[pltpu.VMEM((ff, d), BF16), pltpu.VMEM((ff, d), BF16), pltpu.VMEM((ff, d), BF16),
                        pltpu.VMEM((d // LANES, SPLIT_COLS, LANES), F32)],
    )
    return pl.pallas_call(
        _experts_body,
        grid_spec=grid_spec,
        out_shape=jax.ShapeDtypeStruct((p, d), F32),
        compiler_params=_cparams(("arbitrary",)),
        name="moe_experts",
    )(block_e, n_used, xg, w_gu, b_g, b_u, w_d, b_d)


SC_WORKERS = 32
SC_BUFFER_BYTES = 128 * 1024


def _sc_gather_rows(table, idx):
    b = idx.shape[0]
    d = table.shape[1]
    per_w = b // SC_WORKERS
    SC_CHUNK = SC_BUFFER_BYTES // (4 * d)
    n = per_w // SC_CHUNK
    assert table.dtype.itemsize == 4 and SC_CHUNK <= 128
    assert b % (SC_WORKERS * SC_CHUNK) == 0 and n % 2 == 0
    mesh = plsc.VectorSubcoreMesh(core_axis_name="c", subcore_axis_name="s")

    @functools.partial(
        pl.kernel, mesh=mesh, out_type=jax.ShapeDtypeStruct((b, d), table.dtype),
        scratch_types=[pltpu.VMEM((n, SC_CHUNK), jnp.int32), pltpu.VMEM((2, SC_CHUNK, d), table.dtype),
                       pltpu.SemaphoreType.DMA((2,)), pltpu.SemaphoreType.DMA((2,))],
        name="sc_gather_rows")
    def gather(table_hbm, idx_hbm, out_hbm, idx_v, rows_v, gsem, wsem):
        wid = lax.axis_index("s") * 2 + lax.axis_index("c")
        base = wid * per_w
        pltpu.sync_copy(idx_hbm.at[wid], idx_v)

        def fetch(c, slot):
            return pltpu.make_async_copy(table_hbm.at[idx_v.at[c]], rows_v.at[slot], gsem.at[slot])

        def write(c, slot):
            off = pl.multiple_of(base + c * SC_CHUNK, SC_CHUNK)
            return pltpu.make_async_copy(rows_v.at[slot], out_hbm.at[pl.ds(off, SC_CHUNK)], wsem.at[slot])

        fetch(0, 0).start()

        @pl.loop(0, n, step=2)
        def _(c0):
            for slot in (0, 1):
                c = c0 + slot
                fetch(c, slot).wait()

                @pl.when(c >= 1)
                def _():
                    write(c - 1, 1 - slot).wait()

                @pl.when(c + 1 < n)
                def _():
                    fetch(c + 1, 1 - slot).start()

                write(c, slot).start()

        write(n - 1, 1).wait()

    return gather(table, idx.reshape(SC_WORKERS, n, SC_CHUNK))


SC_LANES = 16


def _sc_row_tokens(dest, n_rows, n_tokens, slots_per_token):
    s = dest.shape[0]
    per_w = n_rows // SC_WORKERS
    assert n_rows % (SC_WORKERS * SC_LANES) == 0 and s % SC_LANES == 0
    shift = slots_per_token.bit_length() - 1
    assert 1 << shift == slots_per_token
    pad_mask = (1 << (n_tokens.bit_length() - 1)) - 1
    mesh = plsc.VectorSubcoreMesh(core_axis_name="c", subcore_axis_name="s")
    params = pltpu.CompilerParams(needs_layout_passes=False)

    @functools.partial(
        pl.kernel, mesh=mesh, out_type=jax.ShapeDtypeStruct((n_rows,), jnp.int32),
        scratch_types=[pltpu.VMEM((s,), jnp.int32), pltpu.VMEM((per_w,), jnp.int32)],
        compiler_params=params, name="sc_row_tokens")
    def invert(dest_hbm, out_hbm, dest_v, out_v):
        wid = lax.axis_index("s") * 2 + lax.axis_index("c")
        base = wid * per_w
        pltpu.sync_copy(dest_hbm, dest_v)
        lanes = lax.iota(jnp.int32, SC_LANES)

        @pl.loop(0, per_w // SC_LANES)
        def _(i):
            off = pl.multiple_of(i * SC_LANES, SC_LANES)
            out_v[pl.ds(off, SC_LANES)] = (base + off + lanes) & pad_mask

        @pl.loop(0, s // SC_LANES)
        def _(i):
            off = pl.multiple_of(i * SC_LANES, SC_LANES)
            local = dest_v[pl.ds(off, SC_LANES)] - base
            mine = jnp.logical_and(local >= 0, local < per_w)
            tok = lax.shift_right_logical(off + lanes, shift)
            plsc.store_scatter(out_v, [jnp.where(mine, local, 0)], tok, mask=mine)

        pltpu.sync_copy(out_v, out_hbm.at[pl.ds(pl.multiple_of(base, SC_LANES), per_w)])

    return invert(dest)


def _route(top_e, rank, counts):
    t, k = top_e.shape
    s = t * k
    e_flat = top_e.reshape(-1)
    padded = (counts + MOE_ROWS - 1) // MOE_ROWS * MOE_ROWS
    pend = jnp.cumsum(padded)
    onehot = e_flat[:, None] == jnp.arange(N_EXPERTS, dtype=jnp.int32)[None, :]
    dest = jnp.sum(jnp.where(onehot, (pend - padded)[None, :], 0), axis=1) + rank.reshape(-1)
    n_blocks = -(-s // MOE_ROWS) + N_EXPERTS
    row_tok = _sc_row_tokens(dest, n_blocks * MOE_ROWS, t, k)
    block_row0 = jnp.arange(n_blocks, dtype=jnp.int32) * MOE_ROWS
    block_e = jnp.minimum(jnp.sum((pend[None, :] <= block_row0[:, None]).astype(jnp.int32), axis=1),
                          N_EXPERTS - 1)
    n_used = (pend[-1:] // MOE_ROWS).astype(jnp.int32)
    return dest.reshape(t, k), row_tok, block_e, n_used


def _final_body(x_ref, ex_ref, tw_ref, g2_ref, nf_ref, y_ref):
    bb, bl, d = x_ref.shape
    tw = tw_ref[...]
    moe = ex_ref[0] * tw[:, 0:1]
    for k in range(1, TOP_K):
        moe = moe + ex_ref[k] * tw[:, k:k + 1]
    x = x_ref[...] + g2_ref[...] * moe.reshape(bb, bl, d)
    ms = jnp.mean(x * x, axis=-1, keepdims=True)
    y_ref[...] = x * lax.rsqrt(ms + EPS) * nf_ref[...]


def _final(x2, expert_out, top_w, row0, mod, norm_f_w, rows=512):
    b, l, d = x2.shape
    bb, bl, rows = _row_blocks(b, l, rows)
    nl = l // bl
    assert row0 % rows == 0
    blk0 = row0 // rows
    return pl.pallas_call(
        _final_body,
        grid=(b // bb, nl),
        in_specs=[pl.BlockSpec((bb, bl, d), lambda i, j: (i, j, 0)),
                  pl.BlockSpec((TOP_K, rows, d), lambda i, j: (0, blk0 + i * nl + j, 0)),
                  pl.BlockSpec((rows, LANES), lambda i, j: (blk0 + i * nl + j, 0)),
                  pl.BlockSpec((bb, 1, d), lambda i, j: (i, 0, 5)),
                  pl.BlockSpec((1, d), lambda i, j: (0, 0))],
        out_specs=pl.BlockSpec((bb, bl, d), lambda i, j: (i, j, 0)),
        out_shape=jax.ShapeDtypeStruct((b, l, d), F32),
        compiler_params=_cparams(("arbitrary", "arbitrary")),
        name="final_norm",
    )(x2, expert_out, top_w, mod, norm_f_w.reshape(1, d))


def _prep_in_weights(w_in, b_fgate, dt_bias):
    aw = N_HEADS_A * HEAD_DIM_A
    o = np.cumsum([0, aw, aw, aw, N_HEADS_A, D_INNER, CONV_DIM, N_HEADS_S, D_MODEL, D_MODEL])
    w_t = w_in.T
    rows = lambda i: w_t[o[i]:o[i + 1]]
    w_small = jnp.concatenate(
        [rows(3), rows(6), jnp.zeros((LANES - N_HEADS_A - N_HEADS_S, D_MODEL), F32)], axis=0).T
    b_small = jnp.concatenate(
        [b_fgate, dt_bias, jnp.zeros((LANES - N_HEADS_A - N_HEADS_S,), F32)]).reshape(1, LANES)
    big = {"q": rows(0), "k": rows(1), "v": rows(2), "z": rows(4), "xbc": rows(5),
           "gate": w_t[o[7]:o[9]]}
    return {n: w.astype(BF16) for n, w in big.items()}, w_small, b_small


def kernel(x_prompt, x_sample, cache_k, cache_v, cache_logf, state_ssm, state_conv, page_table, c_prompt, c_sample, w_cond, b_cond, norm1_w, w_in, b_fgate, conv_w, conv_b, dt_bias, a_log, d_skip, ssm_norm_w, w_attn_out, w_ssm_out, w_o, norm2_w, w_router, b_router, w_gate_up, b_gate_up, w_down, b_down, norm_f_w):
    assert w_in.shape[0] == 1, "single-layer trunk"
    bp, lp, d = x_prompt.shape
    bs, ls, _ = x_sample.shape
    tp, ts = bp * lp, bs * ls
    aw = N_HEADS_A * HEAD_DIM_A
    c_all = jnp.concatenate([c_prompt, c_sample, jnp.zeros((-(bp + bs) % 8, d), F32)], axis=0)
    mod = _adaln_mod(c_all, w_cond[0], b_cond[0])
    mod_p = mod[:bp].reshape(bp, 1, 6 * d)
    mod_s = mod[bp:bp + bs].reshape(bs, 1, 6 * d)

    w_big, w_small, b_small = _prep_in_weights(w_in[0], b_fgate[0], dt_bias[0])
    pad_lanes = LANES - N_HEADS_A - N_HEADS_S
    alog_lane = jnp.concatenate([jnp.zeros((N_HEADS_A,), F32), a_log[0], jnp.zeros((pad_lanes,), F32)]).reshape(1, LANES)
    dskip_exp = jnp.repeat(d_skip[0], HEAD_DIM_S).reshape(1, D_INNER)
    ssm_nw = ssm_norm_w[0].reshape(1, D_INNER)
    conv_b2 = conv_b[0].reshape(1, CONV_DIM)
    wa, ws, wo = w_attn_out[0].astype(BF16), w_ssm_out[0].astype(BF16), w_o[0].astype(BF16)
    w_router_pad = jnp.pad(w_router[0], ((0, 0), (0, LANES - N_EXPERTS)))
    b_router_pad = jnp.pad(b_router[0], (0, LANES - N_EXPERTS)).reshape(1, LANES)

    def in_proj(x, m, half_copies):
        h, small = _norm_in(x, m, norm1_w[0], w_small, b_small)
        return {n: _matmul_nt(h, w, also_bf16=n in half_copies) for n, w in w_big.items()}, small

    proj_p, small_p = in_proj(x_prompt, mod_p, ("k", "v"))
    proj_s, small_s = in_proj(x_sample, mod_s, ())
    (k_p, kb_p), (v_p, vb_p) = proj_p["k"], proj_p["v"]

    fcum = _cumsum_t(small_p.reshape(bp, lp, LANES)).reshape(bp, N_HEADS_A // 2, 2, lp)
    o_p = _fox_prompt(proj_p["q"].reshape(bp, lp, aw), kb_p.reshape(bp, lp, aw),
                      vb_p.reshape(bp, lp, aw), fcum)
    xbc_p = proj_p["xbc"].reshape(bp, lp, CONV_DIM)
    y_p, st_p = _ssd_prompt(xbc_p, proj_p["z"].reshape(bp, lp, D_INNER), small_p.reshape(bp, lp, LANES),
                            conv_w[0], conv_b2, alog_lane, dskip_exp, ssm_nw)

    n_phys = cache_k.shape[1]
    k_t = jnp.transpose(cache_k[0], (0, 2, 3, 1)).reshape(n_phys, aw, PAGE_SIZE)
    v_t = jnp.transpose(cache_v[0], (0, 2, 3, 1)).reshape(n_phys, aw, PAGE_SIZE)
    lf_t = jnp.swapaxes(cache_logf[0], 1, 2).reshape(n_phys * N_HEADS_A, PAGE_SIZE)
    c_pages = _cumsum_lanes(lf_t, rows=512).reshape(n_phys, N_HEADS_A, PAGE_SIZE)
    n_pages = page_table.shape[1]
    o_s = _fox_sample(proj_s["q"].reshape(bs, ls, aw), proj_s["k"].reshape(bs, ls, aw),
                      proj_s["v"].reshape(bs, ls, aw), small_s.reshape(bs, ls, LANES), k_t, v_t,
                      c_pages, page_table, pages_per_step=min(16, n_pages))
    xbc_s = proj_s["xbc"].reshape(bs, ls, CONV_DIM)
    y_s, st_s = _ssd_sample(xbc_s, state_conv[0], proj_s["z"].reshape(bs, ls, D_INNER),
                            small_s.reshape(bs, ls, LANES), state_ssm[0].reshape(bs, D_INNER, D_STATE),
                            conv_w[0], conv_b2, alog_lane, dskip_exp, ssm_nw)

    post = functools.partial(_post_mixer, w_attn_out=wa, w_ssm_out=ws, w_o=wo, norm2_w=norm2_w[0],
                             w_router=w_router_pad, b_router=b_router_pad)
    bufs = [jnp.zeros((tp + ts, d // 2), jnp.uint32), jnp.zeros((tp + ts, LANES), jnp.int32),
            jnp.zeros((tp + ts, LANES), F32), jnp.zeros((tp + ts, LANES), jnp.int32)]
    x2_p, *bufs, counts_p = post(x_prompt, o_p, y_p, proj_p["gate"], mod_p,
                                 counts_in=jnp.zeros((1, LANES), F32), shared=bufs)
    x2_s, hf, top_e, top_w, rank, counts = post(x_sample, o_s, y_s, proj_s["gate"], mod_s,
                                                counts_in=counts_p, row0=tp, shared=bufs)

    dest, row_tok, block_e, n_used = _route(top_e[:, :TOP_K], rank[:, :TOP_K],
                                            counts[0, :N_EXPERTS].astype(jnp.int32))
    rows = _experts(block_e, n_used, _sc_gather_rows(hf, row_tok), w_gate_up[0],
                    b_gate_up[0][:, None, 0::2], b_gate_up[0][:, None, 1::2],
                    w_down[0], b_down[0][:, None, :])
    expert_out = _sc_gather_rows(rows, dest.T.reshape(-1)).reshape(TOP_K, tp + ts, d)

    y_prompt = _final(x2_p, expert_out, top_w, 0, mod_p, norm_f_w)
    y_sample = _final(x2_s, expert_out, top_w, tp, mod_s, norm_f_w)

    def state_rows(k, v, small, st, xbc, conv0, b, l):
        conv_new = jnp.concatenate([conv0, xbc], axis=1)[:, -(CONV_K - 1):]
        return (k.reshape(1, b, l, N_HEADS_A, HEAD_DIM_A), v.reshape(1, b, l, N_HEADS_A, HEAD_DIM_A),
                small[:, :N_HEADS_A].reshape(1, b, l, N_HEADS_A),
                st.reshape(1, b, N_HEADS_S, HEAD_DIM_S, D_STATE), conv_new[None])

    sp = state_rows(k_p, v_p, small_p, st_p, xbc_p,
                    jnp.zeros((bp, CONV_K - 1, CONV_DIM), F32), bp, lp)
    ss = state_rows(proj_s["k"], proj_s["v"], small_s, st_s, xbc_s, state_conv[0], bs, ls)
    return (y_prompt, y_sample) + sp + ss
```

```python
import functools

import numpy as np
import jax
import jax.numpy as jnp
from jax import lax
from jax.experimental import pallas as pl
from jax.experimental.pallas import tpu as pltpu
from jax.experimental.pallas import tpu_sc as plsc

F32 = jnp.float32
BF16 = jnp.bfloat16

D_MODEL = 1024
N_HEADS_A = 16
HEAD_DIM_A = 64
PAGE_SIZE = 128
D_INNER = 2048
HEAD_DIM_S = 64
N_HEADS_S = 32
N_GROUPS_S = 4
HEADS_PER_GROUP = N_HEADS_S // N_GROUPS_S
GROUP_W = D_INNER // N_GROUPS_S
D_STATE = 128
CONV_K = 4
CONV_DIM = D_INNER + 2 * N_GROUPS_S * D_STATE
SSD_CHUNK = 128
N_EXPERTS = 32
TOP_K = 4
D_FF = D_MODEL
SWIGLU_LIMIT = 7.0
SWIGLU_ALPHA = 1.702
EPS = 1e-6

LANES = 128
DT_LANE0 = N_HEADS_A
NEG_INF = float("-inf")
LOG2E = 1.4426950408889634
VMEM_LIMIT = 56 * 1024 * 1024


def _cparams(sem):
    return pltpu.CompilerParams(dimension_semantics=sem, vmem_limit_bytes=VMEM_LIMIT)


def _split3(x):
    hi = x.astype(BF16)
    r = x - hi.astype(F32)
    mid = r.astype(BF16)
    lo = (r - mid.astype(F32)).astype(BF16)
    return hi, mid, lo


def _dot(a, b):
    return jnp.dot(a, b, preferred_element_type=F32)


def _dot_nt(a, b):
    return lax.dot_general(a, b, (((1,), (1,)), ((), ())), preferred_element_type=F32)


def _dot_sel_left(sel_bf16, x):
    hi, mid, lo = _split3(x)
    return _dot(sel_bf16, hi) + _dot(sel_bf16, mid) + _dot(sel_bf16, lo)


def _dot3(a, b):
    ah = a.astype(BF16)
    am = (a - ah.astype(F32)).astype(BF16)
    bh = b.astype(BF16)
    bm = (b - bh.astype(F32)).astype(BF16)
    return _dot(ah, bh) + _dot(ah, bm) + _dot(am, bh)


def _tri_lower(n):
    r = lax.broadcasted_iota(jnp.int32, (n, n), 0)
    c = lax.broadcasted_iota(jnp.int32, (n, n), 1)
    return r >= c


def _sigmoid(x):
    return 1.0 / (1.0 + jnp.exp(-x))


def _silu(x):
    return x * _sigmoid(x)


def _mod_body(c_ref, w_ref, b_ref, o_ref):
    c = c_ref[...]
    o_ref[...] = _dot3(_silu(c), w_ref[...]) + b_ref[...]


def _adaln_mod(c_all, w_cond, b_cond):
    rows, d = c_all.shape
    n = w_cond.shape[1]
    tn = 1024
    return pl.pallas_call(
        _mod_body,
        grid=(n // tn,),
        in_specs=[pl.BlockSpec((rows, d), lambda j: (0, 0)),
                  pl.BlockSpec((d, tn), lambda j: (0, j)),
                  pl.BlockSpec((1, tn), lambda j: (0, j))],
        out_specs=pl.BlockSpec((rows, tn), lambda j: (0, j)),
        out_shape=jax.ShapeDtypeStruct((rows, n), F32),
        compiler_params=_cparams(("arbitrary",)),
        name="adaln_mod",
    )(c_all, w_cond, b_cond.reshape(1, n))


def _norm_in_body(x_ref, sh_ref, sc_ref, nw_ref, ws_ref, bs_ref, h_ref, sm_ref):
    x = x_ref[...]
    bb, bl, d = x.shape
    ms = jnp.mean(x * x, axis=-1, keepdims=True)
    y = x * lax.rsqrt(ms + EPS) * nw_ref[...]
    h = (y * (1.0 + sc_ref[...]) + sh_ref[...]).reshape(bb * bl, d)
    h_ref[...] = h.astype(BF16)
    sm = _dot3(h, ws_ref[...]) + bs_ref[...]
    lane = lax.broadcasted_iota(jnp.int32, sm.shape, 1)
    t = jnp.log(1.0 + jnp.exp(-jnp.abs(sm)))
    sm_ref[...] = jnp.where(lane < N_HEADS_A, jnp.minimum(sm, 0.0) - t, jnp.maximum(sm, 0.0) + t)


def _row_blocks(b, l, rows):
    rows = min(rows, b * l)
    if l >= rows:
        assert l % rows == 0
        return 1, rows, rows
    assert rows % l == 0 and b % (rows // l) == 0
    return rows // l, l, rows


def _norm_in(x, mod, norm_w, w_small, b_small, rows=512):
    b, l, d = x.shape
    bb, bl, rows = _row_blocks(b, l, rows)
    nl = l // bl
    grid = (b // bb, nl)
    return pl.pallas_call(
        _norm_in_body,
        grid=grid,
        in_specs=[pl.BlockSpec((bb, bl, d), lambda i, j: (i, j, 0)),
                  pl.BlockSpec((bb, 1, d), lambda i, j: (i, 0, 0)),
                  pl.BlockSpec((bb, 1, d), lambda i, j: (i, 0, 1)),
                  pl.BlockSpec((1, d), lambda i, j: (0, 0)),
                  pl.BlockSpec((d, LANES), lambda i, j: (0, 0)),
                  pl.BlockSpec((1, LANES), lambda i, j: (0, 0))],
        out_specs=[pl.BlockSpec((rows, d), lambda i, j: (i * nl + j, 0)),
                   pl.BlockSpec((rows, LANES), lambda i, j: (i * nl + j, 0))],
        out_shape=[jax.ShapeDtypeStruct((b * l, d), BF16),
                   jax.ShapeDtypeStruct((b * l, LANES), F32)],
        compiler_params=_cparams(("arbitrary", "arbitrary")),
        name="norm_in",
    )(x, mod, mod, norm_w.reshape(1, d), w_small, b_small)


def _mm_body(a_ref, w_ref, o_ref, *half_ref):
    r = _dot_nt(a_ref[...], w_ref[...])
    o_ref[...] = r
    for h in half_ref:
        h[...] = r.astype(BF16)


def _matmul_nt(a, w_t, also_bf16=False, tm=1024, tn=1024):
    m, k = a.shape
    n = w_t.shape[0]
    tn = min(tn, n)
    tm = min(tm, m)
    out_spec = pl.BlockSpec((tm, tn), lambda j, i: (i, j))
    n_out = 2 if also_bf16 else 1
    res = pl.pallas_call(
        _mm_body,
        grid=(n // tn, m // tm),
        in_specs=[pl.BlockSpec((tm, k), lambda j, i: (i, 0)),
                  pl.BlockSpec((tn, k), lambda j, i: (j, 0))],
        out_specs=[out_spec] * n_out,
        out_shape=[jax.ShapeDtypeStruct((m, n), F32), jax.ShapeDtypeStruct((m, n), BF16)][:n_out],
        compiler_params=_cparams(("arbitrary", "arbitrary")),
        name="proj_matmul",
    )(a, w_t)
    return res if also_bf16 else res[0]


def _block_cumsum_t(blk, carry_row):
    tri = _tri_lower(PAGE_SIZE).astype(BF16)
    cs = _dot_sel_left(tri, blk) + carry_row
    return cs.T[0:N_HEADS_A, :], cs[PAGE_SIZE - 1:PAGE_SIZE, :]


def _cumsum_t_body(x_ref, o_ref, carry_scr, *, blocks):
    @pl.when(pl.program_id(1) == 0)
    def _():
        carry_scr[...] = jnp.zeros_like(carry_scr)

    carry = carry_scr[...]
    for i in range(blocks):
        ps = slice(i * PAGE_SIZE, (i + 1) * PAGE_SIZE)
        o_ref[0, :, ps], carry = _block_cumsum_t(x_ref[0, ps, :], carry)
    carry_scr[...] = carry


def _cumsum_t(x, blocks=4):
    b, l, width = x.shape
    blocks = min(blocks, l // PAGE_SIZE)
    span = blocks * PAGE_SIZE
    assert width == LANES and l % span == 0
    return pl.pallas_call(
        functools.partial(_cumsum_t_body, blocks=blocks),
        grid=(b, l // span),
        in_specs=[pl.BlockSpec((1, span, width), lambda i, j: (i, j, 0))],
        out_specs=pl.BlockSpec((1, N_HEADS_A, span), lambda i, j: (i, 0, j)),
        out_shape=jax.ShapeDtypeStruct((b, N_HEADS_A, l), F32),
        scratch_shapes=[pltpu.VMEM((1, LANES), F32)],
        compiler_params=_cparams(("arbitrary", "arbitrary")),
        name="logf_cumsum",
    )(x)


def _cumsum_lanes_body(x_ref, o_ref):
    n = x_ref.shape[1]
    r = lax.broadcasted_iota(jnp.int32, (n, n), 0)
    c = lax.broadcasted_iota(jnp.int32, (n, n), 1)
    tri = (r <= c).astype(BF16)
    hi, mid, lo = _split3(x_ref[...])
    o_ref[...] = _dot(hi, tri) + _dot(mid, tri) + _dot(lo, tri)


def _cumsum_lanes(x, rows=1024):
    r, n = x.shape
    assert r % rows == 0
    return pl.pallas_call(
        _cumsum_lanes_body,
        grid=(r // rows,),
        in_specs=[pl.BlockSpec((rows, n), lambda i: (i, 0))],
        out_specs=pl.BlockSpec((rows, n), lambda i: (i, 0)),
        out_shape=jax.ShapeDtypeStruct((r, n), F32),
        compiler_params=_cparams(("arbitrary",)),
        name="page_logf_cumsum",
    )(x)


def _fox_prompt_body(it_ref, jt_ref, q_ref, k_ref, v_ref, f_ref, o_ref, qm_scr, m_scr, acc_scr, *,
                     tile, sub):
    t = pl.program_id(2)
    i = it_ref[t]
    j = jt_ref[t]
    lane = lax.broadcasted_iota(jnp.int32, (tile, LANES), 1)
    first = lane < HEAD_DIM_A

    @pl.when(j == 0)
    def _():
        q = q_ref[0] * (HEAD_DIM_A ** -0.5 * LOG2E)
        qm_scr[0] = jnp.where(first, q, 0.0).astype(BF16)
        qm_scr[1] = jnp.where(first, 0.0, q).astype(BF16)
        m_scr[...] = jnp.full_like(m_scr, -1e30)
        acc_scr[...] = jnp.zeros_like(acc_scr)

    def step(diagonal):
        one = jnp.ones((), BF16)
        sub_q, sub_k = sub[1] if diagonal else sub[0]
        first_s = lax.broadcasted_iota(jnp.int32, (sub_k, LANES), 1) < HEAD_DIM_A
        for kj in range(tile // sub_k):
            ks = slice(kj * sub_k, (kj + 1) * sub_k)
            kb = k_ref[0, ks, :]
            vb = v_ref[0, ks, :]
            v_augs = (jnp.where(first_s, vb, one), jnp.where(first_s, one, vb))
            for qi in range(tile // sub_q):
                if diagonal and kj * sub_k >= (qi + 1) * sub_q:
                    continue
                qs = slice(qi * sub_q, (qi + 1) * sub_q)
                for h in range(2):
                    s = _dot_nt(qm_scr[h, qs, :], kb) - f_ref[0, 0, h:h + 1, ks] * LOG2E
                    if diagonal and (kj + 1) * sub_k - 1 > qi * sub_q:
                        row = lax.broadcasted_iota(jnp.int32, (sub_q, sub_k), 0) + qi * sub_q
                        col = lax.broadcasted_iota(jnp.int32, (sub_q, sub_k), 1) + kj * sub_k
                        s = jnp.where(col <= row, s, NEG_INF)
                    m_old = m_scr[h, qs, :]
                    m_new = jnp.maximum(m_old, jnp.max(s, axis=-1, keepdims=True))
                    alpha = jnp.exp2(m_old - m_new)
                    p = jnp.exp2(s - jnp.concatenate([m_new] * (sub_k // LANES), axis=1))
                    acc_scr[h, qs, :] = alpha * acc_scr[h, qs, :] + _dot(p.astype(BF16), v_augs[h])
                    m_scr[h, qs, :] = m_new

    @pl.when(j < i)
    def _():
        step(False)

    @pl.when(j == i)
    def _():
        step(True)
        a0 = acc_scr[0]
        a1 = acc_scr[1]
        den = jnp.where(first, pltpu.roll(a0, HEAD_DIM_A, 1), pltpu.roll(a1, HEAD_DIM_A, 1))
        o_ref[0] = jnp.where(first, a0, a1) / den


def _fox_prompt(q, k, v, fcum, tile=1024, sub=((1024, 512), (512, 512))):
    b, l, w = q.shape
    npair = w // LANES
    tile = min(tile, l)
    sub = tuple((min(sq, tile), min(sk, tile)) for sq, sk in sub)
    assert l % tile == 0 and all(tile % sq == 0 and tile % sk == 0 for sq, sk in sub)
    nt = l // tile
    it = np.concatenate([np.full(i + 1, i, np.int32) for i in range(nt)])
    jt = np.concatenate([np.arange(i + 1, dtype=np.int32) for i in range(nt)])
    grid_spec = pltpu.PrefetchScalarGridSpec(
        num_scalar_prefetch=2,
        grid=(b, npair, len(it)),
        in_specs=[pl.BlockSpec((1, tile, LANES), lambda bi, p, t, it, jt: (bi, it[t], p)),
                  pl.BlockSpec((1, tile, LANES), lambda bi, p, t, it, jt: (bi, jt[t], p)),
                  pl.BlockSpec((1, tile, LANES), lambda bi, p, t, it, jt: (bi, jt[t], p)),
                  pl.BlockSpec((1, 1, 2, tile), lambda bi, p, t, it, jt: (bi, p, 0, jt[t]))],
        out_specs=pl.BlockSpec((1, tile, LANES), lambda bi, p, t, it, jt: (bi, it[t], p)),
        scratch_shapes=[pltpu.VMEM((2, tile, LANES), BF16), pltpu.VMEM((2, tile, LANES), F32),
                        pltpu.VMEM((2, tile, LANES), F32)],
    )
    return pl.pallas_call(
        functools.partial(_fox_prompt_body, tile=tile, sub=sub),
        grid_spec=grid_spec,
        out_shape=jax.ShapeDtypeStruct((b, l, w), F32),
        compiler_params=_cparams(("arbitrary", "arbitrary", "arbitrary")),
        name="fox_prompt",
    )(jnp.asarray(it), jnp.asarray(jt), q, k, v, fcum)


def _fox_sample_body(pt_ref, q_ref, kn_ref, vn_ref, lfn_ref, *rest, pages_per_step):
    pg = pages_per_step
    k_refs = rest[0:pg]
    v_refs = rest[pg:2 * pg]
    c_refs = rest[2 * pg:3 * pg]
    o_ref, qbd_scr, m_scr, l_scr, carry_scr, acc_scr = rest[3 * pg:]
    j = pl.program_id(1)
    nq = q_ref.shape[1]
    rows = N_HEADS_A * nq
    width = N_HEADS_A * HEAD_DIM_A

    @pl.when(j == 0)
    def _():
        q = q_ref[0] * (HEAD_DIM_A ** -0.5)
        qt = jnp.broadcast_to(q[None], (N_HEADS_A, nq, width)).reshape(rows, width)
        row = lax.broadcasted_iota(jnp.int32, (rows, width), 0)
        lane = lax.broadcasted_iota(jnp.int32, (rows, width), 1)
        own = (lane // HEAD_DIM_A) == (row // nq)
        qbd_scr[...] = jnp.where(own, qt, 0.0).astype(BF16)
        m_scr[...] = jnp.full_like(m_scr, -1e30)
        l_scr[...] = jnp.zeros_like(l_scr)
        carry_scr[...] = jnp.zeros_like(carry_scr)
        acc_scr[...] = jnp.zeros_like(acc_scr)

    def attend(blocks, valid, transposed):
        qbd = qbd_scr[...]
        carry = carry_scr[...]
        scores = []
        for kf, _, c16 in blocks:
            s = _dot(qbd, kf.astype(BF16)) if transposed else _dot_nt(qbd, kf.astype(BF16))
            nk = s.shape[1]
            cexp = jnp.broadcast_to(c16[:, None, :], (N_HEADS_A, nq, nk)).reshape(rows, nk)
            s = s - (cexp + carry)
            if valid is not None:
                s = jnp.where(valid, s, NEG_INF)
            scores.append(s)
            carry = carry + cexp[:, nk - 1:nk]
        carry_scr[...] = carry
        smax = scores[0]
        for s in scores[1:]:
            smax = jnp.maximum(smax, s)
        m_old = m_scr[...]
        m_new = jnp.maximum(m_old, jnp.max(smax, axis=-1, keepdims=True))
        alpha = jnp.exp(m_old - m_new)
        m_scr[...] = m_new
        psum = None
        pv = None
        for s, (_, vf, _) in zip(scores, blocks):
            p = jnp.exp(s - m_new)
            psum = p if psum is None else psum + p
            pb = p.astype(BF16)
            d = _dot_nt(pb, vf.astype(BF16)) if transposed else _dot(pb, vf.astype(BF16))
            pv = d if pv is None else pv + d
        l_scr[...] = alpha * l_scr[...] + jnp.sum(psum, axis=-1, keepdims=True)
        acc_scr[...] = acc_scr[...] * alpha + pv

    attend([(k_refs[g][0], v_refs[g][0], c_refs[g][0]) for g in range(pg)], None, True)

    @pl.when(j == pl.num_programs(1) - 1)
    def _():
        zpad = jnp.zeros((PAGE_SIZE - nq, width), F32)
        kpad = jnp.concatenate([kn_ref[0], zpad], axis=0)
        vpad = jnp.concatenate([vn_ref[0], zpad], axis=0)
        row = lax.broadcasted_iota(jnp.int32, (rows, PAGE_SIZE), 0)
        key = lax.broadcasted_iota(jnp.int32, (rows, PAGE_SIZE), 1)
        lf_pad = jnp.concatenate([lfn_ref[0], jnp.zeros((PAGE_SIZE - nq, LANES), F32)], axis=0)
        c_new, _ = _block_cumsum_t(lf_pad, jnp.zeros((1, LANES), F32))
        attend([(kpad, vpad, c_new)], key <= (row % nq), False)
        o = acc_scr[...] / l_scr[...]
        for h in range(N_HEADS_A):
            o_ref[0, :, h * HEAD_DIM_A:(h + 1) * HEAD_DIM_A] = (
                o[h * nq:(h + 1) * nq, h * HEAD_DIM_A:(h + 1) * HEAD_DIM_A])


def _fox_sample(q, k_new, v_new, lf_new, cache_k, cache_v, c_pages, page_table, pages_per_step=8):
    b, nq, w = q.shape
    n_pages = page_table.shape[1]
    pg = pages_per_step
    assert n_pages % pg == 0
    rows = N_HEADS_A * nq

    def page_map(g):
        return lambda bi, j, pt: (pt[bi * n_pages + j * pg + g], 0, 0)

    seq_map = lambda bi, j, pt: (bi, 0, 0)
    in_specs = [pl.BlockSpec((1, nq, w), seq_map), pl.BlockSpec((1, nq, w), seq_map),
                pl.BlockSpec((1, nq, w), seq_map), pl.BlockSpec((1, nq, LANES), seq_map)]
    in_specs += [pl.BlockSpec((1, w, PAGE_SIZE), page_map(g)) for g in range(pg)]
    in_specs += [pl.BlockSpec((1, w, PAGE_SIZE), page_map(g)) for g in range(pg)]
    in_specs += [pl.BlockSpec((1, N_HEADS_A, PAGE_SIZE), page_map(g)) for g in range(pg)]
    grid_spec = pltpu.PrefetchScalarGridSpec(
        num_scalar_prefetch=1,
        grid=(b, n_pages // pg),
        in_specs=in_specs,
        out_specs=pl.BlockSpec((1, nq, w), seq_map),
        scratch_shapes=[pltpu.VMEM((rows, w), BF16), pltpu.VMEM((rows, 1), F32), pltpu.VMEM((rows, 1), F32),
                        pltpu.VMEM((rows, 1), F32), pltpu.VMEM((rows, w), F32)],
    )
    args = [q, k_new, v_new, lf_new] + [cache_k] * pg + [cache_v] * pg + [c_pages] * pg
    return pl.pallas_call(
        functools.partial(_fox_sample_body, pages_per_step=pg),
        grid_spec=grid_spec,
        out_shape=jax.ShapeDtypeStruct((b, nq, w), F32),
        compiler_params=_cparams(("arbitrary", "arbitrary")),
        name="fox_sample",
    )(page_table.reshape(-1), *args)


def _causal_conv_silu(ext_scr, cw_ref, cb_ref, n):
    acc = cb_ref[...] + cw_ref[0:1, :] * ext_scr[8 - (CONV_K - 1):8 - (CONV_K - 1) + n, :]
    for i in range(1, CONV_K):
        acc = acc + cw_ref[i:i + 1, :] * ext_scr[8 - (CONV_K - 1) + i:8 - (CONV_K - 1) + i + n, :]
    return _silu(acc)


def _pair_select(first, a, b):
    return jnp.where(first, a, b)


def _ssd_prompt_body(xbc_ref, z_ref, sm_ref, cw_ref, cb_ref, alog_ref, dskip_ref, nw_ref,
                     y_ref, st_ref, st_scr, ext_scr):
    c = pl.program_id(1)
    q = SSD_CHUNK
    pair_w = 2 * HEAD_DIM_S

    @pl.when(c == 0)
    def _():
        st_scr[...] = jnp.zeros_like(st_scr)
        ext_scr[0:8, :] = jnp.zeros((8, CONV_DIM), F32)

    cur = xbc_ref[0]
    ext_scr[8:8 + q, :] = cur
    act = _causal_conv_silu(ext_scr, cw_ref, cb_ref, q)
    ext_scr[0:8, :] = cur[q - 8:q, :]

    sm = sm_ref[0]
    a_lane = -jnp.exp(alog_ref[...])
    tri_mask = _tri_lower(q)
    a_all = _dot_sel_left(tri_mask.astype(BF16), sm * a_lane)
    a_t = a_all.T
    dt_t = sm.T
    w_t = jnp.exp(a_t[:, q - 1:q] - a_t) * dt_t
    ea_all = jnp.exp(a_all)
    lane = lax.broadcasted_iota(jnp.int32, (1, pair_w), 1)
    first = lane < HEAD_DIM_S

    for g in range(N_GROUPS_S):
        bc = act[:, D_INNER + g * D_STATE:D_INNER + (g + 1) * D_STATE]
        cc = act[:, D_INNER + (N_GROUPS_S + g) * D_STATE:D_INNER + (N_GROUPS_S + g + 1) * D_STATE]
        cb = _dot_nt(cc.astype(BF16), bc.astype(BF16))
        bc_t = bc.T
        gated = []
        ssq = jnp.zeros((q, 1), F32)
        for pp in range(HEADS_PER_GROUP // 2):
            h0 = g * HEADS_PER_GROUP + 2 * pp
            col = h0 * HEAD_DIM_S
            x_pair = act[:, col:col + pair_w]
            s_pair = st_scr[:, col:col + pair_w]
            rhs = jnp.concatenate([x_pair.astype(BF16), s_pair.astype(BF16)], axis=0)
            ys, us, el = [], [], []
            for h in (h0, h0 + 1):
                li = DT_LANE0 + h
                seg = a_all[:, li:li + 1] - a_t[li:li + 1, :]
                dec = jnp.exp(jnp.where(tri_mask, seg, NEG_INF))
                m_h = cb * dec * dt_t[li:li + 1, :]
                lhs = jnp.concatenate([m_h, cc * ea_all[:, li:li + 1]], axis=1).astype(BF16)
                ys.append(_dot(lhs, rhs))
                us.append(_dot((bc_t * w_t[li:li + 1, :]).astype(BF16), x_pair.astype(BF16)))
                el.append(ea_all[q - 1:q, li:li + 1])
            y_pair = _pair_select(first, ys[0], ys[1]) + x_pair * dskip_ref[:, col:col + pair_w]
            st_scr[:, col:col + pair_w] = (s_pair * _pair_select(first, el[0], el[1])
                                           + _pair_select(first, us[0], us[1]))
            gp = y_pair * _silu(z_ref[0, :, col:col + pair_w])
            ssq = ssq + jnp.sum(gp * gp, axis=-1, keepdims=True)
            gated.append((col, gp))
        rs = lax.rsqrt(ssq / GROUP_W + EPS)
        for col, gp in gated:
            y_ref[0, :, col:col + pair_w] = (gp * rs * nw_ref[:, col:col + pair_w]).astype(y_ref.dtype)

    @pl.when(c == pl.num_programs(1) - 1)
    def _():
        for blk in range(D_INNER // pair_w):
            st_ref[0, blk * pair_w:(blk + 1) * pair_w, :] = st_scr[:, blk * pair_w:(blk + 1) * pair_w].T


def _ssd_prompt(xbc, z, small, conv_w, conv_b, alog_lane, dskip_exp, norm_w):
    b, l, _ = xbc.shape
    q = SSD_CHUNK
    full = lambda shape: pl.BlockSpec(shape, lambda bi, c: (0,) * len(shape))
    return pl.pallas_call(
        _ssd_prompt_body,
        grid=(b, l // q),
        in_specs=[pl.BlockSpec((1, q, CONV_DIM), lambda bi, c: (bi, c, 0)),
                  pl.BlockSpec((1, q, D_INNER), lambda bi, c: (bi, c, 0)),
                  pl.BlockSpec((1, q, LANES), lambda bi, c: (bi, c, 0)),
                  full((CONV_K, CONV_DIM)), full((1, CONV_DIM)), full((1, LANES)),
                  full((1, D_INNER)), full((1, D_INNER))],
        out_specs=[pl.BlockSpec((1, q, D_INNER), lambda bi, c: (bi, c, 0)),
                   pl.BlockSpec((1, D_INNER, D_STATE), lambda bi, c: (bi, 0, 0))],
        out_shape=[jax.ShapeDtypeStruct((b, l, D_INNER), BF16),
                   jax.ShapeDtypeStruct((b, D_INNER, D_STATE), F32)],
        scratch_shapes=[pltpu.VMEM((D_STATE, D_INNER), F32), pltpu.VMEM((8 + q, CONV_DIM), F32)],
        compiler_params=_cparams(("arbitrary", "arbitrary")),
        name="ssd_prompt",
    )(xbc, z, small, conv_w, conv_b, alog_lane, dskip_exp, norm_w)


def _ssd_sample_body(xbc_ref, c0_ref, z_ref, sm_ref, st_ref, cw_ref, cb_ref, alog_ref, dskip_ref, nw_ref,
                     y_ref, so_ref, ext_scr):
    l = xbc_ref.shape[1]
    pair_w = 2 * HEAD_DIM_S
    ext_scr[0:8, :] = jnp.zeros((8, CONV_DIM), F32)
    ext_scr[8 - (CONV_K - 1):8, :] = c0_ref[0]
    ext_scr[8:8 + l, :] = xbc_ref[0]
    act = _causal_conv_silu(ext_scr, cw_ref, cb_ref, l)

    sm = sm_ref[0]
    dta = sm * (-jnp.exp(alog_ref[...]))
    trow = lax.broadcasted_iota(jnp.int32, (l, LANES), 0)
    a_c = jnp.zeros((l, LANES), F32)
    for s in range(l):
        a_c = a_c + jnp.where(trow >= s, dta[s:s + 1, :], 0.0)
    lane = lax.broadcasted_iota(jnp.int32, (1, pair_w), 1)
    first = lane < HEAD_DIM_S

    def expand(v):
        cols = []
        for pp in range(N_HEADS_S // 2):
            li = DT_LANE0 + 2 * pp
            cols.append(_pair_select(first, v[:, li:li + 1], v[:, li + 1:li + 2]))
        return jnp.concatenate(cols, axis=1)

    a_x = expand(a_c)
    dt_x = expand(sm)
    ea_x = jnp.exp(a_x)
    w_x = jnp.exp(a_x[l - 1:l, :] - a_x) * dt_x
    xs = act[:, 0:D_INNER]
    zpad = jnp.zeros((PAGE_SIZE - l, D_STATE), F32)
    trow_x = lax.broadcasted_iota(jnp.int32, (l, GROUP_W), 0)

    xw_pad = jnp.concatenate([xs * w_x, jnp.zeros((PAGE_SIZE - l, D_INNER), F32)], axis=0)
    ea_last = jnp.exp(a_c[l - 1:l, :])
    rowsel = lax.broadcasted_iota(jnp.int32, (pair_w, 1), 0) < HEAD_DIM_S

    for g in range(N_GROUPS_S):
        gc = g * GROUP_W
        bc = act[:, D_INNER + g * D_STATE:D_INNER + (g + 1) * D_STATE]
        cc = act[:, D_INNER + (N_GROUPS_S + g) * D_STATE:D_INNER + (N_GROUPS_S + g + 1) * D_STATE]
        b_pad = jnp.concatenate([bc, zpad], axis=0).astype(BF16)
        cb = _dot_nt(cc.astype(BF16), b_pad)
        s_g = st_ref[0, gc:gc + GROUP_W, :]
        y = _dot_nt(cc.astype(BF16), s_g.astype(BF16)) * ea_x[:, gc:gc + GROUP_W]
        x_g = xs[:, gc:gc + GROUP_W]
        a_g = a_x[:, gc:gc + GROUP_W]
        dt_g = dt_x[:, gc:gc + GROUP_W]
        for s in range(l):
            dec = jnp.exp(jnp.where(trow_x >= s, a_g - a_g[s:s + 1, :], NEG_INF))
            y = y + cb[:, s:s + 1] * dec * (dt_g[s:s + 1, :] * x_g[s:s + 1, :])
        y = y + x_g * dskip_ref[:, gc:gc + GROUP_W]
        gp = y * _silu(z_ref[0, :, gc:gc + GROUP_W])
        rs = lax.rsqrt(jnp.sum(gp * gp, axis=-1, keepdims=True) / GROUP_W + EPS)
        y_ref[0, :, gc:gc + GROUP_W] = gp * rs * nw_ref[:, gc:gc + GROUP_W]
        for pp in range(HEADS_PER_GROUP // 2):
            h0 = g * HEADS_PER_GROUP + 2 * pp
            col = h0 * HEAD_DIM_S
            li = DT_LANE0 + h0
            u = _dot(xw_pad[:, col:col + pair_w].T.astype(BF16), b_pad)
            e_col = jnp.where(rowsel, ea_last[:, li:li + 1], ea_last[:, li + 1:li + 2])
            so_ref[0, col:col + pair_w, :] = st_ref[0, col:col + pair_w, :] * e_col + u


def _ssd_sample(xbc, conv0, z, small, state, conv_w, conv_b, alog_lane, dskip_exp, norm_w):
    b, l, _ = xbc.shape
    full = lambda shape: pl.BlockSpec(shape, lambda bi: (0,) * len(shape))
    seq = lambda shape: pl.BlockSpec(shape, lambda bi: (bi, 0, 0))
    return pl.pallas_call(
        _ssd_sample_body,
        grid=(b,),
        in_specs=[seq((1, l, CONV_DIM)), seq((1, CONV_K - 1, CONV_DIM)), seq((1, l, D_INNER)),
                  seq((1, l, LANES)), seq((1, D_INNER, D_STATE)),
                  full((CONV_K, CONV_DIM)), full((1, CONV_DIM)), full((1, LANES)),
                  full((1, D_INNER)), full((1, D_INNER))],
        out_specs=[seq((1, l, D_INNER)), seq((1, D_INNER, D_STATE))],
        out_shape=[jax.ShapeDtypeStruct((b, l, D_INNER), F32),
                   jax.ShapeDtypeStruct((b, D_INNER, D_STATE), F32)],
        scratch_shapes=[pltpu.VMEM((8 + l, CONV_DIM), F32)],
        compiler_params=_cparams(("arbitrary",)),
        name="ssd_sample",
    )(xbc, conv0, z, small, state, conv_w, conv_b, alog_lane, dskip_exp, norm_w)


def _post_mixer_body(x_ref, oa_ref, ys_ref, gate_ref, g1_ref, sh2_ref, sc2_ref, wa_ref, ws_ref, wo_ref,
                     n2_ref, wr_ref, br_ref, cin_ref, h_in, te_in, tw_in, rk_in,
                     x2_ref, h_ref, te_ref, tw_ref, rk_ref, cout_ref, count_scr):
    del h_in, te_in, tw_in, rk_in
    bb, bl, d = x_ref.shape
    rows = bb * bl

    @pl.when(jnp.logical_and(pl.program_id(0) == 0, pl.program_id(1) == 0))
    def _():
        count_scr[...] = cin_ref[...]

    oa = oa_ref[...].reshape(rows, d).astype(BF16)
    ys = ys_ref[...].reshape(rows, D_INNER).astype(BF16)
    y_attn = _dot(oa, wa_ref[...])
    y_ssm = _dot(ys, ws_ref[...])
    gate = gate_ref[...]
    merged = _sigmoid(gate[:, 0:d]) * y_attn + _sigmoid(gate[:, d:2 * d]) * y_ssm
    mix = _dot(merged.astype(BF16), wo_ref[...])
    x2 = x_ref[...] + g1_ref[...] * mix.reshape(bb, bl, d)
    x2_ref[...] = x2
    ms = jnp.mean(x2 * x2, axis=-1, keepdims=True)
    hf = (x2 * lax.rsqrt(ms + EPS) * n2_ref[...]) * (1.0 + sc2_ref[...]) + sh2_ref[...]
    hf = hf.reshape(rows, d)
    bits = lax.bitcast_convert_type(hf.astype(BF16).astype(F32), jnp.uint32)
    h_ref[...] = (bits[:, d // 2:] & jnp.uint32(0xFFFF0000)) | (bits[:, 0:d // 2] >> 16)
    lane = lax.broadcasted_iota(jnp.int32, (rows, LANES), 1)
    logits = jnp.where(lane < N_EXPERTS, _dot3(hf, wr_ref[...]) + br_ref[...], NEG_INF)
    vals, idxs = [], []
    for _ in range(TOP_K):
        mx = jnp.max(logits, axis=-1, keepdims=True)
        idx = jnp.min(jnp.where(logits == mx, lane, LANES), axis=-1, keepdims=True)
        vals.append(mx)
        idxs.append(idx)
        logits = jnp.where(lane == idx, NEG_INF, logits)
    ex = [jnp.exp(v - vals[0]) for v in vals]
    den = ex[0] + ex[1] + ex[2] + ex[3]
    chosen = [lane == idxs[k] for k in range(TOP_K)]
    cnt = chosen[0].astype(F32)
    for k in range(1, TOP_K):
        cnt = cnt + chosen[k].astype(F32)
    r_i = lax.broadcasted_iota(jnp.int32, (rows, rows), 0)
    c_i = lax.broadcasted_iota(jnp.int32, (rows, rows), 1)
    before = _dot((c_i < r_i).astype(BF16), cnt.astype(BF16)) + count_scr[...]
    te = jnp.zeros((rows, LANES), jnp.int32)
    tw = jnp.zeros((rows, LANES), F32)
    rk = jnp.zeros((rows, LANES), jnp.int32)
    for k in range(TOP_K):
        rank = jnp.sum(jnp.where(chosen[k], before, 0.0), axis=-1, keepdims=True)
        te = jnp.where(lane == k, idxs[k], te)
        tw = jnp.where(lane == k, ex[k] / den, tw)
        rk = jnp.where(lane == k, rank.astype(jnp.int32), rk)
    te_ref[...] = te
    tw_ref[...] = tw
    rk_ref[...] = rk
    count_scr[...] = count_scr[...] + jnp.sum(cnt, axis=0, keepdims=True)
    cout_ref[...] = count_scr[...]


def _post_mixer(x, o_attn, y_ssm, gate, mod, w_attn_out, w_ssm_out, w_o, norm2_w, w_router, b_router,
                counts_in, shared, row0=0, rows=256):
    b, l, d = x.shape
    bb, bl, rows = _row_blocks(b, l, rows)
    nl = l // bl
    assert row0 % rows == 0
    blk0 = row0 // rows
    tok = lambda w: pl.BlockSpec((bb, bl, w), lambda i, j: (i, j, 0))
    flat = lambda w: pl.BlockSpec((rows, w), lambda i, j: (i * nl + j, 0))
    buf = lambda w: pl.BlockSpec((rows, w), lambda i, j: (blk0 + i * nl + j, 0))
    modc = lambda c: pl.BlockSpec((bb, 1, d), lambda i, j: (i, 0, c))
    full = lambda shape: pl.BlockSpec(shape, lambda i, j: (0,) * len(shape))
    in_specs = [tok(d), tok(d), tok(D_INNER), flat(2 * d), modc(2), modc(3), modc(4),
                full((d, d)), full((D_INNER, d)), full((d, d)), full((1, d)),
                full((d, LANES)), full((1, LANES)), full((1, LANES))]
    args = [x, o_attn, y_ssm, gate, mod, mod, mod, w_attn_out, w_ssm_out, w_o, norm2_w.reshape(1, d),
            w_router, b_router, counts_in]
    aliases = {len(args) + i: 1 + i for i in range(len(shared))}
    in_specs += [pl.BlockSpec(memory_space=pl.ANY)] * len(shared)
    args += list(shared)
    total_rows = shared[0].shape[0]
    return pl.pallas_call(
        _post_mixer_body,
        grid=(b // bb, nl),
        in_specs=in_specs,
        out_specs=[tok(d), buf(d // 2), buf(LANES), buf(LANES), buf(LANES), full((1, LANES))],
        out_shape=[jax.ShapeDtypeStruct((b, l, d), F32),
                   jax.ShapeDtypeStruct((total_rows, d // 2), jnp.uint32),
                   jax.ShapeDtypeStruct((total_rows, LANES), jnp.int32),
                   jax.ShapeDtypeStruct((total_rows, LANES), F32),
                   jax.ShapeDtypeStruct((total_rows, LANES), jnp.int32),
                   jax.ShapeDtypeStruct((1, LANES), F32)],
        scratch_shapes=[pltpu.VMEM((1, LANES), F32)],
        input_output_aliases=aliases,
        compiler_params=_cparams(("arbitrary", "arbitrary")),
        name="post_mixer",
    )(*args)


MOE_ROWS = 512
SPLIT_COLS = 512


def _experts_body(be_ref, nb_ref, x_ref, wgu_ref, bg_ref, bu_ref, wdn_ref, bd_ref, o_ref,
                  wg_scr, wu_scr, wd_scr, t_scr):
    i = pl.program_id(0)
    d = wdn_ref.shape[2]
    half = SPLIT_COLS // 2

    @pl.when(jnp.logical_or(i == 0, be_ref[i] != be_ref[jnp.maximum(i - 1, 0)]))
    def _():
        for cb in range(wgu_ref.shape[2] // SPLIT_COLS):
            for kc in range(d // LANES):
                ks = slice(kc * LANES, (kc + 1) * LANES)
                t_scr[kc] = wgu_ref[0, ks, cb * SPLIT_COLS:(cb + 1) * SPLIT_COLS].T
                rows = slice(cb * half, (cb + 1) * half)
                wg_scr[rows, ks] = t_scr[kc, pl.ds(0, half, stride=2), :].astype(BF16)
                wu_scr[rows, ks] = t_scr[kc, pl.ds(1, half, stride=2), :].astype(BF16)
        wd_scr[...] = wdn_ref[0].astype(BF16)

    @pl.when(i < nb_ref[0])
    def _():
        packed = x_ref[...]
        lo = lax.bitcast_convert_type(packed << 16, F32)
        hi = lax.bitcast_convert_type(packed & jnp.uint32(0xFFFF0000), F32)
        x = jnp.concatenate([lo, hi], axis=1).astype(BF16)
        gate = jnp.minimum(_dot_nt(x, wg_scr[...]) + bg_ref[0], SWIGLU_LIMIT)
        up = jnp.clip(_dot_nt(x, wu_scr[...]) + bu_ref[0], -SWIGLU_LIMIT, SWIGLU_LIMIT)
        act = (up + 1.0) * (gate * _sigmoid(SWIGLU_ALPHA * gate))
        o_ref[...] = _dot(act.astype(BF16), wd_scr[...]) + bd_ref[0]

    @pl.when(i >= nb_ref[0])
    def _():
        o_ref[...] = jnp.zeros_like(o_ref)


def _experts(block_e, n_used, xg, w_gu, b_g, b_u, w_d, b_d):
    p = xg.shape[0]
    ff, d = w_d.shape[1], w_d.shape[2]
    wspec = lambda shape: pl.BlockSpec(shape, lambda i, be, nb: (be[i], 0, 0))
    grid_spec = pltpu.PrefetchScalarGridSpec(
        num_scalar_prefetch=2,
        grid=(p // MOE_ROWS,),
        in_specs=[pl.BlockSpec((MOE_ROWS, d // 2), lambda i, be, nb: (i, 0)),
                  wspec((1, d, 2 * ff)), wspec((1, 1, ff)), wspec((1, 1, ff)),
                  wspec((1, ff, d)), wspec((1, 1, d))],
        out_specs=pl.BlockSpec((MOE_ROWS, d), lambda i, be, nb: (i, 0)),
        scratch_shapes=[pltpu.VMEM((ff, d), BF16), pltpu.VMEM((ff, d), BF16), pltpu.VMEM((ff, d), BF16),
                        pltpu.VMEM((d // LANES, SPLIT_COLS, LANES), F32)],
    )
    return pl.pallas_call(
        _experts_body,
        grid_spec=grid_spec,
        out_shape=jax.ShapeDtypeStruct((p, d), F32),
        compiler_params=_cparams(("arbitrary",)),
        name="moe_experts",
    )(block_e, n_used, xg, w_gu, b_g, b_u, w_d, b_d)


SC_WORKERS = 32
SC_BUFFER_BYTES = 128 * 1024


def _sc_gather_rows(table, idx):
    b = idx.shape[0]
    d = table.shape[1]
    per_w = b // SC_WORKERS
    SC_CHUNK = SC_BUFFER_BYTES // (4 * d)
    n = per_w // SC_CHUNK
    assert table.dtype.itemsize == 4 and SC_CHUNK <= 128
    assert b % (SC_WORKERS * SC_CHUNK) == 0 and n % 2 == 0
    mesh = plsc.VectorSubcoreMesh(core_axis_name="c", subcore_axis_name="s")

    @functools.partial(
        pl.kernel, mesh=mesh, out_type=jax.ShapeDtypeStruct((b, d), table.dtype),
        scratch_types=[pltpu.VMEM((n, SC_CHUNK), jnp.int32), pltpu.VMEM((2, SC_CHUNK, d), table.dtype),
                       pltpu.SemaphoreType.DMA((2,)), pltpu.SemaphoreType.DMA((2,))],
        name="sc_gather_rows")
    def gather(table_hbm, idx_hbm, out_hbm, idx_v, rows_v, gsem, wsem):
        wid = lax.axis_index("s") * 2 + lax.axis_index("c")
        base = wid * per_w
        pltpu.sync_copy(idx_hbm.at[wid], idx_v)

        def fetch(c, slot):
            return pltpu.make_async_copy(table_hbm.at[idx_v.at[c]], rows_v.at[slot], gsem.at[slot])

        def write(c, slot):
            off = pl.multiple_of(base + c * SC_CHUNK, SC_CHUNK)
            return pltpu.make_async_copy(rows_v.at[slot], out_hbm.at[pl.ds(off, SC_CHUNK)], wsem.at[slot])

        fetch(0, 0).start()

        @pl.loop(0, n, step=2)
        def _(c0):
            for slot in (0, 1):
                c = c0 + slot
                fetch(c, slot).wait()

                @pl.when(c >= 1)
                def _():
                    write(c - 1, 1 - slot).wait()

                @pl.when(c + 1 < n)
                def _():
                    fetch(c + 1, 1 - slot).start()

                write(c, slot).start()

        write(n - 1, 1).wait()

    return gather(table, idx.reshape(SC_WORKERS, n, SC_CHUNK))


SC_LANES = 16


def _sc_row_tokens(dest, n_rows, n_tokens, slots_per_token):
    s = dest.shape[0]
    per_w = n_rows // SC_WORKERS
    assert n_rows % (SC_WORKERS * SC_LANES) == 0 and s % SC_LANES == 0
    shift = slots_per_token.bit_length() - 1
    assert 1 << shift == slots_per_token
    pad_mask = (1 << (n_tokens.bit_length() - 1)) - 1
    mesh = plsc.VectorSubcoreMesh(core_axis_name="c", subcore_axis_name="s")
    params = pltpu.CompilerParams(needs_layout_passes=False)

    @functools.partial(
        pl.kernel, mesh=mesh, out_type=jax.ShapeDtypeStruct((n_rows,), jnp.int32),
        scratch_types=[pltpu.VMEM((s,), jnp.int32), pltpu.VMEM((per_w,), jnp.int32)],
        compiler_params=params, name="sc_row_tokens")
    def invert(dest_hbm, out_hbm, dest_v, out_v):
        wid = lax.axis_index("s") * 2 + lax.axis_index("c")
        base = wid * per_w
        pltpu.sync_copy(dest_hbm, dest_v)
        lanes = lax.iota(jnp.int32, SC_LANES)

        @pl.loop(0, per_w // SC_LANES)
        def _(i):
            off = pl.multiple_of(i * SC_LANES, SC_LANES)
            out_v[pl.ds(off, SC_LANES)] = (base + off + lanes) & pad_mask

        @pl.loop(0, s // SC_LANES)
        def _(i):
            off = pl.multiple_of(i * SC_LANES, SC_LANES)
            local = dest_v[pl.ds(off, SC_LANES)] - base
            mine = jnp.logical_and(local >= 0, local < per_w)
            tok = lax.shift_right_logical(off + lanes, shift)
            plsc.store_scatter(out_v, [jnp.where(mine, local, 0)], tok, mask=mine)

        pltpu.sync_copy(out_v, out_hbm.at[pl.ds(pl.multiple_of(base, SC_LANES), per_w)])

    return invert(dest)


def _route(top_e, rank, counts):
    t, k = top_e.shape
    s = t * k
    e_flat = top_e.reshape(-1)
    padded = (counts + MOE_ROWS - 1) // MOE_ROWS * MOE_ROWS
    pend = jnp.cumsum(padded)
    onehot = e_flat[:, None] == jnp.arange(N_EXPERTS, dtype=jnp.int32)[None, :]
    dest = jnp.sum(jnp.where(onehot, (pend - padded)[None, :], 0), axis=1) + rank.reshape(-1)
    n_blocks = -(-s // MOE_ROWS) + N_EXPERTS
    row_tok = _sc_row_tokens(dest, n_blocks * MOE_ROWS, t, k)
    block_row0 = jnp.arange(n_blocks, dtype=jnp.int32) * MOE_ROWS
    block_e = jnp.minimum(jnp.sum((pend[None, :] <= block_row0[:, None]).astype(jnp.int32), axis=1),
                          N_EXPERTS - 1)
    n_used = (pend[-1:] // MOE_ROWS).astype(jnp.int32)
    return dest.reshape(t, k), row_tok, block_e, n_used


def _final_body(x_ref, ex_ref, tw_ref, g2_ref, nf_ref, y_ref):
    bb, bl, d = x_ref.shape
    tw = tw_ref[...]
    moe = ex_ref[0] * tw[:, 0:1]
    for k in range(1, TOP_K):
        moe = moe + ex_ref[k] * tw[:, k:k + 1]
    x = x_ref[...] + g2_ref[...] * moe.reshape(bb, bl, d)
    ms = jnp.mean(x * x, axis=-1, keepdims=True)
    y_ref[...] = x * lax.rsqrt(ms + EPS) * nf_ref[...]


def _final(x2, expert_out, top_w, row0, mod, norm_f_w, rows=512):
    b, l, d = x2.shape
    bb, bl, rows = _row_blocks(b, l, rows)
    nl = l // bl
    assert row0 % rows == 0
    blk0 = row0 // rows
    return pl.pallas_call(
        _final_body,
        grid=(b // bb, nl),
        in_specs=[pl.BlockSpec((bb, bl, d), lambda i, j: (i, j, 0)),
                  pl.BlockSpec((TOP_K, rows, d), lambda i, j: (0, blk0 + i * nl + j, 0)),
                  pl.BlockSpec((rows, LANES), lambda i, j: (blk0 + i * nl + j, 0)),
                  pl.BlockSpec((bb, 1, d), lambda i, j: (i, 0, 5)),
                  pl.BlockSpec((1, d), lambda i, j: (0, 0))],
        out_specs=pl.BlockSpec((bb, bl, d), lambda i, j: (i, j, 0)),
        out_shape=jax.ShapeDtypeStruct((b, l, d), F32),
        compiler_params=_cparams(("arbitrary", "arbitrary")),
        name="final_norm",
    )(x2, expert_out, top_w, mod, norm_f_w.reshape(1, d))


def _prep_in_weights(w_in, b_fgate, dt_bias):
    aw = N_HEADS_A * HEAD_DIM_A
    o = np.cumsum([0, aw, aw, aw, N_HEADS_A, D_INNER, CONV_DIM, N_HEADS_S, D_MODEL, D_MODEL])
    w_t = w_in.T
    rows = lambda i: w_t[o[i]:o[i + 1]]
    w_small = jnp.concatenate(
        [rows(3), rows(6), jnp.zeros((LANES - N_HEADS_A - N_HEADS_S, D_MODEL), F32)], axis=0).T
    b_small = jnp.concatenate(
        [b_fgate, dt_bias, jnp.zeros((LANES - N_HEADS_A - N_HEADS_S,), F32)]).reshape(1, LANES)
    big = {"q": rows(0), "k": rows(1), "v": rows(2), "z": rows(4), "xbc": rows(5),
           "gate": w_t[o[7]:o[9]]}
    return {n: w.astype(BF16) for n, w in big.items()}, w_small, b_small


def kernel(x_prompt, x_sample, cache_k, cache_v, cache_logf, state_ssm, state_conv, page_table, c_prompt, c_sample, w_cond, b_cond, norm1_w, w_in, b_fgate, conv_w, conv_b, dt_bias, a_log, d_skip, ssm_norm_w, w_attn_out, w_ssm_out, w_o, norm2_w, w_router, b_router, w_gate_up, b_gate_up, w_down, b_down, norm_f_w):
    assert w_in.shape[0] == 1, "single-layer trunk"
    bp, lp, d = x_prompt.shape
    bs, ls, _ = x_sample.shape
    tp, ts = bp * lp, bs * ls
    aw = N_HEADS_A * HEAD_DIM_A
    c_all = jnp.concatenate([c_prompt, c_sample, jnp.zeros((-(bp + bs) % 8, d), F32)], axis=0)
    mod = _adaln_mod(c_all, w_cond[0], b_cond[0])
    mod_p = mod[:bp].reshape(bp, 1, 6 * d)
    mod_s = mod[bp:bp + bs].reshape(bs, 1, 6 * d)

    w_big, w_small, b_small = _prep_in_weights(w_in[0], b_fgate[0], dt_bias[0])
    pad_lanes = LANES - N_HEADS_A - N_HEADS_S
    alog_lane = jnp.concatenate([jnp.zeros((N_HEADS_A,), F32), a_log[0], jnp.zeros((pad_lanes,), F32)]).reshape(1, LANES)
    dskip_exp = jnp.repeat(d_skip[0], HEAD_DIM_S).reshape(1, D_INNER)
    ssm_nw = ssm_norm_w[0].reshape(1, D_INNER)
    conv_b2 = conv_b[0].reshape(1, CONV_DIM)
    wa, ws, wo = w_attn_out[0].astype(BF16), w_ssm_out[0].astype(BF16), w_o[0].astype(BF16)
    w_router_pad = jnp.pad(w_router[0], ((0, 0), (0, LANES - N_EXPERTS)))
    b_router_pad = jnp.pad(b_router[0], (0, LANES - N_EXPERTS)).reshape(1, LANES)

    def in_proj(x, m, half_copies):
        h, small = _norm_in(x, m, norm1_w[0], w_small, b_small)
        return {n: _matmul_nt(h, w, also_bf16=n in half_copies) for n, w in w_big.items()}, small

    proj_p, small_p = in_proj(x_prompt, mod_p, ("k", "v"))
    proj_s, small_s = in_proj(x_sample, mod_s, ())
    (k_p, kb_p), (v_p, vb_p) = proj_p["k"], proj_p["v"]

    fcum = _cumsum_t(small_p.reshape(bp, lp, LANES)).reshape(bp, N_HEADS_A // 2, 2, lp)
    o_p = _fox_prompt(proj_p["q"].reshape(bp, lp, aw), kb_p.reshape(bp, lp, aw),
                      vb_p.reshape(bp, lp, aw), fcum)
    xbc_p = proj_p["xbc"].reshape(bp, lp, CONV_DIM)
    y_p, st_p = _ssd_prompt(xbc_p, proj_p["z"].reshape(bp, lp, D_INNER), small_p.reshape(bp, lp, LANES),
                            conv_w[0], conv_b2, alog_lane, dskip_exp, ssm_nw)

    n_phys = cache_k.shape[1]
    k_t = jnp.transpose(cache_k[0], (0, 2, 3, 1)).reshape(n_phys, aw, PAGE_SIZE)
    v_t = jnp.transpose(cache_v[0], (0, 2, 3, 1)).reshape(n_phys, aw, PAGE_SIZE)
    lf_t = jnp.swapaxes(cache_logf[0], 1, 2).reshape(n_phys * N_HEADS_A, PAGE_SIZE)
    c_pages = _cumsum_lanes(lf_t, rows=512).reshape(n_phys, N_HEADS_A, PAGE_SIZE)
    n_pages = page_table.shape[1]
    o_s = _fox_sample(proj_s["q"].reshape(bs, ls, aw), proj_s["k"].reshape(bs, ls, aw),
                      proj_s["v"].reshape(bs, ls, aw), small_s.reshape(bs, ls, LANES), k_t, v_t,
                      c_pages, page_table, pages_per_step=min(16, n_pages))
    xbc_s = proj_s["xbc"].reshape(bs, ls, CONV_DIM)
    y_s, st_s = _ssd_sample(xbc_s, state_conv[0], proj_s["z"].reshape(bs, ls, D_INNER),
                            small_s.reshape(bs, ls, LANES), state_ssm[0].reshape(bs, D_INNER, D_STATE),
                            conv_w[0], conv_b2, alog_lane, dskip_exp, ssm_nw)

    post = functools.partial(_post_mixer, w_attn_out=wa, w_ssm_out=ws, w_o=wo, norm2_w=norm2_w[0],
                             w_router=w_router_pad, b_router=b_router_pad)
    bufs = [jnp.zeros((tp + ts, d // 2), jnp.uint32), jnp.zeros((tp + ts, LANES), jnp.int32),
            jnp.zeros((tp + ts, LANES), F32), jnp.zeros((tp + ts, LANES), jnp.int32)]
    x2_p, *bufs, counts_p = post(x_prompt, o_p, y_p, proj_p["gate"], mod_p,
                                 counts_in=jnp.zeros((1, LANES), F32), shared=bufs)
    x2_s, hf, top_e, top_w, rank, counts = post(x_sample, o_s, y_s, proj_s["gate"], mod_s,
                                                counts_in=counts_p, row0=tp, shared=bufs)

    dest, row_tok, block_e, n_used = _route(top_e[:, :TOP_K], rank[:, :TOP_K],
                                            counts[0, :N_EXPERTS].astype(jnp.int32))
    rows = _experts(block_e, n_used, _sc_gather_rows(hf, row_tok), w_gate_up[0],
                    b_gate_up[0][:, None, 0::2], b_gate_up[0][:, None, 1::2],
                    w_down[0], b_down[0][:, None, :])
    expert_out = _sc_gather_rows(rows, dest.T.reshape(-1)).reshape(TOP_K, tp + ts, d)

    y_prompt = _final(x2_p, expert_out, top_w, 0, mod_p, norm_f_w)
    y_sample = _final(x2_s, expert_out, top_w, tp, mod_s, norm_f_w)

    def state_rows(k, v, small, st, xbc, conv0, b, l):
        conv_new = jnp.concatenate([conv0, xbc], axis=1)[:, -(CONV_K - 1):]
        return (k.reshape(1, b, l, N_HEADS_A, HEAD_DIM_A), v.reshape(1, b, l, N_HEADS_A, HEAD_DIM_A),
                small[:, :N_HEADS_A].reshape(1, b, l, N_HEADS_A),
                st.reshape(1, b, N_HEADS_S, HEAD_DIM_S, D_STATE), conv_new[None])

    sp = state_rows(k_p, v_p, small_p, st_p, xbc_p,
                    jnp.zeros((bp, CONV_K - 1, CONV_DIM), F32), bp, lp)
    ss = state_rows(proj_s["k"], proj_s["v"], small_s, st_s, xbc_s, state_conv[0], bs, ls)
    return (y_prompt, y_sample) + sp + ss
```

```python
import functools

import numpy as np
import jax
import jax.numpy as jnp
from jax import lax
from jax.experimental import pallas as pl
from jax.experimental.pallas import tpu as pltpu
from jax.experimental.pallas import tpu_sc as plsc

F32 = jnp.float32
BF16 = jnp.bfloat16

D_MODEL = 1024
N_HEADS_A = 16
HEAD_DIM_A = 64
PAGE_SIZE = 128
D_INNER = 2048
HEAD_DIM_S = 64
N_HEADS_S = 32
N_GROUPS_S = 4
HEADS_PER_GROUP = N_HEADS_S // N_GROUPS_S
GROUP_W = D_INNER // N_GROUPS_S
D_STATE = 128
CONV_K = 4
CONV_DIM = D_INNER + 2 * N_GROUPS_S * D_STATE
SSD_CHUNK = 128
N_EXPERTS = 32
TOP_K = 4
D_FF = D_MODEL
SWIGLU_LIMIT = 7.0
SWIGLU_ALPHA = 1.702
EPS = 1e-6

LANES = 128
DT_LANE0 = N_HEADS_A
NEG_INF = float("-inf")
LOG2E = 1.4426950408889634
VMEM_LIMIT = 56 * 1024 * 1024


def _cparams(sem):
    return pltpu.CompilerParams(dimension_semantics=sem, vmem_limit_bytes=VMEM_LIMIT)


def _split3(x):
    hi = x.astype(BF16)
    r = x - hi.astype(F32)
    mid = r.astype(BF16)
    lo = (r - mid.astype(F32)).astype(BF16)
    return hi, mid, lo


def _dot(a, b):
    return jnp.dot(a, b, preferred_element_type=F32)


def _dot_nt(a, b):
    return lax.dot_general(a, b, (((1,), (1,)), ((), ())), preferred_element_type=F32)


def _dot_sel_left(sel_bf16, x):
    hi, mid, lo = _split3(x)
    return _dot(sel_bf16, hi) + _dot(sel_bf16, mid) + _dot(sel_bf16, lo)


def _dot3(a, b):
    ah = a.astype(BF16)
    am = (a - ah.astype(F32)).astype(BF16)
    bh = b.astype(BF16)
    bm = (b - bh.astype(F32)).astype(BF16)
    return _dot(ah, bh) + _dot(ah, bm) + _dot(am, bh)


def _tri_lower(n):
    r = lax.broadcasted_iota(jnp.int32, (n, n), 0)
    c = lax.broadcasted_iota(jnp.int32, (n, n), 1)
    return r >= c


def _sigmoid(x):
    return 1.0 / (1.0 + jnp.exp(-x))


def _silu(x):
    return x * _sigmoid(x)


def _mod_body(c_ref, w_ref, b_ref, o_ref):
    c = c_ref[...]
    o_ref[...] = _dot3(_silu(c), w_ref[...]) + b_ref[...]


def _adaln_mod(c_all, w_cond, b_cond):
    rows, d = c_all.shape
    n = w_cond.shape[1]
    tn = 1024
    return pl.pallas_call(
        _mod_body,
        grid=(n // tn,),
        in_specs=[pl.BlockSpec((rows, d), lambda j: (0, 0)),
                  pl.BlockSpec((d, tn), lambda j: (0, j)),
                  pl.BlockSpec((1, tn), lambda j: (0, j))],
        out_specs=pl.BlockSpec((rows, tn), lambda j: (0, j)),
        out_shape=jax.ShapeDtypeStruct((rows, n), F32),
        compiler_params=_cparams(("arbitrary",)),
        name="adaln_mod",
    )(c_all, w_cond, b_cond.reshape(1, n))


def _norm_in_body(x_ref, sh_ref, sc_ref, nw_ref, ws_ref, bs_ref, h_ref, sm_ref):
    x = x_ref[...]
    bb, bl, d = x.shape
    ms = jnp.mean(x * x, axis=-1, keepdims=True)
    y = x * lax.rsqrt(ms + EPS) * nw_ref[...]
    h = (y * (1.0 + sc_ref[...]) + sh_ref[...]).reshape(bb * bl, d)
    h_ref[...] = h.astype(BF16)
    sm = _dot3(h, ws_ref[...]) + bs_ref[...]
    lane = lax.broadcasted_iota(jnp.int32, sm.shape, 1)
    t = jnp.log(1.0 + jnp.exp(-jnp.abs(sm)))
    sm_ref[...] = jnp.where(lane < N_HEADS_A, jnp.minimum(sm, 0.0) - t, jnp.maximum(sm, 0.0) + t)


def _row_blocks(b, l, rows):
    rows = min(rows, b * l)
    if l >= rows:
        assert l % rows == 0
        return 1, rows, rows
    assert rows % l == 0 and b % (rows // l) == 0
    return rows // l, l, rows


def _norm_in(x, mod, norm_w, w_small, b_small, rows=512):
    b, l, d = x.shape
    bb, bl, rows = _row_blocks(b, l, rows)
    nl = l // bl
    grid = (b // bb, nl)
    return pl.pallas_call(
        _norm_in_body,
        grid=grid,
        in_specs=[pl.BlockSpec((bb, bl, d), lambda i, j: (i, j, 0)),
                  pl.BlockSpec((bb, 1, d), lambda i, j: (i, 0, 0)),
                  pl.BlockSpec((bb, 1, d), lambda i, j: (i, 0, 1)),
                  pl.BlockSpec((1, d), lambda i, j: (0, 0)),
                  pl.BlockSpec((d, LANES), lambda i, j: (0, 0)),
                  pl.BlockSpec((1, LANES), lambda i, j: (0, 0))],
        out_specs=[pl.BlockSpec((rows, d), lambda i, j: (i * nl + j, 0)),
                   pl.BlockSpec((rows, LANES), lambda i, j: (i * nl + j, 0))],
        out_shape=[jax.ShapeDtypeStruct((b * l, d), BF16),
                   jax.ShapeDtypeStruct((b * l, LANES), F32)],
        compiler_params=_cparams(("arbitrary", "arbitrary")),
        name="norm_in",
    )(x, mod, mod, norm_w.reshape(1, d), w_small, b_small)


def _mm_body(a_ref, w_ref, o_ref, *half_ref):
    r = _dot_nt(a_ref[...], w_ref[...])
    o_ref[...] = r
    for h in half_ref:
        h[...] = r.astype(BF16)


def _matmul_nt(a, w_t, also_bf16=False, tm=1024, tn=1024):
    m, k = a.shape
    n = w_t.shape[0]
    tn = min(tn, n)
    tm = min(tm, m)
    out_spec = pl.BlockSpec((tm, tn), lambda j, i: (i, j))
    n_out = 2 if also_bf16 else 1
    res = pl.pallas_call(
        _mm_body,
        grid=(n // tn, m // tm),
        in_specs=[pl.BlockSpec((tm, k), lambda j, i: (i, 0)),
                  pl.BlockSpec((tn, k), lambda j, i: (j, 0))],
        out_specs=[out_spec] * n_out,
        out_shape=[jax.ShapeDtypeStruct((m, n), F32), jax.ShapeDtypeStruct((m, n), BF16)][:n_out],
        compiler_params=_cparams(("arbitrary", "arbitrary")),
        name="proj_matmul",
    )(a, w_t)
    return res if also_bf16 else res[0]


def _block_cumsum_t(blk, carry_row):
    tri = _tri_lower(PAGE_SIZE).astype(BF16)
    cs = _dot_sel_left(tri, blk) + carry_row
    return cs.T[0:N_HEADS_A, :], cs[PAGE_SIZE - 1:PAGE_SIZE, :]


def _cumsum_t_body(x_ref, o_ref, carry_scr, *, blocks):
    @pl.when(pl.program_id(1) == 0)
    def _():
        carry_scr[...] = jnp.zeros_like(carry_scr)

    carry = carry_scr[...]
    for i in range(blocks):
        ps = slice(i * PAGE_SIZE, (i + 1) * PAGE_SIZE)
        o_ref[0, :, ps], carry = _block_cumsum_t(x_ref[0, ps, :], carry)
    carry_scr[...] = carry


def _cumsum_t(x, blocks=4):
    b, l, width = x.shape
    blocks = min(blocks, l // PAGE_SIZE)
    span = blocks * PAGE_SIZE
    assert width == LANES and l % span == 0
    return pl.pallas_call(
        functools.partial(_cumsum_t_body, blocks=blocks),
        grid=(b, l // span),
        in_specs=[pl.BlockSpec((1, span, width), lambda i, j: (i, j, 0))],
        out_specs=pl.BlockSpec((1, N_HEADS_A, span), lambda i, j: (i, 0, j)),
        out_shape=jax.ShapeDtypeStruct((b, N_HEADS_A, l), F32),
        scratch_shapes=[pltpu.VMEM((1, LANES), F32)],
        compiler_params=_cparams(("arbitrary", "arbitrary")),
        name="logf_cumsum",
    )(x)


def _cumsum_lanes_body(x_ref, o_ref):
    n = x_ref.shape[1]
    r = lax.broadcasted_iota(jnp.int32, (n, n), 0)
    c = lax.broadcasted_iota(jnp.int32, (n, n), 1)
    tri = (r <= c).astype(BF16)
    hi, mid, lo = _split3(x_ref[...])
    o_ref[...] = _dot(hi, tri) + _dot(mid, tri) + _dot(lo, tri)


def _cumsum_lanes(x, rows=1024):
    r, n = x.shape
    assert r % rows == 0
    return pl.pallas_call(
        _cumsum_lanes_body,
        grid=(r // rows,),
        in_specs=[pl.BlockSpec((rows, n), lambda i: (i, 0))],
        out_specs=pl.BlockSpec((rows, n), lambda i: (i, 0)),
        out_shape=jax.ShapeDtypeStruct((r, n), F32),
        compiler_params=_cparams(("arbitrary",)),
        name="page_logf_cumsum",
    )(x)


def _fox_prompt_body(it_ref, jt_ref, q_ref, k_ref, v_ref, f_ref, o_ref, qm_scr, m_scr, acc_scr, *,
                     tile, sub):
    t = pl.program_id(2)
    i = it_ref[t]
    j = jt_ref[t]
    lane = lax.broadcasted_iota(jnp.int32, (tile, LANES), 1)
    first = lane < HEAD_DIM_A

    @pl.when(j == 0)
    def _():
        q = q_ref[0] * (HEAD_DIM_A ** -0.5 * LOG2E)
        qm_scr[0] = jnp.where(first, q, 0.0).astype(BF16)
        qm_scr[1] = jnp.where(first, 0.0, q).astype(BF16)
        m_scr[...] = jnp.full_like(m_scr, -1e30)
        acc_scr[...] = jnp.zeros_like(acc_scr)

    def step(diagonal):
        one = jnp.ones((), BF16)
        sub_q, sub_k = sub[1] if diagonal else sub[0]
        first_s = lax.broadcasted_iota(jnp.int32, (sub_k, LANES), 1) < HEAD_DIM_A
        for kj in range(tile // sub_k):
            ks = slice(kj * sub_k, (kj + 1) * sub_k)
            kb = k_ref[0, ks, :]
            vb = v_ref[0, ks, :]
            v_augs = (jnp.where(first_s, vb, one), jnp.where(first_s, one, vb))
            if diagonal:
                q_blocks = [(kj * sub_k, tile)]
            else:
                q_blocks = [(qi * sub_q, (qi + 1) * sub_q) for qi in range(tile // sub_q)]
            for q0, q1 in q_blocks:
                qs = slice(q0, q1)
                for h in range(2):
                    s = _dot_nt(qm_scr[h, qs, :], kb) - f_ref[0, 0, h:h + 1, ks] * LOG2E
                    if diagonal:
                        row = lax.broadcasted_iota(jnp.int32, (q1 - q0, sub_k), 0) + q0
                        col = lax.broadcasted_iota(jnp.int32, (q1 - q0, sub_k), 1) + kj * sub_k
                        s = jnp.where(col <= row, s, NEG_INF)
                    m_old = m_scr[h, qs, :]
                    m_new = jnp.maximum(m_old, jnp.max(s, axis=-1, keepdims=True))
                    alpha = jnp.exp2(m_old - m_new)
                    p = jnp.exp2(s - jnp.concatenate([m_new] * (sub_k // LANES), axis=1))
                    acc_scr[h, qs, :] = alpha * acc_scr[h, qs, :] + _dot(p.astype(BF16), v_augs[h])
                    m_scr[h, qs, :] = m_new

    @pl.when(j < i)
    def _():
        step(False)

    @pl.when(j == i)
    def _():
        step(True)
        a0 = acc_scr[0]
        a1 = acc_scr[1]
        den = jnp.where(first, pltpu.roll(a0, HEAD_DIM_A, 1), pltpu.roll(a1, HEAD_DIM_A, 1))
        o_ref[0] = jnp.where(first, a0, a1) / den


def _fox_prompt(q, k, v, fcum, tile=1024, sub=((1024, 512), (512, 512))):
    b, l, w = q.shape
    npair = w // LANES
    tile = min(tile, l)
    sub = tuple((min(sq, tile), min(sk, tile)) for sq, sk in sub)
    assert l % tile == 0 and all(tile % sq == 0 and tile % sk == 0 for sq, sk in sub)
    nt = l // tile
    it = np.concatenate([np.full(i + 1, i, np.int32) for i in range(nt)])
    jt = np.concatenate([np.arange(i + 1, dtype=np.int32) for i in range(nt)])
    grid_spec = pltpu.PrefetchScalarGridSpec(
        num_scalar_prefetch=2,
        grid=(b, npair, len(it)),
        in_specs=[pl.BlockSpec((1, tile, LANES), lambda bi, p, t, it, jt: (bi, it[t], p)),
                  pl.BlockSpec((1, tile, LANES), lambda bi, p, t, it, jt: (bi, jt[t], p)),
                  pl.BlockSpec((1, tile, LANES), lambda bi, p, t, it, jt: (bi, jt[t], p)),
                  pl.BlockSpec((1, 1, 2, tile), lambda bi, p, t, it, jt: (bi, p, 0, jt[t]))],
        out_specs=pl.BlockSpec((1, tile, LANES), lambda bi, p, t, it, jt: (bi, it[t], p)),
        scratch_shapes=[pltpu.VMEM((2, tile, LANES), BF16), pltpu.VMEM((2, tile, LANES), F32),
                        pltpu.VMEM((2, tile, LANES), F32)],
    )
    return pl.pallas_call(
        functools.partial(_fox_prompt_body, tile=tile, sub=sub),
        grid_spec=grid_spec,
        out_shape=jax.ShapeDtypeStruct((b, l, w), F32),
        compiler_params=_cparams(("arbitrary", "arbitrary", "arbitrary")),
        name="fox_prompt",
    )(jnp.asarray(it), jnp.asarray(jt), q, k, v, fcum)


def _fox_sample_body(pt_ref, q_ref, kn_ref, vn_ref, lfn_ref, *rest, pages_per_step):
    pg = pages_per_step
    k_refs = rest[0:pg]
    v_refs = rest[pg:2 * pg]
    c_refs = rest[2 * pg:3 * pg]
    o_ref, qbd_scr, m_scr, l_scr, carry_scr, acc_scr = rest[3 * pg:]
    j = pl.program_id(1)
    nq = q_ref.shape[1]
    rows = N_HEADS_A * nq
    width = N_HEADS_A * HEAD_DIM_A

    @pl.when(j == 0)
    def _():
        q = q_ref[0] * (HEAD_DIM_A ** -0.5)
        qt = jnp.broadcast_to(q[None], (N_HEADS_A, nq, width)).reshape(rows, width)
        row = lax.broadcasted_iota(jnp.int32, (rows, width), 0)
        lane = lax.broadcasted_iota(jnp.int32, (rows, width), 1)
        own = (lane // HEAD_DIM_A) == (row // nq)
        qbd_scr[...] = jnp.where(own, qt, 0.0).astype(BF16)
        m_scr[...] = jnp.full_like(m_scr, -1e30)
        l_scr[...] = jnp.zeros_like(l_scr)
        carry_scr[...] = jnp.zeros_like(carry_scr)
        acc_scr[...] = jnp.zeros_like(acc_scr)

    def attend(blocks, valid, transposed):
        qbd = qbd_scr[...]
        carry = carry_scr[...]
        scores = []
        for kf, _, c16 in blocks:
            s = _dot(qbd, kf.astype(BF16)) if transposed else _dot_nt(qbd, kf.astype(BF16))
            nk = s.shape[1]
            cexp = jnp.broadcast_to(c16[:, None, :], (N_HEADS_A, nq, nk)).reshape(rows, nk)
            s = s - (cexp + carry)
            if valid is not None:
                s = jnp.where(valid, s, NEG_INF)
            scores.append(s)
            carry = carry + cexp[:, nk - 1:nk]
        carry_scr[...] = carry
        smax = scores[0]
        for s in scores[1:]:
            smax = jnp.maximum(smax, s)
        m_old = m_scr[...]
        m_new = jnp.maximum(m_old, jnp.max(smax, axis=-1, keepdims=True))
        alpha = jnp.exp(m_old - m_new)
        m_scr[...] = m_new
        psum = None
        pv = None
        for s, (_, vf, _) in zip(scores, blocks):
            p = jnp.exp(s - m_new)
            psum = p if psum is None else psum + p
            pb = p.astype(BF16)
            d = _dot_nt(pb, vf.astype(BF16)) if transposed else _dot(pb, vf.astype(BF16))
            pv = d if pv is None else pv + d
        l_scr[...] = alpha * l_scr[...] + jnp.sum(psum, axis=-1, keepdims=True)
        acc_scr[...] = acc_scr[...] * alpha + pv

    attend([(k_refs[g][0], v_refs[g][0], c_refs[g][0]) for g in range(pg)], None, True)

    @pl.when(j == pl.num_programs(1) - 1)
    def _():
        zpad = jnp.zeros((PAGE_SIZE - nq, width), F32)
        kpad = jnp.concatenate([kn_ref[0], zpad], axis=0)
        vpad = jnp.concatenate([vn_ref[0], zpad], axis=0)
        row = lax.broadcasted_iota(jnp.int32, (rows, PAGE_SIZE), 0)
        key = lax.broadcasted_iota(jnp.int32, (rows, PAGE_SIZE), 1)
        lf_pad = jnp.concatenate([lfn_ref[0], jnp.zeros((PAGE_SIZE - nq, LANES), F32)], axis=0)
        c_new, _ = _block_cumsum_t(lf_pad, jnp.zeros((1, LANES), F32))
        attend([(kpad, vpad, c_new)], key <= (row % nq), False)
        o = acc_scr[...] / l_scr[...]
        for h in range(N_HEADS_A):
            o_ref[0, :, h * HEAD_DIM_A:(h + 1) * HEAD_DIM_A] = (
                o[h * nq:(h + 1) * nq, h * HEAD_DIM_A:(h + 1) * HEAD_DIM_A])


def _fox_sample(q, k_new, v_new, lf_new, cache_k, cache_v, c_pages, page_table, pages_per_step=8):
    b, nq, w = q.shape
    n_pages = page_table.shape[1]
    pg = pages_per_step
    assert n_pages % pg == 0
    rows = N_HEADS_A * nq

    def page_map(g):
        return lambda bi, j, pt: (pt[bi * n_pages + j * pg + g], 0, 0)

    seq_map = lambda bi, j, pt: (bi, 0, 0)
    in_specs = [pl.BlockSpec((1, nq, w), seq_map), pl.BlockSpec((1, nq, w), seq_map),
                pl.BlockSpec((1, nq, w), seq_map), pl.BlockSpec((1, nq, LANES), seq_map)]
    in_specs += [pl.BlockSpec((1, w, PAGE_SIZE), page_map(g)) for g in range(pg)]
    in_specs += [pl.BlockSpec((1, w, PAGE_SIZE), page_map(g)) for g in range(pg)]
    in_specs += [pl.BlockSpec((1, N_HEADS_A, PAGE_SIZE), page_map(g)) for g in range(pg)]
    grid_spec = pltpu.PrefetchScalarGridSpec(
        num_scalar_prefetch=1,
        grid=(b, n_pages // pg),
        in_specs=in_specs,
        out_specs=pl.BlockSpec((1, nq, w), seq_map),
        scratch_shapes=[pltpu.VMEM((rows, w), BF16), pltpu.VMEM((rows, 1), F32), pltpu.VMEM((rows, 1), F32),
                        pltpu.VMEM((rows, 1), F32), pltpu.VMEM((rows, w), F32)],
    )
    args = [q, k_new, v_new, lf_new] + [cache_k] * pg + [cache_v] * pg + [c_pages] * pg
    return pl.pallas_call(
        functools.partial(_fox_sample_body, pages_per_step=pg),
        grid_spec=grid_spec,
        out_shape=jax.ShapeDtypeStruct((b, nq, w), F32),
        compiler_params=_cparams(("arbitrary", "arbitrary")),
        name="fox_sample",
    )(page_table.reshape(-1), *args)


def _causal_conv_silu(ext_scr, cw_ref, cb_ref, n):
    acc = cb_ref[...] + cw_ref[0:1, :] * ext_scr[8 - (CONV_K - 1):8 - (CONV_K - 1) + n, :]
    for i in range(1, CONV_K):
        acc = acc + cw_ref[i:i + 1, :] * ext_scr[8 - (CONV_K - 1) + i:8 - (CONV_K - 1) + i + n, :]
    return _silu(acc)


def _pair_select(first, a, b):
    return jnp.where(first, a, b)


def _ssd_prompt_body(xbc_ref, z_ref, sm_ref, cw_ref, cb_ref, alog_ref, dskip_ref, nw_ref,
                     y_ref, st_ref, st_scr, ext_scr):
    c = pl.program_id(1)
    q = SSD_CHUNK
    pair_w = 2 * HEAD_DIM_S

    @pl.when(c == 0)
    def _():
        st_scr[...] = jnp.zeros_like(st_scr)
        ext_scr[0:8, :] = jnp.zeros((8, CONV_DIM), F32)

    cur = xbc_ref[0]
    ext_scr[8:8 + q, :] = cur
    act = _causal_conv_silu(ext_scr, cw_ref, cb_ref, q)
    ext_scr[0:8, :] = cur[q - 8:q, :]

    sm = sm_ref[0]
    a_lane = -jnp.exp(alog_ref[...])
    tri_mask = _tri_lower(q)
    a_all = _dot_sel_left(tri_mask.astype(BF16), sm * a_lane)
    a_t = a_all.T
    dt_t = sm.T
    w_t = jnp.exp(a_t[:, q - 1:q] - a_t) * dt_t
    ea_all = jnp.exp(a_all)
    lane = lax.broadcasted_iota(jnp.int32, (1, pair_w), 1)
    first = lane < HEAD_DIM_S

    for g in range(N_GROUPS_S):
        bc = act[:, D_INNER + g * D_STATE:D_INNER + (g + 1) * D_STATE]
        cc = act[:, D_INNER + (N_GROUPS_S + g) * D_STATE:D_INNER + (N_GROUPS_S + g + 1) * D_STATE]
        cb = _dot_nt(cc.astype(BF16), bc.astype(BF16))
        bc_t = bc.T
        gated = []
        ssq = jnp.zeros((q, 1), F32)
        for pp in range(HEADS_PER_GROUP // 2):
            h0 = g * HEADS_PER_GROUP + 2 * pp
            col = h0 * HEAD_DIM_S
            x_pair = act[:, col:col + pair_w]
            s_pair = st_scr[:, col:col + pair_w]
            rhs = jnp.concatenate([x_pair.astype(BF16), s_pair.astype(BF16)], axis=0)
            ys, us, el = [], [], []
            for h in (h0, h0 + 1):
                li = DT_LANE0 + h
                seg = a_all[:, li:li + 1] - a_t[li:li + 1, :]
                dec = jnp.exp(jnp.where(tri_mask, seg, NEG_INF))
                m_h = cb * dec * dt_t[li:li + 1, :]
                lhs = jnp.concatenate([m_h, cc * ea_all[:, li:li + 1]], axis=1).astype(BF16)
                ys.append(_dot(lhs, rhs))
                us.append(_dot((bc_t * w_t[li:li + 1, :]).astype(BF16), x_pair.astype(BF16)))
                el.append(ea_all[q - 1:q, li:li + 1])
            y_pair = _pair_select(first, ys[0], ys[1]) + x_pair * dskip_ref[:, col:col + pair_w]
            st_scr[:, col:col + pair_w] = (s_pair * _pair_select(first, el[0], el[1])
                                           + _pair_select(first, us[0], us[1]))
            gp = y_pair * _silu(z_ref[0, :, col:col + pair_w])
            ssq = ssq + jnp.sum(gp * gp, axis=-1, keepdims=True)
            gated.append((col, gp))
        rs = lax.rsqrt(ssq / GROUP_W + EPS)
        for col, gp in gated:
            y_ref[0, :, col:col + pair_w] = (gp * rs * nw_ref[:, col:col + pair_w]).astype(y_ref.dtype)

    @pl.when(c == pl.num_programs(1) - 1)
    def _():
        for blk in range(D_INNER // pair_w):
            st_ref[0, blk * pair_w:(blk + 1) * pair_w, :] = st_scr[:, blk * pair_w:(blk + 1) * pair_w].T


def _ssd_prompt(xbc, z, small, conv_w, conv_b, alog_lane, dskip_exp, norm_w):
    b, l, _ = xbc.shape
    q = SSD_CHUNK
    full = lambda shape: pl.BlockSpec(shape, lambda bi, c: (0,) * len(shape))
    return pl.pallas_call(
        _ssd_prompt_body,
        grid=(b, l // q),
        in_specs=[pl.BlockSpec((1, q, CONV_DIM), lambda bi, c: (bi, c, 0)),
                  pl.BlockSpec((1, q, D_INNER), lambda bi, c: (bi, c, 0)),
                  pl.BlockSpec((1, q, LANES), lambda bi, c: (bi, c, 0)),
                  full((CONV_K, CONV_DIM)), full((1, CONV_DIM)), full((1, LANES)),
                  full((1, D_INNER)), full((1, D_INNER))],
        out_specs=[pl.BlockSpec((1, q, D_INNER), lambda bi, c: (bi, c, 0)),
                   pl.BlockSpec((1, D_INNER, D_STATE), lambda bi, c: (bi, 0, 0))],
        out_shape=[jax.ShapeDtypeStruct((b, l, D_INNER), BF16),
                   jax.ShapeDtypeStruct((b, D_INNER, D_STATE), F32)],
        scratch_shapes=[pltpu.VMEM((D_STATE, D_INNER), F32), pltpu.VMEM((8 + q, CONV_DIM), F32)],
        compiler_params=_cparams(("arbitrary", "arbitrary")),
        name="ssd_prompt",
    )(xbc, z, small, conv_w, conv_b, alog_lane, dskip_exp, norm_w)


def _ssd_sample_body(xbc_ref, c0_ref, z_ref, sm_ref, st_ref, cw_ref, cb_ref, alog_ref, dskip_ref, nw_ref,
                     y_ref, so_ref, ext_scr):
    l = xbc_ref.shape[1]
    pair_w = 2 * HEAD_DIM_S
    ext_scr[0:8, :] = jnp.zeros((8, CONV_DIM), F32)
    ext_scr[8 - (CONV_K - 1):8, :] = c0_ref[0]
    ext_scr[8:8 + l, :] = xbc_ref[0]
    act = _causal_conv_silu(ext_scr, cw_ref, cb_ref, l)

    sm = sm_ref[0]
    dta = sm * (-jnp.exp(alog_ref[...]))
    trow = lax.broadcasted_iota(jnp.int32, (l, LANES), 0)
    a_c = jnp.zeros((l, LANES), F32)
    for s in range(l):
        a_c = a_c + jnp.where(trow >= s, dta[s:s + 1, :], 0.0)
    lane = lax.broadcasted_iota(jnp.int32, (1, pair_w), 1)
    first = lane < HEAD_DIM_S

    def expand(v):
        cols = []
        for pp in range(N_HEADS_S // 2):
            li = DT_LANE0 + 2 * pp
            cols.append(_pair_select(first, v[:, li:li + 1], v[:, li + 1:li + 2]))
        return jnp.concatenate(cols, axis=1)

    a_x = expand(a_c)
    dt_x = expand(sm)
    ea_x = jnp.exp(a_x)
    w_x = jnp.exp(a_x[l - 1:l, :] - a_x) * dt_x
    xs = act[:, 0:D_INNER]
    zpad = jnp.zeros((PAGE_SIZE - l, D_STATE), F32)
    trow_x = lax.broadcasted_iota(jnp.int32, (l, GROUP_W), 0)

    xw_pad = jnp.concatenate([xs * w_x, jnp.zeros((PAGE_SIZE - l, D_INNER), F32)], axis=0)
    ea_last = jnp.exp(a_c[l - 1:l, :])
    rowsel = lax.broadcasted_iota(jnp.int32, (pair_w, 1), 0) < HEAD_DIM_S

    for g in range(N_GROUPS_S):
        gc = g * GROUP_W
        bc = act[:, D_INNER + g * D_STATE:D_INNER + (g + 1) * D_STATE]
        cc = act[:, D_INNER + (N_GROUPS_S + g) * D_STATE:D_INNER + (N_GROUPS_S + g + 1) * D_STATE]
        b_pad = jnp.concatenate([bc, zpad], axis=0).astype(BF16)
        cb = _dot_nt(cc.astype(BF16), b_pad)
        s_g = st_ref[0, gc:gc + GROUP_W, :]
        y = _dot_nt(cc.astype(BF16), s_g.astype(BF16)) * ea_x[:, gc:gc + GROUP_W]
        x_g = xs[:, gc:gc + GROUP_W]
        a_g = a_x[:, gc:gc + GROUP_W]
        dt_g = dt_x[:, gc:gc + GROUP_W]
        for s in range(l):
            dec = jnp.exp(jnp.where(trow_x >= s, a_g - a_g[s:s + 1, :], NEG_INF))
            y = y + cb[:, s:s + 1] * dec * (dt_g[s:s + 1, :] * x_g[s:s + 1, :])
        y = y + x_g * dskip_ref[:, gc:gc + GROUP_W]
        gp = y * _silu(z_ref[0, :, gc:gc + GROUP_W])
        rs = lax.rsqrt(jnp.sum(gp * gp, axis=-1, keepdims=True) / GROUP_W + EPS)
        y_ref[0, :, gc:gc + GROUP_W] = gp * rs * nw_ref[:, gc:gc + GROUP_W]
        for pp in range(HEADS_PER_GROUP // 2):
            h0 = g * HEADS_PER_GROUP + 2 * pp
            col = h0 * HEAD_DIM_S
            li = DT_LANE0 + h0
            u = _dot(xw_pad[:, col:col + pair_w].T.astype(BF16), b_pad)
            e_col = jnp.where(rowsel, ea_last[:, li:li + 1], ea_last[:, li + 1:li + 2])
            so_ref[0, col:col + pair_w, :] = st_ref[0, col:col + pair_w, :] * e_col + u


def _ssd_sample(xbc, conv0, z, small, state, conv_w, conv_b, alog_lane, dskip_exp, norm_w):
    b, l, _ = xbc.shape
    full = lambda shape: pl.BlockSpec(shape, lambda bi: (0,) * len(shape))
    seq = lambda shape: pl.BlockSpec(shape, lambda bi: (bi, 0, 0))
    return pl.pallas_call(
        _ssd_sample_body,
        grid=(b,),
        in_specs=[seq((1, l, CONV_DIM)), seq((1, CONV_K - 1, CONV_DIM)), seq((1, l, D_INNER)),
                  seq((1, l, LANES)), seq((1, D_INNER, D_STATE)),
                  full((CONV_K, CONV_DIM)), full((1, CONV_DIM)), full((1, LANES)),
                  full((1, D_INNER)), full((1, D_INNER))],
        out_specs=[seq((1, l, D_INNER)), seq((1, D_INNER, D_STATE))],
        out_shape=[jax.ShapeDtypeStruct((b, l, D_INNER), F32),
                   jax.ShapeDtypeStruct((b, D_INNER, D_STATE), F32)],
        scratch_shapes=[pltpu.VMEM((8 + l, CONV_DIM), F32)],
        compiler_params=_cparams(("arbitrary",)),
        name="ssd_sample",
    )(xbc, conv0, z, small, state, conv_w, conv_b, alog_lane, dskip_exp, norm_w)


def _post_mixer_body(x_ref, oa_ref, ys_ref, gate_ref, g1_ref, sh2_ref, sc2_ref, wa_ref, ws_ref, wo_ref,
                     n2_ref, wr_ref, br_ref, cin_ref, h_in, te_in, tw_in, rk_in,
                     x2_ref, h_ref, te_ref, tw_ref, rk_ref, cout_ref, count_scr):
    del h_in, te_in, tw_in, rk_in
    bb, bl, d = x_ref.shape
    rows = bb * bl

    @pl.when(jnp.logical_and(pl.program_id(0) == 0, pl.program_id(1) == 0))
    def _():
        count_scr[...] = cin_ref[...]

    oa = oa_ref[...].reshape(rows, d).astype(BF16)
    ys = ys_ref[...].reshape(rows, D_INNER).astype(BF16)
    y_attn = _dot(oa, wa_ref[...])
    y_ssm = _dot(ys, ws_ref[...])
    gate = gate_ref[...]
    merged = _sigmoid(gate[:, 0:d]) * y_attn + _sigmoid(gate[:, d:2 * d]) * y_ssm
    mix = _dot(merged.astype(BF16), wo_ref[...])
    x2 = x_ref[...] + g1_ref[...] * mix.reshape(bb, bl, d)
    x2_ref[...] = x2
    ms = jnp.mean(x2 * x2, axis=-1, keepdims=True)
    hf = (x2 * lax.rsqrt(ms + EPS) * n2_ref[...]) * (1.0 + sc2_ref[...]) + sh2_ref[...]
    hf = hf.reshape(rows, d)
    bits = lax.bitcast_convert_type(hf.astype(BF16).astype(F32), jnp.uint32)
    h_ref[...] = (bits[:, d // 2:] & jnp.uint32(0xFFFF0000)) | (bits[:, 0:d // 2] >> 16)
    lane = lax.broadcasted_iota(jnp.int32, (rows, LANES), 1)
    logits = jnp.where(lane < N_EXPERTS, _dot3(hf, wr_ref[...]) + br_ref[...], NEG_INF)
    vals, idxs = [], []
    for _ in range(TOP_K):
        mx = jnp.max(logits, axis=-1, keepdims=True)
        idx = jnp.min(jnp.where(logits == mx, lane, LANES), axis=-1, keepdims=True)
        vals.append(mx)
        idxs.append(idx)
        logits = jnp.where(lane == idx, NEG_INF, logits)
    ex = [jnp.exp(v - vals[0]) for v in vals]
    den = ex[0] + ex[1] + ex[2] + ex[3]
    chosen = [lane == idxs[k] for k in range(TOP_K)]
    cnt = chosen[0].astype(F32)
    for k in range(1, TOP_K):
        cnt = cnt + chosen[k].astype(F32)
    r_i = lax.broadcasted_iota(jnp.int32, (rows, rows), 0)
    c_i = lax.broadcasted_iota(jnp.int32, (rows, rows), 1)
    before = _dot((c_i < r_i).astype(BF16), cnt.astype(BF16)) + count_scr[...]
    te = jnp.zeros((rows, LANES), jnp.int32)
    tw = jnp.zeros((rows, LANES), F32)
    rk = jnp.zeros((rows, LANES), jnp.int32)
    for k in range(TOP_K):
        rank = jnp.sum(jnp.where(chosen[k], before, 0.0), axis=-1, keepdims=True)
        te = jnp.where(lane == k, idxs[k], te)
        tw = jnp.where(lane == k, ex[k] / den, tw)
        rk = jnp.where(lane == k, rank.astype(jnp.int32), rk)
    te_ref[...] = te
    tw_ref[...] = tw
    rk_ref[...] = rk
    count_scr[...] = count_scr[...] + jnp.sum(cnt, axis=0, keepdims=True)
    cout_ref[...] = count_scr[...]


def _post_mixer(x, o_attn, y_ssm, gate, mod, w_attn_out, w_ssm_out, w_o, norm2_w, w_router, b_router,
                counts_in, shared, row0=0, rows=512):
    b, l, d = x.shape
    bb, bl, rows = _row_blocks(b, l, rows)
    nl = l // bl
    assert row0 % rows == 0
    blk0 = row0 // rows
    tok = lambda w: pl.BlockSpec((bb, bl, w), lambda i, j: (i, j, 0))
    flat = lambda w: pl.BlockSpec((rows, w), lambda i, j: (i * nl + j, 0))
    buf = lambda w: pl.BlockSpec((rows, w), lambda i, j: (blk0 + i * nl + j, 0))
    modc = lambda c: pl.BlockSpec((bb, 1, d), lambda i, j: (i, 0, c))
    full = lambda shape: pl.BlockSpec(shape, lambda i, j: (0,) * len(shape))
    in_specs = [tok(d), tok(d), tok(D_INNER), flat(2 * d), modc(2), modc(3), modc(4),
                full((d, d)), full((D_INNER, d)), full((d, d)), full((1, d)),
                full((d, LANES)), full((1, LANES)), full((1, LANES))]
    args = [x, o_attn, y_ssm, gate, mod, mod, mod, w_attn_out, w_ssm_out, w_o, norm2_w.reshape(1, d),
            w_router, b_router, counts_in]
    aliases = {len(args) + i: 1 + i for i in range(len(shared))}
    in_specs += [pl.BlockSpec(memory_space=pl.ANY)] * len(shared)
    args += list(shared)
    total_rows = shared[0].shape[0]
    return pl.pallas_call(
        _post_mixer_body,
        grid=(b // bb, nl),
        in_specs=in_specs,
        out_specs=[tok(d), buf(d // 2), buf(LANES), buf(LANES), buf(LANES), full((1, LANES))],
        out_shape=[jax.ShapeDtypeStruct((b, l, d), F32),
                   jax.ShapeDtypeStruct((total_rows, d // 2), jnp.uint32),
                   jax.ShapeDtypeStruct((total_rows, LANES), jnp.int32),
                   jax.ShapeDtypeStruct((total_rows, LANES), F32),
                   jax.ShapeDtypeStruct((total_rows, LANES), jnp.int32),
                   jax.ShapeDtypeStruct((1, LANES), F32)],
        scratch_shapes=[pltpu.VMEM((1, LANES), F32)],
        input_output_aliases=aliases,
        compiler_params=_cparams(("arbitrary", "arbitrary")),
        name="post_mixer",
    )(*args)


MOE_ROWS = 512
SPLIT_COLS = 512


def _experts_body(be_ref, nb_ref, x_ref, wgu_ref, bg_ref, bu_ref, wdn_ref, bd_ref, o_ref,
                  wg_scr, wu_scr, wd_scr, t_scr):
    i = pl.program_id(0)
    d = wdn_ref.shape[2]
    half = SPLIT_COLS // 2

    @pl.when(jnp.logical_or(i == 0, be_ref[i] != be_ref[jnp.maximum(i - 1, 0)]))
    def _():
        for cb in range(wgu_ref.shape[2] // SPLIT_COLS):
            for kc in range(d // LANES):
                ks = slice(kc * LANES, (kc + 1) * LANES)
                t_scr[kc] = wgu_ref[0, ks, cb * SPLIT_COLS:(cb + 1) * SPLIT_COLS].T
                rows = slice(cb * half, (cb + 1) * half)
                wg_scr[rows, ks] = t_scr[kc, pl.ds(0, half, stride=2), :].astype(BF16)
                wu_scr[rows, ks] = t_scr[kc, pl.ds(1, half, stride=2), :].astype(BF16)
        wd_scr[...] = wdn_ref[0].astype(BF16)

    @pl.when(i < nb_ref[0])
    def _():
        packed = x_ref[...]
        lo = lax.bitcast_convert_type(packed << 16, F32)
        hi = lax.bitcast_convert_type(packed & jnp.uint32(0xFFFF0000), F32)
        x = jnp.concatenate([lo, hi], axis=1).astype(BF16)
        gate = jnp.minimum(_dot_nt(x, wg_scr[...]) + bg_ref[0], SWIGLU_LIMIT)
        up = jnp.clip(_dot_nt(x, wu_scr[...]) + bu_ref[0], -SWIGLU_LIMIT, SWIGLU_LIMIT)
        act = (up + 1.0) * (gate * _sigmoid(SWIGLU_ALPHA * gate))
        o_ref[...] = _dot(act.astype(BF16), wd_scr[...]) + bd_ref[0]

    @pl.when(i >= nb_ref[0])
    def _():
        o_ref[...] = jnp.zeros_like(o_ref)


def _experts(block_e, n_used, xg, w_gu, b_g, b_u, w_d, b_d):
    p = xg.shape[0]
    ff, d = w_d.shape[1], w_d.shape[2]
    wspec = lambda shape: pl.BlockSpec(shape, lambda i, be, nb: (be[i], 0, 0))
    grid_spec = pltpu.PrefetchScalarGridSpec(
        num_scalar_prefetch=2,
        grid=(p // MOE_ROWS,),
        in_specs=[pl.BlockSpec((MOE_ROWS, d // 2), lambda i, be, nb: (i, 0)),
                  wspec((1, d, 2 * ff)), wspec((1, 1, ff)), wspec((1, 1, ff)),
                  wspec((1, ff, d)), wspec((1, 1, d))],
        out_specs=pl.BlockSpec((MOE_ROWS, d), lambda i, be, nb: (i, 0)),
        scratch_shapes=[pltpu.VMEM((ff, d), BF16), pltpu.VMEM((ff, d), BF16), pltpu.VMEM((ff, d), BF16),
                        pltpu.VMEM((d // LANES, SPLIT_COLS, LANES), F32)],
    )
    return pl.pallas_call(
        _experts_body,
        grid_spec=grid_spec,
        out_shape=jax.ShapeDtypeStruct((p, d), F32),
        compiler_params=_cparams(("arbitrary",)),
        name="moe_experts",
    )(block_e, n_used, xg, w_gu, b_g, b_u, w_d, b_d)


SC_WORKERS = 32
SC_BUFFER_BYTES = 128 * 1024


def _sc_gather_rows(table, idx):
    b = idx.shape[0]
    d = table.shape[1]
    per_w = b // SC_WORKERS
    SC_CHUNK = SC_BUFFER_BYTES // (4 * d)
    n = per_w // SC_CHUNK
    assert table.dtype.itemsize == 4 and SC_CHUNK <= 128
    assert b % (SC_WORKERS * SC_CHUNK) == 0 and n % 2 == 0
    mesh = plsc.VectorSubcoreMesh(core_axis_name="c", subcore_axis_name="s")

    @functools.partial(
        pl.kernel, mesh=mesh, out_type=jax.ShapeDtypeStruct((b, d), table.dtype),
        scratch_types=[pltpu.VMEM((n, SC_CHUNK), jnp.int32), pltpu.VMEM((2, SC_CHUNK, d), table.dtype),
                       pltpu.SemaphoreType.DMA((2,)), pltpu.SemaphoreType.DMA((2,))],
        name="sc_gather_rows")
    def gather(table_hbm, idx_hbm, out_hbm, idx_v, rows_v, gsem, wsem):
        wid = lax.axis_index("s") * 2 + lax.axis_index("c")
        base = wid * per_w
        pltpu.sync_copy(idx_hbm.at[wid], idx_v)

        def fetch(c, slot):
            return pltpu.make_async_copy(table_hbm.at[idx_v.at[c]], rows_v.at[slot], gsem.at[slot])

        def write(c, slot):
            off = pl.multiple_of(base + c * SC_CHUNK, SC_CHUNK)
            return pltpu.make_async_copy(rows_v.at[slot], out_hbm.at[pl.ds(off, SC_CHUNK)], wsem.at[slot])

        fetch(0, 0).start()

        @pl.loop(0, n, step=2)
        def _(c0):
            for slot in (0, 1):
                c = c0 + slot
                fetch(c, slot).wait()

                @pl.when(c >= 1)
                def _():
                    write(c - 1, 1 - slot).wait()

                @pl.when(c + 1 < n)
                def _():
                    fetch(c + 1, 1 - slot).start()

                write(c, slot).start()

        write(n - 1, 1).wait()

    return gather(table, idx.reshape(SC_WORKERS, n, SC_CHUNK))


SC_LANES = 16


def _sc_row_tokens(dest, n_rows, n_tokens, slots_per_token):
    s = dest.shape[0]
    per_w = n_rows // SC_WORKERS
    assert n_rows % (SC_WORKERS * SC_LANES) == 0 and s % SC_LANES == 0
    shift = slots_per_token.bit_length() - 1
    assert 1 << shift == slots_per_token
    pad_mask = (1 << (n_tokens.bit_length() - 1)) - 1
    mesh = plsc.VectorSubcoreMesh(core_axis_name="c", subcore_axis_name="s")
    params = pltpu.CompilerParams(needs_layout_passes=False)

    @functools.partial(
        pl.kernel, mesh=mesh, out_type=jax.ShapeDtypeStruct((n_rows,), jnp.int32),
        scratch_types=[pltpu.VMEM((s,), jnp.int32), pltpu.VMEM((per_w,), jnp.int32)],
        compiler_params=params, name="sc_row_tokens")
    def invert(dest_hbm, out_hbm, dest_v, out_v):
        wid = lax.axis_index("s") * 2 + lax.axis_index("c")
        base = wid * per_w
        pltpu.sync_copy(dest_hbm, dest_v)
        lanes = lax.iota(jnp.int32, SC_LANES)

        @pl.loop(0, per_w // SC_LANES)
        def _(i):
            off = pl.multiple_of(i * SC_LANES, SC_LANES)
            out_v[pl.ds(off, SC_LANES)] = (base + off + lanes) & pad_mask

        @pl.loop(0, s // SC_LANES)
        def _(i):
            off = pl.multiple_of(i * SC_LANES, SC_LANES)
            local = dest_v[pl.ds(off, SC_LANES)] - base
            mine = jnp.logical_and(local >= 0, local < per_w)
            tok = lax.shift_right_logical(off + lanes, shift)
            plsc.store_scatter(out_v, [jnp.where(mine, local, 0)], tok, mask=mine)

        pltpu.sync_copy(out_v, out_hbm.at[pl.ds(pl.multiple_of(base, SC_LANES), per_w)])

    return invert(dest)


def _route(top_e, rank, counts):
    t, k = top_e.shape
    s = t * k
    e_flat = top_e.reshape(-1)
    padded = (counts + MOE_ROWS - 1) // MOE_ROWS * MOE_ROWS
    pend = jnp.cumsum(padded)
    onehot = e_flat[:, None] == jnp.arange(N_EXPERTS, dtype=jnp.int32)[None, :]
    dest = jnp.sum(jnp.where(onehot, (pend - padded)[None, :], 0), axis=1) + rank.reshape(-1)
    n_blocks = -(-s // MOE_ROWS) + N_EXPERTS
    row_tok = _sc_row_tokens(dest, n_blocks * MOE_ROWS, t, k)
    block_row0 = jnp.arange(n_blocks, dtype=jnp.int32) * MOE_ROWS
    block_e = jnp.minimum(jnp.sum((pend[None, :] <= block_row0[:, None]).astype(jnp.int32), axis=1),
                          N_EXPERTS - 1)
    n_used = (pend[-1:] // MOE_ROWS).astype(jnp.int32)
    return dest.reshape(t, k), row_tok, block_e, n_used


def _final_body(x_ref, ex_ref, tw_ref, g2_ref, nf_ref, y_ref):
    bb, bl, d = x_ref.shape
    tw = tw_ref[...]
    moe = ex_ref[0] * tw[:, 0:1]
    for k in range(1, TOP_K):
        moe = moe + ex_ref[k] * tw[:, k:k + 1]
    x = x_ref[...] + g2_ref[...] * moe.reshape(bb, bl, d)
    ms = jnp.mean(x * x, axis=-1, keepdims=True)
    y_ref[...] = x * lax.rsqrt(ms + EPS) * nf_ref[...]


def _final(x2, expert_out, top_w, row0, mod, norm_f_w, rows=512):
    b, l, d = x2.shape
    bb, bl, rows = _row_blocks(b, l, rows)
    nl = l // bl
    assert row0 % rows == 0
    blk0 = row0 // rows
    return pl.pallas_call(
        _final_body,
        grid=(b // bb, nl),
        in_specs=[pl.BlockSpec((bb, bl, d), lambda i, j: (i, j, 0)),
                  pl.BlockSpec((TOP_K, rows, d), lambda i, j: (0, blk0 + i * nl + j, 0)),
                  pl.BlockSpec((rows, LANES), lambda i, j: (blk0 + i * nl + j, 0)),
                  pl.BlockSpec((bb, 1, d), lambda i, j: (i, 0, 5)),
                  pl.BlockSpec((1, d), lambda i, j: (0, 0))],
        out_specs=pl.BlockSpec((bb, bl, d), lambda i, j: (i, j, 0)),
        out_shape=jax.ShapeDtypeStruct((b, l, d), F32),
        compiler_params=_cparams(("arbitrary", "arbitrary")),
        name="final_norm",
    )(x2, expert_out, top_w, mod, norm_f_w.reshape(1, d))


def _prep_in_weights(w_in, b_fgate, dt_bias):
    aw = N_HEADS_A * HEAD_DIM_A
    o = np.cumsum([0, aw, aw, aw, N_HEADS_A, D_INNER, CONV_DIM, N_HEADS_S, D_MODEL, D_MODEL])
    w_t = w_in.T
    rows = lambda i: w_t[o[i]:o[i + 1]]
    w_small = jnp.concatenate(
        [rows(3), rows(6), jnp.zeros((LANES - N_HEADS_A - N_HEADS_S, D_MODEL), F32)], axis=0).T
    b_small = jnp.concatenate(
        [b_fgate, dt_bias, jnp.zeros((LANES - N_HEADS_A - N_HEADS_S,), F32)]).reshape(1, LANES)
    big = {"q": rows(0), "k": rows(1), "v": rows(2), "z": rows(4), "xbc": rows(5),
           "gate": w_t[o[7]:o[9]]}
    return {n: w.astype(BF16) for n, w in big.items()}, w_small, b_small


def kernel(x_prompt, x_sample, cache_k, cache_v, cache_logf, state_ssm, state_conv, page_table, c_prompt, c_sample, w_cond, b_cond, norm1_w, w_in, b_fgate, conv_w, conv_b, dt_bias, a_log, d_skip, ssm_norm_w, w_attn_out, w_ssm_out, w_o, norm2_w, w_router, b_router, w_gate_up, b_gate_up, w_down, b_down, norm_f_w):
    assert w_in.shape[0] == 1, "single-layer trunk"
    bp, lp, d = x_prompt.shape
    bs, ls, _ = x_sample.shape
    tp, ts = bp * lp, bs * ls
    aw = N_HEADS_A * HEAD_DIM_A
    c_all = jnp.concatenate([c_prompt, c_sample, jnp.zeros((-(bp + bs) % 8, d), F32)], axis=0)
    mod = _adaln_mod(c_all, w_cond[0], b_cond[0])
    mod_p = mod[:bp].reshape(bp, 1, 6 * d)
    mod_s = mod[bp:bp + bs].reshape(bs, 1, 6 * d)

    w_big, w_small, b_small = _prep_in_weights(w_in[0], b_fgate[0], dt_bias[0])
    pad_lanes = LANES - N_HEADS_A - N_HEADS_S
    alog_lane = jnp.concatenate([jnp.zeros((N_HEADS_A,), F32), a_log[0], jnp.zeros((pad_lanes,), F32)]).reshape(1, LANES)
    dskip_exp = jnp.repeat(d_skip[0], HEAD_DIM_S).reshape(1, D_INNER)
    ssm_nw = ssm_norm_w[0].reshape(1, D_INNER)
    conv_b2 = conv_b[0].reshape(1, CONV_DIM)
    wa, ws, wo = w_attn_out[0].astype(BF16), w_ssm_out[0].astype(BF16), w_o[0].astype(BF16)
    w_router_pad = jnp.pad(w_router[0], ((0, 0), (0, LANES - N_EXPERTS)))
    b_router_pad = jnp.pad(b_router[0], (0, LANES - N_EXPERTS)).reshape(1, LANES)

    def in_proj(x, m, half_copies):
        h, small = _norm_in(x, m, norm1_w[0], w_small, b_small)
        return {n: _matmul_nt(h, w, also_bf16=n in half_copies) for n, w in w_big.items()}, small

    proj_p, small_p = in_proj(x_prompt, mod_p, ("k", "v"))
    proj_s, small_s = in_proj(x_sample, mod_s, ())
    (k_p, kb_p), (v_p, vb_p) = proj_p["k"], proj_p["v"]

    fcum = _cumsum_t(small_p.reshape(bp, lp, LANES)).reshape(bp, N_HEADS_A // 2, 2, lp)
    o_p = _fox_prompt(proj_p["q"].reshape(bp, lp, aw), kb_p.reshape(bp, lp, aw),
                      vb_p.reshape(bp, lp, aw), fcum)
    xbc_p = proj_p["xbc"].reshape(bp, lp, CONV_DIM)
    y_p, st_p = _ssd_prompt(xbc_p, proj_p["z"].reshape(bp, lp, D_INNER), small_p.reshape(bp, lp, LANES),
                            conv_w[0], conv_b2, alog_lane, dskip_exp, ssm_nw)

    n_phys = cache_k.shape[1]
    k_t = jnp.transpose(cache_k[0], (0, 2, 3, 1)).reshape(n_phys, aw, PAGE_SIZE)
    v_t = jnp.transpose(cache_v[0], (0, 2, 3, 1)).reshape(n_phys, aw, PAGE_SIZE)
    lf_t = jnp.swapaxes(cache_logf[0], 1, 2).reshape(n_phys * N_HEADS_A, PAGE_SIZE)
    c_pages = _cumsum_lanes(lf_t, rows=512).reshape(n_phys, N_HEADS_A, PAGE_SIZE)
    n_pages = page_table.shape[1]
    o_s = _fox_sample(proj_s["q"].reshape(bs, ls, aw), proj_s["k"].reshape(bs, ls, aw),
                      proj_s["v"].reshape(bs, ls, aw), small_s.reshape(bs, ls, LANES), k_t, v_t,
                      c_pages, page_table, pages_per_step=min(16, n_pages))
    xbc_s = proj_s["xbc"].reshape(bs, ls, CONV_DIM)
    y_s, st_s = _ssd_sample(xbc_s, state_conv[0], proj_s["z"].reshape(bs, ls, D_INNER),
                            small_s.reshape(bs, ls, LANES), state_ssm[0].reshape(bs, D_INNER, D_STATE),
                            conv_w[0], conv_b2, alog_lane, dskip_exp, ssm_nw)

    post = functools.partial(_post_mixer, w_attn_out=wa, w_ssm_out=ws, w_o=wo, norm2_w=norm2_w[0],
                             w_router=w_router_pad, b_router=b_router_pad)
    bufs = [jnp.zeros((tp + ts, d // 2), jnp.uint32), jnp.zeros((tp + ts, LANES), jnp.int32),
            jnp.zeros((tp + ts, LANES), F32), jnp.zeros((tp + ts, LANES), jnp.int32)]
    x2_p, *bufs, counts_p = post(x_prompt, o_p, y_p, proj_p["gate"], mod_p,
                                 counts_in=jnp.zeros((1, LANES), F32), shared=bufs)
    x2_s, hf, top_e, top_w, rank, counts = post(x_sample, o_s, y_s, proj_s["gate"], mod_s,
                                                counts_in=counts_p, row0=tp, shared=bufs)

    dest, row_tok, block_e, n_used = _route(top_e[:, :TOP_K], rank[:, :TOP_K],
                                            counts[0, :N_EXPERTS].astype(jnp.int32))
    rows = _experts(block_e, n_used, _sc_gather_rows(hf, row_tok), w_gate_up[0],
                    b_gate_up[0][:, None, 0::2], b_gate_up[0][:, None, 1::2],
                    w_down[0], b_down[0][:, None, :])
    expert_out = _sc_gather_rows(rows, dest.T.reshape(-1)).reshape(TOP_K, tp + ts, d)

    y_prompt = _final(x2_p, expert_out, top_w, 0, mod_p, norm_f_w)
    y_sample = _final(x2_s, expert_out, top_w, tp, mod_s, norm_f_w)

    def state_rows(k, v, small, st, xbc, conv0, b, l):
        conv_new = jnp.concatenate([conv0, xbc], axis=1)[:, -(CONV_K - 1):]
        return (k.reshape(1, b, l, N_HEADS_A, HEAD_DIM_A), v.reshape(1, b, l, N_HEADS_A, HEAD_DIM_A),
                small[:, :N_HEADS_A].reshape(1, b, l, N_HEADS_A),
                st.reshape(1, b, N_HEADS_S, HEAD_DIM_S, D_STATE), conv_new[None])

    sp = state_rows(k_p, v_p, small_p, st_p, xbc_p,
                    jnp.zeros((bp, CONV_K - 1, CONV_DIM), F32), bp, lp)
    ss = state_rows(proj_s["k"], proj_s["v"], small_s, st_s, xbc_s, state_conv[0], bs, ls)
    return (y_prompt, y_sample) + sp + ss
```

```python
import functools

import numpy as np
import jax
import jax.numpy as jnp
from jax import lax
from jax.experimental import pallas as pl
from jax.experimental.pallas import tpu as pltpu
from jax.experimental.pallas import tpu_sc as plsc

F32 = jnp.float32
BF16 = jnp.bfloat16

D_MODEL = 1024
N_HEADS_A = 16
HEAD_DIM_A = 64
PAGE_SIZE = 128
D_INNER = 2048
HEAD_DIM_S = 64
N_HEADS_S = 32
N_GROUPS_S = 4
HEADS_PER_GROUP = N_HEADS_S // N_GROUPS_S
GROUP_W = D_INNER // N_GROUPS_S
D_STATE = 128
CONV_K = 4
CONV_DIM = D_INNER + 2 * N_GROUPS_S * D_STATE
SSD_CHUNK = 128
N_EXPERTS = 32
TOP_K = 4
D_FF = D_MODEL
SWIGLU_LIMIT = 7.0
SWIGLU_ALPHA = 1.702
EPS = 1e-6

LANES = 128
DT_LANE0 = N_HEADS_A
NEG_INF = float("-inf")
LOG2E = 1.4426950408889634
VMEM_LIMIT = 56 * 1024 * 1024


def _cparams(sem):
    return pltpu.CompilerParams(dimension_semantics=sem, vmem_limit_bytes=VMEM_LIMIT)


def _split3(x):
    hi = x.astype(BF16)
    r = x - hi.astype(F32)
    mid = r.astype(BF16)
    lo = (r - mid.astype(F32)).astype(BF16)
    return hi, mid, lo


def _dot(a, b):
    return jnp.dot(a, b, preferred_element_type=F32)


def _dot_nt(a, b):
    return lax.dot_general(a, b, (((1,), (1,)), ((), ())), preferred_element_type=F32)


def _dot_sel_left(sel_bf16, x):
    hi, mid, lo = _split3(x)
    return _dot(sel_bf16, hi) + _dot(sel_bf16, mid) + _dot(sel_bf16, lo)


def _dot3(a, b):
    ah = a.astype(BF16)
    am = (a - ah.astype(F32)).astype(BF16)
    bh = b.astype(BF16)
    bm = (b - bh.astype(F32)).astype(BF16)
    return _dot(ah, bh) + _dot(ah, bm) + _dot(am, bh)


def _tri_lower(n):
    r = lax.broadcasted_iota(jnp.int32, (n, n), 0)
    c = lax.broadcasted_iota(jnp.int32, (n, n), 1)
    return r >= c


def _sigmoid(x):
    return 1.0 / (1.0 + jnp.exp(-x))


def _silu(x):
    return x * _sigmoid(x)


def _mod_body(c_ref, w_ref, b_ref, o_ref):
    c = c_ref[...]
    o_ref[...] = _dot3(_silu(c), w_ref[...]) + b_ref[...]


def _adaln_mod(c_all, w_cond, b_cond):
    rows, d = c_all.shape
    n = w_cond.shape[1]
    tn = 1024
    return pl.pallas_call(
        _mod_body,
        grid=(n // tn,),
        in_specs=[pl.BlockSpec((rows, d), lambda j: (0, 0)),
                  pl.BlockSpec((d, tn), lambda j: (0, j)),
                  pl.BlockSpec((1, tn), lambda j: (0, j))],
        out_specs=pl.BlockSpec((rows, tn), lambda j: (0, j)),
        out_shape=jax.ShapeDtypeStruct((rows, n), F32),
        compiler_params=_cparams(("arbitrary",)),
        name="adaln_mod",
    )(c_all, w_cond, b_cond.reshape(1, n))


def _norm_in_body(x_ref, sh_ref, sc_ref, nw_ref, ws_ref, bs_ref, h_ref, sm_ref):
    x = x_ref[...]
    bb, bl, d = x.shape
    ms = jnp.mean(x * x, axis=-1, keepdims=True)
    y = x * lax.rsqrt(ms + EPS) * nw_ref[...]
    h = (y * (1.0 + sc_ref[...]) + sh_ref[...]).reshape(bb * bl, d)
    h_ref[...] = h.astype(BF16)
    sm = _dot3(h, ws_ref[...]) + bs_ref[...]
    lane = lax.broadcasted_iota(jnp.int32, sm.shape, 1)
    t = jnp.log(1.0 + jnp.exp(-jnp.abs(sm)))
    sm_ref[...] = jnp.where(lane < N_HEADS_A, jnp.minimum(sm, 0.0) - t, jnp.maximum(sm, 0.0) + t)


def _row_blocks(b, l, rows):
    rows = min(rows, b * l)
    if l >= rows:
        assert l % rows == 0
        return 1, rows, rows
    assert rows % l == 0 and b % (rows // l) == 0
    return rows // l, l, rows


def _norm_in(x, mod, norm_w, w_small, b_small, rows=512):
    b, l, d = x.shape
    bb, bl, rows = _row_blocks(b, l, rows)
    nl = l // bl
    grid = (b // bb, nl)
    return pl.pallas_call(
        _norm_in_body,
        grid=grid,
        in_specs=[pl.BlockSpec((bb, bl, d), lambda i, j: (i, j, 0)),
                  pl.BlockSpec((bb, 1, d), lambda i, j: (i, 0, 0)),
                  pl.BlockSpec((bb, 1, d), lambda i, j: (i, 0, 1)),
                  pl.BlockSpec((1, d), lambda i, j: (0, 0)),
                  pl.BlockSpec((d, LANES), lambda i, j: (0, 0)),
                  pl.BlockSpec((1, LANES), lambda i, j: (0, 0))],
        out_specs=[pl.BlockSpec((rows, d), lambda i, j: (i * nl + j, 0)),
                   pl.BlockSpec((rows, LANES), lambda i, j: (i * nl + j, 0))],
        out_shape=[jax.ShapeDtypeStruct((b * l, d), BF16),
                   jax.ShapeDtypeStruct((b * l, LANES), F32)],
        compiler_params=_cparams(("arbitrary", "arbitrary")),
        name="norm_in",
    )(x, mod, mod, norm_w.reshape(1, d), w_small, b_small)


def _mm_body(a_ref, w_ref, o_ref, *half_ref):
    r = _dot_nt(a_ref[...], w_ref[...])
    o_ref[...] = r
    for h in half_ref:
        h[...] = r.astype(BF16)


def _matmul_nt(a, w_t, also_bf16=False, tm=1024, tn=1024):
    m, k = a.shape
    n = w_t.shape[0]
    tn = min(tn, n)
    tm = min(tm, m)
    out_spec = pl.BlockSpec((tm, tn), lambda j, i: (i, j))
    n_out = 2 if also_bf16 else 1
    res = pl.pallas_call(
        _mm_body,
        grid=(n // tn, m // tm),
        in_specs=[pl.BlockSpec((tm, k), lambda j, i: (i, 0)),
                  pl.BlockSpec((tn, k), lambda j, i: (j, 0))],
        out_specs=[out_spec] * n_out,
        out_shape=[jax.ShapeDtypeStruct((m, n), F32), jax.ShapeDtypeStruct((m, n), BF16)][:n_out],
        compiler_params=_cparams(("arbitrary", "arbitrary")),
        name="proj_matmul",
    )(a, w_t)
    return res if also_bf16 else res[0]


def _block_cumsum_t(blk, carry_row):
    tri = _tri_lower(PAGE_SIZE).astype(BF16)
    cs = _dot_sel_left(tri, blk) + carry_row
    return cs.T[0:N_HEADS_A, :], cs[PAGE_SIZE - 1:PAGE_SIZE, :]


def _cumsum_t_body(x_ref, o_ref, carry_scr, *, blocks):
    @pl.when(pl.program_id(1) == 0)
    def _():
        carry_scr[...] = jnp.zeros_like(carry_scr)

    carry = carry_scr[...]
    for i in range(blocks):
        ps = slice(i * PAGE_SIZE, (i + 1) * PAGE_SIZE)
        o_ref[0, :, ps], carry = _block_cumsum_t(x_ref[0, ps, :], carry)
    carry_scr[...] = carry


def _cumsum_t(x, blocks=4):
    b, l, width = x.shape
    blocks = min(blocks, l // PAGE_SIZE)
    span = blocks * PAGE_SIZE
    assert width == LANES and l % span == 0
    return pl.pallas_call(
        functools.partial(_cumsum_t_body, blocks=blocks),
        grid=(b, l // span),
        in_specs=[pl.BlockSpec((1, span, width), lambda i, j: (i, j, 0))],
        out_specs=pl.BlockSpec((1, N_HEADS_A, span), lambda i, j: (i, 0, j)),
        out_shape=jax.ShapeDtypeStruct((b, N_HEADS_A, l), F32),
        scratch_shapes=[pltpu.VMEM((1, LANES), F32)],
        compiler_params=_cparams(("arbitrary", "arbitrary")),
        name="logf_cumsum",
    )(x)


def _cumsum_lanes_body(x_ref, o_ref):
    n = x_ref.shape[1]
    r = lax.broadcasted_iota(jnp.int32, (n, n), 0)
    c = lax.broadcasted_iota(jnp.int32, (n, n), 1)
    tri = (r <= c).astype(BF16)
    hi, mid, lo = _split3(x_ref[...])
    o_ref[...] = _dot(hi, tri) + _dot(mid, tri) + _dot(lo, tri)


def _cumsum_lanes(x, rows=1024):
    r, n = x.shape
    assert r % rows == 0
    return pl.pallas_call(
        _cumsum_lanes_body,
        grid=(r // rows,),
        in_specs=[pl.BlockSpec((rows, n), lambda i: (i, 0))],
        out_specs=pl.BlockSpec((rows, n), lambda i: (i, 0)),
        out_shape=jax.ShapeDtypeStruct((r, n), F32),
        compiler_params=_cparams(("arbitrary",)),
        name="page_logf_cumsum",
    )(x)


def _fox_prompt_body(it_ref, jt_ref, q_ref, k_ref, v_ref, f_ref, o_ref, qm_scr, m_scr, acc_scr, *,
                     tile, sub):
    t = pl.program_id(2)
    i = it_ref[t]
    j = jt_ref[t]
    lane = lax.broadcasted_iota(jnp.int32, (tile, LANES), 1)
    first = lane < HEAD_DIM_A

    @pl.when(j == 0)
    def _():
        q = q_ref[0] * (HEAD_DIM_A ** -0.5 * LOG2E)
        qm_scr[0] = jnp.where(first, q, 0.0).astype(BF16)
        qm_scr[1] = jnp.where(first, 0.0, q).astype(BF16)
        m_scr[...] = jnp.full_like(m_scr, -1e30)
        acc_scr[...] = jnp.zeros_like(acc_scr)

    def step(diagonal):
        one = jnp.ones((), BF16)
        sub_q, sub_k = sub[1] if diagonal else sub[0]
        first_s = lax.broadcasted_iota(jnp.int32, (sub_k, LANES), 1) < HEAD_DIM_A
        for kj in range(tile // sub_k):
            ks = slice(kj * sub_k, (kj + 1) * sub_k)
            kb = k_ref[0, ks, :]
            vb = v_ref[0, ks, :]
            v_augs = (jnp.where(first_s, vb, one), jnp.where(first_s, one, vb))
            if diagonal:
                q_blocks = [(kj * sub_k, tile)]
            else:
                q_blocks = [(qi * sub_q, (qi + 1) * sub_q) for qi in range(tile // sub_q)]
            for q0, q1 in q_blocks:
                qs = slice(q0, q1)
                for h in range(2):
                    s = _dot_nt(qm_scr[h, qs, :], kb) - f_ref[0, 0, h:h + 1, ks] * LOG2E
                    if diagonal:
                        row = lax.broadcasted_iota(jnp.int32, (q1 - q0, sub_k), 0) + q0
                        col = lax.broadcasted_iota(jnp.int32, (q1 - q0, sub_k), 1) + kj * sub_k
                        s = jnp.where(col <= row, s, NEG_INF)
                    m_old = m_scr[h, qs, :]
                    m_new = jnp.maximum(m_old, jnp.max(s, axis=-1, keepdims=True))
                    alpha = jnp.exp2(m_old - m_new)
                    p = jnp.exp2(s - jnp.concatenate([m_new] * (sub_k // LANES), axis=1))
                    acc_scr[h, qs, :] = alpha * acc_scr[h, qs, :] + _dot(p.astype(BF16), v_augs[h])
                    m_scr[h, qs, :] = m_new

    @pl.when(j < i)
    def _():
        step(False)

    @pl.when(j == i)
    def _():
        step(True)
        a0 = acc_scr[0]
        a1 = acc_scr[1]
        den = jnp.where(first, pltpu.roll(a0, HEAD_DIM_A, 1), pltpu.roll(a1, HEAD_DIM_A, 1))
        o_ref[0] = jnp.where(first, a0, a1) / den


def _fox_prompt(q, k, v, fcum, tile=2048, sub=((2048, 512), (512, 512))):
    b, l, w = q.shape
    npair = w // LANES
    tile = min(tile, l)
    sub = tuple((min(sq, tile), min(sk, tile)) for sq, sk in sub)
    assert l % tile == 0 and all(tile % sq == 0 and tile % sk == 0 for sq, sk in sub)
    nt = l // tile
    it = np.concatenate([np.full(i + 1, i, np.int32) for i in range(nt)])
    jt = np.concatenate([np.arange(i + 1, dtype=np.int32) for i in range(nt)])
    grid_spec = pltpu.PrefetchScalarGridSpec(
        num_scalar_prefetch=2,
        grid=(b, npair, len(it)),
        in_specs=[pl.BlockSpec((1, tile, LANES), lambda bi, p, t, it, jt: (bi, it[t], p)),
                  pl.BlockSpec((1, tile, LANES), lambda bi, p, t, it, jt: (bi, jt[t], p)),
                  pl.BlockSpec((1, tile, LANES), lambda bi, p, t, it, jt: (bi, jt[t], p)),
                  pl.BlockSpec((1, 1, 2, tile), lambda bi, p, t, it, jt: (bi, p, 0, jt[t]))],
        out_specs=pl.BlockSpec((1, tile, LANES), lambda bi, p, t, it, jt: (bi, it[t], p)),
        scratch_shapes=[pltpu.VMEM((2, tile, LANES), BF16), pltpu.VMEM((2, tile, LANES), F32),
                        pltpu.VMEM((2, tile, LANES), F32)],
    )
    return pl.pallas_call(
        functools.partial(_fox_prompt_body, tile=tile, sub=sub),
        grid_spec=grid_spec,
        out_shape=jax.ShapeDtypeStruct((b, l, w), F32),
        compiler_params=_cparams(("arbitrary", "arbitrary", "arbitrary")),
        name="fox_prompt",
    )(jnp.asarray(it), jnp.asarray(jt), q, k, v, fcum)


def _fox_sample_body(pt_ref, q_ref, kn_ref, vn_ref, lfn_ref, *rest, pages_per_step):
    pg = pages_per_step
    k_refs = rest[0:pg]
    v_refs = rest[pg:2 * pg]
    c_refs = rest[2 * pg:3 * pg]
    o_ref, qbd_scr, m_scr, l_scr, carry_scr, acc_scr = rest[3 * pg:]
    j = pl.program_id(1)
    nq = q_ref.shape[1]
    rows = N_HEADS_A * nq
    width = N_HEADS_A * HEAD_DIM_A

    @pl.when(j == 0)
    def _():
        q = q_ref[0] * (HEAD_DIM_A ** -0.5)
        qt = jnp.broadcast_to(q[None], (N_HEADS_A, nq, width)).reshape(rows, width)
        row = lax.broadcasted_iota(jnp.int32, (rows, width), 0)
        lane = lax.broadcasted_iota(jnp.int32, (rows, width), 1)
        own = (lane // HEAD_DIM_A) == (row // nq)
        qbd_scr[...] = jnp.where(own, qt, 0.0).astype(BF16)
        m_scr[...] = jnp.full_like(m_scr, -1e30)
        l_scr[...] = jnp.zeros_like(l_scr)
        carry_scr[...] = jnp.zeros_like(carry_scr)
        acc_scr[...] = jnp.zeros_like(acc_scr)

    def attend(blocks, valid, transposed):
        qbd = qbd_scr[...]
        carry = carry_scr[...]
        scores = []
        for kf, _, c16 in blocks:
            s = _dot(qbd, kf.astype(BF16)) if transposed else _dot_nt(qbd, kf.astype(BF16))
            nk = s.shape[1]
            cexp = jnp.broadcast_to(c16[:, None, :], (N_HEADS_A, nq, nk)).reshape(rows, nk)
            s = s - (cexp + carry)
            if valid is not None:
                s = jnp.where(valid, s, NEG_INF)
            scores.append(s)
            carry = carry + cexp[:, nk - 1:nk]
        carry_scr[...] = carry
        smax = scores[0]
        for s in scores[1:]:
            smax = jnp.maximum(smax, s)
        m_old = m_scr[...]
        m_new = jnp.maximum(m_old, jnp.max(smax, axis=-1, keepdims=True))
        alpha = jnp.exp(m_old - m_new)
        m_scr[...] = m_new
        psum = None
        pv = None
        for s, (_, vf, _) in zip(scores, blocks):
            p = jnp.exp(s - m_new)
            psum = p if psum is None else psum + p
            pb = p.astype(BF16)
            d = _dot_nt(pb, vf.astype(BF16)) if transposed else _dot(pb, vf.astype(BF16))
            pv = d if pv is None else pv + d
        l_scr[...] = alpha * l_scr[...] + jnp.sum(psum, axis=-1, keepdims=True)
        acc_scr[...] = acc_scr[...] * alpha + pv

    attend([(k_refs[g][0], v_refs[g][0], c_refs[g][0]) for g in range(pg)], None, True)

    @pl.when(j == pl.num_programs(1) - 1)
    def _():
        zpad = jnp.zeros((PAGE_SIZE - nq, width), F32)
        kpad = jnp.concatenate([kn_ref[0], zpad], axis=0)
        vpad = jnp.concatenate([vn_ref[0], zpad], axis=0)
        row = lax.broadcasted_iota(jnp.int32, (rows, PAGE_SIZE), 0)
        key = lax.broadcasted_iota(jnp.int32, (rows, PAGE_SIZE), 1)
        lf_pad = jnp.concatenate([lfn_ref[0], jnp.zeros((PAGE_SIZE - nq, LANES), F32)], axis=0)
        c_new, _ = _block_cumsum_t(lf_pad, jnp.zeros((1, LANES), F32))
        attend([(kpad, vpad, c_new)], key <= (row % nq), False)
        o = acc_scr[...] / l_scr[...]
        for h in range(N_HEADS_A):
            o_ref[0, :, h * HEAD_DIM_A:(h + 1) * HEAD_DIM_A] = (
                o[h * nq:(h + 1) * nq, h * HEAD_DIM_A:(h + 1) * HEAD_DIM_A])


def _fox_sample(q, k_new, v_new, lf_new, cache_k, cache_v, c_pages, page_table, pages_per_step=8):
    b, nq, w = q.shape
    n_pages = page_table.shape[1]
    pg = pages_per_step
    assert n_pages % pg == 0
    rows = N_HEADS_A * nq

    def page_map(g):
        return lambda bi, j, pt: (pt[bi * n_pages + j * pg + g], 0, 0)

    seq_map = lambda bi, j, pt: (bi, 0, 0)
    in_specs = [pl.BlockSpec((1, nq, w), seq_map), pl.BlockSpec((1, nq, w), seq_map),
                pl.BlockSpec((1, nq, w), seq_map), pl.BlockSpec((1, nq, LANES), seq_map)]
    in_specs += [pl.BlockSpec((1, w, PAGE_SIZE), page_map(g)) for g in range(pg)]
    in_specs += [pl.BlockSpec((1, w, PAGE_SIZE), page_map(g)) for g in range(pg)]
    in_specs += [pl.BlockSpec((1, N_HEADS_A, PAGE_SIZE), page_map(g)) for g in range(pg)]
    grid_spec = pltpu.PrefetchScalarGridSpec(
        num_scalar_prefetch=1,
        grid=(b, n_pages // pg),
        in_specs=in_specs,
        out_specs=pl.BlockSpec((1, nq, w), seq_map),
        scratch_shapes=[pltpu.VMEM((rows, w), BF16), pltpu.VMEM((rows, 1), F32), pltpu.VMEM((rows, 1), F32),
                        pltpu.VMEM((rows, 1), F32), pltpu.VMEM((rows, w), F32)],
    )
    args = [q, k_new, v_new, lf_new] + [cache_k] * pg + [cache_v] * pg + [c_pages] * pg
    return pl.pallas_call(
        functools.partial(_fox_sample_body, pages_per_step=pg),
        grid_spec=grid_spec,
        out_shape=jax.ShapeDtypeStruct((b, nq, w), F32),
        compiler_params=_cparams(("arbitrary", "arbitrary")),
        name="fox_sample",
    )(page_table.reshape(-1), *args)


def _causal_conv_silu(ext_scr, cw_ref, cb_ref, n):
    acc = cb_ref[...] + cw_ref[0:1, :] * ext_scr[8 - (CONV_K - 1):8 - (CONV_K - 1) + n, :]
    for i in range(1, CONV_K):
        acc = acc + cw_ref[i:i + 1, :] * ext_scr[8 - (CONV_K - 1) + i:8 - (CONV_K - 1) + i + n, :]
    return _silu(acc)


def _pair_select(first, a, b):
    return jnp.where(first, a, b)


def _ssd_prompt_body(xbc_ref, z_ref, sm_ref, cw_ref, cb_ref, alog_ref, dskip_ref, nw_ref,
                     y_ref, st_ref, st_scr, ext_scr):
    c = pl.program_id(1)
    q = SSD_CHUNK
    pair_w = 2 * HEAD_DIM_S

    @pl.when(c == 0)
    def _():
        st_scr[...] = jnp.zeros_like(st_scr)
        ext_scr[0:8, :] = jnp.zeros((8, CONV_DIM), F32)

    cur = xbc_ref[0]
    ext_scr[8:8 + q, :] = cur
    act = _causal_conv_silu(ext_scr, cw_ref, cb_ref, q)
    ext_scr[0:8, :] = cur[q - 8:q, :]

    sm = sm_ref[0]
    a_lane = -jnp.exp(alog_ref[...])
    tri_mask = _tri_lower(q)
    a_all = _dot_sel_left(tri_mask.astype(BF16), sm * a_lane)
    a_t = a_all.T
    dt_t = sm.T
    w_t = jnp.exp(a_t[:, q - 1:q] - a_t) * dt_t
    ea_all = jnp.exp(a_all)
    lane = lax.broadcasted_iota(jnp.int32, (1, pair_w), 1)
    first = lane < HEAD_DIM_S

    for g in range(N_GROUPS_S):
        bc = act[:, D_INNER + g * D_STATE:D_INNER + (g + 1) * D_STATE]
        cc = act[:, D_INNER + (N_GROUPS_S + g) * D_STATE:D_INNER + (N_GROUPS_S + g + 1) * D_STATE]
        cb = _dot_nt(cc.astype(BF16), bc.astype(BF16))
        bc_t = bc.T
        gated = []
        ssq = jnp.zeros((q, 1), F32)
        for pp in range(HEADS_PER_GROUP // 2):
            h0 = g * HEADS_PER_GROUP + 2 * pp
            col = h0 * HEAD_DIM_S
            x_pair = act[:, col:col + pair_w]
            s_pair = st_scr[:, col:col + pair_w]
            rhs = jnp.concatenate([x_pair.astype(BF16), s_pair.astype(BF16)], axis=0)
            ys, us, el = [], [], []
            for h in (h0, h0 + 1):
                li = DT_LANE0 + h
                seg = a_all[:, li:li + 1] - a_t[li:li + 1, :]
                dec = jnp.exp(jnp.where(tri_mask, seg, NEG_INF))
                m_h = cb * dec * dt_t[li:li + 1, :]
                lhs = jnp.concatenate([m_h, cc * ea_all[:, li:li + 1]], axis=1).astype(BF16)
                ys.append(_dot(lhs, rhs))
                us.append(_dot((bc_t * w_t[li:li + 1, :]).astype(BF16), x_pair.astype(BF16)))
                el.append(ea_all[q - 1:q, li:li + 1])
            y_pair = _pair_select(first, ys[0], ys[1]) + x_pair * dskip_ref[:, col:col + pair_w]
            st_scr[:, col:col + pair_w] = (s_pair * _pair_select(first, el[0], el[1])
                                           + _pair_select(first, us[0], us[1]))
            gp = y_pair * _silu(z_ref[0, :, col:col + pair_w])
            ssq = ssq + jnp.sum(gp * gp, axis=-1, keepdims=True)
            gated.append((col, gp))
        rs = lax.rsqrt(ssq / GROUP_W + EPS)
        for col, gp in gated:
            y_ref[0, :, col:col + pair_w] = (gp * rs * nw_ref[:, col:col + pair_w]).astype(y_ref.dtype)

    @pl.when(c == pl.num_programs(1) - 1)
    def _():
        for blk in range(D_INNER // pair_w):
            st_ref[0, blk * pair_w:(blk + 1) * pair_w, :] = st_scr[:, blk * pair_w:(blk + 1) * pair_w].T


def _ssd_prompt(xbc, z, small, conv_w, conv_b, alog_lane, dskip_exp, norm_w):
    b, l, _ = xbc.shape
    q = SSD_CHUNK
    full = lambda shape: pl.BlockSpec(shape, lambda bi, c: (0,) * len(shape))
    return pl.pallas_call(
        _ssd_prompt_body,
        grid=(b, l // q),
        in_specs=[pl.BlockSpec((1, q, CONV_DIM), lambda bi, c: (bi, c, 0)),
                  pl.BlockSpec((1, q, D_INNER), lambda bi, c: (bi, c, 0)),
                  pl.BlockSpec((1, q, LANES), lambda bi, c: (bi, c, 0)),
                  full((CONV_K, CONV_DIM)), full((1, CONV_DIM)), full((1, LANES)),
                  full((1, D_INNER)), full((1, D_INNER))],
        out_specs=[pl.BlockSpec((1, q, D_INNER), lambda bi, c: (bi, c, 0)),
                   pl.BlockSpec((1, D_INNER, D_STATE), lambda bi, c: (bi, 0, 0))],
        out_shape=[jax.ShapeDtypeStruct((b, l, D_INNER), BF16),
                   jax.ShapeDtypeStruct((b, D_INNER, D_STATE), F32)],
        scratch_shapes=[pltpu.VMEM((D_STATE, D_INNER), F32), pltpu.VMEM((8 + q, CONV_DIM), F32)],
        compiler_params=_cparams(("arbitrary", "arbitrary")),
        name="ssd_prompt",
    )(xbc, z, small, conv_w, conv_b, alog_lane, dskip_exp, norm_w)


def _ssd_sample_body(xbc_ref, c0_ref, z_ref, sm_ref, st_ref, cw_ref, cb_ref, alog_ref, dskip_ref, nw_ref,
                     y_ref, so_ref, ext_scr):
    l = xbc_ref.shape[1]
    pair_w = 2 * HEAD_DIM_S
    ext_scr[0:8, :] = jnp.zeros((8, CONV_DIM), F32)
    ext_scr[8 - (CONV_K - 1):8, :] = c0_ref[0]
    ext_scr[8:8 + l, :] = xbc_ref[0]
    act = _causal_conv_silu(ext_scr, cw_ref, cb_ref, l)

    sm = sm_ref[0]
    dta = sm * (-jnp.exp(alog_ref[...]))
    trow = lax.broadcasted_iota(jnp.int32, (l, LANES), 0)
    a_c = jnp.zeros((l, LANES), F32)
    for s in range(l):
        a_c = a_c + jnp.where(trow >= s, dta[s:s + 1, :], 0.0)
    lane = lax.broadcasted_iota(jnp.int32, (1, pair_w), 1)
    first = lane < HEAD_DIM_S

    def expand(v):
        cols = []
        for pp in range(N_HEADS_S // 2):
            li = DT_LANE0 + 2 * pp
            cols.append(_pair_select(first, v[:, li:li + 1], v[:, li + 1:li + 2]))
        return jnp.concatenate(cols, axis=1)

    a_x = expand(a_c)
    dt_x = expand(sm)
    ea_x = jnp.exp(a_x)
    w_x = jnp.exp(a_x[l - 1:l, :] - a_x) * dt_x
    xs = act[:, 0:D_INNER]
    zpad = jnp.zeros((PAGE_SIZE - l, D_STATE), F32)
    trow_x = lax.broadcasted_iota(jnp.int32, (l, GROUP_W), 0)

    xw_pad = jnp.concatenate([xs * w_x, jnp.zeros((PAGE_SIZE - l, D_INNER), F32)], axis=0)
    ea_last = jnp.exp(a_c[l - 1:l, :])
    rowsel = lax.broadcasted_iota(jnp.int32, (pair_w, 1), 0) < HEAD_DIM_S

    for g in range(N_GROUPS_S):
        gc = g * GROUP_W
        bc = act[:, D_INNER + g * D_STATE:D_INNER + (g + 1) * D_STATE]
        cc = act[:, D_INNER + (N_GROUPS_S + g) * D_STATE:D_INNER + (N_GROUPS_S + g + 1) * D_STATE]
        b_pad = jnp.concatenate([bc, zpad], axis=0).astype(BF16)
        cb = _dot_nt(cc.astype(BF16), b_pad)
        s_g = st_ref[0, gc:gc + GROUP_W, :]
        y = _dot_nt(cc.astype(BF16), s_g.astype(BF16)) * ea_x[:, gc:gc + GROUP_W]
        x_g = xs[:, gc:gc + GROUP_W]
        a_g = a_x[:, gc:gc + GROUP_W]
        dt_g = dt_x[:, gc:gc + GROUP_W]
        for s in range(l):
            dec = jnp.exp(jnp.where(trow_x >= s, a_g - a_g[s:s + 1, :], NEG_INF))
            y = y + cb[:, s:s + 1] * dec * (dt_g[s:s + 1, :] * x_g[s:s + 1, :])
        y = y + x_g * dskip_ref[:, gc:gc + GROUP_W]
        gp = y * _silu(z_ref[0, :, gc:gc + GROUP_W])
        rs = lax.rsqrt(jnp.sum(gp * gp, axis=-1, keepdims=True) / GROUP_W + EPS)
        y_ref[0, :, gc:gc + GROUP_W] = gp * rs * nw_ref[:, gc:gc + GROUP_W]
        for pp in range(HEADS_PER_GROUP // 2):
            h0 = g * HEADS_PER_GROUP + 2 * pp
            col = h0 * HEAD_DIM_S
            li = DT_LANE0 + h0
            u = _dot(xw_pad[:, col:col + pair_w].T.astype(BF16), b_pad)
            e_col = jnp.where(rowsel, ea_last[:, li:li + 1], ea_last[:, li + 1:li + 2])
            so_ref[0, col:col + pair_w, :] = st_ref[0, col:col + pair_w, :] * e_col + u


def _ssd_sample(xbc, conv0, z, small, state, conv_w, conv_b, alog_lane, dskip_exp, norm_w):
    b, l, _ = xbc.shape
    full = lambda shape: pl.BlockSpec(shape, lambda bi: (0,) * len(shape))
    seq = lambda shape: pl.BlockSpec(shape, lambda bi: (bi, 0, 0))
    return pl.pallas_call(
        _ssd_sample_body,
        grid=(b,),
        in_specs=[seq((1, l, CONV_DIM)), seq((1, CONV_K - 1, CONV_DIM)), seq((1, l, D_INNER)),
                  seq((1, l, LANES)), seq((1, D_INNER, D_STATE)),
                  full((CONV_K, CONV_DIM)), full((1, CONV_DIM)), full((1, LANES)),
                  full((1, D_INNER)), full((1, D_INNER))],
        out_specs=[seq((1, l, D_INNER)), seq((1, D_INNER, D_STATE))],
        out_shape=[jax.ShapeDtypeStruct((b, l, D_INNER), F32),
                   jax.ShapeDtypeStruct((b, D_INNER, D_STATE), F32)],
        scratch_shapes=[pltpu.VMEM((8 + l, CONV_DIM), F32)],
        compiler_params=_cparams(("arbitrary",)),
        name="ssd_sample",
    )(xbc, conv0, z, small, state, conv_w, conv_b, alog_lane, dskip_exp, norm_w)


def _post_mixer_body(x_ref, oa_ref, ys_ref, gate_ref, g1_ref, sh2_ref, sc2_ref, wa_ref, ws_ref, wo_ref,
                     n2_ref, wr_ref, br_ref, cin_ref, h_in, te_in, tw_in, rk_in,
                     x2_ref, h_ref, te_ref, tw_ref, rk_ref, cout_ref, count_scr):
    del h_in, te_in, tw_in, rk_in
    bb, bl, d = x_ref.shape
    rows = bb * bl

    @pl.when(jnp.logical_and(pl.program_id(0) == 0, pl.program_id(1) == 0))
    def _():
        count_scr[...] = cin_ref[...]

    oa = oa_ref[...].reshape(rows, d).astype(BF16)
    ys = ys_ref[...].reshape(rows, D_INNER).astype(BF16)
    y_attn = _dot(oa, wa_ref[...])
    y_ssm = _dot(ys, ws_ref[...])
    gate = gate_ref[...]
    merged = _sigmoid(gate[:, 0:d]) * y_attn + _sigmoid(gate[:, d:2 * d]) * y_ssm
    mix = _dot(merged.astype(BF16), wo_ref[...])
    x2 = x_ref[...] + g1_ref[...] * mix.reshape(bb, bl, d)
    x2_ref[...] = x2
    ms = jnp.mean(x2 * x2, axis=-1, keepdims=True)
    hf = (x2 * lax.rsqrt(ms + EPS) * n2_ref[...]) * (1.0 + sc2_ref[...]) + sh2_ref[...]
    hf = hf.reshape(rows, d)
    bits = lax.bitcast_convert_type(hf.astype(BF16).astype(F32), jnp.uint32)
    h_ref[...] = (bits[:, d // 2:] & jnp.uint32(0xFFFF0000)) | (bits[:, 0:d // 2] >> 16)
    lane = lax.broadcasted_iota(jnp.int32, (rows, LANES), 1)
    logits = jnp.where(lane < N_EXPERTS, _dot3(hf, wr_ref[...]) + br_ref[...], NEG_INF)
    vals, idxs = [], []
    for _ in range(TOP_K):
        mx = jnp.max(logits, axis=-1, keepdims=True)
        idx = jnp.min(jnp.where(logits == mx, lane, LANES), axis=-1, keepdims=True)
        vals.append(mx)
        idxs.append(idx)
        logits = jnp.where(lane == idx, NEG_INF, logits)
    ex = [jnp.exp(v - vals[0]) for v in vals]
    den = ex[0] + ex[1] + ex[2] + ex[3]
    chosen = [lane == idxs[k] for k in range(TOP_K)]
    cnt = chosen[0].astype(F32)
    for k in range(1, TOP_K):
        cnt = cnt + chosen[k].astype(F32)
    r_i = lax.broadcasted_iota(jnp.int32, (rows, rows), 0)
    c_i = lax.broadcasted_iota(jnp.int32, (rows, rows), 1)
    before = _dot((c_i < r_i).astype(BF16), cnt.astype(BF16)) + count_scr[...]
    te = jnp.zeros((rows, LANES), jnp.int32)
    tw = jnp.zeros((rows, LANES), F32)
    rk = jnp.zeros((rows, LANES), jnp.int32)
    for k in range(TOP_K):
        rank = jnp.sum(jnp.where(chosen[k], before, 0.0), axis=-1, keepdims=True)
        te = jnp.where(lane == k, idxs[k], te)
        tw = jnp.where(lane == k, ex[k] / den, tw)
        rk = jnp.where(lane == k, rank.astype(jnp.int32), rk)
    te_ref[...] = te
    tw_ref[...] = tw
    rk_ref[...] = rk
    count_scr[...] = count_scr[...] + jnp.sum(cnt, axis=0, keepdims=True)
    cout_ref[...] = count_scr[...]


def _post_mixer(x, o_attn, y_ssm, gate, mod, w_attn_out, w_ssm_out, w_o, norm2_w, w_router, b_router,
                counts_in, shared, row0=0, rows=512):
    b, l, d = x.shape
    bb, bl, rows = _row_blocks(b, l, rows)
    nl = l // bl
    assert row0 % rows == 0
    blk0 = row0 // rows
    tok = lambda w: pl.BlockSpec((bb, bl, w), lambda i, j: (i, j, 0))
    flat = lambda w: pl.BlockSpec((rows, w), lambda i, j: (i * nl + j, 0))
    buf = lambda w: pl.BlockSpec((rows, w), lambda i, j: (blk0 + i * nl + j, 0))
    modc = lambda c: pl.BlockSpec((bb, 1, d), lambda i, j: (i, 0, c))
    full = lambda shape: pl.BlockSpec(shape, lambda i, j: (0,) * len(shape))
    in_specs = [tok(d), tok(d), tok(D_INNER), flat(2 * d), modc(2), modc(3), modc(4),
                full((d, d)), full((D_INNER, d)), full((d, d)), full((1, d)),
                full((d, LANES)), full((1, LANES)), full((1, LANES))]
    args = [x, o_attn, y_ssm, gate, mod, mod, mod, w_attn_out, w_ssm_out, w_o, norm2_w.reshape(1, d),
            w_router, b_router, counts_in]
    aliases = {len(args) + i: 1 + i for i in range(len(shared))}
    in_specs += [pl.BlockSpec(memory_space=pl.ANY)] * len(shared)
    args += list(shared)
    total_rows = shared[0].shape[0]
    return pl.pallas_call(
        _post_mixer_body,
        grid=(b // bb, nl),
        in_specs=in_specs,
        out_specs=[tok(d), buf(d // 2), buf(LANES), buf(LANES), buf(LANES), full((1, LANES))],
        out_shape=[jax.ShapeDtypeStruct((b, l, d), F32),
                   jax.ShapeDtypeStruct((total_rows, d // 2), jnp.uint32),
                   jax.ShapeDtypeStruct((total_rows, LANES), jnp.int32),
                   jax.ShapeDtypeStruct((total_rows, LANES), F32),
                   jax.ShapeDtypeStruct((total_rows, LANES), jnp.int32),
                   jax.ShapeDtypeStruct((1, LANES), F32)],
        scratch_shapes=[pltpu.VMEM((1, LANES), F32)],
        input_output_aliases=aliases,
        compiler_params=_cparams(("arbitrary", "arbitrary")),
        name="post_mixer",
    )(*args)


MOE_ROWS = 512
SPLIT_COLS = 512


def _experts_body(be_ref, nb_ref, x_ref, wgu_ref, bg_ref, bu_ref, wdn_ref, bd_ref, o_ref,
                  wg_scr, wu_scr, wd_scr, t_scr):
    i = pl.program_id(0)
    d = wdn_ref.shape[2]
    half = SPLIT_COLS // 2

    @pl.when(jnp.logical_or(i == 0, be_ref[i] != be_ref[jnp.maximum(i - 1, 0)]))
    def _():
        for cb in range(wgu_ref.shape[2] // SPLIT_COLS):
            for kc in range(d // LANES):
                ks = slice(kc * LANES, (kc + 1) * LANES)
                t_scr[kc] = wgu_ref[0, ks, cb * SPLIT_COLS:(cb + 1) * SPLIT_COLS].T
                rows = slice(cb * half, (cb + 1) * half)
                wg_scr[rows, ks] = t_scr[kc, pl.ds(0, half, stride=2), :].astype(BF16)
                wu_scr[rows, ks] = t_scr[kc, pl.ds(1, half, stride=2), :].astype(BF16)
        wd_scr[...] = wdn_ref[0].astype(BF16)

    @pl.when(i < nb_ref[0])
    def _():
        packed = x_ref[...]
        lo = lax.bitcast_convert_type(packed << 16, F32)
        hi = lax.bitcast_convert_type(packed & jnp.uint32(0xFFFF0000), F32)
        x = jnp.concatenate([lo, hi], axis=1).astype(BF16)
        gate = jnp.minimum(_dot_nt(x, wg_scr[...]) + bg_ref[0], SWIGLU_LIMIT)
        up = jnp.clip(_dot_nt(x, wu_scr[...]) + bu_ref[0], -SWIGLU_LIMIT, SWIGLU_LIMIT)
        act = (up + 1.0) * (gate * _sigmoid(SWIGLU_ALPHA * gate))
        o_ref[...] = _dot(act.astype(BF16), wd_scr[...]) + bd_ref[0]

    @pl.when(i >= nb_ref[0])
    def _():
        o_ref[...] = jnp.zeros_like(o_ref)


def _experts(block_e, n_used, xg, w_gu, b_g, b_u, w_d, b_d):
    p = xg.shape[0]
    ff, d = w_d.shape[1], w_d.shape[2]
    wspec = lambda shape: pl.BlockSpec(shape, lambda i, be, nb: (be[i], 0, 0))
    grid_spec = pltpu.PrefetchScalarGridSpec(
        num_scalar_prefetch=2,
        grid=(p // MOE_ROWS,),
        in_specs=[pl.BlockSpec((MOE_ROWS, d // 2), lambda i, be, nb: (i, 0)),
                  wspec((1, d, 2 * ff)), wspec((1, 1, ff)), wspec((1, 1, ff)),
                  wspec((1, ff, d)), wspec((1, 1, d))],
        out_specs=pl.BlockSpec((MOE_ROWS, d), lambda i, be, nb: (i, 0)),
        scratch_shapes=[pltpu.VMEM((ff, d), BF16), pltpu.VMEM((ff, d), BF16), pltpu.VMEM((ff, d), BF16),
                        pltpu.VMEM((d // LANES, SPLIT_COLS, LANES), F32)],
    )
    return pl.pallas_call(
        _experts_body,
        grid_spec=grid_spec,
        out_shape=jax.ShapeDtypeStruct((p, d), F32),
        compiler_params=_cparams(("arbitrary",)),
        name="moe_experts",
    )(block_e, n_used, xg, w_gu, b_g, b_u, w_d, b_d)


SC_WORKERS = 32
SC_BUFFER_BYTES = 128 * 1024


def _sc_gather_rows(table, idx):
    b = idx.shape[0]
    d = table.shape[1]
    per_w = b // SC_WORKERS
    SC_CHUNK = SC_BUFFER_BYTES // (4 * d)
    n = per_w // SC_CHUNK
    assert table.dtype.itemsize == 4 and SC_CHUNK <= 128
    assert b % (SC_WORKERS * SC_CHUNK) == 0 and n % 2 == 0
    mesh = plsc.VectorSubcoreMesh(core_axis_name="c", subcore_axis_name="s")

    @functools.partial(
        pl.kernel, mesh=mesh, out_type=jax.ShapeDtypeStruct((b, d), table.dtype),
        scratch_types=[pltpu.VMEM((n, SC_CHUNK), jnp.int32), pltpu.VMEM((2, SC_CHUNK, d), table.dtype),
                       pltpu.SemaphoreType.DMA((2,)), pltpu.SemaphoreType.DMA((2,))],
        name="sc_gather_rows")
    def gather(table_hbm, idx_hbm, out_hbm, idx_v, rows_v, gsem, wsem):
        wid = lax.axis_index("s") * 2 + lax.axis_index("c")
        base = wid * per_w
        pltpu.sync_copy(idx_hbm.at[wid], idx_v)

        def fetch(c, slot):
            return pltpu.make_async_copy(table_hbm.at[idx_v.at[c]], rows_v.at[slot], gsem.at[slot])

        def write(c, slot):
            off = pl.multiple_of(base + c * SC_CHUNK, SC_CHUNK)
            return pltpu.make_async_copy(rows_v.at[slot], out_hbm.at[pl.ds(off, SC_CHUNK)], wsem.at[slot])

        fetch(0, 0).start()

        @pl.loop(0, n, step=2)
        def _(c0):
            for slot in (0, 1):
                c = c0 + slot
                fetch(c, slot).wait()

                @pl.when(c >= 1)
                def _():
                    write(c - 1, 1 - slot).wait()

                @pl.when(c + 1 < n)
                def _():
                    fetch(c + 1, 1 - slot).start()

                write(c, slot).start()

        write(n - 1, 1).wait()

    return gather(table, idx.reshape(SC_WORKERS, n, SC_CHUNK))


SC_LANES = 16


def _sc_row_tokens(dest, n_rows, n_tokens, slots_per_token):
    s = dest.shape[0]
    per_w = n_rows // SC_WORKERS
    assert n_rows % (SC_WORKERS * SC_LANES) == 0 and s % SC_LANES == 0
    shift = slots_per_token.bit_length() - 1
    assert 1 << shift == slots_per_token
    pad_mask = (1 << (n_tokens.bit_length() - 1)) - 1
    mesh = plsc.VectorSubcoreMesh(core_axis_name="c", subcore_axis_name="s")
    params = pltpu.CompilerParams(needs_layout_passes=False)

    @functools.partial(
        pl.kernel, mesh=mesh, out_type=jax.ShapeDtypeStruct((n_rows,), jnp.int32),
        scratch_types=[pltpu.VMEM((s,), jnp.int32), pltpu.VMEM((per_w,), jnp.int32)],
        compiler_params=params, name="sc_row_tokens")
    def invert(dest_hbm, out_hbm, dest_v, out_v):
        wid = lax.axis_index("s") * 2 + lax.axis_index("c")
        base = wid * per_w
        pltpu.sync_copy(dest_hbm, dest_v)
        lanes = lax.iota(jnp.int32, SC_LANES)

        @pl.loop(0, per_w // SC_LANES)
        def _(i):
            off = pl.multiple_of(i * SC_LANES, SC_LANES)
            out_v[pl.ds(off, SC_LANES)] = (base + off + lanes) & pad_mask

        @pl.loop(0, s // SC_LANES)
        def _(i):
            off = pl.multiple_of(i * SC_LANES, SC_LANES)
            local = dest_v[pl.ds(off, SC_LANES)] - base
            mine = jnp.logical_and(local >= 0, local < per_w)
            tok = lax.shift_right_logical(off + lanes, shift)
            plsc.store_scatter(out_v, [jnp.where(mine, local, 0)], tok, mask=mine)

        pltpu.sync_copy(out_v, out_hbm.at[pl.ds(pl.multiple_of(base, SC_LANES), per_w)])

    return invert(dest)


def _route(top_e, rank, counts):
    t, k = top_e.shape
    s = t * k
    e_flat = top_e.reshape(-1)
    padded = (counts + MOE_ROWS - 1) // MOE_ROWS * MOE_ROWS
    pend = jnp.cumsum(padded)
    onehot = e_flat[:, None] == jnp.arange(N_EXPERTS, dtype=jnp.int32)[None, :]
    dest = jnp.sum(jnp.where(onehot, (pend - padded)[None, :], 0), axis=1) + rank.reshape(-1)
    n_blocks = -(-s // MOE_ROWS) + N_EXPERTS
    row_tok = _sc_row_tokens(dest, n_blocks * MOE_ROWS, t, k)
    block_row0 = jnp.arange(n_blocks, dtype=jnp.int32) * MOE_ROWS
    block_e = jnp.minimum(jnp.sum((pend[None, :] <= block_row0[:, None]).astype(jnp.int32), axis=1),
                          N_EXPERTS - 1)
    n_used = (pend[-1:] // MOE_ROWS).astype(jnp.int32)
    return dest.reshape(t, k), row_tok, block_e, n_used


def _final_body(x_ref, ex_ref, tw_ref, g2_ref, nf_ref, y_ref):
    bb, bl, d = x_ref.shape
    tw = tw_ref[...]
    moe = ex_ref[0] * tw[:, 0:1]
    for k in range(1, TOP_K):
        moe = moe + ex_ref[k] * tw[:, k:k + 1]
    x = x_ref[...] + g2_ref[...] * moe.reshape(bb, bl, d)
    ms = jnp.mean(x * x, axis=-1, keepdims=True)
    y_ref[...] = x * lax.rsqrt(ms + EPS) * nf_ref[...]


def _final(x2, expert_out, top_w, row0, mod, norm_f_w, rows=512):
    b, l, d = x2.shape
    bb, bl, rows = _row_blocks(b, l, rows)
    nl = l // bl
    assert row0 % rows == 0
    blk0 = row0 // rows
    return pl.pallas_call(
        _final_body,
        grid=(b // bb, nl),
        in_specs=[pl.BlockSpec((bb, bl, d), lambda i, j: (i, j, 0)),
                  pl.BlockSpec((TOP_K, rows, d), lambda i, j: (0, blk0 + i * nl + j, 0)),
                  pl.BlockSpec((rows, LANES), lambda i, j: (blk0 + i * nl + j, 0)),
                  pl.BlockSpec((bb, 1, d), lambda i, j: (i, 0, 5)),
                  pl.BlockSpec((1, d), lambda i, j: (0, 0))],
        out_specs=pl.BlockSpec((bb, bl, d), lambda i, j: (i, j, 0)),
        out_shape=jax.ShapeDtypeStruct((b, l, d), F32),
        compiler_params=_cparams(("arbitrary", "arbitrary")),
        name="final_norm",
    )(x2, expert_out, top_w, mod, norm_f_w.reshape(1, d))


def _prep_in_weights(w_in, b_fgate, dt_bias):
    aw = N_HEADS_A * HEAD_DIM_A
    o = np.cumsum([0, aw, aw, aw, N_HEADS_A, D_INNER, CONV_DIM, N_HEADS_S, D_MODEL, D_MODEL])
    w_t = w_in.T
    rows = lambda i: w_t[o[i]:o[i + 1]]
    w_small = jnp.concatenate(
        [rows(3), rows(6), jnp.zeros((LANES - N_HEADS_A - N_HEADS_S, D_MODEL), F32)], axis=0).T
    b_small = jnp.concatenate(
        [b_fgate, dt_bias, jnp.zeros((LANES - N_HEADS_A - N_HEADS_S,), F32)]).reshape(1, LANES)
    big = {"q": rows(0), "k": rows(1), "v": rows(2), "z": rows(4), "xbc": rows(5),
           "gate": w_t[o[7]:o[9]]}
    return {n: w.astype(BF16) for n, w in big.items()}, w_small, b_small


def kernel(x_prompt, x_sample, cache_k, cache_v, cache_logf, state_ssm, state_conv, page_table, c_prompt, c_sample, w_cond, b_cond, norm1_w, w_in, b_fgate, conv_w, conv_b, dt_bias, a_log, d_skip, ssm_norm_w, w_attn_out, w_ssm_out, w_o, norm2_w, w_router, b_router, w_gate_up, b_gate_up, w_down, b_down, norm_f_w):
    assert w_in.shape[0] == 1, "single-layer trunk"
    bp, lp, d = x_prompt.shape
    bs, ls, _ = x_sample.shape
    tp, ts = bp * lp, bs * ls
    aw = N_HEADS_A * HEAD_DIM_A
    c_all = jnp.concatenate([c_prompt, c_sample, jnp.zeros((-(bp + bs) % 8, d), F32)], axis=0)
    mod = _adaln_mod(c_all, w_cond[0], b_cond[0])
    mod_p = mod[:bp].reshape(bp, 1, 6 * d)
    mod_s = mod[bp:bp + bs].reshape(bs, 1, 6 * d)

    w_big, w_small, b_small = _prep_in_weights(w_in[0], b_fgate[0], dt_bias[0])
    pad_lanes = LANES - N_HEADS_A - N_HEADS_S
    alog_lane = jnp.concatenate([jnp.zeros((N_HEADS_A,), F32), a_log[0], jnp.zeros((pad_lanes,), F32)]).reshape(1, LANES)
    dskip_exp = jnp.repeat(d_skip[0], HEAD_DIM_S).reshape(1, D_INNER)
    ssm_nw = ssm_norm_w[0].reshape(1, D_INNER)
    conv_b2 = conv_b[0].reshape(1, CONV_DIM)
    wa, ws, wo = w_attn_out[0].astype(BF16), w_ssm_out[0].astype(BF16), w_o[0].astype(BF16)
    w_router_pad = jnp.pad(w_router[0], ((0, 0), (0, LANES - N_EXPERTS)))
    b_router_pad = jnp.pad(b_router[0], (0, LANES - N_EXPERTS)).reshape(1, LANES)

    def in_proj(x, m, half_copies):
        h, small = _norm_in(x, m, norm1_w[0], w_small, b_small)
        return {n: _matmul_nt(h, w, also_bf16=n in half_copies) for n, w in w_big.items()}, small

    proj_p, small_p = in_proj(x_prompt, mod_p, ("k", "v"))
    proj_s, small_s = in_proj(x_sample, mod_s, ())
    (k_p, kb_p), (v_p, vb_p) = proj_p["k"], proj_p["v"]

    fcum = _cumsum_t(small_p.reshape(bp, lp, LANES)).reshape(bp, N_HEADS_A // 2, 2, lp)
    o_p = _fox_prompt(proj_p["q"].reshape(bp, lp, aw), kb_p.reshape(bp, lp, aw),
                      vb_p.reshape(bp, lp, aw), fcum)
    xbc_p = proj_p["xbc"].reshape(bp, lp, CONV_DIM)
    y_p, st_p = _ssd_prompt(xbc_p, proj_p["z"].reshape(bp, lp, D_INNER), small_p.reshape(bp, lp, LANES),
                            conv_w[0], conv_b2, alog_lane, dskip_exp, ssm_nw)

    n_phys = cache_k.shape[1]
    k_t = jnp.transpose(cache_k[0], (0, 2, 3, 1)).reshape(n_phys, aw, PAGE_SIZE)
    v_t = jnp.transpose(cache_v[0], (0, 2, 3, 1)).reshape(n_phys, aw, PAGE_SIZE)
    lf_t = jnp.swapaxes(cache_logf[0], 1, 2).reshape(n_phys * N_HEADS_A, PAGE_SIZE)
    c_pages = _cumsum_lanes(lf_t, rows=512).reshape(n_phys, N_HEADS_A, PAGE_SIZE)
    n_pages = page_table.shape[1]
    o_s = _fox_sample(proj_s["q"].reshape(bs, ls, aw), proj_s["k"].reshape(bs, ls, aw),
                      proj_s["v"].reshape(bs, ls, aw), small_s.reshape(bs, ls, LANES), k_t, v_t,
                      c_pages, page_table, pages_per_step=min(16, n_pages))
    xbc_s = proj_s["xbc"].reshape(bs, ls, CONV_DIM)
    y_s, st_s = _ssd_sample(xbc_s, state_conv[0], proj_s["z"].reshape(bs, ls, D_INNER),
                            small_s.reshape(bs, ls, LANES), state_ssm[0].reshape(bs, D_INNER, D_STATE),
                            conv_w[0], conv_b2, alog_lane, dskip_exp, ssm_nw)

    post = functools.partial(_post_mixer, w_attn_out=wa, w_ssm_out=ws, w_o=wo, norm2_w=norm2_w[0],
                             w_router=w_router_pad, b_router=b_router_pad)
    bufs = [jnp.zeros((tp + ts, d // 2), jnp.uint32), jnp.zeros((tp + ts, LANES), jnp.int32),
            jnp.zeros((tp + ts, LANES), F32), jnp.zeros((tp + ts, LANES), jnp.int32)]
    x2_p, *bufs, counts_p = post(x_prompt, o_p, y_p, proj_p["gate"], mod_p,
                                 counts_in=jnp.zeros((1, LANES), F32), shared=bufs)
    x2_s, hf, top_e, top_w, rank, counts = post(x_sample, o_s, y_s, proj_s["gate"], mod_s,
                                                counts_in=counts_p, row0=tp, shared=bufs)

    dest, row_tok, block_e, n_used = _route(top_e[:, :TOP_K], rank[:, :TOP_K],
                                            counts[0, :N_EXPERTS].astype(jnp.int32))
    rows = _experts(block_e, n_used, _sc_gather_rows(hf, row_tok), w_gate_up[0],
                    b_gate_up[0][:, None, 0::2], b_gate_up[0][:, None, 1::2],
                    w_down[0], b_down[0][:, None, :])
    expert_out = _sc_gather_rows(rows, dest.T.reshape(-1)).reshape(TOP_K, tp + ts, d)

    y_prompt = _final(x2_p, expert_out, top_w, 0, mod_p, norm_f_w)
    y_sample = _final(x2_s, expert_out, top_w, tp, mod_s, norm_f_w)

    def state_rows(k, v, small, st, xbc, conv0, b, l):
        conv_new = jnp.concatenate([conv0, xbc], axis=1)[:, -(CONV_K - 1):]
        return (k.reshape(1, b, l, N_HEADS_A, HEAD_DIM_A), v.reshape(1, b, l, N_HEADS_A, HEAD_DIM_A),
                small[:, :N_HEADS_A].reshape(1, b, l, N_HEADS_A),
                st.reshape(1, b, N_HEADS_S, HEAD_DIM_S, D_STATE), conv_new[None])

    sp = state_rows(k_p, v_p, small_p, st_p, xbc_p,
                    jnp.zeros((bp, CONV_K - 1, CONV_DIM), F32), bp, lp)
    ss = state_rows(proj_s["k"], proj_s["v"], small_s, st_s, xbc_s, state_conv[0], bs, ls)
    return (y_prompt, y_sample) + sp + ss
```
